```python
import jax, jax.numpy as jnp
from jax import lax
import numpy as np

D_MODEL = 2048
BATCH = 1
SEQ = 8192
DEPTH = 4

D_MIX = D_MODEL
D_FF = 5632
NORM_EPS = 1e-6
BLOCK = 128

SB_HEADS = 4
SB_HEAD_DIM = 128
SB_WIDTH = SB_HEADS * SB_HEAD_DIM

MLA_HEADS = 4
MLA_NOPE_DIM = 128
MLA_ROPE_DIM = 64
MLA_V_DIM = 128
MLA_Q_LORA = 512
MLA_KV_LORA = 256
MLA_WIDTH = MLA_HEADS * MLA_V_DIM
ROPE_THETA = 10000.0

RWKV_HEADS = 16
RWKV_HEAD_DIM = 64
RWKV_WIDTH = RWKV_HEADS * RWKV_HEAD_DIM
RWKV_DECAY_LORA = 64
RWKV_A_LORA = 64
RWKV_GATE_LORA = 128
RWKV_GN_EPS = 64e-5

SB_COLS = 3 * SB_WIDTH
MLA_COLS = MLA_Q_LORA + MLA_KV_LORA + MLA_ROPE_DIM
RWKV_COLS = 3 * RWKV_WIDTH + RWKV_DECAY_LORA + RWKV_A_LORA + RWKV_GATE_LORA
IN_COLS = SB_COLS + MLA_COLS + RWKV_COLS

kernel_name = "hybrid_sb_mla_rwkv7_macaron"


def rmsnorm(x, g, eps=NORM_EPS):
    x32 = x.astype(jnp.float32)
    y = x32 * lax.rsqrt(jnp.mean(x32 * x32, axis=-1, keepdims=True) + eps)
    return (y * g.astype(jnp.float32)).astype(x.dtype)


def swiglu(h, w_gate, w_up, w_down):
    return (jax.nn.silu(h @ w_gate) * (h @ w_up)) @ w_down


def split_cols(z, sizes):
    idx = [int(i) for i in np.cumsum(sizes)[:-1]]
    return jnp.split(z, idx, axis=-1)


def to_blocks(t):
    b, s, h, d = t.shape
    return t.reshape(b, s // BLOCK, BLOCK, h, d).transpose(1, 0, 2, 3, 4)


def from_blocks(t):
    nb, b, blk, h, d = t.shape
    return t.transpose(1, 0, 2, 3, 4).reshape(b, nb * blk, h, d)


def stick_breaking_attention(q, k, v):
    s_len = q.shape[1]
    scale = SB_HEAD_DIM ** -0.5
    key_pos = jnp.arange(s_len)

    def block(args):
        q_blk, i = args
        z = jnp.einsum('bqhd,bkhd->bhqk', q_blk, k,
                       preferred_element_type=jnp.float32) * scale
        q_pos = i * BLOCK + jnp.arange(BLOCK)
        before = key_pos[None, :] < q_pos[:, None]
        log_beta = jax.nn.log_sigmoid(z)
        log_keep = jnp.where(before, jax.nn.log_sigmoid(-z), 0.0)
        later = lax.cumsum(log_keep, axis=3, reverse=True) - log_keep
        a = jnp.where(before, jnp.exp(log_beta + later), 0.0)
        return jnp.einsum('bhqk,bkhd->bqhd', a.astype(v.dtype), v)

    out = lax.map(block, (to_blocks(q), jnp.arange(s_len // BLOCK)))
    return from_blocks(out)


def apply_rope(t, cos, sin):
    t1, t2 = jnp.split(t.astype(jnp.float32), 2, axis=-1)
    return jnp.concatenate([t1 * cos - t2 * sin, t1 * sin + t2 * cos], axis=-1).astype(t.dtype)


def mla_attention(c_q, c_kv, k_rope, positions, q_norm, w_uq, kv_norm, w_ukv):
    b, s_len, _ = c_q.shape
    q = (rmsnorm(c_q, q_norm) @ w_uq).reshape(b, s_len, MLA_HEADS, MLA_NOPE_DIM + MLA_ROPE_DIM)
    q_nope, q_pe = jnp.split(q, [MLA_NOPE_DIM], axis=-1)
    kv = (rmsnorm(c_kv, kv_norm) @ w_ukv).reshape(b, s_len, MLA_HEADS, MLA_NOPE_DIM + MLA_V_DIM)
    k_nope, v = jnp.split(kv, [MLA_NOPE_DIM], axis=-1)

    half = MLA_ROPE_DIM // 2
    inv_freq = ROPE_THETA ** (-jnp.arange(half, dtype=jnp.float32) / half)
    ang = positions.astype(jnp.float32)[..., None] * inv_freq
    cos, sin = jnp.cos(ang), jnp.sin(ang)
    q_pe = apply_rope(q_pe, cos[:, :, None, :], sin[:, :, None, :])
    k_pe = apply_rope(k_rope, cos, sin)

    q = jnp.concatenate([q_nope, q_pe], axis=-1)
    k = jnp.concatenate(
        [k_nope, jnp.broadcast_to(k_pe[:, :, None, :], (b, s_len, MLA_HEADS, MLA_ROPE_DIM))], axis=-1)
    scale = (MLA_NOPE_DIM + MLA_ROPE_DIM) ** -0.5
    key_pos = jnp.arange(s_len)

    def block(args):
        q_blk, i = args
        sc = jnp.einsum('bqhd,bkhd->bhqk', q_blk, k,
                        preferred_element_type=jnp.float32) * scale
        q_pos = i * BLOCK + jnp.arange(BLOCK)
        sc = jnp.where(key_pos[None, :] <= q_pos[:, None], sc, jnp.finfo(jnp.float32).min)
        p = jax.nn.softmax(sc, axis=-1)
        return jnp.einsum('bhqk,bkhd->bqhd', p.astype(v.dtype), v)

    out = lax.map(block, (to_blocks(q), jnp.arange(s_len // BLOCK)))
    return from_blocks(out)


def rwkv7_time_mix(z, mu, w0, w2, a0, a2, g2, k_k, k_a, r_k, ln_w, ln_b):
    b, s_len, _ = z.shape
    f32 = jnp.float32
    z_prev = jnp.pad(z, ((0, 0), (1, 0), (0, 0)))[:, :-1]
    z = z + (z_prev - z) * mu
    r, k, v, xw, xa, xg = split_cols(
        z, [RWKV_WIDTH] * 3 + [RWKV_DECAY_LORA, RWKV_A_LORA, RWKV_GATE_LORA])

    log_w = -jax.nn.softplus(-(w0 + jnp.tanh(xw) @ w2)) - 0.5
    decay = jnp.exp(-jnp.exp(log_w.astype(f32)))
    a = jax.nn.sigmoid(a0 + xa @ a2)
    g = jax.nn.sigmoid(xg) @ g2

    def heads(t):
        return t.reshape(b, s_len, RWKV_HEADS, RWKV_HEAD_DIM).astype(f32)

    kk = heads(k * k_k)
    kk = kk / jnp.maximum(jnp.sqrt(jnp.sum(kk * kk, axis=-1, keepdims=True)), 1e-12)
    k = k * (1 + (a - 1) * k_a)
    r_h, k_h, v_h, a_h, w_h = heads(r), heads(k), heads(v), heads(a), heads(decay)

    def step(state, inp):
        r_t, w_t, k_t, v_t, kk_t, b_t = inp
        sa = jnp.einsum('bhvk,bhk->bhv', state, -kk_t)
        state = (state * w_t[:, :, None, :] + sa[..., None] * b_t[:, :, None, :]
                 + v_t[..., None] * k_t[:, :, None, :])
        return state, jnp.einsum('bhvk,bhk->bhv', state, r_t)

    xs = tuple(t.transpose(1, 0, 2, 3) for t in (r_h, w_h, k_h, v_h, kk, kk * a_h))
    s0 = jnp.zeros((b, RWKV_HEADS, RWKV_HEAD_DIM, RWKV_HEAD_DIM), f32)
    _, y = lax.scan(step, s0, xs)
    y = y.transpose(1, 0, 2, 3)

    mean = jnp.mean(y, axis=-1, keepdims=True)
    var = jnp.mean(jnp.square(y - mean), axis=-1, keepdims=True)
    y = (y - mean) * lax.rsqrt(var + RWKV_GN_EPS)
    y = (y * ln_w.astype(f32).reshape(RWKV_HEADS, RWKV_HEAD_DIM)
         + ln_b.astype(f32).reshape(RWKV_HEADS, RWKV_HEAD_DIM))
    y = y + jnp.sum(r_h * k_h * r_k.astype(f32), axis=-1, keepdims=True) * v_h
    return (y.reshape(b, s_len, RWKV_WIDTH) * g).astype(z.dtype)


def setup_inputs(seed: int = 0) -> dict:
    key = jax.random.key(seed)
    ks = iter(jax.random.split(key, 40))
    f32 = jnp.float32
    L = DEPTH

    def normal(shape, scale):
        return jax.random.normal(next(ks), shape, f32) * scale

    def gain(shape):
        return 1.0 + normal(shape, 0.02)

    ratio = jnp.arange(RWKV_WIDTH, dtype=f32) / (RWKV_WIDTH - 1)
    inputs = {
        "x": normal((BATCH, SEQ, D_MODEL), 1.0),
        "positions": jnp.broadcast_to(jnp.arange(SEQ, dtype=jnp.int32)[None, :], (BATCH, SEQ)),
        "ffn1_norm": gain((L, D_MODEL)),
        "ffn1_gate": normal((L, D_MODEL, D_FF), D_MODEL ** -0.5),
        "ffn1_up": normal((L, D_MODEL, D_FF), D_MODEL ** -0.5),
        "ffn1_down": normal((L, D_FF, D_MODEL), D_FF ** -0.5),
        "mix_norm": gain((L, D_MODEL)),
        "w_in": normal((L, D_MODEL, IN_COLS), D_MODEL ** -0.5),
        "mla_q_norm": gain((L, MLA_Q_LORA)),
        "mla_w_uq": normal((L, MLA_Q_LORA, MLA_HEADS * (MLA_NOPE_DIM + MLA_ROPE_DIM)), MLA_Q_LORA ** -0.5),
        "mla_kv_norm": gain((L, MLA_KV_LORA)),
        "mla_w_ukv": normal((L, MLA_KV_LORA, MLA_HEADS * (MLA_NOPE_DIM + MLA_V_DIM)), MLA_KV_LORA ** -0.5),
        "rwkv_mu": jax.random.uniform(next(ks), (L, RWKV_COLS), f32, 0.1, 0.9),
        "rwkv_w0": -6.0 + 5.0 * ratio ** 0.85 + normal((L, RWKV_WIDTH), 0.1),
        "rwkv_w2": normal((L, RWKV_DECAY_LORA, RWKV_WIDTH), 0.5 * RWKV_DECAY_LORA ** -0.5),
        "rwkv_a0": normal((L, RWKV_WIDTH), 0.1),
        "rwkv_a2": normal((L, RWKV_A_LORA, RWKV_WIDTH), 0.5 * RWKV_A_LORA ** -0.5),
        "rwkv_g2": normal((L, RWKV_GATE_LORA, RWKV_WIDTH), RWKV_GATE_LORA ** -0.5),
        "rwkv_k_k": 0.85 + normal((L, RWKV_WIDTH), 0.02),
        "rwkv_k_a": gain((L, RWKV_WIDTH)),
        "rwkv_r_k": normal((L, RWKV_HEADS, RWKV_HEAD_DIM), 0.1),
        "rwkv_ln_w": gain((L, RWKV_WIDTH)),
        "rwkv_ln_b": normal((L, RWKV_WIDTH), 0.02),
        "sb_out_norm": gain((L, SB_HEADS, SB_HEAD_DIM)),
        "mla_out_norm": gain((L, MLA_HEADS, MLA_V_DIM)),
        "w_out": normal((L, D_MIX, D_MODEL), D_MIX ** -0.5),
        "ffn2_norm": gain((L, D_MODEL)),
        "ffn2_gate": normal((L, D_MODEL, D_FF), D_MODEL ** -0.5),
        "ffn2_up": normal((L, D_MODEL, D_FF), D_MODEL ** -0.5),
        "ffn2_down": normal((L, D_FF, D_MODEL), D_FF ** -0.5),
        "final_norm": gain((D_MODEL,)),
    }
    return inputs


def reference(x, positions, ffn1_norm, ffn1_gate, ffn1_up, ffn1_down, mix_norm, w_in,
              mla_q_norm, mla_w_uq, mla_kv_norm, mla_w_ukv,
              rwkv_mu, rwkv_w0, rwkv_w2, rwkv_a0, rwkv_a2, rwkv_g2, rwkv_k_k, rwkv_k_a,
              rwkv_r_k, rwkv_ln_w, rwkv_ln_b, sb_out_norm, mla_out_norm, w_out,
              ffn2_norm, ffn2_gate, ffn2_up, ffn2_down, final_norm):
    b, s_len, _ = x.shape
    for l in range(DEPTH):
        x = x + 0.5 * swiglu(rmsnorm(x, ffn1_norm[l]), ffn1_gate[l], ffn1_up[l], ffn1_down[l])

        h = rmsnorm(x, mix_norm[l])
        z = h @ w_in[l]
        sb_q, sb_k, sb_v, c_q, c_kv, k_rope, z_rwkv = split_cols(
            z, [SB_WIDTH] * 3 + [MLA_Q_LORA, MLA_KV_LORA, MLA_ROPE_DIM, RWKV_COLS])

        sb_shape = (b, s_len, SB_HEADS, SB_HEAD_DIM)
        y_sb = stick_breaking_attention(sb_q.reshape(sb_shape), sb_k.reshape(sb_shape),
                                        sb_v.reshape(sb_shape))
        y_sb = rmsnorm(y_sb, sb_out_norm[l]).reshape(b, s_len, SB_WIDTH)

        y_mla = mla_attention(c_q, c_kv, k_rope, positions, mla_q_norm[l], mla_w_uq[l],
                              mla_kv_norm[l], mla_w_ukv[l])
        y_mla = rmsnorm(y_mla, mla_out_norm[l]).reshape(b, s_len, MLA_WIDTH)

        y_rwkv = rwkv7_time_mix(z_rwkv, rwkv_mu[l], rwkv_w0[l], rwkv_w2[l], rwkv_a0[l],
                                rwkv_a2[l], rwkv_g2[l], rwkv_k_k[l], rwkv_k_a[l],
                                rwkv_r_k[l], rwkv_ln_w[l], rwkv_ln_b[l])

        x = x + jnp.concatenate([y_sb, y_mla, y_rwkv], axis=-1) @ w_out[l]

        x = x + 0.5 * swiglu(rmsnorm(x, ffn2_norm[l]), ffn2_gate[l], ffn2_up[l], ffn2_down[l])
    return rmsnorm(x, final_norm)
```

```python
import functools

import jax
import jax.numpy as jnp
import numpy as np
from jax import lax
from jax.experimental import pallas as pl
from jax.experimental.pallas import tpu as pltpu

F32 = jnp.float32
BF16 = jnp.bfloat16

NORM_EPS = 1e-6
SB_HEADS = 4
SB_HEAD_DIM = 128
SB_WIDTH = SB_HEADS * SB_HEAD_DIM
MLA_HEADS = 4
MLA_NOPE_DIM = 128
MLA_ROPE_DIM = 64
MLA_V_DIM = 128
MLA_Q_LORA = 512
MLA_KV_LORA = 256
MLA_WIDTH = MLA_HEADS * MLA_V_DIM
MLA_QK_PAD = 256
ROPE_THETA = 10000.0
RWKV_HEADS = 16
RWKV_HEAD_DIM = 64
RWKV_WIDTH = RWKV_HEADS * RWKV_HEAD_DIM
RWKV_DECAY_LORA = 64
RWKV_A_LORA = 64
RWKV_GATE_LORA = 128
RWKV_GN_EPS = 64e-5
RWKV_CHUNK = 64

COL_R, COL_K, COL_V = 0, 1024, 2048
COL_SBQ, COL_SBK, COL_SBV = 3072, 3584, 4096
COL_CQ, COL_CKV = 4608, 5120
COL_LORA, COL_XG, COL_ROPE = 5376, 5504, 5632
Z_COLS = 5760

VMEM_LIMIT = 48 * 1024 * 1024


def _cparams(sem):
    return pltpu.CompilerParams(dimension_semantics=sem, vmem_limit_bytes=VMEM_LIMIT)


def _dot(a, b):
    return jnp.dot(a, b, preferred_element_type=F32)


def _dot_nt(a, b):
    return lax.dot_general(a, b, (((1,), (1,)), ((), ())), preferred_element_type=F32)


def _dot_tn(a, b):
    return lax.dot_general(a, b, (((0,), (0,)), ((), ())), preferred_element_type=F32)


def _split3(x):
    hi = x.astype(BF16)
    r1 = x - hi.astype(F32)
    mid = r1.astype(BF16)
    lo = (r1 - mid.astype(F32)).astype(BF16)
    return hi, mid, lo


def _split2(x):
    hi = x.astype(BF16)
    lo = (x - hi.astype(F32)).astype(BF16)
    return hi, lo


def _mm3(dot, a, b):
    ah, al = _split2(a)
    bh, bl = _split2(b)
    return dot(ah, bh) + (dot(ah, bl) + dot(al, bh))


def _exact_lhs_dot(m_bf16, x):
    hi, mid, lo = _split3(x)
    return _dot(m_bf16, hi) + (_dot(m_bf16, mid) + _dot(m_bf16, lo))


def _rms_rows(x, g):
    ms = jnp.mean(x * x, axis=-1, keepdims=True)
    return x * lax.rsqrt(ms + NORM_EPS) * g


def _norm_proj_kernel(x_ref, g_ref, w_ref, o_ref, h_ref):
    @pl.when(pl.program_id(1) == 0)
    def _():
        h_ref[...] = _rms_rows(x_ref[...], g_ref[...]).astype(BF16)

    o_ref[...] = _dot(h_ref[...], w_ref[...]).astype(o_ref.dtype)


def norm_proj(x, g, w, *, tm, tn, out_dtype=F32):
    s, d = x.shape
    n = w.shape[1]
    return pl.pallas_call(
        _norm_proj_kernel,
        grid=(s // tm, n // tn),
        in_specs=[
            pl.BlockSpec((tm, d), lambda i, j: (i, 0)),
            pl.BlockSpec((1, d), lambda i, j: (0, 0)),
            pl.BlockSpec((d, tn), lambda i, j: (0, j)),
        ],
        out_specs=pl.BlockSpec((tm, tn), lambda i, j: (i, j)),
        out_shape=jax.ShapeDtypeStruct((s, n), out_dtype),
        scratch_shapes=[pltpu.VMEM((tm, d), BF16)],
        compiler_params=_cparams(("parallel", "arbitrary")),
        name="norm_proj",
    )(x, g.reshape(1, d), w)


def _ffn_up_kernel(x_ref, g_ref, wg_ref, wu_ref, o_ref, h_ref):
    @pl.when(pl.program_id(1) == 0)
    def _():
        h_ref[...] = _rms_rows(x_ref[...], g_ref[...]).astype(BF16)

    h = h_ref[...]
    a = _dot(h, wg_ref[...])
    u = _dot(h, wu_ref[...])
    o_ref[...] = (a * jax.nn.sigmoid(a) * u).astype(o_ref.dtype)


def ffn_up(x, g, wg, wu, *, tm, tn):
    s, d = x.shape
    f = wg.shape[1]
    return pl.pallas_call(
        _ffn_up_kernel,
        grid=(s // tm, f // tn),
        in_specs=[
            pl.BlockSpec((tm, d), lambda i, j: (i, 0)),
            pl.BlockSpec((1, d), lambda i, j: (0, 0)),
            pl.BlockSpec((d, tn), lambda i, j: (0, j)),
            pl.BlockSpec((d, tn), lambda i, j: (0, j)),
        ],
        out_specs=pl.BlockSpec((tm, tn), lambda i, j: (i, j)),
        out_shape=jax.ShapeDtypeStruct((s, f), BF16),
        scratch_shapes=[pltpu.VMEM((tm, d), BF16)],
        compiler_params=_cparams(("parallel", "arbitrary")),
        name="ffn_up",
    )(x, g.reshape(1, d), wg, wu)


def _ffn_down_kernel(x_ref, a_ref, w_ref, o_ref):
    o_ref[...] = x_ref[...] + 0.5 * _dot(a_ref[...], w_ref[...])


def ffn_down(x, act, wd, *, tm, tn):
    s, d = x.shape
    f = act.shape[1]
    return pl.pallas_call(
        _ffn_down_kernel,
        grid=(s // tm, d // tn),
        in_specs=[
            pl.BlockSpec((tm, tn), lambda i, j: (i, j)),
            pl.BlockSpec((tm, f), lambda i, j: (i, 0)),
            pl.BlockSpec((f, tn), lambda i, j: (0, j)),
        ],
        out_specs=pl.BlockSpec((tm, tn), lambda i, j: (i, j)),
        out_shape=jax.ShapeDtypeStruct((s, d), F32),
        compiler_params=_cparams(("parallel", "arbitrary")),
        name="ffn_down",
    )(x, act, wd)


def _out_proj_kernel(x_ref, a_ref, b_ref, c_ref, wa_ref, wb_ref, wc_ref, o_ref):
    acc = _dot(a_ref[...], wa_ref[...])
    acc += _dot(b_ref[...], wb_ref[...])
    acc += _dot(c_ref[...], wc_ref[...])
    o_ref[...] = x_ref[...] + acc


def out_proj(x, y_sb, y_mla, y_rwkv, w_out, *, tm, tn):
    s, d = x.shape
    wa, wb, wc = w_out[:SB_WIDTH], w_out[SB_WIDTH:SB_WIDTH + MLA_WIDTH], w_out[SB_WIDTH + MLA_WIDTH:]
    return pl.pallas_call(
        _out_proj_kernel,
        grid=(s // tm, d // tn),
        in_specs=[
            pl.BlockSpec((tm, tn), lambda i, j: (i, j)),
            pl.BlockSpec((tm, SB_WIDTH), lambda i, j: (i, 0)),
            pl.BlockSpec((tm, MLA_WIDTH), lambda i, j: (i, 0)),
            pl.BlockSpec((tm, RWKV_WIDTH), lambda i, j: (i, 0)),
            pl.BlockSpec((SB_WIDTH, tn), lambda i, j: (0, j)),
            pl.BlockSpec((MLA_WIDTH, tn), lambda i, j: (0, j)),
            pl.BlockSpec((RWKV_WIDTH, tn), lambda i, j: (0, j)),
        ],
        out_specs=pl.BlockSpec((tm, tn), lambda i, j: (i, j)),
        out_shape=jax.ShapeDtypeStruct((s, d), F32),
        compiler_params=_cparams(("parallel", "arbitrary")),
        name="out_proj",
    )(x, y_sb, y_mla, y_rwkv, wa, wb, wc)


def _final_norm_kernel(x_ref, g_ref, o_ref):
    o_ref[...] = _rms_rows(x_ref[...], g_ref[...])


def final_rmsnorm(x, g, *, tm):
    s, d = x.shape
    return pl.pallas_call(
        _final_norm_kernel,
        grid=(s // tm,),
        in_specs=[pl.BlockSpec((tm, d), lambda i: (i, 0)), pl.BlockSpec((1, d), lambda i: (0, 0))],
        out_specs=pl.BlockSpec((tm, d), lambda i: (i, 0)),
        out_shape=jax.ShapeDtypeStruct((s, d), F32),
        compiler_params=_cparams(("parallel",)),
        name="final_norm",
    )(x, g.reshape(1, d))


SB_BQ = 256
SB_BK = 128
SB_DEAD_LOG = -104.0


def _sb_attn_kernel(q_ref, k_ref, v_ref, g_ref, o_ref):
    i = pl.program_id(1)
    bq, bk = SB_BQ, SB_BK
    q = (q_ref[...] * (SB_HEAD_DIM ** -0.5)).astype(BF16)
    row = lax.broadcasted_iota(jnp.int32, (bq, bk), 0)
    col = lax.broadcasted_iota(jnp.int32, (bq, bk), 1)
    kr = lax.broadcasted_iota(jnp.int32, (bk, bk), 0)
    kc = lax.broadcasted_iota(jnp.int32, (bk, bk), 1)
    upper = jnp.where(kr > kc, 1.0, 0.0).astype(BF16)

    def block(kb, carry, acc, masked):
        start = pl.multiple_of(kb * bk, bk)
        kblk = k_ref[pl.ds(start, bk), :].astype(BF16)
        vblk = v_ref[pl.ds(start, bk), :].astype(BF16)
        z = _dot_nt(q, kblk)
        log_beta = jnp.minimum(z, 0.0) - jnp.log1p(jnp.exp(-jnp.abs(z)))
        log_keep = log_beta - z
        if masked:
            before = (start + col) < (i * bq + row)
            log_keep = jnp.where(before, log_keep, 0.0)
        hi, lo = _split2(log_keep)
        within = _dot(hi, upper) + _dot(lo, upper)
        a = jnp.exp(log_beta + (carry + within))
        if masked:
            a = jnp.where(before, a, 0.0)
        acc = acc + _dot(a.astype(BF16), vblk)
        carry = carry + jnp.sum(log_keep, axis=-1, keepdims=True)
        return carry, acc

    carry = jnp.zeros((bq, 1), F32)
    acc = jnp.zeros((bq, SB_HEAD_DIM), F32)
    nd = bq // bk
    first = i * nd
    for d in range(nd - 1, -1, -1):
        carry, acc = block(first + d, carry, acc, True)

    def cond(st):
        kb, alive, _, _ = st
        return jnp.logical_and(kb >= 0, alive > SB_DEAD_LOG)

    def body(st):
        kb, _, carry, acc = st
        carry, acc = block(kb, carry, acc, False)
        return kb - 1, jnp.max(carry), carry, acc

    _, _, _, acc = lax.while_loop(cond, body, (first - 1, jnp.max(carry), carry, acc))
    o_ref[...] = _rms_rows(acc, g_ref[0]).astype(o_ref.dtype)


def sb_attention(z, gains):
    s = z.shape[0]
    qb, kb_, vb = COL_SBQ // SB_HEAD_DIM, COL_SBK // SB_HEAD_DIM, COL_SBV // SB_HEAD_DIM
    return pl.pallas_call(
        _sb_attn_kernel,
        grid=(SB_HEADS, s // SB_BQ),
        in_specs=[
            pl.BlockSpec((SB_BQ, SB_HEAD_DIM), lambda h, i: (i, qb + h)),
            pl.BlockSpec((s, SB_HEAD_DIM), lambda h, i: (0, kb_ + h)),
            pl.BlockSpec((s, SB_HEAD_DIM), lambda h, i: (0, vb + h)),
            pl.BlockSpec((1, 1, SB_HEAD_DIM), lambda h, i: (h, 0, 0)),
        ],
        out_specs=pl.BlockSpec((SB_BQ, SB_HEAD_DIM), lambda h, i: (i, h)),
        out_shape=jax.ShapeDtypeStruct((s, SB_WIDTH), BF16),
        compiler_params=_cparams(("parallel", "arbitrary")),
        name="sb_attention",
    )(z, z, z, gains)


def _mla_prep_kernel(cq_ref, ckv_ref, rope_ref, tab_ref, qn_ref, kvn_ref, wq_ref, wkv_ref,
                     q_ref, k_ref, v_ref):
    scale = (MLA_NOPE_DIM + MLA_ROPE_DIM) ** -0.5
    hq = _rms_rows(cq_ref[...], qn_ref[...]).astype(BF16)
    hkv = _rms_rows(ckv_ref[...], kvn_ref[...]).astype(BF16)
    cos2 = tab_ref[:, :128]
    sin2 = tab_ref[:, 128:]
    kr = rope_ref[...]
    k_pe = kr * cos2 + pltpu.roll(kr, 64, 1) * sin2
    k_pe = k_pe.astype(BF16)
    qall = _dot(hq, wq_ref[...])
    kvall = _dot(hkv, wkv_ref[...])
    for h in range(MLA_HEADS):
        qh = qall[:, h * 384:(h + 1) * 384]
        q_ref[h, :, :128] = (qh[:, :128] * scale).astype(BF16)
        q_ref[h, :, 128:] = ((qh[:, 128:256] * cos2 + qh[:, 256:384] * sin2) * scale).astype(BF16)
        k_ref[h, :, :128] = kvall[:, h * 256:h * 256 + 128].astype(BF16)
        k_ref[h, :, 128:] = k_pe
        v_ref[h] = kvall[:, h * 256 + 128:(h + 1) * 256].astype(BF16)


def mla_prep(z, tab, q_norm, kv_norm, wq, wkv, *, ts):
    s = z.shape[0]
    return pl.pallas_call(
        _mla_prep_kernel,
        grid=(s // ts,),
        in_specs=[
            pl.BlockSpec((ts, MLA_Q_LORA), lambda i: (i, COL_CQ // MLA_Q_LORA)),
            pl.BlockSpec((ts, MLA_KV_LORA), lambda i: (i, COL_CKV // MLA_KV_LORA)),
            pl.BlockSpec((ts, 128), lambda i: (i, COL_ROPE // 128)),
            pl.BlockSpec((ts, 256), lambda i: (i, 0)),
            pl.BlockSpec((1, MLA_Q_LORA), lambda i: (0, 0)),
            pl.BlockSpec((1, MLA_KV_LORA), lambda i: (0, 0)),
            pl.BlockSpec(wq.shape, lambda i: (0, 0)),
            pl.BlockSpec(wkv.shape, lambda i: (0, 0)),
        ],
        out_specs=[
            pl.BlockSpec((MLA_HEADS, ts, MLA_QK_PAD), lambda i: (0, i, 0)),
            pl.BlockSpec((MLA_HEADS, ts, MLA_QK_PAD), lambda i: (0, i, 0)),
            pl.BlockSpec((MLA_HEADS, ts, MLA_V_DIM), lambda i: (0, i, 0)),
        ],
        out_shape=[
            jax.ShapeDtypeStruct((MLA_HEADS, s, MLA_QK_PAD), BF16),
            jax.ShapeDtypeStruct((MLA_HEADS, s, MLA_QK_PAD), BF16),
            jax.ShapeDtypeStruct((MLA_HEADS, s, MLA_V_DIM), BF16),
        ],
        compiler_params=_cparams(("parallel",)),
        name="mla_prep",
    )(z, z, z, tab, q_norm.reshape(1, -1), kv_norm.reshape(1, -1), wq, wkv)


MLA_BQ = 512
MLA_BK = 512


def _mla_attn_kernel(q_ref, k_ref, v_ref, g_ref, o_ref):
    i = pl.program_id(1)
    bq, bk = MLA_BQ, MLA_BK
    q = q_ref[0]

    def block(kb, m, l, acc, masked):
        start = pl.multiple_of(kb * bk, bk)
        kblk = k_ref[0, pl.ds(start, bk), :]
        vblk = v_ref[0, pl.ds(start, bk), :]
        sc = _dot_nt(q, kblk)
        if masked:
            row = lax.broadcasted_iota(jnp.int32, (bq, bk), 0)
            col = lax.broadcasted_iota(jnp.int32, (bq, bk), 1)
            sc = jnp.where(col <= row, sc, -1e30)
        m_new = jnp.maximum(m, jnp.max(sc, axis=-1, keepdims=True))
        alpha = jnp.exp(m - m_new)
        p = jnp.exp(sc - m_new)
        l = l * alpha + jnp.sum(p, axis=-1, keepdims=True)
        acc = acc * alpha + _dot(p.astype(BF16), vblk)
        return m_new, l, acc

    def body(kb, st):
        return block(kb, *st, False)

    m0 = jnp.full((bq, 1), -1e30, F32)
    l0 = jnp.zeros((bq, 1), F32)
    a0 = jnp.zeros((bq, MLA_V_DIM), F32)
    m, l, acc = lax.fori_loop(0, i, body, (m0, l0, a0))
    m, l, acc = block(i, m, l, acc, True)
    y = acc / l
    o_ref[...] = _rms_rows(y, g_ref[0]).astype(o_ref.dtype)


def mla_attention(q, k, v, gains):
    s = q.shape[1]
    return pl.pallas_call(
        _mla_attn_kernel,
        grid=(MLA_HEADS, s // MLA_BQ),
        in_specs=[
            pl.BlockSpec((1, MLA_BQ, MLA_QK_PAD), lambda h, i: (h, i, 0)),
            pl.BlockSpec((1, s, MLA_QK_PAD), lambda h, i: (h, 0, 0)),
            pl.BlockSpec((1, s, MLA_V_DIM), lambda h, i: (h, 0, 0)),
            pl.BlockSpec((1, 1, MLA_V_DIM), lambda h, i: (h, 0, 0)),
        ],
        out_specs=pl.BlockSpec((MLA_BQ, MLA_V_DIM), lambda h, i: (i, h)),
        out_shape=jax.ShapeDtypeStruct((s, MLA_WIDTH), BF16),
        compiler_params=_cparams(("parallel", "arbitrary")),
        name="mla_attention",
    )(q, k, v, gains)


def _head_sum_matrix():
    r = lax.broadcasted_iota(jnp.int32, (128, 128), 0) // RWKV_HEAD_DIM
    c = lax.broadcasted_iota(jnp.int32, (128, 128), 1) // RWKV_HEAD_DIM
    return jnp.where(r == c, 1.0, 0.0).astype(BF16)


def _head_sums(x, ones_bd):
    parts = []
    for j in range(x.shape[1] // 128):
        parts.append(_exact_lhs_dot_rhs(x[:, j * 128:(j + 1) * 128], ones_bd))
    return jnp.concatenate(parts, axis=-1)


def _exact_lhs_dot_rhs(x, m_bf16):
    hi, mid, lo = _split3(x)
    return _dot(hi, m_bf16) + (_dot(mid, m_bf16) + _dot(lo, m_bf16))


def _rwkv_prep_kernel(r_ref, k_ref, v_ref, lora_ref, xg_ref, mu_ref, mul_ref, mug_ref,
                      w0_ref, w2_ref, a0_ref, a2_ref, g2_ref, kk_ref, ka_ref,
                      ro_ref, lw_ref, ko_ref, vo_ref, kko_ref, bo_ref, go_ref,
                      prev_ref, prevl_ref):
    ts = r_ref.shape[0]

    @pl.when(pl.program_id(0) == 0)
    def _():
        prev_ref[...] = jnp.zeros_like(prev_ref)
        prevl_ref[...] = jnp.zeros_like(prevl_ref)

    first_row = lax.broadcasted_iota(jnp.int32, (ts, 1), 0) == 0

    def shift_mix(x, prev_row, mu):
        x_prev = jnp.where(first_row, prev_row, pltpu.roll(x, 1, 0))
        return x + (x_prev - x) * mu

    r_in, k_in, v_in = r_ref[...], k_ref[...], v_ref[...]
    lora_in, xg_in = lora_ref[...], xg_ref[...]
    r = shift_mix(r_in, prev_ref[0:1, :], mu_ref[0:1, :])
    k = shift_mix(k_in, prev_ref[1:2, :], mu_ref[1:2, :])
    v = shift_mix(v_in, prev_ref[2:3, :], mu_ref[2:3, :])
    lora = shift_mix(lora_in, prevl_ref[0:1, :], mul_ref[...])
    xg = shift_mix(xg_in, prevl_ref[1:2, :], mug_ref[...])
    prev_ref[0:1, :] = r_in[ts - 1:ts, :]
    prev_ref[1:2, :] = k_in[ts - 1:ts, :]
    prev_ref[2:3, :] = v_in[ts - 1:ts, :]
    prevl_ref[0:1, :] = lora_in[ts - 1:ts, :]
    prevl_ref[1:2, :] = xg_in[ts - 1:ts, :]

    dw = w0_ref[...] + _dot(jnp.tanh(lora).astype(BF16), w2_ref[...])
    log_w = -jax.nn.softplus(-dw) - 0.5
    lw_ref[...] = -jnp.exp(log_w)
    a = jax.nn.sigmoid(a0_ref[...] + _dot(lora.astype(BF16), a2_ref[...]))
    go_ref[...] = _dot(jax.nn.sigmoid(xg).astype(BF16), g2_ref[...])

    kk = k * kk_ref[...]
    ss = _head_sums(kk * kk, _head_sum_matrix())
    kk = kk / jnp.maximum(jnp.sqrt(ss), 1e-12)
    ro_ref[...] = r
    vo_ref[...] = v
    ko_ref[...] = k * (1.0 + (a - 1.0) * ka_ref[...])
    kko_ref[...] = kk
    bo_ref[...] = kk * a


def rwkv_prep(z, p, *, ts):
    s = z.shape[0]
    w = RWKV_WIDTH
    row = lambda n: pl.BlockSpec((1, n), lambda i: (0, 0))
    big = pl.BlockSpec((ts, w), lambda i: (i, 0))
    out = jax.ShapeDtypeStruct((s, w), F32)
    return pl.pallas_call(
        _rwkv_prep_kernel,
        grid=(s // ts,),
        in_specs=[
            pl.BlockSpec((ts, w), lambda i: (i, COL_R // w)),
            pl.BlockSpec((ts, w), lambda i: (i, COL_K // w)),
            pl.BlockSpec((ts, w), lambda i: (i, COL_V // w)),
            pl.BlockSpec((ts, 128), lambda i: (i, COL_LORA // 128)),
            pl.BlockSpec((ts, 128), lambda i: (i, COL_XG // 128)),
            pl.BlockSpec((3, w), lambda i: (0, 0)),
            row(128), row(128),
            row(w), pl.BlockSpec((128, w), lambda i: (0, 0)),
            row(w), pl.BlockSpec((128, w), lambda i: (0, 0)),
            pl.BlockSpec((128, w), lambda i: (0, 0)),
            row(w), row(w),
        ],
        out_specs=[big] * 7,
        out_shape=[out] * 7,
        scratch_shapes=[pltpu.VMEM((8, w), F32), pltpu.VMEM((8, 128), F32)],
        compiler_params=_cparams(("arbitrary",)),
        name="rwkv_prep",
    )(z, z, z, z, z, p["mu_rkv"], p["mu_lora"], p["mu_g"], p["w0"], p["w2"], p["a0"], p["a2"],
      p["g2"], p["k_k"], p["k_a"])


def _rwkv_chunk_kernel(r_ref, lw_ref, k_ref, v_ref, kk_ref, b_ref, g_ref, rk_ref, lnw_ref, lnb_ref,
                       o_ref, st_ref, y_ref):
    c = RWKV_CHUNK
    n = RWKV_HEAD_DIM
    nh = r_ref.shape[1] // n

    @pl.when(pl.program_id(1) == 0)
    def _():
        st_ref[...] = jnp.zeros_like(st_ref)

    ri = lax.broadcasted_iota(jnp.int32, (c, c), 0)
    ci = lax.broadcasted_iota(jnp.int32, (c, c), 1)
    strict = ri > ci
    incl = ri >= ci
    eye = jnp.where(ri == ci, 1.0, 0.0).astype(F32)
    tril_ones = jnp.where(incl, 1.0, 0.0).astype(BF16)
    d16 = (ri // 16) == (ci // 16)
    d32 = (ri // 32) == (ci // 32)

    r, lw, k, v = r_ref[...], lw_ref[...], k_ref[...], v_ref[...]
    kk, b = kk_ref[...], b_ref[...]
    cum = _exact_lhs_dot(tril_ones, lw)
    p_incl = jnp.exp(cum)
    p_excl = jnp.exp(cum - lw)
    p_inv = jnp.exp(-cum)
    p_end = p_incl[c - 1:c, :]
    r_t = r * p_incl
    kk_t = kk * p_excl
    b_t = b * p_inv
    k_t = k * p_inv
    b_d = b_t * p_end
    k_d = k_t * p_end

    mm = functools.partial(_mm3, _dot)
    mm_nt = functools.partial(_mm3, _dot_nt)
    mm_tn = functools.partial(_mm3, _dot_tn)

    for h in range(nh):
        sl = slice(h * n, (h + 1) * n)
        rt_h, kkt_h, bt_h, kt_h, v_h = r_t[:, sl], kk_t[:, sl], b_t[:, sl], k_t[:, sl], v[:, sl]
        a_ub = jnp.where(strict, mm_nt(kkt_h, bt_h), 0.0)
        a_uk = jnp.where(strict, mm_nt(kkt_h, kt_h), 0.0)
        a_rb = jnp.where(incl, mm_nt(rt_h, bt_h), 0.0)
        a_rk = jnp.where(incl, mm_nt(rt_h, kt_h), 0.0)
        ld = jnp.where(d16, a_ub, 0.0)
        x = eye - ld
        pw = mm(ld, ld)
        x = x + mm(x, pw)
        pw = mm(pw, pw)
        x = x + mm(x, pw)
        pw = mm(pw, pw)
        x = x + mm(x, pw)
        off = jnp.where(jnp.logical_and(d32, jnp.logical_not(d16)), a_ub, 0.0)
        x = x - mm(mm(x, off), x)
        off = jnp.where(d32, 0.0, a_ub)
        x = x - mm(mm(x, off), x)

        st = st_ref[h]
        u = -mm(x, mm_nt(kkt_h, st) + mm(a_uk, v_h))
        y = mm_nt(rt_h, st) + mm(a_rb, u) + mm(a_rk, v_h)
        st_ref[h] = st * p_end[:, sl] + mm_tn(u, b_d[:, sl]) + mm_tn(v_h, k_d[:, sl])
        y_ref[:, sl] = y

    ones_bd = _head_sum_matrix()
    y = y_ref[...]
    mean = _head_sums(y, ones_bd) * (1.0 / n)
    yc = y - mean
    var = _head_sums(yc * yc, ones_bd) * (1.0 / n)
    yn = yc * lax.rsqrt(var + RWKV_GN_EPS) * lnw_ref[...] + lnb_ref[...]
    bonus = _head_sums(r * k * rk_ref[...], ones_bd)
    o_ref[...] = ((yn + bonus * v) * g_ref[...]).astype(o_ref.dtype)


def rwkv_chunk(r, lw, k, v, kk, b, g, p, *, heads_per_step):
    s = r.shape[0]
    c = RWKV_CHUNK
    wb = heads_per_step * RWKV_HEAD_DIM
    big = pl.BlockSpec((c, wb), lambda hg, t: (t, hg))
    row = pl.BlockSpec((1, wb), lambda hg, t: (0, hg))
    return pl.pallas_call(
        _rwkv_chunk_kernel,
        grid=(RWKV_WIDTH // wb, s // c),
        in_specs=[big] * 7 + [row] * 3,
        out_specs=big,
        out_shape=jax.ShapeDtypeStruct((s, RWKV_WIDTH), BF16),
        scratch_shapes=[
            pltpu.VMEM((heads_per_step, RWKV_HEAD_DIM, RWKV_HEAD_DIM), F32),
            pltpu.VMEM((c, wb), F32),
        ],
        compiler_params=_cparams(("parallel", "arbitrary")),
        name="rwkv_chunk",
    )(r, lw, k, v, kk, b, g, p["r_k"], p["ln_w"], p["ln_b"])


def _swap_halves(w):
    half = w.shape[-1] // 2
    return jnp.concatenate([w[..., half:], w[..., :half]], axis=-1)


def _prep_w_in(w_in):
    sizes = [SB_WIDTH] * 3 + [MLA_Q_LORA, MLA_KV_LORA, MLA_ROPE_DIM] + [RWKV_WIDTH] * 3 + [
        RWKV_DECAY_LORA, RWKV_A_LORA, RWKV_GATE_LORA]
    idx = [int(i) for i in np.cumsum(sizes)[:-1]]
    sbq, sbk, sbv, cq, ckv, krope, r, k, v, xw, xa, xg = jnp.split(w_in, idx, axis=-1)
    cols = [r, k, v, sbq, sbk, sbv, cq, ckv, xw, xa, xg, krope, _swap_halves(krope)]
    return jnp.concatenate(cols, axis=-1).astype(BF16)


def _prep_mla_w(w_uq, w_ukv):
    q = w_uq.reshape(MLA_Q_LORA, MLA_HEADS, MLA_NOPE_DIM + MLA_ROPE_DIM)
    nope, pe = q[..., :MLA_NOPE_DIM], q[..., MLA_NOPE_DIM:]
    zpad = jnp.zeros_like(pe)
    wq = jnp.concatenate([nope, pe, zpad, _swap_halves(pe), zpad], axis=-1)
    return wq.reshape(MLA_Q_LORA, MLA_HEADS * 384).astype(BF16), w_ukv.astype(BF16)


def _rope_table(positions):
    half = MLA_ROPE_DIM // 2
    inv_freq = ROPE_THETA ** (-jnp.arange(half, dtype=F32) / half)
    ang = positions.astype(F32)[:, None] * inv_freq
    cos, sin = jnp.cos(ang), jnp.sin(ang)
    z = jnp.zeros((positions.shape[0], MLA_ROPE_DIM), F32)
    return jnp.concatenate([cos, cos, z, -sin, sin, z], axis=-1)


def _pick_tile(n, candidates):
    for c in candidates:
        if n % c == 0:
            return c
    raise ValueError(f"no tile for {n}")


def kernel(x, positions, ffn1_norm, ffn1_gate, ffn1_up, ffn1_down, mix_norm, w_in, mla_q_norm,
           mla_w_uq, mla_kv_norm, mla_w_ukv, rwkv_mu, rwkv_w0, rwkv_w2, rwkv_a0, rwkv_a2, rwkv_g2,
           rwkv_k_k, rwkv_k_a, rwkv_r_k, rwkv_ln_w, rwkv_ln_b, sb_out_norm, mla_out_norm, w_out,
           ffn2_norm, ffn2_gate, ffn2_up, ffn2_down, final_norm):
    bsz, s, d = x.shape
    assert bsz == 1 and d == SB_WIDTH + MLA_WIDTH + RWKV_WIDTH
    depth = w_in.shape[0]
    d_ff = ffn1_gate.shape[-1]
    tm = _pick_tile(s, (512, 256, 128))
    tf = _pick_tile(d_ff, (512, 256, 128))
    x = x[0]
    tab = _rope_table(positions[0])
    w = RWKV_WIDTH
    zl = jnp.zeros((RWKV_DECAY_LORA, w), F32)

    def ffn(x, g, wg, wu, wd):
        act = ffn_up(x, g, wg.astype(BF16), wu.astype(BF16), tm=tm, tn=tf)
        return ffn_down(x, act, wd.astype(BF16), tm=tm, tn=512)

    for l in range(depth):
        x = ffn(x, ffn1_norm[l], ffn1_gate[l], ffn1_up[l], ffn1_down[l])

        z = norm_proj(x, mix_norm[l], _prep_w_in(w_in[l]), tm=tm, tn=640)

        y_sb = sb_attention(z, sb_out_norm[l].reshape(SB_HEADS, 1, SB_HEAD_DIM))

        wq, wkv = _prep_mla_w(mla_w_uq[l], mla_w_ukv[l])
        q, k, v = mla_prep(z, tab, mla_q_norm[l], mla_kv_norm[l], wq, wkv, ts=_pick_tile(s, (256, 128)))
        y_mla = mla_attention(q, k, v, mla_out_norm[l].reshape(MLA_HEADS, 1, MLA_V_DIM))

        mu = rwkv_mu[l]
        p = {
            "mu_rkv": mu[:3 * w].reshape(3, w),
            "mu_lora": mu[3 * w:3 * w + 128].reshape(1, 128),
            "mu_g": mu[3 * w + 128:].reshape(1, 128),
            "w0": rwkv_w0[l].reshape(1, w),
            "w2": jnp.concatenate([rwkv_w2[l], zl], axis=0).astype(BF16),
            "a0": rwkv_a0[l].reshape(1, w),
            "a2": jnp.concatenate([zl, rwkv_a2[l]], axis=0).astype(BF16),
            "g2": rwkv_g2[l].astype(BF16),
            "k_k": rwkv_k_k[l].reshape(1, w),
            "k_a": rwkv_k_a[l].reshape(1, w),
            "r_k": rwkv_r_k[l].reshape(1, w),
            "ln_w": rwkv_ln_w[l].reshape(1, w),
            "ln_b": rwkv_ln_b[l].reshape(1, w),
        }
        r_, lw_, k_, v_, kk_, b_, g_ = rwkv_prep(z, p, ts=_pick_tile(s, (256, 128)))
        y_rwkv = rwkv_chunk(r_, lw_, k_, v_, kk_, b_, g_, p, heads_per_step=4)

        x = out_proj(x, y_sb, y_mla, y_rwkv, w_out[l].astype(BF16), tm=tm, tn=512)

        x = ffn(x, ffn2_norm[l], ffn2_gate[l], ffn2_up[l], ffn2_down[l])

    return final_rmsnorm(x, final_norm, tm=_pick_tile(s, (256, 128)))[None]
```

```python
import functools

import jax
import jax.numpy as jnp
import numpy as np
from jax import lax
from jax.experimental import pallas as pl
from jax.experimental.pallas import tpu as pltpu

F32 = jnp.float32
BF16 = jnp.bfloat16

NORM_EPS = 1e-6
SB_HEADS = 4
SB_HEAD_DIM = 128
SB_WIDTH = SB_HEADS * SB_HEAD_DIM
MLA_HEADS = 4
MLA_NOPE_DIM = 128
MLA_ROPE_DIM = 64
MLA_V_DIM = 128
MLA_Q_LORA = 512
MLA_KV_LORA = 256
MLA_WIDTH = MLA_HEADS * MLA_V_DIM
MLA_QK_PAD = 256
ROPE_THETA = 10000.0
RWKV_HEADS = 16
RWKV_HEAD_DIM = 64
RWKV_WIDTH = RWKV_HEADS * RWKV_HEAD_DIM
RWKV_DECAY_LORA = 64
RWKV_A_LORA = 64
RWKV_GATE_LORA = 128
RWKV_GN_EPS = 64e-5
RWKV_CHUNK = 64
RWKV_PASSES_A = 1
RWKV_PASSES_T = 1
RWKV_PASSES_S = 1

COL_R, COL_K, COL_V = 0, 1024, 2048
COL_SBQ, COL_SBK, COL_SBV = 3072, 3584, 4096
COL_CQ, COL_CKV = 4608, 5120
COL_LORA, COL_XG, COL_ROPE = 5376, 5504, 5632
Z_COLS = 5760

VMEM_LIMIT = 48 * 1024 * 1024


def _cparams(sem):
    return pltpu.CompilerParams(dimension_semantics=sem, vmem_limit_bytes=VMEM_LIMIT)


def _dot(a, b):
    return jnp.dot(a, b, preferred_element_type=F32)


def _dot_nt(a, b):
    return lax.dot_general(a, b, (((1,), (1,)), ((), ())), preferred_element_type=F32)


def _dot_tn(a, b):
    return lax.dot_general(a, b, (((0,), (0,)), ((), ())), preferred_element_type=F32)


def _split3(x):
    hi = x.astype(BF16)
    r1 = x - hi.astype(F32)
    mid = r1.astype(BF16)
    lo = (r1 - mid.astype(F32)).astype(BF16)
    return hi, mid, lo


def _split2(x):
    hi = x.astype(BF16)
    lo = (x - hi.astype(F32)).astype(BF16)
    return hi, lo


def _mm3(dot, a, b):
    ah, al = _split2(a)
    bh, bl = _split2(b)
    return dot(ah, bh) + (dot(ah, bl) + dot(al, bh))


def _mm(dot, a, b, passes):
    if passes == 1:
        return dot(a.astype(BF16), b.astype(BF16))
    return _mm3(dot, a, b)


def _exact_lhs_dot(m_bf16, x):
    hi, mid, lo = _split3(x)
    return _dot(m_bf16, hi) + (_dot(m_bf16, mid) + _dot(m_bf16, lo))


def _rms_rows(x, g):
    ms = jnp.mean(x * x, axis=-1, keepdims=True)
    return x * lax.rsqrt(ms + NORM_EPS) * g


def _norm_proj_kernel(x_ref, g_ref, w_ref, o_ref, h_ref):
    @pl.when(pl.program_id(1) == 0)
    def _():
        h_ref[...] = _rms_rows(x_ref[...], g_ref[...]).astype(BF16)

    o_ref[...] = _dot(h_ref[...], w_ref[...]).astype(o_ref.dtype)


def norm_proj(x, g, w, *, tm, tn, out_dtype=F32):
    s, d = x.shape
    n = w.shape[1]
    return pl.pallas_call(
        _norm_proj_kernel,
        grid=(s // tm, n // tn),
        in_specs=[
            pl.BlockSpec((tm, d), lambda i, j: (i, 0)),
            pl.BlockSpec((1, d), lambda i, j: (0, 0)),
            pl.BlockSpec((d, tn), lambda i, j: (0, j)),
        ],
        out_specs=pl.BlockSpec((tm, tn), lambda i, j: (i, j)),
        out_shape=jax.ShapeDtypeStruct((s, n), out_dtype),
        scratch_shapes=[pltpu.VMEM((tm, d), BF16)],
        compiler_params=_cparams(("parallel", "arbitrary")),
        name="norm_proj",
    )(x, g.reshape(1, d), w)


def _ffn_up_kernel(x_ref, g_ref, wg_ref, wu_ref, o_ref, h_ref):
    @pl.when(pl.program_id(1) == 0)
    def _():
        h_ref[...] = _rms_rows(x_ref[...], g_ref[...]).astype(BF16)

    h = h_ref[...]
    a = _dot(h, wg_ref[...])
    u = _dot(h, wu_ref[...])
    o_ref[...] = (a * jax.nn.sigmoid(a) * u).astype(o_ref.dtype)


def ffn_up(x, g, wg, wu, *, tm, tn):
    s, d = x.shape
    f = wg.shape[1]
    return pl.pallas_call(
        _ffn_up_kernel,
        grid=(s // tm, f // tn),
        in_specs=[
            pl.BlockSpec((tm, d), lambda i, j: (i, 0)),
            pl.BlockSpec((1, d), lambda i, j: (0, 0)),
            pl.BlockSpec((d, tn), lambda i, j: (0, j)),
            pl.BlockSpec((d, tn), lambda i, j: (0, j)),
        ],
        out_specs=pl.BlockSpec((tm, tn), lambda i, j: (i, j)),
        out_shape=jax.ShapeDtypeStruct((s, f), BF16),
        scratch_shapes=[pltpu.VMEM((tm, d), BF16)],
        compiler_params=_cparams(("parallel", "arbitrary")),
        name="ffn_up",
    )(x, g.reshape(1, d), wg, wu)


def _ffn_down_kernel(x_ref, a_ref, w_ref, o_ref):
    o_ref[...] = x_ref[...] + 0.5 * _dot(a_ref[...], w_ref[...])


def ffn_down(x, act, wd, *, tm, tn):
    s, d = x.shape
    f = act.shape[1]
    return pl.pallas_call(
        _ffn_down_kernel,
        grid=(s // tm, d // tn),
        in_specs=[
            pl.BlockSpec((tm, tn), lambda i, j: (i, j)),
            pl.BlockSpec((tm, f), lambda i, j: (i, 0)),
            pl.BlockSpec((f, tn), lambda i, j: (0, j)),
        ],
        out_specs=pl.BlockSpec((tm, tn), lambda i, j: (i, j)),
        out_shape=jax.ShapeDtypeStruct((s, d), F32),
        compiler_params=_cparams(("parallel", "arbitrary")),
        name="ffn_down",
    )(x, act, wd)


def _out_proj_kernel(x_ref, a_ref, b_ref, c_ref, wa_ref, wb_ref, wc_ref, o_ref):
    acc = _dot(a_ref[...], wa_ref[...])
    acc += _dot(b_ref[...], wb_ref[...])
    acc += _dot(c_ref[...], wc_ref[...])
    o_ref[...] = x_ref[...] + acc


def out_proj(x, y_sb, y_mla, y_rwkv, w_out, *, tm, tn):
    s, d = x.shape
    wa, wb, wc = w_out[:SB_WIDTH], w_out[SB_WIDTH:SB_WIDTH + MLA_WIDTH], w_out[SB_WIDTH + MLA_WIDTH:]
    return pl.pallas_call(
        _out_proj_kernel,
        grid=(s // tm, d // tn),
        in_specs=[
            pl.BlockSpec((tm, tn), lambda i, j: (i, j)),
            pl.BlockSpec((tm, SB_WIDTH), lambda i, j: (i, 0)),
            pl.BlockSpec((tm, MLA_WIDTH), lambda i, j: (i, 0)),
            pl.BlockSpec((tm, RWKV_WIDTH), lambda i, j: (i, 0)),
            pl.BlockSpec((SB_WIDTH, tn), lambda i, j: (0, j)),
            pl.BlockSpec((MLA_WIDTH, tn), lambda i, j: (0, j)),
            pl.BlockSpec((RWKV_WIDTH, tn), lambda i, j: (0, j)),
        ],
        out_specs=pl.BlockSpec((tm, tn), lambda i, j: (i, j)),
        out_shape=jax.ShapeDtypeStruct((s, d), F32),
        compiler_params=_cparams(("parallel", "arbitrary")),
        name="out_proj",
    )(x, y_sb, y_mla, y_rwkv, wa, wb, wc)


def _final_norm_kernel(x_ref, g_ref, o_ref):
    o_ref[...] = _rms_rows(x_ref[...], g_ref[...])


def final_rmsnorm(x, g, *, tm):
    s, d = x.shape
    return pl.pallas_call(
        _final_norm_kernel,
        grid=(s // tm,),
        in_specs=[pl.BlockSpec((tm, d), lambda i: (i, 0)), pl.BlockSpec((1, d), lambda i: (0, 0))],
        out_specs=pl.BlockSpec((tm, d), lambda i: (i, 0)),
        out_shape=jax.ShapeDtypeStruct((s, d), F32),
        compiler_params=_cparams(("parallel",)),
        name="final_norm",
    )(x, g.reshape(1, d))


SB_BQ = 256
SB_BK = 128
SB_DEAD_LOG = -104.0


def _sb_attn_kernel(q_ref, k_ref, v_ref, g_ref, o_ref):
    i = pl.program_id(1)
    bq, bk = SB_BQ, SB_BK
    q = (q_ref[...] * (SB_HEAD_DIM ** -0.5)).astype(BF16)
    row = lax.broadcasted_iota(jnp.int32, (bq, bk), 0)
    col = lax.broadcasted_iota(jnp.int32, (bq, bk), 1)
    kr = lax.broadcasted_iota(jnp.int32, (bk, bk), 0)
    kc = lax.broadcasted_iota(jnp.int32, (bk, bk), 1)
    upper = jnp.where(kr > kc, 1.0, 0.0).astype(BF16)

    def block(kb, carry, acc, masked):
        start = pl.multiple_of(kb * bk, bk)
        kblk = k_ref[pl.ds(start, bk), :].astype(BF16)
        vblk = v_ref[pl.ds(start, bk), :].astype(BF16)
        z = _dot_nt(q, kblk)
        log_beta = jnp.minimum(z, 0.0) - jnp.log1p(jnp.exp(-jnp.abs(z)))
        log_keep = log_beta - z
        if masked:
            before = (start + col) < (i * bq + row)
            log_keep = jnp.where(before, log_keep, 0.0)
        hi, lo = _split2(log_keep)
        within = _dot(hi, upper) + _dot(lo, upper)
        a = jnp.exp(log_beta + (carry + within))
        if masked:
            a = jnp.where(before, a, 0.0)
        acc = acc + _dot(a.astype(BF16), vblk)
        carry = carry + jnp.sum(log_keep, axis=-1, keepdims=True)
        return carry, acc

    carry = jnp.zeros((bq, 1), F32)
    acc = jnp.zeros((bq, SB_HEAD_DIM), F32)
    nd = bq // bk
    first = i * nd
    for d in range(nd - 1, -1, -1):
        carry, acc = block(first + d, carry, acc, True)

    def cond(st):
        kb, alive, _, _ = st
        return jnp.logical_and(kb >= 0, alive > SB_DEAD_LOG)

    def body(st):
        kb, _, carry, acc = st
        carry, acc = block(kb, carry, acc, False)
        return kb - 1, jnp.max(carry), carry, acc

    _, _, _, acc = lax.while_loop(cond, body, (first - 1, jnp.max(carry), carry, acc))
    o_ref[...] = _rms_rows(acc, g_ref[0]).astype(o_ref.dtype)


def sb_attention(z, gains):
    s = z.shape[0]
    qb, kb_, vb = COL_SBQ // SB_HEAD_DIM, COL_SBK // SB_HEAD_DIM, COL_SBV // SB_HEAD_DIM
    return pl.pallas_call(
        _sb_attn_kernel,
        grid=(SB_HEADS, s // SB_BQ),
        in_specs=[
            pl.BlockSpec((SB_BQ, SB_HEAD_DIM), lambda h, i: (i, qb + h)),
            pl.BlockSpec((s, SB_HEAD_DIM), lambda h, i: (0, kb_ + h)),
            pl.BlockSpec((s, SB_HEAD_DIM), lambda h, i: (0, vb + h)),
            pl.BlockSpec((1, 1, SB_HEAD_DIM), lambda h, i: (h, 0, 0)),
        ],
        out_specs=pl.BlockSpec((SB_BQ, SB_HEAD_DIM), lambda h, i: (i, h)),
        out_shape=jax.ShapeDtypeStruct((s, SB_WIDTH), BF16),
        compiler_params=_cparams(("parallel", "arbitrary")),
        name="sb_attention",
    )(z, z, z, gains)


def _mla_prep_kernel(cq_ref, ckv_ref, rope_ref, tab_ref, qn_ref, kvn_ref, wq_ref, wkv_ref,
                     q_ref, k_ref, v_ref):
    scale = (MLA_NOPE_DIM + MLA_ROPE_DIM) ** -0.5
    hq = _rms_rows(cq_ref[...], qn_ref[...]).astype(BF16)
    hkv = _rms_rows(ckv_ref[...], kvn_ref[...]).astype(BF16)
    cos2 = tab_ref[:, :128]
    sin2 = tab_ref[:, 128:]
    kr = rope_ref[...]
    k_pe = kr * cos2 + pltpu.roll(kr, 64, 1) * sin2
    k_pe = k_pe.astype(BF16)
    qall = _dot(hq, wq_ref[...])
    kvall = _dot(hkv, wkv_ref[...])
    for h in range(MLA_HEADS):
        qh = qall[:, h * 384:(h + 1) * 384]
        q_ref[h, :, :128] = (qh[:, :128] * scale).astype(BF16)
        q_ref[h, :, 128:] = ((qh[:, 128:256] * cos2 + qh[:, 256:384] * sin2) * scale).astype(BF16)
        k_ref[h, :, :128] = kvall[:, h * 256:h * 256 + 128].astype(BF16)
        k_ref[h, :, 128:] = k_pe
        v_ref[h] = kvall[:, h * 256 + 128:(h + 1) * 256].astype(BF16)


def mla_prep(z, tab, q_norm, kv_norm, wq, wkv, *, ts):
    s = z.shape[0]
    return pl.pallas_call(
        _mla_prep_kernel,
        grid=(s // ts,),
        in_specs=[
            pl.BlockSpec((ts, MLA_Q_LORA), lambda i: (i, COL_CQ // MLA_Q_LORA)),
            pl.BlockSpec((ts, MLA_KV_LORA), lambda i: (i, COL_CKV // MLA_KV_LORA)),
            pl.BlockSpec((ts, 128), lambda i: (i, COL_ROPE // 128)),
            pl.BlockSpec((ts, 256), lambda i: (i, 0)),
            pl.BlockSpec((1, MLA_Q_LORA), lambda i: (0, 0)),
            pl.BlockSpec((1, MLA_KV_LORA), lambda i: (0, 0)),
            pl.BlockSpec(wq.shape, lambda i: (0, 0)),
            pl.BlockSpec(wkv.shape, lambda i: (0, 0)),
        ],
        out_specs=[
            pl.BlockSpec((MLA_HEADS, ts, MLA_QK_PAD), lambda i: (0, i, 0)),
            pl.BlockSpec((MLA_HEADS, ts, MLA_QK_PAD), lambda i: (0, i, 0)),
            pl.BlockSpec((MLA_HEADS, ts, MLA_V_DIM), lambda i: (0, i, 0)),
        ],
        out_shape=[
            jax.ShapeDtypeStruct((MLA_HEADS, s, MLA_QK_PAD), BF16),
            jax.ShapeDtypeStruct((MLA_HEADS, s, MLA_QK_PAD), BF16),
            jax.ShapeDtypeStruct((MLA_HEADS, s, MLA_V_DIM), BF16),
        ],
        compiler_params=_cparams(("parallel",)),
        name="mla_prep",
    )(z, z, z, tab, q_norm.reshape(1, -1), kv_norm.reshape(1, -1), wq, wkv)


MLA_BQ = 512
MLA_BK = 512


def _mla_attn_kernel(q_ref, k_ref, v_ref, g_ref, o_ref):
    i = pl.program_id(1)
    bq, bk = MLA_BQ, MLA_BK
    q = q_ref[0]

    def block(kb, m, l, acc, masked):
        start = pl.multiple_of(kb * bk, bk)
        kblk = k_ref[0, pl.ds(start, bk), :]
        vblk = v_ref[0, pl.ds(start, bk), :]
        sc = _dot_nt(q, kblk)
        if masked:
            row = lax.broadcasted_iota(jnp.int32, (bq, bk), 0)
            col = lax.broadcasted_iota(jnp.int32, (bq, bk), 1)
            sc = jnp.where(col <= row, sc, -1e30)
        m_new = jnp.maximum(m, jnp.max(sc, axis=-1, keepdims=True))
        alpha = jnp.exp(m - m_new)
        p = jnp.exp(sc - m_new)
        l = l * alpha + jnp.sum(p, axis=-1, keepdims=True)
        acc = acc * alpha + _dot(p.astype(BF16), vblk)
        return m_new, l, acc

    def body(kb, st):
        return block(kb, *st, False)

    m0 = jnp.full((bq, 1), -1e30, F32)
    l0 = jnp.zeros((bq, 1), F32)
    a0 = jnp.zeros((bq, MLA_V_DIM), F32)
    m, l, acc = lax.fori_loop(0, i, body, (m0, l0, a0))
    m, l, acc = block(i, m, l, acc, True)
    y = acc / l
    o_ref[...] = _rms_rows(y, g_ref[0]).astype(o_ref.dtype)


def mla_attention(q, k, v, gains):
    s = q.shape[1]
    return pl.pallas_call(
        _mla_attn_kernel,
        grid=(MLA_HEADS, s // MLA_BQ),
        in_specs=[
            pl.BlockSpec((1, MLA_BQ, MLA_QK_PAD), lambda h, i: (h, i, 0)),
            pl.BlockSpec((1, s, MLA_QK_PAD), lambda h, i: (h, 0, 0)),
            pl.BlockSpec((1, s, MLA_V_DIM), lambda h, i: (h, 0, 0)),
            pl.BlockSpec((1, 1, MLA_V_DIM), lambda h, i: (h, 0, 0)),
        ],
        out_specs=pl.BlockSpec((MLA_BQ, MLA_V_DIM), lambda h, i: (i, h)),
        out_shape=jax.ShapeDtypeStruct((s, MLA_WIDTH), BF16),
        compiler_params=_cparams(("parallel", "arbitrary")),
        name="mla_attention",
    )(q, k, v, gains)


def _head_sum_matrix():
    r = lax.broadcasted_iota(jnp.int32, (128, 128), 0) // RWKV_HEAD_DIM
    c = lax.broadcasted_iota(jnp.int32, (128, 128), 1) // RWKV_HEAD_DIM
    return jnp.where(r == c, 1.0, 0.0).astype(BF16)


def _head_sums(x, ones_bd):
    parts = []
    for j in range(x.shape[1] // 128):
        parts.append(_exact_lhs_dot_rhs(x[:, j * 128:(j + 1) * 128], ones_bd))
    return jnp.concatenate(parts, axis=-1)


def _exact_lhs_dot_rhs(x, m_bf16):
    hi, mid, lo = _split3(x)
    return _dot(hi, m_bf16) + (_dot(mid, m_bf16) + _dot(lo, m_bf16))


def _rwkv_prep_kernel(r_ref, k_ref, v_ref, lora_ref, xg_ref, mu_ref, mul_ref, mug_ref,
                      w0_ref, w2_ref, a0_ref, a2_ref, g2_ref, kk_ref, ka_ref,
                      ro_ref, lw_ref, ko_ref, vo_ref, kko_ref, bo_ref, go_ref,
                      prev_ref, prevl_ref):
    ts = r_ref.shape[0]

    @pl.when(pl.program_id(0) == 0)
    def _():
        prev_ref[...] = jnp.zeros_like(prev_ref)
        prevl_ref[...] = jnp.zeros_like(prevl_ref)

    first_row = lax.broadcasted_iota(jnp.int32, (ts, 1), 0) == 0

    def shift_mix(x, prev_row, mu):
        x_prev = jnp.where(first_row, prev_row, pltpu.roll(x, 1, 0))
        return x + (x_prev - x) * mu

    r_in, k_in, v_in = r_ref[...], k_ref[...], v_ref[...]
    lora_in, xg_in = lora_ref[...], xg_ref[...]
    r = shift_mix(r_in, prev_ref[0:1, :], mu_ref[0:1, :])
    k = shift_mix(k_in, prev_ref[1:2, :], mu_ref[1:2, :])
    v = shift_mix(v_in, prev_ref[2:3, :], mu_ref[2:3, :])
    lora = shift_mix(lora_in, prevl_ref[0:1, :], mul_ref[...])
    xg = shift_mix(xg_in, prevl_ref[1:2, :], mug_ref[...])
    prev_ref[0:1, :] = r_in[ts - 1:ts, :]
    prev_ref[1:2, :] = k_in[ts - 1:ts, :]
    prev_ref[2:3, :] = v_in[ts - 1:ts, :]
    prevl_ref[0:1, :] = lora_in[ts - 1:ts, :]
    prevl_ref[1:2, :] = xg_in[ts - 1:ts, :]

    dw = w0_ref[...] + _dot(jnp.tanh(lora).astype(BF16), w2_ref[...])
    log_w = -jax.nn.softplus(-dw) - 0.5
    lw_ref[...] = -jnp.exp(log_w)
    a = jax.nn.sigmoid(a0_ref[...] + _dot(lora.astype(BF16), a2_ref[...]))
    go_ref[...] = _dot(jax.nn.sigmoid(xg).astype(BF16), g2_ref[...])

    kk = k * kk_ref[...]
    ss = _head_sums(kk * kk, _head_sum_matrix())
    kk = kk / jnp.maximum(jnp.sqrt(ss), 1e-12)
    ro_ref[...] = r
    vo_ref[...] = v
    ko_ref[...] = k * (1.0 + (a - 1.0) * ka_ref[...])
    kko_ref[...] = kk
    bo_ref[...] = kk * a


def rwkv_prep(z, p, *, ts):
    s = z.shape[0]
    w = RWKV_WIDTH
    row = lambda n: pl.BlockSpec((1, n), lambda i: (0, 0))
    big = pl.BlockSpec((ts, w), lambda i: (i, 0))
    out = jax.ShapeDtypeStruct((s, w), F32)
    return pl.pallas_call(
        _rwkv_prep_kernel,
        grid=(s // ts,),
        in_specs=[
            pl.BlockSpec((ts, w), lambda i: (i, COL_R // w)),
            pl.BlockSpec((ts, w), lambda i: (i, COL_K // w)),
            pl.BlockSpec((ts, w), lambda i: (i, COL_V // w)),
            pl.BlockSpec((ts, 128), lambda i: (i, COL_LORA // 128)),
            pl.BlockSpec((ts, 128), lambda i: (i, COL_XG // 128)),
            pl.BlockSpec((3, w), lambda i: (0, 0)),
            row(128), row(128),
            row(w), pl.BlockSpec((128, w), lambda i: (0, 0)),
            row(w), pl.BlockSpec((128, w), lambda i: (0, 0)),
            pl.BlockSpec((128, w), lambda i: (0, 0)),
            row(w), row(w),
        ],
        out_specs=[big] * 7,
        out_shape=[out] * 7,
        scratch_shapes=[pltpu.VMEM((8, w), F32), pltpu.VMEM((8, 128), F32)],
        compiler_params=_cparams(("arbitrary",)),
        name="rwkv_prep",
    )(z, z, z, z, z, p["mu_rkv"], p["mu_lora"], p["mu_g"], p["w0"], p["w2"], p["a0"], p["a2"],
      p["g2"], p["k_k"], p["k_a"])


def _rwkv_chunk_kernel(r_ref, lw_ref, k_ref, v_ref, kk_ref, b_ref, g_ref, rk_ref, lnw_ref, lnb_ref,
                       o_ref, st_ref, y_ref):
    c = RWKV_CHUNK
    n = RWKV_HEAD_DIM
    nh = r_ref.shape[1] // n

    @pl.when(pl.program_id(1) == 0)
    def _():
        st_ref[...] = jnp.zeros_like(st_ref)

    ri = lax.broadcasted_iota(jnp.int32, (c, c), 0)
    ci = lax.broadcasted_iota(jnp.int32, (c, c), 1)
    strict = ri > ci
    incl = ri >= ci
    eye = jnp.where(ri == ci, 1.0, 0.0).astype(F32)
    tril_ones = jnp.where(incl, 1.0, 0.0).astype(BF16)
    d16 = (ri // 16) == (ci // 16)
    d32 = (ri // 32) == (ci // 32)

    r, lw, k, v = r_ref[...], lw_ref[...], k_ref[...], v_ref[...]
    kk, b = kk_ref[...], b_ref[...]
    cum = _exact_lhs_dot(tril_ones, lw)
    p_incl = jnp.exp(cum)
    p_excl = jnp.exp(cum - lw)
    p_inv = jnp.exp(-cum)
    p_end = p_incl[c - 1:c, :]
    r_t = r * p_incl
    kk_t = kk * p_excl
    b_t = b * p_inv
    k_t = k * p_inv
    b_d = b_t * p_end
    k_d = k_t * p_end

    lhs_all = jnp.concatenate([kk_t, r_t], axis=0)
    rhs_all = jnp.concatenate([b_t, k_t], axis=0)
    dec_all = jnp.concatenate([b_d, k_d], axis=0)
    ri2 = lax.broadcasted_iota(jnp.int32, (2 * c, 2 * c), 0)
    ci2 = lax.broadcasted_iota(jnp.int32, (2 * c, 2 * c), 1) % c
    causal2 = ci2 <= jnp.where(ri2 < c, ri2 - 1, ri2 - c)
    blk_lo = jnp.logical_and(d32, jnp.logical_not(d16))

    heads = range(nh)
    sls = [slice(h * n, (h + 1) * n) for h in heads]
    mm_s = functools.partial(_mm, passes=RWKV_PASSES_S)
    mm_t = functools.partial(_mm, _dot, passes=RWKV_PASSES_T)
    a = [jnp.where(causal2, _mm(_dot_nt, lhs_all[:, sl], rhs_all[:, sl], RWKV_PASSES_A), 0.0)
         for sl in sls]
    st = [st_ref[h] for h in heads]
    part = [mm_s(_dot_nt, lhs_all[:, sls[h]], st[h]) + mm_s(_dot, a[h][:, c:], v[:, sls[h]])
            for h in heads]
    a_ub = [a[h][:c, :c] for h in heads]
    ld = [jnp.where(d16, a_ub[h], 0.0) for h in heads]
    x = [eye - ld[h] for h in heads]
    pw = [mm_t(ld[h], ld[h]) for h in heads]
    x = [x[h] + mm_t(x[h], pw[h]) for h in heads]
    pw = [mm_t(pw[h], pw[h]) for h in heads]
    x = [x[h] + mm_t(x[h], pw[h]) for h in heads]
    pw = [mm_t(pw[h], pw[h]) for h in heads]
    x = [x[h] + mm_t(x[h], pw[h]) for h in heads]
    t = [mm_t(x[h], jnp.where(blk_lo, a_ub[h], 0.0)) for h in heads]
    x = [x[h] - mm_t(t[h], x[h]) for h in heads]
    t = [mm_t(x[h], jnp.where(d32, 0.0, a_ub[h])) for h in heads]
    x = [x[h] - mm_t(t[h], x[h]) for h in heads]

    u = [-mm_s(_dot, x[h], part[h][:c]) for h in heads]
    for h in heads:
        y_ref[:, sls[h]] = part[h][c:] + mm_s(_dot, a[h][c:, :c], u[h])
    for h in heads:
        uv = jnp.concatenate([u[h], v[:, sls[h]]], axis=0)
        st_ref[h] = st[h] * p_end[:, sls[h]] + mm_s(_dot_tn, uv, dec_all[:, sls[h]])

    ones_bd = _head_sum_matrix()
    y = y_ref[...]
    mean = _head_sums(y, ones_bd) * (1.0 / n)
    yc = y - mean
    var = _head_sums(yc * yc, ones_bd) * (1.0 / n)
    yn = yc * lax.rsqrt(var + RWKV_GN_EPS) * lnw_ref[...] + lnb_ref[...]
    bonus = _head_sums(r * k * rk_ref[...], ones_bd)
    o_ref[...] = ((yn + bonus * v) * g_ref[...]).astype(o_ref.dtype)


def rwkv_chunk(r, lw, k, v, kk, b, g, p, *, heads_per_step):
    s = r.shape[0]
    c = RWKV_CHUNK
    wb = heads_per_step * RWKV_HEAD_DIM
    big = pl.BlockSpec((c, wb), lambda hg, t: (t, hg))
    row = pl.BlockSpec((1, wb), lambda hg, t: (0, hg))
    return pl.pallas_call(
        _rwkv_chunk_kernel,
        grid=(RWKV_WIDTH // wb, s // c),
        in_specs=[big] * 7 + [row] * 3,
        out_specs=big,
        out_shape=jax.ShapeDtypeStruct((s, RWKV_WIDTH), BF16),
        scratch_shapes=[
            pltpu.VMEM((heads_per_step, RWKV_HEAD_DIM, RWKV_HEAD_DIM), F32),
            pltpu.VMEM((c, wb), F32),
        ],
        compiler_params=_cparams(("parallel", "arbitrary")),
        name="rwkv_chunk",
    )(r, lw, k, v, kk, b, g, p["r_k"], p["ln_w"], p["ln_b"])


def _swap_halves(w):
    half = w.shape[-1] // 2
    return jnp.concatenate([w[..., half:], w[..., :half]], axis=-1)


def _prep_w_in(w_in):
    sizes = [SB_WIDTH] * 3 + [MLA_Q_LORA, MLA_KV_LORA, MLA_ROPE_DIM] + [RWKV_WIDTH] * 3 + [
        RWKV_DECAY_LORA, RWKV_A_LORA, RWKV_GATE_LORA]
    idx = [int(i) for i in np.cumsum(sizes)[:-1]]
    sbq, sbk, sbv, cq, ckv, krope, r, k, v, xw, xa, xg = jnp.split(w_in, idx, axis=-1)
    cols = [r, k, v, sbq, sbk, sbv, cq, ckv, xw, xa, xg, krope, _swap_halves(krope)]
    return jnp.concatenate(cols, axis=-1).astype(BF16)


def _prep_mla_w(w_uq, w_ukv):
    q = w_uq.reshape(MLA_Q_LORA, MLA_HEADS, MLA_NOPE_DIM + MLA_ROPE_DIM)
    nope, pe = q[..., :MLA_NOPE_DIM], q[..., MLA_NOPE_DIM:]
    zpad = jnp.zeros_like(pe)
    wq = jnp.concatenate([nope, pe, zpad, _swap_halves(pe), zpad], axis=-1)
    return wq.reshape(MLA_Q_LORA, MLA_HEADS * 384).astype(BF16), w_ukv.astype(BF16)


def _rope_table(positions):
    half = MLA_ROPE_DIM // 2
    inv_freq = ROPE_THETA ** (-jnp.arange(half, dtype=F32) / half)
    ang = positions.astype(F32)[:, None] * inv_freq
    cos, sin = jnp.cos(ang), jnp.sin(ang)
    z = jnp.zeros((positions.shape[0], MLA_ROPE_DIM), F32)
    return jnp.concatenate([cos, cos, z, -sin, sin, z], axis=-1)


def _pick_tile(n, candidates):
    for c in candidates:
        if n % c == 0:
            return c
    raise ValueError(f"no tile for {n}")


def kernel(x, positions, ffn1_norm, ffn1_gate, ffn1_up, ffn1_down, mix_norm, w_in, mla_q_norm,
           mla_w_uq, mla_kv_norm, mla_w_ukv, rwkv_mu, rwkv_w0, rwkv_w2, rwkv_a0, rwkv_a2, rwkv_g2,
           rwkv_k_k, rwkv_k_a, rwkv_r_k, rwkv_ln_w, rwkv_ln_b, sb_out_norm, mla_out_norm, w_out,
           ffn2_norm, ffn2_gate, ffn2_up, ffn2_down, final_norm):
    bsz, s, d = x.shape
    assert bsz == 1 and d == SB_WIDTH + MLA_WIDTH + RWKV_WIDTH
    depth = w_in.shape[0]
    d_ff = ffn1_gate.shape[-1]
    tm = _pick_tile(s, (512, 256, 128))
    tf = _pick_tile(d_ff, (512, 256, 128))
    x = x[0]
    tab = _rope_table(positions[0])
    w = RWKV_WIDTH
    zl = jnp.zeros((RWKV_DECAY_LORA, w), F32)

    def ffn(x, g, wg, wu, wd):
        act = ffn_up(x, g, wg.astype(BF16), wu.astype(BF16), tm=tm, tn=tf)
        return ffn_down(x, act, wd.astype(BF16), tm=tm, tn=512)

    for l in range(depth):
        x = ffn(x, ffn1_norm[l], ffn1_gate[l], ffn1_up[l], ffn1_down[l])

        z = norm_proj(x, mix_norm[l], _prep_w_in(w_in[l]), tm=tm, tn=640)

        y_sb = sb_attention(z, sb_out_norm[l].reshape(SB_HEADS, 1, SB_HEAD_DIM))

        wq, wkv = _prep_mla_w(mla_w_uq[l], mla_w_ukv[l])
        q, k, v = mla_prep(z, tab, mla_q_norm[l], mla_kv_norm[l], wq, wkv, ts=_pick_tile(s, (256, 128)))
        y_mla = mla_attention(q, k, v, mla_out_norm[l].reshape(MLA_HEADS, 1, MLA_V_DIM))

        mu = rwkv_mu[l]
        p = {
            "mu_rkv": mu[:3 * w].reshape(3, w),
            "mu_lora": mu[3 * w:3 * w + 128].reshape(1, 128),
            "mu_g": mu[3 * w + 128:].reshape(1, 128),
            "w0": rwkv_w0[l].reshape(1, w),
            "w2": jnp.concatenate([rwkv_w2[l], zl], axis=0).astype(BF16),
            "a0": rwkv_a0[l].reshape(1, w),
            "a2": jnp.concatenate([zl, rwkv_a2[l]], axis=0).astype(BF16),
            "g2": rwkv_g2[l].astype(BF16),
            "k_k": rwkv_k_k[l].reshape(1, w),
            "k_a": rwkv_k_a[l].reshape(1, w),
            "r_k": rwkv_r_k[l].reshape(1, w),
            "ln_w": rwkv_ln_w[l].reshape(1, w),
            "ln_b": rwkv_ln_b[l].reshape(1, w),
        }
        r_, lw_, k_, v_, kk_, b_, g_ = rwkv_prep(z, p, ts=_pick_tile(s, (256, 128)))
        y_rwkv = rwkv_chunk(r_, lw_, k_, v_, kk_, b_, g_, p, heads_per_step=16)

        x = out_proj(x, y_sb, y_mla, y_rwkv, w_out[l].astype(BF16), tm=tm, tn=512)

        x = ffn(x, ffn2_norm[l], ffn2_gate[l], ffn2_up[l], ffn2_down[l])

    return final_rmsnorm(x, final_norm, tm=_pick_tile(s, (256, 128)))[None]
```

```python
import functools

import jax
import jax.numpy as jnp
import numpy as np
from jax import lax
from jax.experimental import pallas as pl
from jax.experimental.pallas import tpu as pltpu

F32 = jnp.float32
BF16 = jnp.bfloat16

NORM_EPS = 1e-6
SB_HEADS = 4
SB_HEAD_DIM = 128
SB_WIDTH = SB_HEADS * SB_HEAD_DIM
MLA_HEADS = 4
MLA_NOPE_DIM = 128
MLA_ROPE_DIM = 64
MLA_V_DIM = 128
MLA_Q_LORA = 512
MLA_KV_LORA = 256
MLA_WIDTH = MLA_HEADS * MLA_V_DIM
MLA_QK_PAD = 256
ROPE_THETA = 10000.0
RWKV_HEADS = 16
RWKV_HEAD_DIM = 64
RWKV_WIDTH = RWKV_HEADS * RWKV_HEAD_DIM
RWKV_DECAY_LORA = 64
RWKV_A_LORA = 64
RWKV_GATE_LORA = 128
RWKV_GN_EPS = 64e-5
RWKV_CHUNK = 64
RWKV_PASSES_A = 1
RWKV_PASSES_T = 1
RWKV_PASSES_S = 1

COL_R, COL_K, COL_V = 0, 1024, 2048
COL_SBQ, COL_SBK, COL_SBV = 3072, 3584, 4096
COL_CQ, COL_CKV = 4608, 5120
COL_LORA, COL_XG, COL_ROPE = 5376, 5504, 5632
Z_COLS = 5760

VMEM_LIMIT = 48 * 1024 * 1024


def _cparams(sem):
    return pltpu.CompilerParams(dimension_semantics=sem, vmem_limit_bytes=VMEM_LIMIT)


def _dot(a, b):
    return jnp.dot(a, b, preferred_element_type=F32)


def _dot_nt(a, b):
    return lax.dot_general(a, b, (((1,), (1,)), ((), ())), preferred_element_type=F32)


def _dot_tn(a, b):
    return lax.dot_general(a, b, (((0,), (0,)), ((), ())), preferred_element_type=F32)


def _split3(x):
    hi = x.astype(BF16)
    r1 = x - hi.astype(F32)
    mid = r1.astype(BF16)
    lo = (r1 - mid.astype(F32)).astype(BF16)
    return hi, mid, lo


def _split2(x):
    hi = x.astype(BF16)
    lo = (x - hi.astype(F32)).astype(BF16)
    return hi, lo


def _mm3(dot, a, b):
    ah, al = _split2(a)
    bh, bl = _split2(b)
    return dot(ah, bh) + (dot(ah, bl) + dot(al, bh))


def _mm(dot, a, b, passes):
    if passes == 1:
        return dot(a.astype(BF16), b.astype(BF16))
    return _mm3(dot, a, b)


def _exact_lhs_dot(m_bf16, x):
    hi, mid, lo = _split3(x)
    return _dot(m_bf16, hi) + (_dot(m_bf16, mid) + _dot(m_bf16, lo))


def _rms_rows(x, g):
    ms = jnp.mean(x * x, axis=-1, keepdims=True)
    return x * lax.rsqrt(ms + NORM_EPS) * g


def _norm_proj_kernel(x_ref, g_ref, w_ref, o_ref, h_ref):
    @pl.when(pl.program_id(1) == 0)
    def _():
        h_ref[...] = _rms_rows(x_ref[...], g_ref[...]).astype(BF16)

    o_ref[...] = _dot(h_ref[...], w_ref[...]).astype(o_ref.dtype)


def norm_proj(x, g, w, l, *, tm, tn, out_dtype=F32):
    s, d = x.shape
    n = w.shape[2]
    return pl.pallas_call(
        _norm_proj_kernel,
        grid=(s // tm, n // tn),
        in_specs=[
            pl.BlockSpec((tm, d), lambda i, j: (i, 0)),
            pl.BlockSpec((1, d), lambda i, j: (0, 0)),
            pl.BlockSpec((None, d, tn), lambda i, j: (l, 0, j)),
        ],
        out_specs=pl.BlockSpec((tm, tn), lambda i, j: (i, j)),
        out_shape=jax.ShapeDtypeStruct((s, n), out_dtype),
        scratch_shapes=[pltpu.VMEM((tm, d), BF16)],
        compiler_params=_cparams(("parallel", "arbitrary")),
        name="norm_proj",
    )(x, g.reshape(1, d), w)


def _ffn_up_kernel(x_ref, g_ref, wg_ref, wu_ref, o_ref, h_ref):
    @pl.when(pl.program_id(1) == 0)
    def _():
        h_ref[...] = _rms_rows(x_ref[...], g_ref[...]).astype(BF16)

    h = h_ref[...]
    a = _dot(h, wg_ref[...])
    u = _dot(h, wu_ref[...])
    o_ref[...] = (a * jax.nn.sigmoid(a) * u).astype(o_ref.dtype)


def ffn_up(x, g, wg, wu, l, *, tm, tn):
    s, d = x.shape
    f = wg.shape[2]
    return pl.pallas_call(
        _ffn_up_kernel,
        grid=(s // tm, f // tn),
        in_specs=[
            pl.BlockSpec((tm, d), lambda i, j: (i, 0)),
            pl.BlockSpec((1, d), lambda i, j: (0, 0)),
            pl.BlockSpec((None, d, tn), lambda i, j: (l, 0, j)),
            pl.BlockSpec((None, d, tn), lambda i, j: (l, 0, j)),
        ],
        out_specs=pl.BlockSpec((tm, tn), lambda i, j: (i, j)),
        out_shape=jax.ShapeDtypeStruct((s, f), BF16),
        scratch_shapes=[pltpu.VMEM((tm, d), BF16)],
        compiler_params=_cparams(("parallel", "arbitrary")),
        name="ffn_up",
    )(x, g.reshape(1, d), wg, wu)


def _ffn_down_kernel(x_ref, a_ref, w_ref, o_ref):
    o_ref[...] = x_ref[...] + 0.5 * _dot(a_ref[...], w_ref[...])


def ffn_down(x, act, wd, l, *, tm, tn):
    s, d = x.shape
    f = act.shape[1]
    return pl.pallas_call(
        _ffn_down_kernel,
        grid=(s // tm, d // tn),
        in_specs=[
            pl.BlockSpec((tm, tn), lambda i, j: (i, j)),
            pl.BlockSpec((tm, f), lambda i, j: (i, 0)),
            pl.BlockSpec((None, f, tn), lambda i, j: (l, 0, j)),
        ],
        out_specs=pl.BlockSpec((tm, tn), lambda i, j: (i, j)),
        out_shape=jax.ShapeDtypeStruct((s, d), F32),
        compiler_params=_cparams(("parallel", "arbitrary")),
        name="ffn_down",
    )(x, act, wd)


def _out_proj_kernel(x_ref, a_ref, b_ref, c_ref, wa_ref, wb_ref, wc_ref, o_ref):
    acc = _dot(a_ref[...], wa_ref[...])
    acc += _dot(b_ref[...], wb_ref[...])
    acc += _dot(c_ref[...], wc_ref[...])
    o_ref[...] = x_ref[...] + acc


def out_proj(x, y_sb, y_mla, y_rwkv, w_out, l, *, tm, tn):
    s, d = x.shape
    assert SB_WIDTH == MLA_WIDTH and RWKV_WIDTH == SB_WIDTH + MLA_WIDTH
    return pl.pallas_call(
        _out_proj_kernel,
        grid=(s // tm, d // tn),
        in_specs=[
            pl.BlockSpec((tm, tn), lambda i, j: (i, j)),
            pl.BlockSpec((tm, SB_WIDTH), lambda i, j: (i, 0)),
            pl.BlockSpec((tm, MLA_WIDTH), lambda i, j: (i, 0)),
            pl.BlockSpec((tm, RWKV_WIDTH), lambda i, j: (i, 0)),
            pl.BlockSpec((None, SB_WIDTH, tn), lambda i, j: (l, 0, j)),
            pl.BlockSpec((None, MLA_WIDTH, tn), lambda i, j: (l, 1, j)),
            pl.BlockSpec((None, RWKV_WIDTH, tn), lambda i, j: (l, 1, j)),
        ],
        out_specs=pl.BlockSpec((tm, tn), lambda i, j: (i, j)),
        out_shape=jax.ShapeDtypeStruct((s, d), F32),
        compiler_params=_cparams(("parallel", "arbitrary")),
        name="out_proj",
    )(x, y_sb, y_mla, y_rwkv, w_out, w_out, w_out)


def _final_norm_kernel(x_ref, g_ref, o_ref):
    o_ref[...] = _rms_rows(x_ref[...], g_ref[...])


def final_rmsnorm(x, g, *, tm):
    s, d = x.shape
    return pl.pallas_call(
        _final_norm_kernel,
        grid=(s // tm,),
        in_specs=[pl.BlockSpec((tm, d), lambda i: (i, 0)), pl.BlockSpec((1, d), lambda i: (0, 0))],
        out_specs=pl.BlockSpec((tm, d), lambda i: (i, 0)),
        out_shape=jax.ShapeDtypeStruct((s, d), F32),
        compiler_params=_cparams(("parallel",)),
        name="final_norm",
    )(x, g.reshape(1, d))


SB_BQ = 256
SB_BK = 128
SB_DEAD_LOG = -104.0


def _sb_attn_kernel(q_ref, k_ref, v_ref, g_ref, o_ref):
    i = pl.program_id(1)
    bq, bk = SB_BQ, SB_BK
    q = (q_ref[...] * (SB_HEAD_DIM ** -0.5)).astype(BF16)
    row = lax.broadcasted_iota(jnp.int32, (bq, bk), 0)
    col = lax.broadcasted_iota(jnp.int32, (bq, bk), 1)
    kr = lax.broadcasted_iota(jnp.int32, (bk, bk), 0)
    kc = lax.broadcasted_iota(jnp.int32, (bk, bk), 1)
    upper = jnp.where(kr > kc, 1.0, 0.0).astype(BF16)

    def block(kb, carry, acc, masked):
        start = pl.multiple_of(kb * bk, bk)
        kblk = k_ref[pl.ds(start, bk), :].astype(BF16)
        vblk = v_ref[pl.ds(start, bk), :].astype(BF16)
        z = _dot_nt(q, kblk)
        log_beta = jnp.minimum(z, 0.0) - jnp.log1p(jnp.exp(-jnp.abs(z)))
        log_keep = log_beta - z
        if masked:
            before = (start + col) < (i * bq + row)
            log_keep = jnp.where(before, log_keep, 0.0)
        hi, lo = _split2(log_keep)
        within = _dot(hi, upper) + _dot(lo, upper)
        a = jnp.exp(log_beta + (carry + within))
        if masked:
            a = jnp.where(before, a, 0.0)
        acc = acc + _dot(a.astype(BF16), vblk)
        carry = carry + jnp.sum(log_keep, axis=-1, keepdims=True)
        return carry, acc

    carry = jnp.zeros((bq, 1), F32)
    acc = jnp.zeros((bq, SB_HEAD_DIM), F32)
    nd = bq // bk
    first = i * nd
    for d in range(nd - 1, -1, -1):
        carry, acc = block(first + d, carry, acc, True)

    def cond(st):
        kb, alive, _, _ = st
        return jnp.logical_and(kb >= 0, alive > SB_DEAD_LOG)

    def body(st):
        kb, _, carry, acc = st
        carry, acc = block(kb, carry, acc, False)
        return kb - 1, jnp.max(carry), carry, acc

    _, _, _, acc = lax.while_loop(cond, body, (first - 1, jnp.max(carry), carry, acc))
    o_ref[...] = _rms_rows(acc, g_ref[0]).astype(o_ref.dtype)


def sb_attention(z, gains):
    s = z.shape[0]
    qb, kb_, vb = COL_SBQ // SB_HEAD_DIM, COL_SBK // SB_HEAD_DIM, COL_SBV // SB_HEAD_DIM
    return pl.pallas_call(
        _sb_attn_kernel,
        grid=(SB_HEADS, s // SB_BQ),
        in_specs=[
            pl.BlockSpec((SB_BQ, SB_HEAD_DIM), lambda h, i: (i, qb + h)),
            pl.BlockSpec((s, SB_HEAD_DIM), lambda h, i: (0, kb_ + h)),
            pl.BlockSpec((s, SB_HEAD_DIM), lambda h, i: (0, vb + h)),
            pl.BlockSpec((1, 1, SB_HEAD_DIM), lambda h, i: (h, 0, 0)),
        ],
        out_specs=pl.BlockSpec((SB_BQ, SB_HEAD_DIM), lambda h, i: (i, h)),
        out_shape=jax.ShapeDtypeStruct((s, SB_WIDTH), BF16),
        compiler_params=_cparams(("parallel", "arbitrary")),
        name="sb_attention",
    )(z, z, z, gains)


def _mla_prep_kernel(cq_ref, ckv_ref, rope_ref, tab_ref, qn_ref, kvn_ref, wq_ref, wkv_ref,
                     q_ref, k_ref, v_ref):
    scale = (MLA_NOPE_DIM + MLA_ROPE_DIM) ** -0.5
    hq = _rms_rows(cq_ref[...], qn_ref[...]).astype(BF16)
    hkv = _rms_rows(ckv_ref[...], kvn_ref[...]).astype(BF16)
    cos2 = tab_ref[:, :128]
    sin2 = tab_ref[:, 128:]
    kr = rope_ref[...]
    k_pe = kr * cos2 + pltpu.roll(kr, 64, 1) * sin2
    k_pe = k_pe.astype(BF16)
    qall = _dot(hq, wq_ref[...])
    kvall = _dot(hkv, wkv_ref[...])
    for h in range(MLA_HEADS):
        qh = qall[:, h * 384:(h + 1) * 384]
        q_ref[h, :, :128] = (qh[:, :128] * scale).astype(BF16)
        q_ref[h, :, 128:] = ((qh[:, 128:256] * cos2 + qh[:, 256:384] * sin2) * scale).astype(BF16)
        k_ref[h, :, :128] = kvall[:, h * 256:h * 256 + 128].astype(BF16)
        k_ref[h, :, 128:] = k_pe
        v_ref[h] = kvall[:, h * 256 + 128:(h + 1) * 256].astype(BF16)


def mla_prep(z, tab, q_norm, kv_norm, wq, wkv, *, ts):
    s = z.shape[0]
    return pl.pallas_call(
        _mla_prep_kernel,
        grid=(s // ts,),
        in_specs=[
            pl.BlockSpec((ts, MLA_Q_LORA), lambda i: (i, COL_CQ // MLA_Q_LORA)),
            pl.BlockSpec((ts, MLA_KV_LORA), lambda i: (i, COL_CKV // MLA_KV_LORA)),
            pl.BlockSpec((ts, 128), lambda i: (i, COL_ROPE // 128)),
            pl.BlockSpec((ts, 256), lambda i: (i, 0)),
            pl.BlockSpec((1, MLA_Q_LORA), lambda i: (0, 0)),
            pl.BlockSpec((1, MLA_KV_LORA), lambda i: (0, 0)),
            pl.BlockSpec(wq.shape, lambda i: (0, 0)),
            pl.BlockSpec(wkv.shape, lambda i: (0, 0)),
        ],
        out_specs=[
            pl.BlockSpec((MLA_HEADS, ts, MLA_QK_PAD), lambda i: (0, i, 0)),
            pl.BlockSpec((MLA_HEADS, ts, MLA_QK_PAD), lambda i: (0, i, 0)),
            pl.BlockSpec((MLA_HEADS, ts, MLA_V_DIM), lambda i: (0, i, 0)),
        ],
        out_shape=[
            jax.ShapeDtypeStruct((MLA_HEADS, s, MLA_QK_PAD), BF16),
            jax.ShapeDtypeStruct((MLA_HEADS, s, MLA_QK_PAD), BF16),
            jax.ShapeDtypeStruct((MLA_HEADS, s, MLA_V_DIM), BF16),
        ],
        compiler_params=_cparams(("parallel",)),
        name="mla_prep",
    )(z, z, z, tab, q_norm.reshape(1, -1), kv_norm.reshape(1, -1), wq, wkv)


MLA_BQ = 512
MLA_BK = 512


def _mla_attn_kernel(q_ref, k_ref, v_ref, g_ref, o_ref):
    i = pl.program_id(1)
    bq, bk = MLA_BQ, MLA_BK
    q = q_ref[0]

    def block(kb, m, l, acc, masked):
        start = pl.multiple_of(kb * bk, bk)
        kblk = k_ref[0, pl.ds(start, bk), :]
        vblk = v_ref[0, pl.ds(start, bk), :]
        sc = _dot_nt(q, kblk)
        if masked:
            row = lax.broadcasted_iota(jnp.int32, (bq, bk), 0)
            col = lax.broadcasted_iota(jnp.int32, (bq, bk), 1)
            sc = jnp.where(col <= row, sc, -1e30)
        m_new = jnp.maximum(m, jnp.max(sc, axis=-1, keepdims=True))
        alpha = jnp.exp(m - m_new)
        p = jnp.exp(sc - m_new)
        l = l * alpha + jnp.sum(p, axis=-1, keepdims=True)
        acc = acc * alpha + _dot(p.astype(BF16), vblk)
        return m_new, l, acc

    def body(kb, st):
        return block(kb, *st, False)

    m0 = jnp.full((bq, 1), -1e30, F32)
    l0 = jnp.zeros((bq, 1), F32)
    a0 = jnp.zeros((bq, MLA_V_DIM), F32)
    m, l, acc = lax.fori_loop(0, i, body, (m0, l0, a0))
    m, l, acc = block(i, m, l, acc, True)
    y = acc / l
    o_ref[...] = _rms_rows(y, g_ref[0]).astype(o_ref.dtype)


def mla_attention(q, k, v, gains):
    s = q.shape[1]
    return pl.pallas_call(
        _mla_attn_kernel,
        grid=(MLA_HEADS, s // MLA_BQ),
        in_specs=[
            pl.BlockSpec((1, MLA_BQ, MLA_QK_PAD), lambda h, i: (h, i, 0)),
            pl.BlockSpec((1, s, MLA_QK_PAD), lambda h, i: (h, 0, 0)),
            pl.BlockSpec((1, s, MLA_V_DIM), lambda h, i: (h, 0, 0)),
            pl.BlockSpec((1, 1, MLA_V_DIM), lambda h, i: (h, 0, 0)),
        ],
        out_specs=pl.BlockSpec((MLA_BQ, MLA_V_DIM), lambda h, i: (i, h)),
        out_shape=jax.ShapeDtypeStruct((s, MLA_WIDTH), BF16),
        compiler_params=_cparams(("parallel", "arbitrary")),
        name="mla_attention",
    )(q, k, v, gains)


def _head_sum_matrix():
    r = lax.broadcasted_iota(jnp.int32, (128, 128), 0) // RWKV_HEAD_DIM
    c = lax.broadcasted_iota(jnp.int32, (128, 128), 1) // RWKV_HEAD_DIM
    return jnp.where(r == c, 1.0, 0.0).astype(BF16)


def _head_sums(x, ones_bd):
    parts = []
    for j in range(x.shape[1] // 128):
        parts.append(_exact_lhs_dot_rhs(x[:, j * 128:(j + 1) * 128], ones_bd))
    return jnp.concatenate(parts, axis=-1)


def _exact_lhs_dot_rhs(x, m_bf16):
    hi, mid, lo = _split3(x)
    return _dot(hi, m_bf16) + (_dot(mid, m_bf16) + _dot(lo, m_bf16))


def _rwkv_prep_kernel(r_ref, k_ref, v_ref, lora_ref, xg_ref, mu_ref, mul_ref, mug_ref,
                      w0_ref, w2_ref, a0_ref, a2_ref, g2_ref, kk_ref, ka_ref,
                      ro_ref, lw_ref, ko_ref, vo_ref, kko_ref, bo_ref, go_ref,
                      prev_ref, prevl_ref):
    ts = r_ref.shape[0]

    @pl.when(pl.program_id(0) == 0)
    def _():
        prev_ref[...] = jnp.zeros_like(prev_ref)
        prevl_ref[...] = jnp.zeros_like(prevl_ref)

    first_row = lax.broadcasted_iota(jnp.int32, (ts, 1), 0) == 0

    def shift_mix(x, prev_row, mu):
        x_prev = jnp.where(first_row, prev_row, pltpu.roll(x, 1, 0))
        return x + (x_prev - x) * mu

    r_in, k_in, v_in = r_ref[...], k_ref[...], v_ref[...]
    lora_in, xg_in = lora_ref[...], xg_ref[...]
    r = shift_mix(r_in, prev_ref[0:1, :], mu_ref[0:1, :])
    k = shift_mix(k_in, prev_ref[1:2, :], mu_ref[1:2, :])
    v = shift_mix(v_in, prev_ref[2:3, :], mu_ref[2:3, :])
    lora = shift_mix(lora_in, prevl_ref[0:1, :], mul_ref[...])
    xg = shift_mix(xg_in, prevl_ref[1:2, :], mug_ref[...])
    prev_ref[0:1, :] = r_in[ts - 1:ts, :]
    prev_ref[1:2, :] = k_in[ts - 1:ts, :]
    prev_ref[2:3, :] = v_in[ts - 1:ts, :]
    prevl_ref[0:1, :] = lora_in[ts - 1:ts, :]
    prevl_ref[1:2, :] = xg_in[ts - 1:ts, :]

    dw = w0_ref[...] + _dot(jnp.tanh(lora).astype(BF16), w2_ref[...])
    log_w = -jax.nn.softplus(-dw) - 0.5
    lw_ref[...] = -jnp.exp(log_w)
    a = jax.nn.sigmoid(a0_ref[...] + _dot(lora.astype(BF16), a2_ref[...]))
    go_ref[...] = _dot(jax.nn.sigmoid(xg).astype(BF16), g2_ref[...])

    kk = k * kk_ref[...]
    ss = _head_sums(kk * kk, _head_sum_matrix())
    kk = kk / jnp.maximum(jnp.sqrt(ss), 1e-12)
    ro_ref[...] = r
    vo_ref[...] = v
    ko_ref[...] = k * (1.0 + (a - 1.0) * ka_ref[...])
    kko_ref[...] = kk
    bo_ref[...] = kk * a


def rwkv_prep(z, p, *, ts):
    s = z.shape[0]
    w = RWKV_WIDTH
    row = lambda n: pl.BlockSpec((1, n), lambda i: (0, 0))
    big = pl.BlockSpec((ts, w), lambda i: (i, 0))
    out = jax.ShapeDtypeStruct((s, w), F32)
    return pl.pallas_call(
        _rwkv_prep_kernel,
        grid=(s // ts,),
        in_specs=[
            pl.BlockSpec((ts, w), lambda i: (i, COL_R // w)),
            pl.BlockSpec((ts, w), lambda i: (i, COL_K // w)),
            pl.BlockSpec((ts, w), lambda i: (i, COL_V // w)),
            pl.BlockSpec((ts, 128), lambda i: (i, COL_LORA // 128)),
            pl.BlockSpec((ts, 128), lambda i: (i, COL_XG // 128)),
            pl.BlockSpec((3, w), lambda i: (0, 0)),
            row(128), row(128),
            row(w), pl.BlockSpec((128, w), lambda i: (0, 0)),
            row(w), pl.BlockSpec((128, w), lambda i: (0, 0)),
            pl.BlockSpec((128, w), lambda i: (0, 0)),
            row(w), row(w),
        ],
        out_specs=[big] * 7,
        out_shape=[out] * 7,
        scratch_shapes=[pltpu.VMEM((8, w), F32), pltpu.VMEM((8, 128), F32)],
        compiler_params=_cparams(("arbitrary",)),
        name="rwkv_prep",
    )(z, z, z, z, z, p["mu_rkv"], p["mu_lora"], p["mu_g"], p["w0"], p["w2"], p["a0"], p["a2"],
      p["g2"], p["k_k"], p["k_a"])


def _rwkv_chunk_kernel(r_ref, lw_ref, k_ref, v_ref, kk_ref, b_ref, g_ref, rk_ref, lnw_ref, lnb_ref,
                       o_ref, st_ref, y_ref):
    c = RWKV_CHUNK
    n = RWKV_HEAD_DIM
    nh = r_ref.shape[1] // n

    @pl.when(pl.program_id(1) == 0)
    def _():
        st_ref[...] = jnp.zeros_like(st_ref)

    ri = lax.broadcasted_iota(jnp.int32, (c, c), 0)
    ci = lax.broadcasted_iota(jnp.int32, (c, c), 1)
    strict = ri > ci
    incl = ri >= ci
    eye = jnp.where(ri == ci, 1.0, 0.0).astype(F32)
    tril_ones = jnp.where(incl, 1.0, 0.0).astype(BF16)
    d16 = (ri // 16) == (ci // 16)
    d32 = (ri // 32) == (ci // 32)

    r, lw, k, v = r_ref[...], lw_ref[...], k_ref[...], v_ref[...]
    kk, b = kk_ref[...], b_ref[...]
    cum = _exact_lhs_dot(tril_ones, lw)
    p_incl = jnp.exp(cum)
    p_excl = jnp.exp(cum - lw)
    p_inv = jnp.exp(-cum)
    p_end = p_incl[c - 1:c, :]
    r_t = r * p_incl
    kk_t = kk * p_excl
    b_t = b * p_inv
    k_t = k * p_inv
    b_d = b_t * p_end
    k_d = k_t * p_end

    lhs_all = jnp.concatenate([kk_t, r_t], axis=0)
    rhs_all = jnp.concatenate([b_t, k_t], axis=0)
    dec_all = jnp.concatenate([b_d, k_d], axis=0)
    ri2 = lax.broadcasted_iota(jnp.int32, (2 * c, 2 * c), 0)
    ci2 = lax.broadcasted_iota(jnp.int32, (2 * c, 2 * c), 1) % c
    causal2 = ci2 <= jnp.where(ri2 < c, ri2 - 1, ri2 - c)
    blk_lo = jnp.logical_and(d32, jnp.logical_not(d16))

    heads = range(nh)
    sls = [slice(h * n, (h + 1) * n) for h in heads]
    mm_s = functools.partial(_mm, passes=RWKV_PASSES_S)
    mm_t = functools.partial(_mm, _dot, passes=RWKV_PASSES_T)
    a = [jnp.where(causal2, _mm(_dot_nt, lhs_all[:, sl], rhs_all[:, sl], RWKV_PASSES_A), 0.0)
         for sl in sls]
    st = [st_ref[h] for h in heads]
    part = [mm_s(_dot_nt, lhs_all[:, sls[h]], st[h]) + mm_s(_dot, a[h][:, c:], v[:, sls[h]])
            for h in heads]
    a_ub = [a[h][:c, :c] for h in heads]
    ld = [jnp.where(d16, a_ub[h], 0.0) for h in heads]
    x = [eye - ld[h] for h in heads]
    pw = [mm_t(ld[h], ld[h]) for h in heads]
    x = [x[h] + mm_t(x[h], pw[h]) for h in heads]
    pw = [mm_t(pw[h], pw[h]) for h in heads]
    x = [x[h] + mm_t(x[h], pw[h]) for h in heads]
    pw = [mm_t(pw[h], pw[h]) for h in heads]
    x = [x[h] + mm_t(x[h], pw[h]) for h in heads]
    t = [mm_t(x[h], jnp.where(blk_lo, a_ub[h], 0.0)) for h in heads]
    x = [x[h] - mm_t(t[h], x[h]) for h in heads]
    t = [mm_t(x[h], jnp.where(d32, 0.0, a_ub[h])) for h in heads]
    x = [x[h] - mm_t(t[h], x[h]) for h in heads]

    u = [-mm_s(_dot, x[h], part[h][:c]) for h in heads]
    for h in heads:
        y_ref[:, sls[h]] = part[h][c:] + mm_s(_dot, a[h][c:, :c], u[h])
    for h in heads:
        uv = jnp.concatenate([u[h], v[:, sls[h]]], axis=0)
        st_ref[h] = st[h] * p_end[:, sls[h]] + mm_s(_dot_tn, uv, dec_all[:, sls[h]])

    ones_bd = _head_sum_matrix()
    y = y_ref[...]
    mean = _head_sums(y, ones_bd) * (1.0 / n)
    yc = y - mean
    var = _head_sums(yc * yc, ones_bd) * (1.0 / n)
    yn = yc * lax.rsqrt(var + RWKV_GN_EPS) * lnw_ref[...] + lnb_ref[...]
    bonus = _head_sums(r * k * rk_ref[...], ones_bd)
    o_ref[...] = ((yn + bonus * v) * g_ref[...]).astype(o_ref.dtype)


def rwkv_chunk(r, lw, k, v, kk, b, g, p, *, heads_per_step):
    s = r.shape[0]
    c = RWKV_CHUNK
    wb = heads_per_step * RWKV_HEAD_DIM
    big = pl.BlockSpec((c, wb), lambda hg, t: (t, hg))
    row = pl.BlockSpec((1, wb), lambda hg, t: (0, hg))
    return pl.pallas_call(
        _rwkv_chunk_kernel,
        grid=(RWKV_WIDTH // wb, s // c),
        in_specs=[big] * 7 + [row] * 3,
        out_specs=big,
        out_shape=jax.ShapeDtypeStruct((s, RWKV_WIDTH), BF16),
        scratch_shapes=[
            pltpu.VMEM((heads_per_step, RWKV_HEAD_DIM, RWKV_HEAD_DIM), F32),
            pltpu.VMEM((c, wb), F32),
        ],
        compiler_params=_cparams(("parallel", "arbitrary")),
        name="rwkv_chunk",
    )(r, lw, k, v, kk, b, g, p["r_k"], p["ln_w"], p["ln_b"])


def _swap_halves(w):
    half = w.shape[-1] // 2
    return jnp.concatenate([w[..., half:], w[..., :half]], axis=-1)


def _prep_w_in(w_in):
    sizes = [SB_WIDTH] * 3 + [MLA_Q_LORA, MLA_KV_LORA, MLA_ROPE_DIM] + [RWKV_WIDTH] * 3 + [
        RWKV_DECAY_LORA, RWKV_A_LORA, RWKV_GATE_LORA]
    idx = [int(i) for i in np.cumsum(sizes)[:-1]]
    sbq, sbk, sbv, cq, ckv, krope, r, k, v, xw, xa, xg = jnp.split(w_in, idx, axis=-1)
    cols = [r, k, v, sbq, sbk, sbv, cq, ckv, xw, xa, xg, krope, _swap_halves(krope)]
    return jnp.concatenate(cols, axis=-1).astype(BF16)


def _prep_mla_w(w_uq, w_ukv):
    q = w_uq.reshape(MLA_Q_LORA, MLA_HEADS, MLA_NOPE_DIM + MLA_ROPE_DIM)
    nope, pe = q[..., :MLA_NOPE_DIM], q[..., MLA_NOPE_DIM:]
    zpad = jnp.zeros_like(pe)
    wq = jnp.concatenate([nope, pe, zpad, _swap_halves(pe), zpad], axis=-1)
    return wq.reshape(MLA_Q_LORA, MLA_HEADS * 384).astype(BF16), w_ukv.astype(BF16)


def _rope_table(positions):
    half = MLA_ROPE_DIM // 2
    inv_freq = ROPE_THETA ** (-jnp.arange(half, dtype=F32) / half)
    ang = positions.astype(F32)[:, None] * inv_freq
    cos, sin = jnp.cos(ang), jnp.sin(ang)
    z = jnp.zeros((positions.shape[0], MLA_ROPE_DIM), F32)
    return jnp.concatenate([cos, cos, z, -sin, sin, z], axis=-1)


def _pick_tile(n, candidates):
    for c in candidates:
        if n % c == 0:
            return c
    raise ValueError(f"no tile for {n}")


def kernel(x, positions, ffn1_norm, ffn1_gate, ffn1_up, ffn1_down, mix_norm, w_in, mla_q_norm,
           mla_w_uq, mla_kv_norm, mla_w_ukv, rwkv_mu, rwkv_w0, rwkv_w2, rwkv_a0, rwkv_a2, rwkv_g2,
           rwkv_k_k, rwkv_k_a, rwkv_r_k, rwkv_ln_w, rwkv_ln_b, sb_out_norm, mla_out_norm, w_out,
           ffn2_norm, ffn2_gate, ffn2_up, ffn2_down, final_norm):
    bsz, s, d = x.shape
    assert bsz == 1 and d == SB_WIDTH + MLA_WIDTH + RWKV_WIDTH
    depth = w_in.shape[0]
    d_ff = ffn1_gate.shape[-1]
    tm = _pick_tile(s, (1024, 512, 256, 128))
    tf = _pick_tile(d_ff, (512, 256, 128))
    x = x[0]
    tab = _rope_table(positions[0])
    w = RWKV_WIDTH
    zl = jnp.zeros((RWKV_DECAY_LORA, w), F32)
    ffn_w = [tuple(t.astype(BF16) for t in ws)
             for ws in ((ffn1_gate, ffn1_up, ffn1_down), (ffn2_gate, ffn2_up, ffn2_down))]
    w_in_b = _prep_w_in(w_in)
    w_out_b = w_out.astype(BF16)

    def ffn(x, g, ws, l):
        act = ffn_up(x, g, ws[0], ws[1], l, tm=tm, tn=tf)
        return ffn_down(x, act, ws[2], l, tm=tm, tn=256)

    for l in range(depth):
        x = ffn(x, ffn1_norm[l], ffn_w[0], l)

        z = norm_proj(x, mix_norm[l], w_in_b, l, tm=tm, tn=640)

        y_sb = sb_attention(z, sb_out_norm[l].reshape(SB_HEADS, 1, SB_HEAD_DIM))

        wq, wkv = _prep_mla_w(mla_w_uq[l], mla_w_ukv[l])
        q, k, v = mla_prep(z, tab, mla_q_norm[l], mla_kv_norm[l], wq, wkv, ts=_pick_tile(s, (256, 128)))
        y_mla = mla_attention(q, k, v, mla_out_norm[l].reshape(MLA_HEADS, 1, MLA_V_DIM))

        mu = rwkv_mu[l]
        p = {
            "mu_rkv": mu[:3 * w].reshape(3, w),
            "mu_lora": mu[3 * w:3 * w + 128].reshape(1, 128),
            "mu_g": mu[3 * w + 128:].reshape(1, 128),
            "w0": rwkv_w0[l].reshape(1, w),
            "w2": jnp.concatenate([rwkv_w2[l], zl], axis=0).astype(BF16),
            "a0": rwkv_a0[l].reshape(1, w),
            "a2": jnp.concatenate([zl, rwkv_a2[l]], axis=0).astype(BF16),
            "g2": rwkv_g2[l].astype(BF16),
            "k_k": rwkv_k_k[l].reshape(1, w),
            "k_a": rwkv_k_a[l].reshape(1, w),
            "r_k": rwkv_r_k[l].reshape(1, w),
            "ln_w": rwkv_ln_w[l].reshape(1, w),
            "ln_b": rwkv_ln_b[l].reshape(1, w),
        }
        r_, lw_, k_, v_, kk_, b_, g_ = rwkv_prep(z, p, ts=_pick_tile(s, (256, 128)))
        y_rwkv = rwkv_chunk(r_, lw_, k_, v_, kk_, b_, g_, p, heads_per_step=16)

        x = out_proj(x, y_sb, y_mla, y_rwkv, w_out_b, l, tm=tm, tn=512)

        x = ffn(x, ffn2_norm[l], ffn_w[1], l)

    return final_rmsnorm(x, final_norm, tm=_pick_tile(s, (256, 128)))[None]
```

```python
import functools

import jax
import jax.numpy as jnp
import numpy as np
from jax import lax
from jax.experimental import pallas as pl
from jax.experimental.pallas import tpu as pltpu

F32 = jnp.float32
BF16 = jnp.bfloat16

NORM_EPS = 1e-6
SB_HEADS = 4
SB_HEAD_DIM = 128
SB_WIDTH = SB_HEADS * SB_HEAD_DIM
MLA_HEADS = 4
MLA_NOPE_DIM = 128
MLA_ROPE_DIM = 64
MLA_V_DIM = 128
MLA_Q_LORA = 512
MLA_KV_LORA = 256
MLA_WIDTH = MLA_HEADS * MLA_V_DIM
MLA_QK_PAD = 256
ROPE_THETA = 10000.0
RWKV_HEADS = 16
RWKV_HEAD_DIM = 64
RWKV_WIDTH = RWKV_HEADS * RWKV_HEAD_DIM
RWKV_DECAY_LORA = 64
RWKV_A_LORA = 64
RWKV_GATE_LORA = 128
RWKV_GN_EPS = 64e-5
RWKV_CHUNK = 64
RWKV_PASSES_A = 1
RWKV_PASSES_T = 1
RWKV_PASSES_S = 1

COL_R, COL_K, COL_V = 0, 1024, 2048
COL_SBQ, COL_SBK, COL_SBV = 3072, 3584, 4096
COL_CQ, COL_CKV = 4608, 5120
COL_LORA, COL_XG, COL_ROPE = 5376, 5504, 5632
Z_COLS = 5760

VMEM_LIMIT = 48 * 1024 * 1024


def _cparams(sem):
    return pltpu.CompilerParams(dimension_semantics=sem, vmem_limit_bytes=VMEM_LIMIT)


def _dot(a, b):
    return jnp.dot(a, b, preferred_element_type=F32)


def _dot_nt(a, b):
    return lax.dot_general(a, b, (((1,), (1,)), ((), ())), preferred_element_type=F32)


def _dot_tn(a, b):
    return lax.dot_general(a, b, (((0,), (0,)), ((), ())), preferred_element_type=F32)


def _split3(x):
    hi = x.astype(BF16)
    r1 = x - hi.astype(F32)
    mid = r1.astype(BF16)
    lo = (r1 - mid.astype(F32)).astype(BF16)
    return hi, mid, lo


def _split2(x):
    hi = x.astype(BF16)
    lo = (x - hi.astype(F32)).astype(BF16)
    return hi, lo


def _mm3(dot, a, b):
    ah, al = _split2(a)
    bh, bl = _split2(b)
    return dot(ah, bh) + (dot(ah, bl) + dot(al, bh))


def _mm(dot, a, b, passes):
    if passes == 1:
        return dot(a.astype(BF16), b.astype(BF16))
    return _mm3(dot, a, b)


def _exact_lhs_dot(m_bf16, x):
    hi, mid, lo = _split3(x)
    return _dot(m_bf16, hi) + (_dot(m_bf16, mid) + _dot(m_bf16, lo))


def _rms_rows(x, g):
    ms = jnp.mean(x * x, axis=-1, keepdims=True)
    return x * lax.rsqrt(ms + NORM_EPS) * g


def _norm_proj_kernel(x_ref, g_ref, w_ref, o_ref, ob_ref, h_ref):
    @pl.when(pl.program_id(1) == 0)
    def _():
        h_ref[...] = _rms_rows(x_ref[...], g_ref[...]).astype(BF16)

    z = _dot(h_ref[...], w_ref[...])
    o_ref[...] = z
    ob_ref[...] = z.astype(BF16)


def norm_proj(x, g, w, l, *, tm, tn):
    s, d = x.shape
    n = w.shape[2]
    return pl.pallas_call(
        _norm_proj_kernel,
        grid=(s // tm, n // tn),
        in_specs=[
            pl.BlockSpec((tm, d), lambda i, j: (i, 0)),
            pl.BlockSpec((1, d), lambda i, j: (0, 0)),
            pl.BlockSpec((None, d, tn), lambda i, j: (l, 0, j)),
        ],
        out_specs=[pl.BlockSpec((tm, tn), lambda i, j: (i, j))] * 2,
        out_shape=[jax.ShapeDtypeStruct((s, n), F32), jax.ShapeDtypeStruct((s, n), BF16)],
        scratch_shapes=[pltpu.VMEM((tm, d), BF16)],
        compiler_params=_cparams(("parallel", "arbitrary")),
        name="norm_proj",
    )(x, g.reshape(1, d), w)


def _ffn_up_kernel(x_ref, g_ref, wg_ref, wu_ref, o_ref, h_ref):
    @pl.when(pl.program_id(1) == 0)
    def _():
        h_ref[...] = _rms_rows(x_ref[...], g_ref[...]).astype(BF16)

    h = h_ref[...]
    a = _dot(h, wg_ref[...])
    u = _dot(h, wu_ref[...])
    o_ref[...] = (a * jax.nn.sigmoid(a) * u).astype(o_ref.dtype)


def ffn_up(x, g, wg, wu, l, *, tm, tn):
    s, d = x.shape
    f = wg.shape[2]
    return pl.pallas_call(
        _ffn_up_kernel,
        grid=(s // tm, f // tn),
        in_specs=[
            pl.BlockSpec((tm, d), lambda i, j: (i, 0)),
            pl.BlockSpec((1, d), lambda i, j: (0, 0)),
            pl.BlockSpec((None, d, tn), lambda i, j: (l, 0, j)),
            pl.BlockSpec((None, d, tn), lambda i, j: (l, 0, j)),
        ],
        out_specs=pl.BlockSpec((tm, tn), lambda i, j: (i, j)),
        out_shape=jax.ShapeDtypeStruct((s, f), BF16),
        scratch_shapes=[pltpu.VMEM((tm, d), BF16)],
        compiler_params=_cparams(("parallel", "arbitrary")),
        name="ffn_up",
    )(x, g.reshape(1, d), wg, wu)


def _ffn_down_kernel(x_ref, a_ref, w_ref, o_ref):
    o_ref[...] = x_ref[...] + 0.5 * _dot(a_ref[...], w_ref[...])


def ffn_down(x, act, wd, l, *, tm, tn):
    s, d = x.shape
    f = act.shape[1]
    return pl.pallas_call(
        _ffn_down_kernel,
        grid=(s // tm, d // tn),
        in_specs=[
            pl.BlockSpec((tm, tn), lambda i, j: (i, j)),
            pl.BlockSpec((tm, f), lambda i, j: (i, 0)),
            pl.BlockSpec((None, f, tn), lambda i, j: (l, 0, j)),
        ],
        out_specs=pl.BlockSpec((tm, tn), lambda i, j: (i, j)),
        out_shape=jax.ShapeDtypeStruct((s, d), F32),
        compiler_params=_cparams(("parallel", "arbitrary")),
        name="ffn_down",
    )(x, act, wd)


def _out_proj_kernel(x_ref, a_ref, b_ref, c_ref, wa_ref, wb_ref, wc_ref, o_ref):
    acc = _dot(a_ref[...], wa_ref[...])
    acc += _dot(b_ref[...], wb_ref[...])
    acc += _dot(c_ref[...], wc_ref[...])
    o_ref[...] = x_ref[...] + acc


def out_proj(x, y_sb, y_mla, y_rwkv, w_out, l, *, tm, tn):
    s, d = x.shape
    assert SB_WIDTH == MLA_WIDTH and RWKV_WIDTH == SB_WIDTH + MLA_WIDTH
    return pl.pallas_call(
        _out_proj_kernel,
        grid=(s // tm, d // tn),
        in_specs=[
            pl.BlockSpec((tm, tn), lambda i, j: (i, j)),
            pl.BlockSpec((tm, SB_WIDTH), lambda i, j: (i, 0)),
            pl.BlockSpec((tm, MLA_WIDTH), lambda i, j: (i, 0)),
            pl.BlockSpec((tm, RWKV_WIDTH), lambda i, j: (i, 0)),
            pl.BlockSpec((None, SB_WIDTH, tn), lambda i, j: (l, 0, j)),
            pl.BlockSpec((None, MLA_WIDTH, tn), lambda i, j: (l, 1, j)),
            pl.BlockSpec((None, RWKV_WIDTH, tn), lambda i, j: (l, 1, j)),
        ],
        out_specs=pl.BlockSpec((tm, tn), lambda i, j: (i, j)),
        out_shape=jax.ShapeDtypeStruct((s, d), F32),
        compiler_params=_cparams(("parallel", "arbitrary")),
        name="out_proj",
    )(x, y_sb, y_mla, y_rwkv, w_out, w_out, w_out)


def _final_norm_kernel(x_ref, g_ref, o_ref):
    o_ref[...] = _rms_rows(x_ref[...], g_ref[...])


def final_rmsnorm(x, g, *, tm):
    s, d = x.shape
    return pl.pallas_call(
        _final_norm_kernel,
        grid=(s // tm,),
        in_specs=[pl.BlockSpec((tm, d), lambda i: (i, 0)), pl.BlockSpec((1, d), lambda i: (0, 0))],
        out_specs=pl.BlockSpec((tm, d), lambda i: (i, 0)),
        out_shape=jax.ShapeDtypeStruct((s, d), F32),
        compiler_params=_cparams(("parallel",)),
        name="final_norm",
    )(x, g.reshape(1, d))


SB_BQ = 256
SB_BK = 128
SB_DEAD_LOG = -104.0


def _sb_attn_kernel(q_ref, k_ref, v_ref, g_ref, o_ref):
    i = pl.program_id(0)
    bq, bk, dh = SB_BQ, SB_BK, SB_HEAD_DIM
    heads = range(SB_HEADS)
    hs = [slice(h * dh, (h + 1) * dh) for h in heads]
    q = [q_ref[:, hs[h]] for h in heads]
    row = lax.broadcasted_iota(jnp.int32, (bq, bk), 0)
    col = lax.broadcasted_iota(jnp.int32, (bq, bk), 1)
    kr = lax.broadcasted_iota(jnp.int32, (bk, bk), 0)
    kc = lax.broadcasted_iota(jnp.int32, (bk, bk), 1)
    upper = jnp.where(kr > kc, 1.0, 0.0).astype(BF16)

    def block(kb, carry, acc, masked):
        start = pl.multiple_of(kb * bk, bk)
        z = [_dot_nt(q[h], k_ref[pl.ds(start, bk), hs[h]]) for h in heads]
        log_beta = [jnp.minimum(z[h], 0.0) - jnp.log(1.0 + jnp.exp(-jnp.abs(z[h]))) for h in heads]
        log_keep = [log_beta[h] - z[h] for h in heads]
        if masked:
            before = (start + col) < (i * bq + row)
            log_keep = [jnp.where(before, log_keep[h], 0.0) for h in heads]
        split = [_split2(log_keep[h]) for h in heads]
        within = [_dot(split[h][0], upper) + _dot(split[h][1], upper) for h in heads]
        a = [jnp.exp(log_beta[h] + (carry[h] + within[h])) for h in heads]
        if masked:
            a = [jnp.where(before, a[h], 0.0) for h in heads]
        acc = [acc[h] + _dot(a[h].astype(BF16), v_ref[pl.ds(start, bk), hs[h]]) for h in heads]
        carry = [carry[h] + jnp.sum(log_keep[h], axis=-1, keepdims=True) for h in heads]
        return carry, acc

    def alive_of(carry):
        m = jnp.max(carry[0])
        for h in heads[1:]:
            m = jnp.maximum(m, jnp.max(carry[h]))
        return m

    carry = [jnp.zeros((bq, 1), F32) for _ in heads]
    acc = [jnp.zeros((bq, dh), F32) for _ in heads]
    nd = bq // bk
    first = i * nd
    for d in range(nd - 1, -1, -1):
        carry, acc = block(first + d, carry, acc, True)

    def cond(st):
        return jnp.logical_and(st[0] >= 0, st[1] > SB_DEAD_LOG)

    def body(st):
        kb, _, carry, acc = st
        carry, acc = block(kb, list(carry), list(acc), False)
        return kb - 1, alive_of(carry), tuple(carry), tuple(acc)

    st = lax.while_loop(cond, body, (first - 1, alive_of(carry), tuple(carry), tuple(acc)))
    acc = st[3]
    for h in heads:
        o_ref[:, hs[h]] = _rms_rows(acc[h], g_ref[h:h + 1, :]).astype(o_ref.dtype)


def sb_attention(zb, gains):
    s = zb.shape[0]
    whole = lambda c: pl.BlockSpec((s, SB_WIDTH), lambda i: (0, c // SB_WIDTH),
                                   pipeline_mode=pl.Buffered(1))
    return pl.pallas_call(
        _sb_attn_kernel,
        grid=(s // SB_BQ,),
        in_specs=[
            pl.BlockSpec((SB_BQ, SB_WIDTH), lambda i: (i, COL_SBQ // SB_WIDTH)),
            whole(COL_SBK),
            whole(COL_SBV),
            pl.BlockSpec((SB_HEADS, SB_HEAD_DIM), lambda i: (0, 0)),
        ],
        out_specs=pl.BlockSpec((SB_BQ, SB_WIDTH), lambda i: (i, 0)),
        out_shape=jax.ShapeDtypeStruct((s, SB_WIDTH), BF16),
        compiler_params=_cparams(("arbitrary",)),
        name="sb_attention",
    )(zb, zb, zb, gains)


def _mla_prep_kernel(cq_ref, ckv_ref, rope_ref, tab_ref, qn_ref, kvn_ref, wq_ref, wkv_ref,
                     q_ref, k_ref, v_ref):
    scale = (MLA_NOPE_DIM + MLA_ROPE_DIM) ** -0.5
    hq = _rms_rows(cq_ref[...], qn_ref[...]).astype(BF16)
    hkv = _rms_rows(ckv_ref[...], kvn_ref[...]).astype(BF16)
    cos2 = tab_ref[:, :128]
    sin2 = tab_ref[:, 128:]
    kr = rope_ref[...]
    k_pe = kr * cos2 + pltpu.roll(kr, 64, 1) * sin2
    k_pe = k_pe.astype(BF16)
    qall = _dot(hq, wq_ref[...])
    kvall = _dot(hkv, wkv_ref[...])
    for h in range(MLA_HEADS):
        qh = qall[:, h * 384:(h + 1) * 384]
        q_ref[h, :, :128] = (qh[:, :128] * scale).astype(BF16)
        q_ref[h, :, 128:] = ((qh[:, 128:256] * cos2 + qh[:, 256:384] * sin2) * scale).astype(BF16)
        k_ref[h, :, :128] = kvall[:, h * 256:h * 256 + 128].astype(BF16)
        k_ref[h, :, 128:] = k_pe
        v_ref[h, :, :128] = kvall[:, h * 256 + 128:(h + 1) * 256].astype(BF16)
        v_ref[h, :, 128:] = jnp.ones((kr.shape[0], 128), BF16)


def mla_prep(z, tab, q_norm, kv_norm, wq, wkv, *, ts):
    s = z.shape[0]
    return pl.pallas_call(
        _mla_prep_kernel,
        grid=(s // ts,),
        in_specs=[
            pl.BlockSpec((ts, MLA_Q_LORA), lambda i: (i, COL_CQ // MLA_Q_LORA)),
            pl.BlockSpec((ts, MLA_KV_LORA), lambda i: (i, COL_CKV // MLA_KV_LORA)),
            pl.BlockSpec((ts, 128), lambda i: (i, COL_ROPE // 128)),
            pl.BlockSpec((ts, 256), lambda i: (i, 0)),
            pl.BlockSpec((1, MLA_Q_LORA), lambda i: (0, 0)),
            pl.BlockSpec((1, MLA_KV_LORA), lambda i: (0, 0)),
            pl.BlockSpec(wq.shape, lambda i: (0, 0)),
            pl.BlockSpec(wkv.shape, lambda i: (0, 0)),
        ],
        out_specs=[
            pl.BlockSpec((MLA_HEADS, ts, MLA_QK_PAD), lambda i: (0, i, 0)),
            pl.BlockSpec((MLA_HEADS, ts, MLA_QK_PAD), lambda i: (0, i, 0)),
            pl.BlockSpec((MLA_HEADS, ts, 2 * MLA_V_DIM), lambda i: (0, i, 0)),
        ],
        out_shape=[
            jax.ShapeDtypeStruct((MLA_HEADS, s, MLA_QK_PAD), BF16),
            jax.ShapeDtypeStruct((MLA_HEADS, s, MLA_QK_PAD), BF16),
            jax.ShapeDtypeStruct((MLA_HEADS, s, 2 * MLA_V_DIM), BF16),
        ],
        compiler_params=_cparams(("parallel",)),
        name="mla_prep",
    )(z, z, z, tab, q_norm.reshape(1, -1), kv_norm.reshape(1, -1), wq, wkv)


MLA_BQ = 256
MLA_BK = 512
MLA_HEADS_PER_STEP = 4


def _mla_attn_kernel(q_ref, k_ref, v_ref, g_ref, o_ref):
    i = pl.program_id(1)
    bq, bk = MLA_BQ, MLA_BK
    heads = range(MLA_HEADS_PER_STEP)
    q = [q_ref[h] for h in heads]
    nd = bk // bq

    def block(kb, m, acc, masked):
        start = pl.multiple_of(kb * bk, bk)
        sc = [_dot_nt(q[h], k_ref[h, pl.ds(start, bk), :]) for h in heads]
        if masked:
            row = lax.broadcasted_iota(jnp.int32, (bq, bk), 0) + i * bq
            col = lax.broadcasted_iota(jnp.int32, (bq, bk), 1) + start
            sc = [jnp.where(col <= row, sc[h], -1e30) for h in heads]
        m_new = [jnp.maximum(m[h], jnp.max(sc[h], axis=-1, keepdims=True)) for h in heads]
        for h in heads:
            alpha = jnp.exp(m[h] - m_new[h])
            p = jnp.exp(sc[h] - m_new[h])
            acc[h] = acc[h] * alpha + _dot(p.astype(BF16), v_ref[h, pl.ds(start, bk), :])
        return m_new, acc

    def body(kb, st):
        m, acc = block(kb, list(st[0]), list(st[1]), False)
        return tuple(m), tuple(acc)

    m0 = tuple(jnp.full((bq, 1), -1e30, F32) for _ in heads)
    a0 = tuple(jnp.zeros((bq, 2 * MLA_V_DIM), F32) for _ in heads)
    last = i // nd
    m, acc = lax.fori_loop(0, last, body, (m0, a0))
    m, acc = block(last, list(m), list(acc), True)
    for h in heads:
        y = acc[h][:, :MLA_V_DIM] / acc[h][:, MLA_V_DIM:]
        o_ref[:, h * MLA_V_DIM:(h + 1) * MLA_V_DIM] = _rms_rows(y, g_ref[h:h + 1, :]).astype(o_ref.dtype)


def mla_attention(q, k, v, gains):
    s = q.shape[1]
    hps = MLA_HEADS_PER_STEP
    whole = lambda n: pl.BlockSpec((hps, s, n), lambda g, i: (g, 0, 0), pipeline_mode=pl.Buffered(1))
    return pl.pallas_call(
        _mla_attn_kernel,
        grid=(MLA_HEADS // hps, s // MLA_BQ),
        in_specs=[
            pl.BlockSpec((hps, MLA_BQ, MLA_QK_PAD), lambda g, i: (g, i, 0)),
            whole(MLA_QK_PAD),
            whole(2 * MLA_V_DIM),
            pl.BlockSpec((hps, MLA_V_DIM), lambda g, i: (g, 0)),
        ],
        out_specs=pl.BlockSpec((MLA_BQ, hps * MLA_V_DIM), lambda g, i: (i, g)),
        out_shape=jax.ShapeDtypeStruct((s, MLA_WIDTH), BF16),
        compiler_params=_cparams(("parallel", "arbitrary")),
        name="mla_attention",
    )(q, k, v, gains)


def _head_sum_matrix():
    r = lax.broadcasted_iota(jnp.int32, (128, 128), 0) // RWKV_HEAD_DIM
    c = lax.broadcasted_iota(jnp.int32, (128, 128), 1) // RWKV_HEAD_DIM
    return jnp.where(r == c, 1.0, 0.0).astype(BF16)


def _head_sums(x, ones_bd):
    parts = []
    for j in range(x.shape[1] // 128):
        parts.append(_exact_lhs_dot_rhs(x[:, j * 128:(j + 1) * 128], ones_bd))
    return jnp.concatenate(parts, axis=-1)


def _exact_lhs_dot_rhs(x, m_bf16):
    hi, mid, lo = _split3(x)
    return _dot(hi, m_bf16) + (_dot(mid, m_bf16) + _dot(lo, m_bf16))


def _rwkv_prep_kernel(r_ref, k_ref, v_ref, lora_ref, xg_ref, mu_ref, mul_ref, mug_ref,
                      w0_ref, w2_ref, a0_ref, a2_ref, g2_ref, kk_ref, ka_ref,
                      ro_ref, lw_ref, ko_ref, vo_ref, kko_ref, bo_ref, go_ref,
                      prev_ref, prevl_ref):
    ts = r_ref.shape[0]

    @pl.when(pl.program_id(0) == 0)
    def _():
        prev_ref[...] = jnp.zeros_like(prev_ref)
        prevl_ref[...] = jnp.zeros_like(prevl_ref)

    first_row = lax.broadcasted_iota(jnp.int32, (ts, 1), 0) == 0

    def shift_mix(x, prev_row, mu):
        x_prev = jnp.where(first_row, prev_row, pltpu.roll(x, 1, 0))
        return x + (x_prev - x) * mu

    r_in, k_in, v_in = r_ref[...], k_ref[...], v_ref[...]
    lora_in, xg_in = lora_ref[...], xg_ref[...]
    r = shift_mix(r_in, prev_ref[0:1, :], mu_ref[0:1, :])
    k = shift_mix(k_in, prev_ref[1:2, :], mu_ref[1:2, :])
    v = shift_mix(v_in, prev_ref[2:3, :], mu_ref[2:3, :])
    lora = shift_mix(lora_in, prevl_ref[0:1, :], mul_ref[...])
    xg = shift_mix(xg_in, prevl_ref[1:2, :], mug_ref[...])
    prev_ref[0:1, :] = r_in[ts - 1:ts, :]
    prev_ref[1:2, :] = k_in[ts - 1:ts, :]
    prev_ref[2:3, :] = v_in[ts - 1:ts, :]
    prevl_ref[0:1, :] = lora_in[ts - 1:ts, :]
    prevl_ref[1:2, :] = xg_in[ts - 1:ts, :]

    dw = w0_ref[...] + _dot(jnp.tanh(lora).astype(BF16), w2_ref[...])
    log_w = -jax.nn.softplus(-dw) - 0.5
    lw_ref[...] = -jnp.exp(log_w)
    a = jax.nn.sigmoid(a0_ref[...] + _dot(lora.astype(BF16), a2_ref[...]))
    go_ref[...] = _dot(jax.nn.sigmoid(xg).astype(BF16), g2_ref[...])

    kk = k * kk_ref[...]
    ss = _head_sums(kk * kk, _head_sum_matrix())
    kk = kk / jnp.maximum(jnp.sqrt(ss), 1e-12)
    ro_ref[...] = r
    vo_ref[...] = v
    ko_ref[...] = k * (1.0 + (a - 1.0) * ka_ref[...])
    kko_ref[...] = kk
    bo_ref[...] = kk * a


def rwkv_prep(z, p, *, ts):
    s = z.shape[0]
    w = RWKV_WIDTH
    row = lambda n: pl.BlockSpec((1, n), lambda i: (0, 0))
    big = pl.BlockSpec((ts, w), lambda i: (i, 0))
    out = jax.ShapeDtypeStruct((s, w), F32)
    return pl.pallas_call(
        _rwkv_prep_kernel,
        grid=(s // ts,),
        in_specs=[
            pl.BlockSpec((ts, w), lambda i: (i, COL_R // w)),
            pl.BlockSpec((ts, w), lambda i: (i, COL_K // w)),
            pl.BlockSpec((ts, w), lambda i: (i, COL_V // w)),
            pl.BlockSpec((ts, 128), lambda i: (i, COL_LORA // 128)),
            pl.BlockSpec((ts, 128), lambda i: (i, COL_XG // 128)),
            pl.BlockSpec((3, w), lambda i: (0, 0)),
            row(128), row(128),
            row(w), pl.BlockSpec((128, w), lambda i: (0, 0)),
            row(w), pl.BlockSpec((128, w), lambda i: (0, 0)),
            pl.BlockSpec((128, w), lambda i: (0, 0)),
            row(w), row(w),
        ],
        out_specs=[big] * 7,
        out_shape=[out] * 7,
        scratch_shapes=[pltpu.VMEM((8, w), F32), pltpu.VMEM((8, 128), F32)],
        compiler_params=_cparams(("arbitrary",)),
        name="rwkv_prep",
    )(z, z, z, z, z, p["mu_rkv"], p["mu_lora"], p["mu_g"], p["w0"], p["w2"], p["a0"], p["a2"],
      p["g2"], p["k_k"], p["k_a"])


def _rwkv_chunk_kernel(r_ref, lw_ref, k_ref, v_ref, kk_ref, b_ref, g_ref, rk_ref, lnw_ref, lnb_ref,
                       o_ref, st_ref, y_ref):
    c = RWKV_CHUNK
    n = RWKV_HEAD_DIM
    nh = r_ref.shape[1] // n

    @pl.when(pl.program_id(1) == 0)
    def _():
        st_ref[...] = jnp.zeros_like(st_ref)

    ri = lax.broadcasted_iota(jnp.int32, (c, c), 0)
    ci = lax.broadcasted_iota(jnp.int32, (c, c), 1)
    strict = ri > ci
    incl = ri >= ci
    eye = jnp.where(ri == ci, 1.0, 0.0).astype(F32)
    tril_ones = jnp.where(incl, 1.0, 0.0).astype(BF16)
    d16 = (ri // 16) == (ci // 16)
    d32 = (ri // 32) == (ci // 32)

    r, lw, k, v = r_ref[...], lw_ref[...], k_ref[...], v_ref[...]
    kk, b = kk_ref[...], b_ref[...]
    cum = _exact_lhs_dot(tril_ones, lw)
    p_incl = jnp.exp(cum)
    p_excl = jnp.exp(cum - lw)
    p_inv = jnp.exp(-cum)
    p_end = p_incl[c - 1:c, :]
    r_t = r * p_incl
    kk_t = kk * p_excl
    b_t = b * p_inv
    k_t = k * p_inv
    b_d = b_t * p_end
    k_d = k_t * p_end

    lhs_all = jnp.concatenate([kk_t, r_t], axis=0)
    rhs_all = jnp.concatenate([b_t, k_t], axis=0)
    dec_all = jnp.concatenate([b_d, k_d], axis=0)
    ri2 = lax.broadcasted_iota(jnp.int32, (2 * c, 2 * c), 0)
    ci2 = lax.broadcasted_iota(jnp.int32, (2 * c, 2 * c), 1) % c
    causal2 = ci2 <= jnp.where(ri2 < c, ri2 - 1, ri2 - c)
    blk_lo = jnp.logical_and(d32, jnp.logical_not(d16))

    heads = range(nh)
    sls = [slice(h * n, (h + 1) * n) for h in heads]
    mm_s = functools.partial(_mm, passes=RWKV_PASSES_S)
    mm_t = functools.partial(_mm, _dot, passes=RWKV_PASSES_T)
    a = [jnp.where(causal2, _mm(_dot_nt, lhs_all[:, sl], rhs_all[:, sl], RWKV_PASSES_A), 0.0)
         for sl in sls]
    st = [st_ref[h] for h in heads]
    part = [mm_s(_dot_nt, lhs_all[:, sls[h]], st[h]) + mm_s(_dot, a[h][:, c:], v[:, sls[h]])
            for h in heads]
    a_ub = [a[h][:c, :c] for h in heads]
    ld = [jnp.where(d16, a_ub[h], 0.0) for h in heads]
    x = [eye - ld[h] for h in heads]
    pw = [mm_t(ld[h], ld[h]) for h in heads]
    x = [x[h] + mm_t(x[h], pw[h]) for h in heads]
    pw = [mm_t(pw[h], pw[h]) for h in heads]
    x = [x[h] + mm_t(x[h], pw[h]) for h in heads]
    pw = [mm_t(pw[h], pw[h]) for h in heads]
    x = [x[h] + mm_t(x[h], pw[h]) for h in heads]
    t = [mm_t(x[h], jnp.where(blk_lo, a_ub[h], 0.0)) for h in heads]
    x = [x[h] - mm_t(t[h], x[h]) for h in heads]
    t = [mm_t(x[h], jnp.where(d32, 0.0, a_ub[h])) for h in heads]
    x = [x[h] - mm_t(t[h], x[h]) for h in heads]

    u = [-mm_s(_dot, x[h], part[h][:c]) for h in heads]
    for h in heads:
        y_ref[:, sls[h]] = part[h][c:] + mm_s(_dot, a[h][c:, :c], u[h])
    for h in heads:
        uv = jnp.concatenate([u[h], v[:, sls[h]]], axis=0)
        st_ref[h] = st[h] * p_end[:, sls[h]] + mm_s(_dot_tn, uv, dec_all[:, sls[h]])

    ones_bd = _head_sum_matrix()
    y = y_ref[...]
    mean = _head_sums(y, ones_bd) * (1.0 / n)
    yc = y - mean
    var = _head_sums(yc * yc, ones_bd) * (1.0 / n)
    yn = yc * lax.rsqrt(var + RWKV_GN_EPS) * lnw_ref[...] + lnb_ref[...]
    bonus = _head_sums(r * k * rk_ref[...], ones_bd)
    o_ref[...] = ((yn + bonus * v) * g_ref[...]).astype(o_ref.dtype)


def rwkv_chunk(r, lw, k, v, kk, b, g, p, *, heads_per_step):
    s = r.shape[0]
    c = RWKV_CHUNK
    wb = heads_per_step * RWKV_HEAD_DIM
    big = pl.BlockSpec((c, wb), lambda hg, t: (t, hg))
    row = pl.BlockSpec((1, wb), lambda hg, t: (0, hg))
    return pl.pallas_call(
        _rwkv_chunk_kernel,
        grid=(RWKV_WIDTH // wb, s // c),
        in_specs=[big] * 7 + [row] * 3,
        out_specs=big,
        out_shape=jax.ShapeDtypeStruct((s, RWKV_WIDTH), BF16),
        scratch_shapes=[
            pltpu.VMEM((heads_per_step, RWKV_HEAD_DIM, RWKV_HEAD_DIM), F32),
            pltpu.VMEM((c, wb), F32),
        ],
        compiler_params=_cparams(("parallel", "arbitrary")),
        name="rwkv_chunk",
    )(r, lw, k, v, kk, b, g, p["r_k"], p["ln_w"], p["ln_b"])


def _swap_halves(w):
    half = w.shape[-1] // 2
    return jnp.concatenate([w[..., half:], w[..., :half]], axis=-1)


def _prep_w_in(w_in):
    sizes = [SB_WIDTH] * 3 + [MLA_Q_LORA, MLA_KV_LORA, MLA_ROPE_DIM] + [RWKV_WIDTH] * 3 + [
        RWKV_DECAY_LORA, RWKV_A_LORA, RWKV_GATE_LORA]
    idx = [int(i) for i in np.cumsum(sizes)[:-1]]
    sbq, sbk, sbv, cq, ckv, krope, r, k, v, xw, xa, xg = jnp.split(w_in, idx, axis=-1)
    sbq = sbq * (SB_HEAD_DIM ** -0.5)
    cols = [r, k, v, sbq, sbk, sbv, cq, ckv, xw, xa, xg, krope, _swap_halves(krope)]
    return jnp.concatenate(cols, axis=-1).astype(BF16)


def _prep_mla_w(w_uq, w_ukv):
    q = w_uq.reshape(MLA_Q_LORA, MLA_HEADS, MLA_NOPE_DIM + MLA_ROPE_DIM)
    nope, pe = q[..., :MLA_NOPE_DIM], q[..., MLA_NOPE_DIM:]
    zpad = jnp.zeros_like(pe)
    wq = jnp.concatenate([nope, pe, zpad, _swap_halves(pe), zpad], axis=-1)
    return wq.reshape(MLA_Q_LORA, MLA_HEADS * 384).astype(BF16), w_ukv.astype(BF16)


def _rope_table(positions):
    half = MLA_ROPE_DIM // 2
    inv_freq = ROPE_THETA ** (-jnp.arange(half, dtype=F32) / half)
    ang = positions.astype(F32)[:, None] * inv_freq
    cos, sin = jnp.cos(ang), jnp.sin(ang)
    z = jnp.zeros((positions.shape[0], MLA_ROPE_DIM), F32)
    return jnp.concatenate([cos, cos, z, -sin, sin, z], axis=-1)


def _pick_tile(n, candidates):
    for c in candidates:
        if n % c == 0:
            return c
    raise ValueError(f"no tile for {n}")


def kernel(x, positions, ffn1_norm, ffn1_gate, ffn1_up, ffn1_down, mix_norm, w_in, mla_q_norm,
           mla_w_uq, mla_kv_norm, mla_w_ukv, rwkv_mu, rwkv_w0, rwkv_w2, rwkv_a0, rwkv_a2, rwkv_g2,
           rwkv_k_k, rwkv_k_a, rwkv_r_k, rwkv_ln_w, rwkv_ln_b, sb_out_norm, mla_out_norm, w_out,
           ffn2_norm, ffn2_gate, ffn2_up, ffn2_down, final_norm):
    bsz, s, d = x.shape
    assert bsz == 1 and d == SB_WIDTH + MLA_WIDTH + RWKV_WIDTH
    depth = w_in.shape[0]
    d_ff = ffn1_gate.shape[-1]
    tm = _pick_tile(s, (1024, 512, 256, 128))
    tf = _pick_tile(d_ff, (512, 256, 128))
    x = x[0]
    tab = _rope_table(positions[0])
    w = RWKV_WIDTH
    zl = jnp.zeros((RWKV_DECAY_LORA, w), F32)
    ffn_w = [tuple(t.astype(BF16) for t in ws)
             for ws in ((ffn1_gate, ffn1_up, ffn1_down), (ffn2_gate, ffn2_up, ffn2_down))]
    w_in_b = _prep_w_in(w_in)
    w_out_b = w_out.astype(BF16)

    def ffn(x, g, ws, l):
        act = ffn_up(x, g, ws[0], ws[1], l, tm=tm, tn=tf)
        return ffn_down(x, act, ws[2], l, tm=tm, tn=256)

    for l in range(depth):
        x = ffn(x, ffn1_norm[l], ffn_w[0], l)

        z, zb = norm_proj(x, mix_norm[l], w_in_b, l, tm=tm, tn=640)

        y_sb = sb_attention(zb, sb_out_norm[l])

        wq, wkv = _prep_mla_w(mla_w_uq[l], mla_w_ukv[l])
        q, k, v = mla_prep(z, tab, mla_q_norm[l], mla_kv_norm[l], wq, wkv, ts=_pick_tile(s, (256, 128)))
        y_mla = mla_attention(q, k, v, mla_out_norm[l])

        mu = rwkv_mu[l]
        p = {
            "mu_rkv": mu[:3 * w].reshape(3, w),
            "mu_lora": mu[3 * w:3 * w + 128].reshape(1, 128),
            "mu_g": mu[3 * w + 128:].reshape(1, 128),
            "w0": rwkv_w0[l].reshape(1, w),
            "w2": jnp.concatenate([rwkv_w2[l], zl], axis=0).astype(BF16),
            "a0": rwkv_a0[l].reshape(1, w),
            "a2": jnp.concatenate([zl, rwkv_a2[l]], axis=0).astype(BF16),
            "g2": rwkv_g2[l].astype(BF16),
            "k_k": rwkv_k_k[l].reshape(1, w),
            "k_a": rwkv_k_a[l].reshape(1, w),
            "r_k": rwkv_r_k[l].reshape(1, w),
            "ln_w": rwkv_ln_w[l].reshape(1, w),
            "ln_b": rwkv_ln_b[l].reshape(1, w),
        }
        r_, lw_, k_, v_, kk_, b_, g_ = rwkv_prep(z, p, ts=_pick_tile(s, (256, 128)))
        y_rwkv = rwkv_chunk(r_, lw_, k_, v_, kk_, b_, g_, p, heads_per_step=16)

        x = out_proj(x, y_sb, y_mla, y_rwkv, w_out_b, l, tm=tm, tn=512)

        x = ffn(x, ffn2_norm[l], ffn_w[1], l)

    return final_rmsnorm(x, final_norm, tm=_pick_tile(s, (256, 128)))[None]
```

```python
import functools

import jax
import jax.numpy as jnp
import numpy as np
from jax import lax
from jax.experimental import pallas as pl
from jax.experimental.pallas import tpu as pltpu

F32 = jnp.float32
BF16 = jnp.bfloat16

NORM_EPS = 1e-6
SB_HEADS = 4
SB_HEAD_DIM = 128
SB_WIDTH = SB_HEADS * SB_HEAD_DIM
MLA_HEADS = 4
MLA_NOPE_DIM = 128
MLA_ROPE_DIM = 64
MLA_V_DIM = 128
MLA_Q_LORA = 512
MLA_KV_LORA = 256
MLA_WIDTH = MLA_HEADS * MLA_V_DIM
MLA_QK_PAD = 256
ROPE_THETA = 10000.0
RWKV_HEADS = 16
RWKV_HEAD_DIM = 64
RWKV_WIDTH = RWKV_HEADS * RWKV_HEAD_DIM
RWKV_DECAY_LORA = 64
RWKV_A_LORA = 64
RWKV_GATE_LORA = 128
RWKV_GN_EPS = 64e-5
RWKV_CHUNK = 64
RWKV_PASSES_A = 1
RWKV_PASSES_T = 1
RWKV_PASSES_S = 1

COL_R, COL_K, COL_V = 0, 1024, 2048
COL_SBQ, COL_SBK, COL_SBV = 3072, 3584, 4096
COL_CQ, COL_CKV = 4608, 5120
COL_LORA, COL_XG, COL_ROPE = 5376, 5504, 5632
Z_COLS = 5760

VMEM_LIMIT = 48 * 1024 * 1024


def _cparams(sem):
    return pltpu.CompilerParams(dimension_semantics=sem, vmem_limit_bytes=VMEM_LIMIT)


def _dot(a, b):
    return jnp.dot(a, b, preferred_element_type=F32)


def _dot_nt(a, b):
    return lax.dot_general(a, b, (((1,), (1,)), ((), ())), preferred_element_type=F32)


def _dot_tn(a, b):
    return lax.dot_general(a, b, (((0,), (0,)), ((), ())), preferred_element_type=F32)


def _split3(x):
    hi = x.astype(BF16)
    r1 = x - hi.astype(F32)
    mid = r1.astype(BF16)
    lo = (r1 - mid.astype(F32)).astype(BF16)
    return hi, mid, lo


def _split2(x):
    hi = x.astype(BF16)
    lo = (x - hi.astype(F32)).astype(BF16)
    return hi, lo


def _mm3(dot, a, b):
    ah, al = _split2(a)
    bh, bl = _split2(b)
    return dot(ah, bh) + (dot(ah, bl) + dot(al, bh))


def _mm(dot, a, b, passes):
    if passes == 1:
        return dot(a.astype(BF16), b.astype(BF16))
    return _mm3(dot, a, b)


def _exact_lhs_dot(m_bf16, x):
    hi, mid, lo = _split3(x)
    return _dot(m_bf16, hi) + (_dot(m_bf16, mid) + _dot(m_bf16, lo))


def _rms_rows(x, g):
    ms = jnp.mean(x * x, axis=-1, keepdims=True)
    return x * lax.rsqrt(ms + NORM_EPS) * g


def _norm_proj_kernel(x_ref, g_ref, w_ref, o_ref, ob_ref, h_ref):
    @pl.when(pl.program_id(1) == 0)
    def _():
        h_ref[...] = _rms_rows(x_ref[...], g_ref[...]).astype(BF16)

    z = _dot(h_ref[...], w_ref[...])
    o_ref[...] = z
    ob_ref[...] = z.astype(BF16)


def norm_proj(x, g, w, l, *, tm, tn):
    s, d = x.shape
    n = w.shape[2]
    return pl.pallas_call(
        _norm_proj_kernel,
        grid=(s // tm, n // tn),
        in_specs=[
            pl.BlockSpec((tm, d), lambda i, j: (i, 0)),
            pl.BlockSpec((1, d), lambda i, j: (0, 0)),
            pl.BlockSpec((None, d, tn), lambda i, j: (l, 0, j)),
        ],
        out_specs=[pl.BlockSpec((tm, tn), lambda i, j: (i, j))] * 2,
        out_shape=[jax.ShapeDtypeStruct((s, n), F32), jax.ShapeDtypeStruct((s, n), BF16)],
        scratch_shapes=[pltpu.VMEM((tm, d), BF16)],
        compiler_params=_cparams(("parallel", "arbitrary")),
        name="norm_proj",
    )(x, g.reshape(1, d), w)


def _ffn_up_kernel(x_ref, g_ref, wg_ref, wu_ref, o_ref, h_ref):
    @pl.when(pl.program_id(1) == 0)
    def _():
        h_ref[...] = _rms_rows(x_ref[...], g_ref[...]).astype(BF16)

    h = h_ref[...]
    a = _dot(h, wg_ref[...])
    u = _dot(h, wu_ref[...])
    o_ref[...] = (a * jax.nn.sigmoid(a) * u).astype(o_ref.dtype)


def _col_tiled(w, tn):
    nl, k, n = w.shape
    return w.reshape(nl, k, n // tn, tn).transpose(0, 2, 1, 3)


def ffn_up(x, g, wg, wu, l, *, tm):
    s, d = x.shape
    _, nt, _, tn = wg.shape
    f = nt * tn
    return pl.pallas_call(
        _ffn_up_kernel,
        grid=(s // tm, nt),
        in_specs=[
            pl.BlockSpec((tm, d), lambda i, j: (i, 0)),
            pl.BlockSpec((1, d), lambda i, j: (0, 0)),
            pl.BlockSpec((None, None, d, tn), lambda i, j: (l, j, 0, 0)),
            pl.BlockSpec((None, None, d, tn), lambda i, j: (l, j, 0, 0)),
        ],
        out_specs=pl.BlockSpec((tm, tn), lambda i, j: (i, j)),
        out_shape=jax.ShapeDtypeStruct((s, f), BF16),
        scratch_shapes=[pltpu.VMEM((tm, d), BF16)],
        compiler_params=_cparams(("parallel", "arbitrary")),
        name="ffn_up",
    )(x, g.reshape(1, d), wg, wu)


def _ffn_down_kernel(x_ref, a_ref, w_ref, o_ref):
    o_ref[...] = x_ref[...] + 0.5 * _dot(a_ref[...], w_ref[...])


def ffn_down(x, act, wd, l, *, tm):
    s, d = x.shape
    f = act.shape[1]
    tn = wd.shape[3]
    return pl.pallas_call(
        _ffn_down_kernel,
        grid=(s // tm, d // tn),
        in_specs=[
            pl.BlockSpec((tm, tn), lambda i, j: (i, j)),
            pl.BlockSpec((tm, f), lambda i, j: (i, 0)),
            pl.BlockSpec((None, None, f, tn), lambda i, j: (l, j, 0, 0)),
        ],
        out_specs=pl.BlockSpec((tm, tn), lambda i, j: (i, j)),
        out_shape=jax.ShapeDtypeStruct((s, d), F32),
        compiler_params=_cparams(("parallel", "arbitrary")),
        name="ffn_down",
    )(x, act, wd)


def _out_proj_kernel(x_ref, a_ref, b_ref, c_ref, wa_ref, wb_ref, wc_ref, o_ref):
    acc = _dot(a_ref[...], wa_ref[...])
    acc += _dot(b_ref[...], wb_ref[...])
    acc += _dot(c_ref[...], wc_ref[...])
    o_ref[...] = x_ref[...] + acc


def out_proj(x, y_sb, y_mla, y_rwkv, w_out, l, *, tm):
    s, d = x.shape
    assert SB_WIDTH == MLA_WIDTH and RWKV_WIDTH == SB_WIDTH + MLA_WIDTH
    wspec = lambda rows, blk: pl.BlockSpec((None, rows, d), lambda i: (l, blk, 0),
                                           pipeline_mode=pl.Buffered(1))
    return pl.pallas_call(
        _out_proj_kernel,
        grid=(s // tm,),
        in_specs=[
            pl.BlockSpec((tm, d), lambda i: (i, 0)),
            pl.BlockSpec((tm, SB_WIDTH), lambda i: (i, 0)),
            pl.BlockSpec((tm, MLA_WIDTH), lambda i: (i, 0)),
            pl.BlockSpec((tm, RWKV_WIDTH), lambda i: (i, 0)),
            wspec(SB_WIDTH, 0),
            wspec(MLA_WIDTH, 1),
            wspec(RWKV_WIDTH, 1),
        ],
        out_specs=pl.BlockSpec((tm, d), lambda i: (i, 0)),
        out_shape=jax.ShapeDtypeStruct((s, d), F32),
        compiler_params=_cparams(("parallel",)),
        name="out_proj",
    )(x, y_sb, y_mla, y_rwkv, w_out, w_out, w_out)


def _final_norm_kernel(x_ref, g_ref, o_ref):
    o_ref[...] = _rms_rows(x_ref[...], g_ref[...])


def final_rmsnorm(x, g, *, tm):
    s, d = x.shape
    return pl.pallas_call(
        _final_norm_kernel,
        grid=(s // tm,),
        in_specs=[pl.BlockSpec((tm, d), lambda i: (i, 0)), pl.BlockSpec((1, d), lambda i: (0, 0))],
        out_specs=pl.BlockSpec((tm, d), lambda i: (i, 0)),
        out_shape=jax.ShapeDtypeStruct((s, d), F32),
        compiler_params=_cparams(("parallel",)),
        name="final_norm",
    )(x, g.reshape(1, d))


SB_BQ = 256
SB_BK = 128
SB_DEAD_LOG = -104.0


def _sb_attn_kernel(q_ref, k_ref, v_ref, g_ref, o_ref):
    i = pl.program_id(0)
    bq, bk, dh = SB_BQ, SB_BK, SB_HEAD_DIM
    heads = range(SB_HEADS)
    hs = [slice(h * dh, (h + 1) * dh) for h in heads]
    q = [q_ref[:, hs[h]] for h in heads]
    row = lax.broadcasted_iota(jnp.int32, (bq, bk), 0)
    col = lax.broadcasted_iota(jnp.int32, (bq, bk), 1)
    kr = lax.broadcasted_iota(jnp.int32, (bk, bk), 0)
    kc = lax.broadcasted_iota(jnp.int32, (bk, bk), 1)
    upper = jnp.where(kr > kc, 1.0, 0.0).astype(BF16)

    def block(kb, carry, acc, masked):
        start = pl.multiple_of(kb * bk, bk)
        z = [_dot_nt(q[h], k_ref[pl.ds(start, bk), hs[h]]) for h in heads]
        log_beta = [jnp.minimum(z[h], 0.0) - jnp.log(1.0 + jnp.exp(-jnp.abs(z[h]))) for h in heads]
        log_keep = [log_beta[h] - z[h] for h in heads]
        if masked:
            before = (start + col) < (i * bq + row)
            log_keep = [jnp.where(before, log_keep[h], 0.0) for h in heads]
        split = [_split2(log_keep[h]) for h in heads]
        within = [_dot(split[h][0], upper) + _dot(split[h][1], upper) for h in heads]
        a = [jnp.exp(log_beta[h] + (carry[h] + within[h])) for h in heads]
        if masked:
            a = [jnp.where(before, a[h], 0.0) for h in heads]
        acc = [acc[h] + _dot(a[h].astype(BF16), v_ref[pl.ds(start, bk), hs[h]]) for h in heads]
        carry = [carry[h] + jnp.sum(log_keep[h], axis=-1, keepdims=True) for h in heads]
        return carry, acc

    def alive_of(carry):
        m = jnp.max(carry[0])
        for h in heads[1:]:
            m = jnp.maximum(m, jnp.max(carry[h]))
        return m

    carry = [jnp.zeros((bq, 1), F32) for _ in heads]
    acc = [jnp.zeros((bq, dh), F32) for _ in heads]
    nd = bq // bk
    first = i * nd
    for d in range(nd - 1, -1, -1):
        carry, acc = block(first + d, carry, acc, True)

    def cond(st):
        return jnp.logical_and(st[0] >= 0, st[1] > SB_DEAD_LOG)

    def body(st):
        kb, _, carry, acc = st
        carry, acc = block(kb, list(carry), list(acc), False)
        return kb - 1, alive_of(carry), tuple(carry), tuple(acc)

    st = lax.while_loop(cond, body, (first - 1, alive_of(carry), tuple(carry), tuple(acc)))
    acc = st[3]
    for h in heads:
        o_ref[:, hs[h]] = _rms_rows(acc[h], g_ref[h:h + 1, :]).astype(o_ref.dtype)


def sb_attention(zb, gains):
    s = zb.shape[0]
    whole = lambda c: pl.BlockSpec((s, SB_WIDTH), lambda i: (0, c // SB_WIDTH),
                                   pipeline_mode=pl.Buffered(1))
    return pl.pallas_call(
        _sb_attn_kernel,
        grid=(s // SB_BQ,),
        in_specs=[
            pl.BlockSpec((SB_BQ, SB_WIDTH), lambda i: (i, COL_SBQ // SB_WIDTH)),
            whole(COL_SBK),
            whole(COL_SBV),
            pl.BlockSpec((SB_HEADS, SB_HEAD_DIM), lambda i: (0, 0)),
        ],
        out_specs=pl.BlockSpec((SB_BQ, SB_WIDTH), lambda i: (i, 0)),
        out_shape=jax.ShapeDtypeStruct((s, SB_WIDTH), BF16),
        compiler_params=_cparams(("arbitrary",)),
        name="sb_attention",
    )(zb, zb, zb, gains)


def _mla_prep_kernel(cq_ref, ckv_ref, rope_ref, tab_ref, qn_ref, kvn_ref, wq_ref, wkv_ref,
                     q_ref, k_ref, v_ref):
    scale = (MLA_NOPE_DIM + MLA_ROPE_DIM) ** -0.5
    hq = _rms_rows(cq_ref[...], qn_ref[...]).astype(BF16)
    hkv = _rms_rows(ckv_ref[...], kvn_ref[...]).astype(BF16)
    cos2 = tab_ref[:, :128]
    sin2 = tab_ref[:, 128:]
    kr = rope_ref[...]
    k_pe = kr * cos2 + pltpu.roll(kr, 64, 1) * sin2
    k_pe = k_pe.astype(BF16)
    qall = _dot(hq, wq_ref[...])
    kvall = _dot(hkv, wkv_ref[...])
    for h in range(MLA_HEADS):
        qh = qall[:, h * 384:(h + 1) * 384]
        q_ref[h, :, :128] = (qh[:, :128] * scale).astype(BF16)
        q_ref[h, :, 128:] = ((qh[:, 128:256] * cos2 + qh[:, 256:384] * sin2) * scale).astype(BF16)
        k_ref[h, :, :128] = kvall[:, h * 256:h * 256 + 128].astype(BF16)
        k_ref[h, :, 128:] = k_pe
        v_ref[h, :, :128] = kvall[:, h * 256 + 128:(h + 1) * 256].astype(BF16)
        v_ref[h, :, 128:] = jnp.ones((kr.shape[0], 128), BF16)


def mla_prep(z, tab, q_norm, kv_norm, wq, wkv, *, ts):
    s = z.shape[0]
    return pl.pallas_call(
        _mla_prep_kernel,
        grid=(s // ts,),
        in_specs=[
            pl.BlockSpec((ts, MLA_Q_LORA), lambda i: (i, COL_CQ // MLA_Q_LORA)),
            pl.BlockSpec((ts, MLA_KV_LORA), lambda i: (i, COL_CKV // MLA_KV_LORA)),
            pl.BlockSpec((ts, 128), lambda i: (i, COL_ROPE // 128)),
            pl.BlockSpec((ts, 256), lambda i: (i, 0)),
            pl.BlockSpec((1, MLA_Q_LORA), lambda i: (0, 0)),
            pl.BlockSpec((1, MLA_KV_LORA), lambda i: (0, 0)),
            pl.BlockSpec(wq.shape, lambda i: (0, 0)),
            pl.BlockSpec(wkv.shape, lambda i: (0, 0)),
        ],
        out_specs=[
            pl.BlockSpec((MLA_HEADS, ts, MLA_QK_PAD), lambda i: (0, i, 0)),
            pl.BlockSpec((MLA_HEADS, ts, MLA_QK_PAD), lambda i: (0, i, 0)),
            pl.BlockSpec((MLA_HEADS, ts, 2 * MLA_V_DIM), lambda i: (0, i, 0)),
        ],
        out_shape=[
            jax.ShapeDtypeStruct((MLA_HEADS, s, MLA_QK_PAD), BF16),
            jax.ShapeDtypeStruct((MLA_HEADS, s, MLA_QK_PAD), BF16),
            jax.ShapeDtypeStruct((MLA_HEADS, s, 2 * MLA_V_DIM), BF16),
        ],
        compiler_params=_cparams(("parallel",)),
        name="mla_prep",
    )(z, z, z, tab, q_norm.reshape(1, -1), kv_norm.reshape(1, -1), wq, wkv)


MLA_BQ = 256
MLA_BK = 512
MLA_HEADS_PER_STEP = 4


def _mla_attn_kernel(q_ref, k_ref, v_ref, g_ref, o_ref):
    i = pl.program_id(1)
    bq, bk = MLA_BQ, MLA_BK
    heads = range(MLA_HEADS_PER_STEP)
    q = [q_ref[h] for h in heads]
    nd = bk // bq

    def block(kb, m, acc, masked):
        start = pl.multiple_of(kb * bk, bk)
        sc = [_dot_nt(q[h], k_ref[h, pl.ds(start, bk), :]) for h in heads]
        if masked:
            row = lax.broadcasted_iota(jnp.int32, (bq, bk), 0) + i * bq
            col = lax.broadcasted_iota(jnp.int32, (bq, bk), 1) + start
            sc = [jnp.where(col <= row, sc[h], -1e30) for h in heads]
        m_new = [jnp.maximum(m[h], jnp.max(sc[h], axis=-1, keepdims=True)) for h in heads]
        for h in heads:
            alpha = jnp.exp(m[h] - m_new[h])
            p = jnp.exp(sc[h] - m_new[h])
            acc[h] = acc[h] * alpha + _dot(p.astype(BF16), v_ref[h, pl.ds(start, bk), :])
        return m_new, acc

    def body(kb, st):
        m, acc = block(kb, list(st[0]), list(st[1]), False)
        return tuple(m), tuple(acc)

    m0 = tuple(jnp.full((bq, 1), -1e30, F32) for _ in heads)
    a0 = tuple(jnp.zeros((bq, 2 * MLA_V_DIM), F32) for _ in heads)
    last = i // nd
    m, acc = lax.fori_loop(0, last, body, (m0, a0))
    m, acc = block(last, list(m), list(acc), True)
    for h in heads:
        y = acc[h][:, :MLA_V_DIM] / acc[h][:, MLA_V_DIM:]
        o_ref[:, h * MLA_V_DIM:(h + 1) * MLA_V_DIM] = _rms_rows(y, g_ref[h:h + 1, :]).astype(o_ref.dtype)


def mla_attention(q, k, v, gains):
    s = q.shape[1]
    hps = MLA_HEADS_PER_STEP
    whole = lambda n: pl.BlockSpec((hps, s, n), lambda g, i: (g, 0, 0), pipeline_mode=pl.Buffered(1))
    return pl.pallas_call(
        _mla_attn_kernel,
        grid=(MLA_HEADS // hps, s // MLA_BQ),
        in_specs=[
            pl.BlockSpec((hps, MLA_BQ, MLA_QK_PAD), lambda g, i: (g, i, 0)),
            whole(MLA_QK_PAD),
            whole(2 * MLA_V_DIM),
            pl.BlockSpec((hps, MLA_V_DIM), lambda g, i: (g, 0)),
        ],
        out_specs=pl.BlockSpec((MLA_BQ, hps * MLA_V_DIM), lambda g, i: (i, g)),
        out_shape=jax.ShapeDtypeStruct((s, MLA_WIDTH), BF16),
        compiler_params=_cparams(("parallel", "arbitrary")),
        name="mla_attention",
    )(q, k, v, gains)


def _head_sum_matrix():
    r = lax.broadcasted_iota(jnp.int32, (128, 128), 0) // RWKV_HEAD_DIM
    c = lax.broadcasted_iota(jnp.int32, (128, 128), 1) // RWKV_HEAD_DIM
    return jnp.where(r == c, 1.0, 0.0).astype(BF16)


def _head_sums(x, ones_bd):
    parts = []
    for j in range(x.shape[1] // 128):
        parts.append(_exact_lhs_dot_rhs(x[:, j * 128:(j + 1) * 128], ones_bd))
    return jnp.concatenate(parts, axis=-1)


def _exact_lhs_dot_rhs(x, m_bf16):
    hi, mid, lo = _split3(x)
    return _dot(hi, m_bf16) + (_dot(mid, m_bf16) + _dot(lo, m_bf16))


def _rwkv_prep_kernel(r_ref, k_ref, v_ref, lora_ref, xg_ref, mu_ref, mul_ref, mug_ref,
                      w0_ref, w2_ref, a0_ref, a2_ref, g2_ref, kk_ref, ka_ref,
                      ro_ref, lw_ref, ko_ref, vo_ref, kko_ref, bo_ref, go_ref,
                      prev_ref, prevl_ref):
    ts = r_ref.shape[0]

    @pl.when(pl.program_id(0) == 0)
    def _():
        prev_ref[...] = jnp.zeros_like(prev_ref)
        prevl_ref[...] = jnp.zeros_like(prevl_ref)

    first_row = lax.broadcasted_iota(jnp.int32, (ts, 1), 0) == 0

    def shift_mix(x, prev_row, mu):
        x_prev = jnp.where(first_row, prev_row, pltpu.roll(x, 1, 0))
        return x + (x_prev - x) * mu

    r_in, k_in, v_in = r_ref[...], k_ref[...], v_ref[...]
    lora_in, xg_in = lora_ref[...], xg_ref[...]
    r = shift_mix(r_in, prev_ref[0:1, :], mu_ref[0:1, :])
    k = shift_mix(k_in, prev_ref[1:2, :], mu_ref[1:2, :])
    v = shift_mix(v_in, prev_ref[2:3, :], mu_ref[2:3, :])
    lora = shift_mix(lora_in, prevl_ref[0:1, :], mul_ref[...])
    xg = shift_mix(xg_in, prevl_ref[1:2, :], mug_ref[...])
    prev_ref[0:1, :] = r_in[ts - 1:ts, :]
    prev_ref[1:2, :] = k_in[ts - 1:ts, :]
    prev_ref[2:3, :] = v_in[ts - 1:ts, :]
    prevl_ref[0:1, :] = lora_in[ts - 1:ts, :]
    prevl_ref[1:2, :] = xg_in[ts - 1:ts, :]

    dw = w0_ref[...] + _dot(jnp.tanh(lora).astype(BF16), w2_ref[...])
    log_w = -jax.nn.softplus(-dw) - 0.5
    lw_ref[...] = -jnp.exp(log_w)
    a = jax.nn.sigmoid(a0_ref[...] + _dot(lora.astype(BF16), a2_ref[...]))
    go_ref[...] = _dot(jax.nn.sigmoid(xg).astype(BF16), g2_ref[...])

    kk = k * kk_ref[...]
    ss = _head_sums(kk * kk, _head_sum_matrix())
    kk = kk / jnp.maximum(jnp.sqrt(ss), 1e-12)
    ro_ref[...] = r
    vo_ref[...] = v
    ko_ref[...] = k * (1.0 + (a - 1.0) * ka_ref[...])
    kko_ref[...] = kk
    bo_ref[...] = kk * a


def rwkv_prep(z, p, *, ts):
    s = z.shape[0]
    w = RWKV_WIDTH
    row = lambda n: pl.BlockSpec((1, n), lambda i: (0, 0))
    big = pl.BlockSpec((ts, w), lambda i: (i, 0))
    out = jax.ShapeDtypeStruct((s, w), F32)
    return pl.pallas_call(
        _rwkv_prep_kernel,
        grid=(s // ts,),
        in_specs=[
            pl.BlockSpec((ts, w), lambda i: (i, COL_R // w)),
            pl.BlockSpec((ts, w), lambda i: (i, COL_K // w)),
            pl.BlockSpec((ts, w), lambda i: (i, COL_V // w)),
            pl.BlockSpec((ts, 128), lambda i: (i, COL_LORA // 128)),
            pl.BlockSpec((ts, 128), lambda i: (i, COL_XG // 128)),
            pl.BlockSpec((3, w), lambda i: (0, 0)),
            row(128), row(128),
            row(w), pl.BlockSpec((128, w), lambda i: (0, 0)),
            row(w), pl.BlockSpec((128, w), lambda i: (0, 0)),
            pl.BlockSpec((128, w), lambda i: (0, 0)),
            row(w), row(w),
        ],
        out_specs=[big] * 7,
        out_shape=[out] * 7,
        scratch_shapes=[pltpu.VMEM((8, w), F32), pltpu.VMEM((8, 128), F32)],
        compiler_params=_cparams(("arbitrary",)),
        name="rwkv_prep",
    )(z, z, z, z, z, p["mu_rkv"], p["mu_lora"], p["mu_g"], p["w0"], p["w2"], p["a0"], p["a2"],
      p["g2"], p["k_k"], p["k_a"])


def _rwkv_chunk_kernel(r_ref, lw_ref, k_ref, v_ref, kk_ref, b_ref, g_ref, rk_ref, lnw_ref, lnb_ref,
                       o_ref, st_ref, y_ref):
    c = RWKV_CHUNK
    n = RWKV_HEAD_DIM
    nh = r_ref.shape[1] // n

    @pl.when(pl.program_id(1) == 0)
    def _():
        st_ref[...] = jnp.zeros_like(st_ref)

    ri = lax.broadcasted_iota(jnp.int32, (c, c), 0)
    ci = lax.broadcasted_iota(jnp.int32, (c, c), 1)
    strict = ri > ci
    incl = ri >= ci
    eye = jnp.where(ri == ci, 1.0, 0.0).astype(F32)
    tril_ones = jnp.where(incl, 1.0, 0.0).astype(BF16)
    d16 = (ri // 16) == (ci // 16)
    d32 = (ri // 32) == (ci // 32)

    r, lw, k, v = r_ref[...], lw_ref[...], k_ref[...], v_ref[...]
    kk, b = kk_ref[...], b_ref[...]
    cum = _exact_lhs_dot(tril_ones, lw)
    p_incl = jnp.exp(cum)
    p_excl = jnp.exp(cum - lw)
    p_inv = jnp.exp(-cum)
    p_end = p_incl[c - 1:c, :]
    r_t = r * p_incl
    kk_t = kk * p_excl
    b_t = b * p_inv
    k_t = k * p_inv
    b_d = b_t * p_end
    k_d = k_t * p_end

    lhs_all = jnp.concatenate([kk_t, r_t], axis=0)
    rhs_all = jnp.concatenate([b_t, k_t], axis=0)
    dec_all = jnp.concatenate([b_d, k_d], axis=0)
    ri2 = lax.broadcasted_iota(jnp.int32, (2 * c, 2 * c), 0)
    ci2 = lax.broadcasted_iota(jnp.int32, (2 * c, 2 * c), 1) % c
    causal2 = ci2 <= jnp.where(ri2 < c, ri2 - 1, ri2 - c)
    blk_lo = jnp.logical_and(d32, jnp.logical_not(d16))

    heads = range(nh)
    sls = [slice(h * n, (h + 1) * n) for h in heads]
    mm_s = functools.partial(_mm, passes=RWKV_PASSES_S)
    mm_t = functools.partial(_mm, _dot, passes=RWKV_PASSES_T)
    a = [jnp.where(causal2, _mm(_dot_nt, lhs_all[:, sl], rhs_all[:, sl], RWKV_PASSES_A), 0.0)
         for sl in sls]
    st = [st_ref[h] for h in heads]
    part = [mm_s(_dot_nt, lhs_all[:, sls[h]], st[h]) + mm_s(_dot, a[h][:, c:], v[:, sls[h]])
            for h in heads]
    a_ub = [a[h][:c, :c] for h in heads]
    ld = [jnp.where(d16, a_ub[h], 0.0) for h in heads]
    x = [eye - ld[h] for h in heads]
    pw = [mm_t(ld[h], ld[h]) for h in heads]
    x = [x[h] + mm_t(x[h], pw[h]) for h in heads]
    pw = [mm_t(pw[h], pw[h]) for h in heads]
    x = [x[h] + mm_t(x[h], pw[h]) for h in heads]
    pw = [mm_t(pw[h], pw[h]) for h in heads]
    x = [x[h] + mm_t(x[h], pw[h]) for h in heads]
    t = [mm_t(x[h], jnp.where(blk_lo, a_ub[h], 0.0)) for h in heads]
    x = [x[h] - mm_t(t[h], x[h]) for h in heads]
    t = [mm_t(x[h], jnp.where(d32, 0.0, a_ub[h])) for h in heads]
    x = [x[h] - mm_t(t[h], x[h]) for h in heads]

    u = [-mm_s(_dot, x[h], part[h][:c]) for h in heads]
    for h in heads:
        y_ref[:, sls[h]] = part[h][c:] + mm_s(_dot, a[h][c:, :c], u[h])
    for h in heads:
        uv = jnp.concatenate([u[h], v[:, sls[h]]], axis=0)
        st_ref[h] = st[h] * p_end[:, sls[h]] + mm_s(_dot_tn, uv, dec_all[:, sls[h]])

    ones_bd = _head_sum_matrix()
    y = y_ref[...]
    mean = _head_sums(y, ones_bd) * (1.0 / n)
    yc = y - mean
    var = _head_sums(yc * yc, ones_bd) * (1.0 / n)
    yn = yc * lax.rsqrt(var + RWKV_GN_EPS) * lnw_ref[...] + lnb_ref[...]
    bonus = _head_sums(r * k * rk_ref[...], ones_bd)
    o_ref[...] = ((yn + bonus * v) * g_ref[...]).astype(o_ref.dtype)


def rwkv_chunk(r, lw, k, v, kk, b, g, p, *, heads_per_step):
    s = r.shape[0]
    c = RWKV_CHUNK
    wb = heads_per_step * RWKV_HEAD_DIM
    big = pl.BlockSpec((c, wb), lambda hg, t: (t, hg))
    row = pl.BlockSpec((1, wb), lambda hg, t: (0, hg))
    return pl.pallas_call(
        _rwkv_chunk_kernel,
        grid=(RWKV_WIDTH // wb, s // c),
        in_specs=[big] * 7 + [row] * 3,
        out_specs=big,
        out_shape=jax.ShapeDtypeStruct((s, RWKV_WIDTH), BF16),
        scratch_shapes=[
            pltpu.VMEM((heads_per_step, RWKV_HEAD_DIM, RWKV_HEAD_DIM), F32),
            pltpu.VMEM((c, wb), F32),
        ],
        compiler_params=_cparams(("parallel", "arbitrary")),
        name="rwkv_chunk",
    )(r, lw, k, v, kk, b, g, p["r_k"], p["ln_w"], p["ln_b"])


def _swap_halves(w):
    half = w.shape[-1] // 2
    return jnp.concatenate([w[..., half:], w[..., :half]], axis=-1)


def _prep_w_in(w_in):
    sizes = [SB_WIDTH] * 3 + [MLA_Q_LORA, MLA_KV_LORA, MLA_ROPE_DIM] + [RWKV_WIDTH] * 3 + [
        RWKV_DECAY_LORA, RWKV_A_LORA, RWKV_GATE_LORA]
    idx = [int(i) for i in np.cumsum(sizes)[:-1]]
    sbq, sbk, sbv, cq, ckv, krope, r, k, v, xw, xa, xg = jnp.split(w_in, idx, axis=-1)
    sbq = sbq * (SB_HEAD_DIM ** -0.5)
    cols = [r, k, v, sbq, sbk, sbv, cq, ckv, xw, xa, xg, krope, _swap_halves(krope)]
    return jnp.concatenate(cols, axis=-1).astype(BF16)


def _prep_mla_w(w_uq, w_ukv):
    q = w_uq.reshape(MLA_Q_LORA, MLA_HEADS, MLA_NOPE_DIM + MLA_ROPE_DIM)
    nope, pe = q[..., :MLA_NOPE_DIM], q[..., MLA_NOPE_DIM:]
    zpad = jnp.zeros_like(pe)
    wq = jnp.concatenate([nope, pe, zpad, _swap_halves(pe), zpad], axis=-1)
    return wq.reshape(MLA_Q_LORA, MLA_HEADS * 384).astype(BF16), w_ukv.astype(BF16)


def _rope_table(positions):
    half = MLA_ROPE_DIM // 2
    inv_freq = ROPE_THETA ** (-jnp.arange(half, dtype=F32) / half)
    ang = positions.astype(F32)[:, None] * inv_freq
    cos, sin = jnp.cos(ang), jnp.sin(ang)
    z = jnp.zeros((positions.shape[0], MLA_ROPE_DIM), F32)
    return jnp.concatenate([cos, cos, z, -sin, sin, z], axis=-1)


def _pick_tile(n, candidates):
    for c in candidates:
        if n % c == 0:
            return c
    raise ValueError(f"no tile for {n}")


def kernel(x, positions, ffn1_norm, ffn1_gate, ffn1_up, ffn1_down, mix_norm, w_in, mla_q_norm,
           mla_w_uq, mla_kv_norm, mla_w_ukv, rwkv_mu, rwkv_w0, rwkv_w2, rwkv_a0, rwkv_a2, rwkv_g2,
           rwkv_k_k, rwkv_k_a, rwkv_r_k, rwkv_ln_w, rwkv_ln_b, sb_out_norm, mla_out_norm, w_out,
           ffn2_norm, ffn2_gate, ffn2_up, ffn2_down, final_norm):
    bsz, s, d = x.shape
    assert bsz == 1 and d == SB_WIDTH + MLA_WIDTH + RWKV_WIDTH
    depth = w_in.shape[0]
    d_ff = ffn1_gate.shape[-1]
    tm = _pick_tile(s, (1024, 512, 256, 128))
    tf = _pick_tile(d_ff, (512, 256, 128))
    x = x[0]
    tab = _rope_table(positions[0])
    w = RWKV_WIDTH
    zl = jnp.zeros((RWKV_DECAY_LORA, w), F32)
    ffn_w = [(_col_tiled(wg.astype(BF16), tf), _col_tiled(wu.astype(BF16), tf),
              _col_tiled(wd.astype(BF16), 256))
             for wg, wu, wd in ((ffn1_gate, ffn1_up, ffn1_down), (ffn2_gate, ffn2_up, ffn2_down))]
    w_in_b = _prep_w_in(w_in)
    w_out_b = w_out.astype(BF16)

    def ffn(x, g, ws, l):
        act = ffn_up(x, g, ws[0], ws[1], l, tm=tm)
        return ffn_down(x, act, ws[2], l, tm=tm)

    for l in range(depth):
        x = ffn(x, ffn1_norm[l], ffn_w[0], l)

        z, zb = norm_proj(x, mix_norm[l], w_in_b, l, tm=tm, tn=640)

        y_sb = sb_attention(zb, sb_out_norm[l])

        wq, wkv = _prep_mla_w(mla_w_uq[l], mla_w_ukv[l])
        q, k, v = mla_prep(z, tab, mla_q_norm[l], mla_kv_norm[l], wq, wkv, ts=_pick_tile(s, (256, 128)))
        y_mla = mla_attention(q, k, v, mla_out_norm[l])

        mu = rwkv_mu[l]
        p = {
            "mu_rkv": mu[:3 * w].reshape(3, w),
            "mu_lora": mu[3 * w:3 * w + 128].reshape(1, 128),
            "mu_g": mu[3 * w + 128:].reshape(1, 128),
            "w0": rwkv_w0[l].reshape(1, w),
            "w2": jnp.concatenate([rwkv_w2[l], zl], axis=0).astype(BF16),
            "a0": rwkv_a0[l].reshape(1, w),
            "a2": jnp.concatenate([zl, rwkv_a2[l]], axis=0).astype(BF16),
            "g2": rwkv_g2[l].astype(BF16),
            "k_k": rwkv_k_k[l].reshape(1, w),
            "k_a": rwkv_k_a[l].reshape(1, w),
            "r_k": rwkv_r_k[l].reshape(1, w),
            "ln_w": rwkv_ln_w[l].reshape(1, w),
            "ln_b": rwkv_ln_b[l].reshape(1, w),
        }
        r_, lw_, k_, v_, kk_, b_, g_ = rwkv_prep(z, p, ts=_pick_tile(s, (256, 128)))
        y_rwkv = rwkv_chunk(r_, lw_, k_, v_, kk_, b_, g_, p, heads_per_step=16)

        x = out_proj(x, y_sb, y_mla, y_rwkv, w_out_b, l, tm=_pick_tile(s, (512, 256, 128)))

        x = ffn(x, ffn2_norm[l], ffn_w[1], l)

    return final_rmsnorm(x, final_norm, tm=_pick_tile(s, (256, 128)))[None]
```

```python
import functools

import jax
import jax.numpy as jnp
import numpy as np
from jax import lax
from jax.experimental import pallas as pl
from jax.experimental.pallas import tpu as pltpu

F32 = jnp.float32
BF16 = jnp.bfloat16

NORM_EPS = 1e-6
SB_HEADS = 4
SB_HEAD_DIM = 128
SB_WIDTH = SB_HEADS * SB_HEAD_DIM
MLA_HEADS = 4
MLA_NOPE_DIM = 128
MLA_ROPE_DIM = 64
MLA_V_DIM = 128
MLA_Q_LORA = 512
MLA_KV_LORA = 256
MLA_WIDTH = MLA_HEADS * MLA_V_DIM
MLA_QK_PAD = 256
ROPE_THETA = 10000.0
RWKV_HEADS = 16
RWKV_HEAD_DIM = 64
RWKV_WIDTH = RWKV_HEADS * RWKV_HEAD_DIM
RWKV_DECAY_LORA = 64
RWKV_A_LORA = 64
RWKV_GATE_LORA = 128
RWKV_GN_EPS = 64e-5
RWKV_CHUNK = 64
RWKV_PASSES_A = 1
RWKV_PASSES_T = 1
RWKV_PASSES_S = 1

COL_R, COL_K, COL_V = 0, 1024, 2048
COL_SBQ, COL_SBK, COL_SBV = 3072, 3584, 4096
COL_CQ, COL_CKV = 4608, 5120
COL_LORA, COL_XG, COL_ROPE = 5376, 5504, 5632
Z_COLS = 6144

VMEM_LIMIT = 48 * 1024 * 1024


def _cparams(sem):
    return pltpu.CompilerParams(dimension_semantics=sem, vmem_limit_bytes=VMEM_LIMIT)


def _dot(a, b):
    return jnp.dot(a, b, preferred_element_type=F32)


def _dot_nt(a, b):
    return lax.dot_general(a, b, (((1,), (1,)), ((), ())), preferred_element_type=F32)


def _dot_tn(a, b):
    return lax.dot_general(a, b, (((0,), (0,)), ((), ())), preferred_element_type=F32)


def _split3(x):
    hi = x.astype(BF16)
    r1 = x - hi.astype(F32)
    mid = r1.astype(BF16)
    lo = (r1 - mid.astype(F32)).astype(BF16)
    return hi, mid, lo


def _split2(x):
    hi = x.astype(BF16)
    lo = (x - hi.astype(F32)).astype(BF16)
    return hi, lo


def _mm3(dot, a, b):
    ah, al = _split2(a)
    bh, bl = _split2(b)
    return dot(ah, bh) + (dot(ah, bl) + dot(al, bh))


def _mm(dot, a, b, passes):
    if passes == 1:
        return dot(a.astype(BF16), b.astype(BF16))
    return _mm3(dot, a, b)


def _exact_lhs_dot(m_bf16, x):
    hi, mid, lo = _split3(x)
    return _dot(m_bf16, hi) + (_dot(m_bf16, mid) + _dot(m_bf16, lo))


def _rms_rows(x, g):
    ms = jnp.mean(x * x, axis=-1, keepdims=True)
    return x * lax.rsqrt(ms + NORM_EPS) * g


def _norm_proj_kernel(x_ref, g_ref, w_ref, o_ref, ob_ref, h_ref):
    @pl.when(pl.program_id(1) == 0)
    def _():
        h_ref[...] = _rms_rows(x_ref[...], g_ref[...]).astype(BF16)

    z = _dot(h_ref[...], w_ref[...])
    o_ref[...] = z
    ob_ref[...] = z.astype(BF16)


def norm_proj(x, g, w, l, *, tm, tn):
    s, d = x.shape
    n = w.shape[2]
    return pl.pallas_call(
        _norm_proj_kernel,
        grid=(s // tm, n // tn),
        in_specs=[
            pl.BlockSpec((tm, d), lambda i, j: (i, 0)),
            pl.BlockSpec((1, d), lambda i, j: (0, 0)),
            pl.BlockSpec((None, d, tn), lambda i, j: (l, 0, j)),
        ],
        out_specs=[pl.BlockSpec((tm, tn), lambda i, j: (i, j))] * 2,
        out_shape=[jax.ShapeDtypeStruct((s, n), F32), jax.ShapeDtypeStruct((s, n), BF16)],
        scratch_shapes=[pltpu.VMEM((tm, d), BF16)],
        compiler_params=_cparams(("parallel", "arbitrary")),
        name="norm_proj",
    )(x, g.reshape(1, d), w)


def _ffn_up_kernel(x_ref, g_ref, wg_ref, wu_ref, o_ref, h_ref):
    @pl.when(pl.program_id(1) == 0)
    def _():
        h_ref[...] = _rms_rows(x_ref[...], g_ref[...]).astype(BF16)

    h = h_ref[...]
    a = _dot(h, wg_ref[...].astype(BF16))
    u = _dot(h, wu_ref[...].astype(BF16))
    o_ref[...] = (a * jax.nn.sigmoid(a) * u).astype(o_ref.dtype)


def ffn_up(x, g, wg, wu, l, *, tm, tn):
    s, d = x.shape
    f = wg.shape[2]
    return pl.pallas_call(
        _ffn_up_kernel,
        grid=(s // tm, f // tn),
        in_specs=[
            pl.BlockSpec((tm, d), lambda i, j: (i, 0)),
            pl.BlockSpec((1, d), lambda i, j: (0, 0)),
            pl.BlockSpec((None, d, tn), lambda i, j: (l, 0, j)),
            pl.BlockSpec((None, d, tn), lambda i, j: (l, 0, j)),
        ],
        out_specs=pl.BlockSpec((tm, tn), lambda i, j: (i, j)),
        out_shape=jax.ShapeDtypeStruct((s, f), BF16),
        scratch_shapes=[pltpu.VMEM((tm, d), BF16)],
        compiler_params=_cparams(("parallel", "arbitrary")),
        name="ffn_up",
    )(x, g.reshape(1, d), wg, wu)


def _ffn_down_kernel(x_ref, a_ref, w_ref, o_ref):
    o_ref[...] = x_ref[...] + 0.5 * _dot(a_ref[...], w_ref[...])


def ffn_down(x, act, wd, l, *, tm, tn):
    s, d = x.shape
    f = act.shape[1]
    return pl.pallas_call(
        _ffn_down_kernel,
        grid=(s // tm, d // tn),
        in_specs=[
            pl.BlockSpec((tm, tn), lambda i, j: (i, j)),
            pl.BlockSpec((tm, f), lambda i, j: (i, 0)),
            pl.BlockSpec((None, f, tn), lambda i, j: (l, 0, j)),
        ],
        out_specs=pl.BlockSpec((tm, tn), lambda i, j: (i, j)),
        out_shape=jax.ShapeDtypeStruct((s, d), F32),
        compiler_params=_cparams(("parallel", "arbitrary")),
        name="ffn_down",
    )(x, act, wd)


def _out_proj_kernel(x_ref, a_ref, b_ref, c_ref, wa_ref, wb_ref, wc_ref, o_ref):
    acc = _dot(a_ref[...], wa_ref[...])
    acc += _dot(b_ref[...], wb_ref[...])
    acc += _dot(c_ref[...], wc_ref[...])
    o_ref[...] = x_ref[...] + acc


def out_proj(x, y_sb, y_mla, y_rwkv, w_out, l, *, tm):
    s, d = x.shape
    assert SB_WIDTH == MLA_WIDTH and RWKV_WIDTH == SB_WIDTH + MLA_WIDTH
    wspec = lambda rows, blk: pl.BlockSpec((None, rows, d), lambda i: (l, blk, 0),
                                           pipeline_mode=pl.Buffered(1))
    return pl.pallas_call(
        _out_proj_kernel,
        grid=(s // tm,),
        in_specs=[
            pl.BlockSpec((tm, d), lambda i: (i, 0)),
            pl.BlockSpec((tm, SB_WIDTH), lambda i: (i, 0)),
            pl.BlockSpec((tm, MLA_WIDTH), lambda i: (i, 0)),
            pl.BlockSpec((tm, RWKV_WIDTH), lambda i: (i, 0)),
            wspec(SB_WIDTH, 0),
            wspec(MLA_WIDTH, 1),
            wspec(RWKV_WIDTH, 1),
        ],
        out_specs=pl.BlockSpec((tm, d), lambda i: (i, 0)),
        out_shape=jax.ShapeDtypeStruct((s, d), F32),
        compiler_params=_cparams(("parallel",)),
        name="out_proj",
    )(x, y_sb, y_mla, y_rwkv, w_out, w_out, w_out)


def _final_norm_kernel(x_ref, g_ref, o_ref):
    o_ref[...] = _rms_rows(x_ref[...], g_ref[...])


def final_rmsnorm(x, g, *, tm):
    s, d = x.shape
    return pl.pallas_call(
        _final_norm_kernel,
        grid=(s // tm,),
        in_specs=[pl.BlockSpec((tm, d), lambda i: (i, 0)), pl.BlockSpec((1, d), lambda i: (0, 0))],
        out_specs=pl.BlockSpec((tm, d), lambda i: (i, 0)),
        out_shape=jax.ShapeDtypeStruct((s, d), F32),
        compiler_params=_cparams(("parallel",)),
        name="final_norm",
    )(x, g.reshape(1, d))


SB_BQ = 256
SB_BK = 128
SB_DEAD_LOG = -104.0


def _sb_attn_kernel(q_ref, k_ref, v_ref, g_ref, o_ref):
    i = pl.program_id(0)
    bq, bk, dh = SB_BQ, SB_BK, SB_HEAD_DIM
    heads = range(SB_HEADS)
    hs = [slice(h * dh, (h + 1) * dh) for h in heads]
    q = [q_ref[:, hs[h]] for h in heads]
    row = lax.broadcasted_iota(jnp.int32, (bq, bk), 0)
    col = lax.broadcasted_iota(jnp.int32, (bq, bk), 1)
    kr = lax.broadcasted_iota(jnp.int32, (bk, bk), 0)
    kc = lax.broadcasted_iota(jnp.int32, (bk, bk), 1)
    upper = jnp.where(kr > kc, 1.0, 0.0).astype(BF16)

    def block(kb, carry, acc, masked):
        start = pl.multiple_of(kb * bk, bk)
        z = [_dot_nt(q[h], k_ref[pl.ds(start, bk), hs[h]]) for h in heads]
        log_beta = [jnp.minimum(z[h], 0.0) - jnp.log(1.0 + jnp.exp(-jnp.abs(z[h]))) for h in heads]
        log_keep = [log_beta[h] - z[h] for h in heads]
        if masked:
            before = (start + col) < (i * bq + row)
            log_keep = [jnp.where(before, log_keep[h], 0.0) for h in heads]
        split = [_split2(log_keep[h]) for h in heads]
        within = [_dot(split[h][0], upper) + _dot(split[h][1], upper) for h in heads]
        a = [jnp.exp(log_beta[h] + (carry[h] + within[h])) for h in heads]
        if masked:
            a = [jnp.where(before, a[h], 0.0) for h in heads]
        acc = [acc[h] + _dot(a[h].astype(BF16), v_ref[pl.ds(start, bk), hs[h]]) for h in heads]
        carry = [carry[h] + jnp.sum(log_keep[h], axis=-1, keepdims=True) for h in heads]
        return carry, acc

    def alive_of(carry):
        m = jnp.max(carry[0])
        for h in heads[1:]:
            m = jnp.maximum(m, jnp.max(carry[h]))
        return m

    carry = [jnp.zeros((bq, 1), F32) for _ in heads]
    acc = [jnp.zeros((bq, dh), F32) for _ in heads]
    nd = bq // bk
    first = i * nd
    for d in range(nd - 1, -1, -1):
        carry, acc = block(first + d, carry, acc, True)

    def cond(st):
        return jnp.logical_and(st[0] >= 0, st[1] > SB_DEAD_LOG)

    def body(st):
        kb, _, carry, acc = st
        carry, acc = block(kb, list(carry), list(acc), False)
        return kb - 1, alive_of(carry), tuple(carry), tuple(acc)

    st = lax.while_loop(cond, body, (first - 1, alive_of(carry), tuple(carry), tuple(acc)))
    acc = st[3]
    for h in heads:
        o_ref[:, hs[h]] = _rms_rows(acc[h], g_ref[h:h + 1, :]).astype(o_ref.dtype)


def sb_attention(zb, gains):
    s = zb.shape[0]
    whole = lambda c: pl.BlockSpec((s, SB_WIDTH), lambda i: (0, c // SB_WIDTH),
                                   pipeline_mode=pl.Buffered(1))
    return pl.pallas_call(
        _sb_attn_kernel,
        grid=(s // SB_BQ,),
        in_specs=[
            pl.BlockSpec((SB_BQ, SB_WIDTH), lambda i: (i, COL_SBQ // SB_WIDTH)),
            whole(COL_SBK),
            whole(COL_SBV),
            pl.BlockSpec((SB_HEADS, SB_HEAD_DIM), lambda i: (0, 0)),
        ],
        out_specs=pl.BlockSpec((SB_BQ, SB_WIDTH), lambda i: (i, 0)),
        out_shape=jax.ShapeDtypeStruct((s, SB_WIDTH), BF16),
        compiler_params=_cparams(("arbitrary",)),
        name="sb_attention",
    )(zb, zb, zb, gains)


def _mla_prep_kernel(cq_ref, ckv_ref, rope_ref, tab_ref, qn_ref, kvn_ref, wq_ref, wkv_ref,
                     q_ref, k_ref, v_ref):
    scale = (MLA_NOPE_DIM + MLA_ROPE_DIM) ** -0.5
    hq = _rms_rows(cq_ref[...], qn_ref[...]).astype(BF16)
    hkv = _rms_rows(ckv_ref[...], kvn_ref[...]).astype(BF16)
    cos2 = tab_ref[:, :128]
    sin2 = tab_ref[:, 128:]
    kr = rope_ref[...]
    k_pe = kr * cos2 + pltpu.roll(kr, 64, 1) * sin2
    k_pe = k_pe.astype(BF16)
    qall = _dot(hq, wq_ref[...])
    kvall = _dot(hkv, wkv_ref[...])
    for h in range(MLA_HEADS):
        qh = qall[:, h * 384:(h + 1) * 384]
        q_ref[h, :, :128] = (qh[:, :128] * scale).astype(BF16)
        q_ref[h, :, 128:] = ((qh[:, 128:256] * cos2 + qh[:, 256:384] * sin2) * scale).astype(BF16)
        k_ref[h, :, :128] = kvall[:, h * 256:h * 256 + 128].astype(BF16)
        k_ref[h, :, 128:] = k_pe
        v_ref[h, :, :128] = kvall[:, h * 256 + 128:(h + 1) * 256].astype(BF16)
        v_ref[h, :, 128:] = jnp.ones((kr.shape[0], 128), BF16)


def mla_prep(z, tab, q_norm, kv_norm, wq, wkv, *, ts):
    s = z.shape[0]
    return pl.pallas_call(
        _mla_prep_kernel,
        grid=(s // ts,),
        in_specs=[
            pl.BlockSpec((ts, MLA_Q_LORA), lambda i: (i, COL_CQ // MLA_Q_LORA)),
            pl.BlockSpec((ts, MLA_KV_LORA), lambda i: (i, COL_CKV // MLA_KV_LORA)),
            pl.BlockSpec((ts, 128), lambda i: (i, COL_ROPE // 128)),
            pl.BlockSpec((ts, 256), lambda i: (i, 0)),
            pl.BlockSpec((1, MLA_Q_LORA), lambda i: (0, 0)),
            pl.BlockSpec((1, MLA_KV_LORA), lambda i: (0, 0)),
            pl.BlockSpec(wq.shape, lambda i: (0, 0)),
            pl.BlockSpec(wkv.shape, lambda i: (0, 0)),
        ],
        out_specs=[
            pl.BlockSpec((MLA_HEADS, ts, MLA_QK_PAD), lambda i: (0, i, 0)),
            pl.BlockSpec((MLA_HEADS, ts, MLA_QK_PAD), lambda i: (0, i, 0)),
            pl.BlockSpec((MLA_HEADS, ts, 2 * MLA_V_DIM), lambda i: (0, i, 0)),
        ],
        out_shape=[
            jax.ShapeDtypeStruct((MLA_HEADS, s, MLA_QK_PAD), BF16),
            jax.ShapeDtypeStruct((MLA_HEADS, s, MLA_QK_PAD), BF16),
            jax.ShapeDtypeStruct((MLA_HEADS, s, 2 * MLA_V_DIM), BF16),
        ],
        compiler_params=_cparams(("parallel",)),
        name="mla_prep",
    )(z, z, z, tab, q_norm.reshape(1, -1), kv_norm.reshape(1, -1), wq, wkv)


MLA_BQ = 256
MLA_BK = 512
MLA_HEADS_PER_STEP = 4


def _mla_attn_kernel(q_ref, k_ref, v_ref, g_ref, o_ref):
    i = pl.program_id(1)
    bq, bk = MLA_BQ, MLA_BK
    heads = range(MLA_HEADS_PER_STEP)
    q = [q_ref[h] for h in heads]
    nd = bk // bq

    def block(kb, m, acc, masked):
        start = pl.multiple_of(kb * bk, bk)
        sc = [_dot_nt(q[h], k_ref[h, pl.ds(start, bk), :]) for h in heads]
        if masked:
            row = lax.broadcasted_iota(jnp.int32, (bq, bk), 0) + i * bq
            col = lax.broadcasted_iota(jnp.int32, (bq, bk), 1) + start
            sc = [jnp.where(col <= row, sc[h], -1e30) for h in heads]
        m_new = [jnp.maximum(m[h], jnp.max(sc[h], axis=-1, keepdims=True)) for h in heads]
        for h in heads:
            alpha = jnp.exp(m[h] - m_new[h])
            p = jnp.exp(sc[h] - m_new[h])
            acc[h] = acc[h] * alpha + _dot(p.astype(BF16), v_ref[h, pl.ds(start, bk), :])
        return m_new, acc

    def body(kb, st):
        m, acc = block(kb, list(st[0]), list(st[1]), False)
        return tuple(m), tuple(acc)

    m0 = tuple(jnp.full((bq, 1), -1e30, F32) for _ in heads)
    a0 = tuple(jnp.zeros((bq, 2 * MLA_V_DIM), F32) for _ in heads)
    last = i // nd
    m, acc = lax.fori_loop(0, last, body, (m0, a0))
    m, acc = block(last, list(m), list(acc), True)
    for h in heads:
        y = acc[h][:, :MLA_V_DIM] / acc[h][:, MLA_V_DIM:]
        o_ref[:, h * MLA_V_DIM:(h + 1) * MLA_V_DIM] = _rms_rows(y, g_ref[h:h + 1, :]).astype(o_ref.dtype)


def mla_attention(q, k, v, gains):
    s = q.shape[1]
    hps = MLA_HEADS_PER_STEP
    whole = lambda n: pl.BlockSpec((hps, s, n), lambda g, i: (g, 0, 0), pipeline_mode=pl.Buffered(1))
    return pl.pallas_call(
        _mla_attn_kernel,
        grid=(MLA_HEADS // hps, s // MLA_BQ),
        in_specs=[
            pl.BlockSpec((hps, MLA_BQ, MLA_QK_PAD), lambda g, i: (g, i, 0)),
            whole(MLA_QK_PAD),
            whole(2 * MLA_V_DIM),
            pl.BlockSpec((hps, MLA_V_DIM), lambda g, i: (g, 0)),
        ],
        out_specs=pl.BlockSpec((MLA_BQ, hps * MLA_V_DIM), lambda g, i: (i, g)),
        out_shape=jax.ShapeDtypeStruct((s, MLA_WIDTH), BF16),
        compiler_params=_cparams(("parallel", "arbitrary")),
        name="mla_attention",
    )(q, k, v, gains)


def _head_sum_matrix():
    r = lax.broadcasted_iota(jnp.int32, (128, 128), 0) // RWKV_HEAD_DIM
    c = lax.broadcasted_iota(jnp.int32, (128, 128), 1) // RWKV_HEAD_DIM
    return jnp.where(r == c, 1.0, 0.0).astype(BF16)


def _head_sums(x, ones_bd):
    parts = []
    for j in range(x.shape[1] // 128):
        parts.append(_exact_lhs_dot_rhs(x[:, j * 128:(j + 1) * 128], ones_bd))
    return jnp.concatenate(parts, axis=-1)


def _exact_lhs_dot_rhs(x, m_bf16):
    hi, mid, lo = _split3(x)
    return _dot(hi, m_bf16) + (_dot(mid, m_bf16) + _dot(lo, m_bf16))


def _rwkv_prep_kernel(r_ref, k_ref, v_ref, lora_ref, xg_ref, mu_ref, mul_ref, mug_ref,
                      w0_ref, w2_ref, a0_ref, a2_ref, g2_ref, kk_ref, ka_ref,
                      ro_ref, lw_ref, ko_ref, vo_ref, kko_ref, bo_ref, go_ref,
                      prev_ref, prevl_ref):
    ts = r_ref.shape[0]

    @pl.when(pl.program_id(0) == 0)
    def _():
        prev_ref[...] = jnp.zeros_like(prev_ref)
        prevl_ref[...] = jnp.zeros_like(prevl_ref)

    first_row = lax.broadcasted_iota(jnp.int32, (ts, 1), 0) == 0

    def shift_mix(x, prev_row, mu):
        x_prev = jnp.where(first_row, prev_row, pltpu.roll(x, 1, 0))
        return x + (x_prev - x) * mu

    r_in, k_in, v_in = r_ref[...], k_ref[...], v_ref[...]
    lora_in, xg_in = lora_ref[...], xg_ref[...]
    r = shift_mix(r_in, prev_ref[0:1, :], mu_ref[0:1, :])
    k = shift_mix(k_in, prev_ref[1:2, :], mu_ref[1:2, :])
    v = shift_mix(v_in, prev_ref[2:3, :], mu_ref[2:3, :])
    lora = shift_mix(lora_in, prevl_ref[0:1, :], mul_ref[...])
    xg = shift_mix(xg_in, prevl_ref[1:2, :], mug_ref[...])
    prev_ref[0:1, :] = r_in[ts - 1:ts, :]
    prev_ref[1:2, :] = k_in[ts - 1:ts, :]
    prev_ref[2:3, :] = v_in[ts - 1:ts, :]
    prevl_ref[0:1, :] = lora_in[ts - 1:ts, :]
    prevl_ref[1:2, :] = xg_in[ts - 1:ts, :]

    dw = w0_ref[...] + _dot(jnp.tanh(lora).astype(BF16), w2_ref[...])
    log_w = -jax.nn.softplus(-dw) - 0.5
    lw_ref[...] = -jnp.exp(log_w)
    a = jax.nn.sigmoid(a0_ref[...] + _dot(lora.astype(BF16), a2_ref[...]))
    go_ref[...] = _dot(jax.nn.sigmoid(xg).astype(BF16), g2_ref[...])

    kk = k * kk_ref[...]
    ss = _head_sums(kk * kk, _head_sum_matrix())
    kk = kk / jnp.maximum(jnp.sqrt(ss), 1e-12)
    ro_ref[...] = r
    vo_ref[...] = v
    ko_ref[...] = k * (1.0 + (a - 1.0) * ka_ref[...])
    kko_ref[...] = kk
    bo_ref[...] = kk * a


def rwkv_prep(z, p, *, ts):
    s = z.shape[0]
    w = RWKV_WIDTH
    row = lambda n: pl.BlockSpec((1, n), lambda i: (0, 0))
    big = pl.BlockSpec((ts, w), lambda i: (i, 0))
    out = jax.ShapeDtypeStruct((s, w), F32)
    return pl.pallas_call(
        _rwkv_prep_kernel,
        grid=(s // ts,),
        in_specs=[
            pl.BlockSpec((ts, w), lambda i: (i, COL_R // w)),
            pl.BlockSpec((ts, w), lambda i: (i, COL_K // w)),
            pl.BlockSpec((ts, w), lambda i: (i, COL_V // w)),
            pl.BlockSpec((ts, 128), lambda i: (i, COL_LORA // 128)),
            pl.BlockSpec((ts, 128), lambda i: (i, COL_XG // 128)),
            pl.BlockSpec((3, w), lambda i: (0, 0)),
            row(128), row(128),
            row(w), pl.BlockSpec((128, w), lambda i: (0, 0)),
            row(w), pl.BlockSpec((128, w), lambda i: (0, 0)),
            pl.BlockSpec((128, w), lambda i: (0, 0)),
            row(w), row(w),
        ],
        out_specs=[big] * 7,
        out_shape=[out] * 7,
        scratch_shapes=[pltpu.VMEM((8, w), F32), pltpu.VMEM((8, 128), F32)],
        compiler_params=_cparams(("arbitrary",)),
        name="rwkv_prep",
    )(z, z, z, z, z, p["mu_rkv"], p["mu_lora"], p["mu_g"], p["w0"], p["w2"], p["a0"], p["a2"],
      p["g2"], p["k_k"], p["k_a"])


def _rwkv_chunk_kernel(r_ref, lw_ref, k_ref, v_ref, kk_ref, b_ref, g_ref, rk_ref, lnw_ref, lnb_ref,
                       o_ref, st_ref, y_ref):
    c = RWKV_CHUNK
    n = RWKV_HEAD_DIM
    nh = r_ref.shape[1] // n

    @pl.when(pl.program_id(1) == 0)
    def _():
        st_ref[...] = jnp.zeros_like(st_ref)

    ri = lax.broadcasted_iota(jnp.int32, (c, c), 0)
    ci = lax.broadcasted_iota(jnp.int32, (c, c), 1)
    strict = ri > ci
    incl = ri >= ci
    eye = jnp.where(ri == ci, 1.0, 0.0).astype(F32)
    tril_ones = jnp.where(incl, 1.0, 0.0).astype(BF16)
    d16 = (ri // 16) == (ci // 16)
    d32 = (ri // 32) == (ci // 32)

    r, lw, k, v = r_ref[...], lw_ref[...], k_ref[...], v_ref[...]
    kk, b = kk_ref[...], b_ref[...]
    cum = _exact_lhs_dot(tril_ones, lw)
    p_incl = jnp.exp(cum)
    p_excl = jnp.exp(cum - lw)
    p_inv = jnp.exp(-cum)
    p_end = p_incl[c - 1:c, :]
    r_t = r * p_incl
    kk_t = kk * p_excl
    b_t = b * p_inv
    k_t = k * p_inv
    b_d = b_t * p_end
    k_d = k_t * p_end

    lhs_all = jnp.concatenate([kk_t, r_t], axis=0)
    rhs_all = jnp.concatenate([b_t, k_t], axis=0)
    dec_all = jnp.concatenate([b_d, k_d], axis=0)
    ri2 = lax.broadcasted_iota(jnp.int32, (2 * c, 2 * c), 0)
    ci2 = lax.broadcasted_iota(jnp.int32, (2 * c, 2 * c), 1) % c
    causal2 = ci2 <= jnp.where(ri2 < c, ri2 - 1, ri2 - c)
    blk_lo = jnp.logical_and(d32, jnp.logical_not(d16))

    heads = range(nh)
    sls = [slice(h * n, (h + 1) * n) for h in heads]
    mm_s = functools.partial(_mm, passes=RWKV_PASSES_S)
    mm_t = functools.partial(_mm, _dot, passes=RWKV_PASSES_T)
    a = [jnp.where(causal2, _mm(_dot_nt, lhs_all[:, sl], rhs_all[:, sl], RWKV_PASSES_A), 0.0)
         for sl in sls]
    st = [st_ref[h] for h in heads]
    part = [mm_s(_dot_nt, lhs_all[:, sls[h]], st[h]) + mm_s(_dot, a[h][:, c:], v[:, sls[h]])
            for h in heads]
    a_ub = [a[h][:c, :c] for h in heads]
    ld = [jnp.where(d16, a_ub[h], 0.0) for h in heads]
    x = [eye - ld[h] for h in heads]
    pw = [mm_t(ld[h], ld[h]) for h in heads]
    x = [x[h] + mm_t(x[h], pw[h]) for h in heads]
    pw = [mm_t(pw[h], pw[h]) for h in heads]
    x = [x[h] + mm_t(x[h], pw[h]) for h in heads]
    pw = [mm_t(pw[h], pw[h]) for h in heads]
    x = [x[h] + mm_t(x[h], pw[h]) for h in heads]
    t = [mm_t(x[h], jnp.where(blk_lo, a_ub[h], 0.0)) for h in heads]
    x = [x[h] - mm_t(t[h], x[h]) for h in heads]
    t = [mm_t(x[h], jnp.where(d32, 0.0, a_ub[h])) for h in heads]
    x = [x[h] - mm_t(t[h], x[h]) for h in heads]

    u = [-mm_s(_dot, x[h], part[h][:c]) for h in heads]
    for h in heads:
        y_ref[:, sls[h]] = part[h][c:] + mm_s(_dot, a[h][c:, :c], u[h])
    for h in heads:
        uv = jnp.concatenate([u[h], v[:, sls[h]]], axis=0)
        st_ref[h] = st[h] * p_end[:, sls[h]] + mm_s(_dot_tn, uv, dec_all[:, sls[h]])

    ones_bd = _head_sum_matrix()
    y = y_ref[...]
    mean = _head_sums(y, ones_bd) * (1.0 / n)
    yc = y - mean
    var = _head_sums(yc * yc, ones_bd) * (1.0 / n)
    yn = yc * lax.rsqrt(var + RWKV_GN_EPS) * lnw_ref[...] + lnb_ref[...]
    bonus = _head_sums(r * k * rk_ref[...], ones_bd)
    o_ref[...] = ((yn + bonus * v) * g_ref[...]).astype(o_ref.dtype)


def rwkv_chunk(r, lw, k, v, kk, b, g, p, *, heads_per_step):
    s = r.shape[0]
    c = RWKV_CHUNK
    wb = heads_per_step * RWKV_HEAD_DIM
    big = pl.BlockSpec((c, wb), lambda hg, t: (t, hg))
    row = pl.BlockSpec((1, wb), lambda hg, t: (0, hg))
    return pl.pallas_call(
        _rwkv_chunk_kernel,
        grid=(RWKV_WIDTH // wb, s // c),
        in_specs=[big] * 7 + [row] * 3,
        out_specs=big,
        out_shape=jax.ShapeDtypeStruct((s, RWKV_WIDTH), BF16),
        scratch_shapes=[
            pltpu.VMEM((heads_per_step, RWKV_HEAD_DIM, RWKV_HEAD_DIM), F32),
            pltpu.VMEM((c, wb), F32),
        ],
        compiler_params=_cparams(("parallel", "arbitrary")),
        name="rwkv_chunk",
    )(r, lw, k, v, kk, b, g, p["r_k"], p["ln_w"], p["ln_b"])


def _swap_halves(w):
    half = w.shape[-1] // 2
    return jnp.concatenate([w[..., half:], w[..., :half]], axis=-1)


def _prep_w_in(w_in):
    sizes = [SB_WIDTH] * 3 + [MLA_Q_LORA, MLA_KV_LORA, MLA_ROPE_DIM] + [RWKV_WIDTH] * 3 + [
        RWKV_DECAY_LORA, RWKV_A_LORA, RWKV_GATE_LORA]
    idx = [int(i) for i in np.cumsum(sizes)[:-1]]
    sbq, sbk, sbv, cq, ckv, krope, r, k, v, xw, xa, xg = jnp.split(w_in, idx, axis=-1)
    sbq = sbq * (SB_HEAD_DIM ** -0.5)
    cols = [r, k, v, sbq, sbk, sbv, cq, ckv, xw, xa, xg, krope, _swap_halves(krope)]
    used = sum(c.shape[-1] for c in cols)
    cols.append(jnp.zeros(w_in.shape[:-1] + (Z_COLS - used,), w_in.dtype))
    return jnp.concatenate(cols, axis=-1).astype(BF16)


def _prep_mla_w(w_uq, w_ukv):
    q = w_uq.reshape(MLA_Q_LORA, MLA_HEADS, MLA_NOPE_DIM + MLA_ROPE_DIM)
    nope, pe = q[..., :MLA_NOPE_DIM], q[..., MLA_NOPE_DIM:]
    zpad = jnp.zeros_like(pe)
    wq = jnp.concatenate([nope, pe, zpad, _swap_halves(pe), zpad], axis=-1)
    return wq.reshape(MLA_Q_LORA, MLA_HEADS * 384).astype(BF16), w_ukv.astype(BF16)


def _rope_table(positions):
    half = MLA_ROPE_DIM // 2
    inv_freq = ROPE_THETA ** (-jnp.arange(half, dtype=F32) / half)
    ang = positions.astype(F32)[:, None] * inv_freq
    cos, sin = jnp.cos(ang), jnp.sin(ang)
    z = jnp.zeros((positions.shape[0], MLA_ROPE_DIM), F32)
    return jnp.concatenate([cos, cos, z, -sin, sin, z], axis=-1)


def _pick_tile(n, candidates):
    for c in candidates:
        if n % c == 0:
            return c
    raise ValueError(f"no tile for {n}")


def kernel(x, positions, ffn1_norm, ffn1_gate, ffn1_up, ffn1_down, mix_norm, w_in, mla_q_norm,
           mla_w_uq, mla_kv_norm, mla_w_ukv, rwkv_mu, rwkv_w0, rwkv_w2, rwkv_a0, rwkv_a2, rwkv_g2,
           rwkv_k_k, rwkv_k_a, rwkv_r_k, rwkv_ln_w, rwkv_ln_b, sb_out_norm, mla_out_norm, w_out,
           ffn2_norm, ffn2_gate, ffn2_up, ffn2_down, final_norm):
    bsz, s, d = x.shape
    assert bsz == 1 and d == SB_WIDTH + MLA_WIDTH + RWKV_WIDTH
    depth = w_in.shape[0]
    d_ff = ffn1_gate.shape[-1]
    tm = _pick_tile(s, (1024, 512, 256, 128))
    tf = _pick_tile(d_ff, (512, 256, 128))
    x = x[0]
    tab = _rope_table(positions[0])
    w = RWKV_WIDTH
    zl = jnp.zeros((RWKV_DECAY_LORA, w), F32)
    ffn_w = [(wg, wu, wd.astype(BF16))
             for wg, wu, wd in ((ffn1_gate, ffn1_up, ffn1_down), (ffn2_gate, ffn2_up, ffn2_down))]
    w_in_b = _prep_w_in(w_in)
    w_out_b = w_out.astype(BF16)

    def ffn(x, g, ws, l):
        act = ffn_up(x, g, ws[0], ws[1], l, tm=tm, tn=tf)
        return ffn_down(x, act, ws[2], l, tm=tm, tn=512)

    for l in range(depth):
        x = ffn(x, ffn1_norm[l], ffn_w[0], l)

        z, zb = norm_proj(x, mix_norm[l], w_in_b, l, tm=tm, tn=768)

        y_sb = sb_attention(zb, sb_out_norm[l])

        wq, wkv = _prep_mla_w(mla_w_uq[l], mla_w_ukv[l])
        q, k, v = mla_prep(z, tab, mla_q_norm[l], mla_kv_norm[l], wq, wkv, ts=_pick_tile(s, (256, 128)))
        y_mla = mla_attention(q, k, v, mla_out_norm[l])

        mu = rwkv_mu[l]
        p = {
            "mu_rkv": mu[:3 * w].reshape(3, w),
            "mu_lora": mu[3 * w:3 * w + 128].reshape(1, 128),
            "mu_g": mu[3 * w + 128:].reshape(1, 128),
            "w0": rwkv_w0[l].reshape(1, w),
            "w2": jnp.concatenate([rwkv_w2[l], zl], axis=0).astype(BF16),
            "a0": rwkv_a0[l].reshape(1, w),
            "a2": jnp.concatenate([zl, rwkv_a2[l]], axis=0).astype(BF16),
            "g2": rwkv_g2[l].astype(BF16),
            "k_k": rwkv_k_k[l].reshape(1, w),
            "k_a": rwkv_k_a[l].reshape(1, w),
            "r_k": rwkv_r_k[l].reshape(1, w),
            "ln_w": rwkv_ln_w[l].reshape(1, w),
            "ln_b": rwkv_ln_b[l].reshape(1, w),
        }
        r_, lw_, k_, v_, kk_, b_, g_ = rwkv_prep(z, p, ts=_pick_tile(s, (256, 128)))
        y_rwkv = rwkv_chunk(r_, lw_, k_, v_, kk_, b_, g_, p, heads_per_step=16)

        x = out_proj(x, y_sb, y_mla, y_rwkv, w_out_b, l, tm=_pick_tile(s, (512, 256, 128)))

        x = ffn(x, ffn2_norm[l], ffn_w[1], l)

    return final_rmsnorm(x, final_norm, tm=_pick_tile(s, (256, 128)))[None]
```

```python
import functools

import jax
import jax.numpy as jnp
import numpy as np
from jax import lax
from jax.experimental import pallas as pl
from jax.experimental.pallas import tpu as pltpu

F32 = jnp.float32
BF16 = jnp.bfloat16

NORM_EPS = 1e-6
SB_HEADS = 4
SB_HEAD_DIM = 128
SB_WIDTH = SB_HEADS * SB_HEAD_DIM
MLA_HEADS = 4
MLA_NOPE_DIM = 128
MLA_ROPE_DIM = 64
MLA_V_DIM = 128
MLA_Q_LORA = 512
MLA_KV_LORA = 256
MLA_WIDTH = MLA_HEADS * MLA_V_DIM
MLA_QK_PAD = 256
ROPE_THETA = 10000.0
RWKV_HEADS = 16
RWKV_HEAD_DIM = 64
RWKV_WIDTH = RWKV_HEADS * RWKV_HEAD_DIM
RWKV_DECAY_LORA = 64
RWKV_A_LORA = 64
RWKV_GATE_LORA = 128
RWKV_GN_EPS = 64e-5
RWKV_CHUNK = 64
RWKV_GROUP = 4
RWKV_CHUNKS_PER_STEP = 4

COL_R, COL_K, COL_V = 0, 1024, 2048
COL_SBQ, COL_SBK, COL_SBV = 3072, 3584, 4096
COL_CQ, COL_CKV = 4608, 5120
COL_LORA, COL_XG, COL_ROPE = 5376, 5504, 5632
Z_COLS = 6144

VMEM_LIMIT = 48 * 1024 * 1024


def _cparams(sem):
    return pltpu.CompilerParams(dimension_semantics=sem, vmem_limit_bytes=VMEM_LIMIT)


def _dot(a, b):
    return jnp.dot(a, b, preferred_element_type=F32)


def _dot_nt(a, b):
    return lax.dot_general(a, b, (((1,), (1,)), ((), ())), preferred_element_type=F32)


def _dot_tn(a, b):
    return lax.dot_general(a, b, (((0,), (0,)), ((), ())), preferred_element_type=F32)


def _split3(x):
    hi = x.astype(BF16)
    r1 = x - hi.astype(F32)
    mid = r1.astype(BF16)
    lo = (r1 - mid.astype(F32)).astype(BF16)
    return hi, mid, lo


def _split2(x):
    hi = x.astype(BF16)
    lo = (x - hi.astype(F32)).astype(BF16)
    return hi, lo


def _mm3(dot, a, b):
    ah, al = _split2(a)
    bh, bl = _split2(b)
    return dot(ah, bh) + (dot(ah, bl) + dot(al, bh))


def _mm(dot, a, b, passes):
    if passes == 1:
        return dot(a.astype(BF16), b.astype(BF16))
    return _mm3(dot, a, b)


def _exact_lhs_dot(m_bf16, x):
    hi, mid, lo = _split3(x)
    return _dot(m_bf16, hi) + (_dot(m_bf16, mid) + _dot(m_bf16, lo))


def _rms_rows(x, g):
    ms = jnp.mean(x * x, axis=-1, keepdims=True)
    return x * lax.rsqrt(ms + NORM_EPS) * g


def _norm_proj_kernel(x_ref, g_ref, w_ref, o_ref, ob_ref, h_ref):
    @pl.when(pl.program_id(1) == 0)
    def _():
        h_ref[...] = _rms_rows(x_ref[...], g_ref[...]).astype(BF16)

    z = _dot(h_ref[...], w_ref[...])
    o_ref[...] = z
    ob_ref[...] = z.astype(BF16)


def norm_proj(x, g, w, l, *, tm, tn):
    s, d = x.shape
    n = w.shape[2]
    return pl.pallas_call(
        _norm_proj_kernel,
        grid=(s // tm, n // tn),
        in_specs=[
            pl.BlockSpec((tm, d), lambda i, j: (i, 0)),
            pl.BlockSpec((1, d), lambda i, j: (0, 0)),
            pl.BlockSpec((None, d, tn), lambda i, j: (l, 0, j)),
        ],
        out_specs=[pl.BlockSpec((tm, tn), lambda i, j: (i, j))] * 2,
        out_shape=[jax.ShapeDtypeStruct((s, n), F32), jax.ShapeDtypeStruct((s, n), BF16)],
        scratch_shapes=[pltpu.VMEM((tm, d), BF16)],
        compiler_params=_cparams(("parallel", "arbitrary")),
        name="norm_proj",
    )(x, g.reshape(1, d), w)


def _ffn_up_kernel(x_ref, g_ref, wg_ref, wu_ref, o_ref, h_ref):
    @pl.when(pl.program_id(1) == 0)
    def _():
        h_ref[...] = _rms_rows(x_ref[...], g_ref[...]).astype(BF16)

    h = h_ref[...]
    a = _dot(h, wg_ref[...].astype(BF16))
    u = _dot(h, wu_ref[...].astype(BF16))
    o_ref[...] = (a * jax.nn.sigmoid(a) * u).astype(o_ref.dtype)


def ffn_up(x, g, wg, wu, l, *, tm, tn):
    s, d = x.shape
    f = wg.shape[2]
    return pl.pallas_call(
        _ffn_up_kernel,
        grid=(s // tm, f // tn),
        in_specs=[
            pl.BlockSpec((tm, d), lambda i, j: (i, 0)),
            pl.BlockSpec((1, d), lambda i, j: (0, 0)),
            pl.BlockSpec((None, d, tn), lambda i, j: (l, 0, j)),
            pl.BlockSpec((None, d, tn), lambda i, j: (l, 0, j)),
        ],
        out_specs=pl.BlockSpec((tm, tn), lambda i, j: (i, j)),
        out_shape=jax.ShapeDtypeStruct((s, f), BF16),
        scratch_shapes=[pltpu.VMEM((tm, d), BF16)],
        compiler_params=_cparams(("parallel", "arbitrary")),
        name="ffn_up",
    )(x, g.reshape(1, d), wg, wu)


def _ffn_down_kernel(x_ref, a_ref, w_ref, o_ref):
    o_ref[...] = x_ref[...] + 0.5 * _dot(a_ref[...], w_ref[...])


def ffn_down(x, act, wd, l, *, tm, tn):
    s, d = x.shape
    f = act.shape[1]
    return pl.pallas_call(
        _ffn_down_kernel,
        grid=(s // tm, d // tn),
        in_specs=[
            pl.BlockSpec((tm, tn), lambda i, j: (i, j)),
            pl.BlockSpec((tm, f), lambda i, j: (i, 0)),
            pl.BlockSpec((None, f, tn), lambda i, j: (l, 0, j)),
        ],
        out_specs=pl.BlockSpec((tm, tn), lambda i, j: (i, j)),
        out_shape=jax.ShapeDtypeStruct((s, d), F32),
        compiler_params=_cparams(("parallel", "arbitrary")),
        name="ffn_down",
    )(x, act, wd)


def _out_proj_kernel(x_ref, a_ref, b_ref, c_ref, wa_ref, wb_ref, wc_ref, o_ref):
    acc = _dot(a_ref[...], wa_ref[...])
    acc += _dot(b_ref[...], wb_ref[...])
    acc += _dot(c_ref[...], wc_ref[...])
    o_ref[...] = x_ref[...] + acc


def out_proj(x, y_sb, y_mla, y_rwkv, w_out, l, *, tm):
    s, d = x.shape
    assert SB_WIDTH == MLA_WIDTH and RWKV_WIDTH == SB_WIDTH + MLA_WIDTH
    wspec = lambda rows, blk: pl.BlockSpec((None, rows, d), lambda i: (l, blk, 0),
                                           pipeline_mode=pl.Buffered(1))
    return pl.pallas_call(
        _out_proj_kernel,
        grid=(s // tm,),
        in_specs=[
            pl.BlockSpec((tm, d), lambda i: (i, 0)),
            pl.BlockSpec((tm, SB_WIDTH), lambda i: (i, 0)),
            pl.BlockSpec((tm, MLA_WIDTH), lambda i: (i, 0)),
            pl.BlockSpec((tm, RWKV_WIDTH), lambda i: (i, 0)),
            wspec(SB_WIDTH, 0),
            wspec(MLA_WIDTH, 1),
            wspec(RWKV_WIDTH, 1),
        ],
        out_specs=pl.BlockSpec((tm, d), lambda i: (i, 0)),
        out_shape=jax.ShapeDtypeStruct((s, d), F32),
        compiler_params=_cparams(("parallel",)),
        name="out_proj",
    )(x, y_sb, y_mla, y_rwkv, w_out, w_out, w_out)


def _final_norm_kernel(x_ref, g_ref, o_ref):
    o_ref[...] = _rms_rows(x_ref[...], g_ref[...])


def final_rmsnorm(x, g, *, tm):
    s, d = x.shape
    return pl.pallas_call(
        _final_norm_kernel,
        grid=(s // tm,),
        in_specs=[pl.BlockSpec((tm, d), lambda i: (i, 0)), pl.BlockSpec((1, d), lambda i: (0, 0))],
        out_specs=pl.BlockSpec((tm, d), lambda i: (i, 0)),
        out_shape=jax.ShapeDtypeStruct((s, d), F32),
        compiler_params=_cparams(("parallel",)),
        name="final_norm",
    )(x, g.reshape(1, d))


SB_BQ = 256
SB_BK = 128
SB_DEAD_LOG = -104.0


def _sb_attn_kernel(q_ref, k_ref, v_ref, g_ref, o_ref):
    i = pl.program_id(0)
    bq, bk, dh = SB_BQ, SB_BK, SB_HEAD_DIM
    heads = range(SB_HEADS)
    hs = [slice(h * dh, (h + 1) * dh) for h in heads]
    q = [q_ref[:, hs[h]] for h in heads]
    row = lax.broadcasted_iota(jnp.int32, (bq, bk), 0)
    col = lax.broadcasted_iota(jnp.int32, (bq, bk), 1)
    kr = lax.broadcasted_iota(jnp.int32, (bk, bk), 0)
    kc = lax.broadcasted_iota(jnp.int32, (bk, bk), 1)
    upper = jnp.where(kr > kc, 1.0, 0.0).astype(BF16)

    def block(kb, carry, acc, masked):
        start = pl.multiple_of(kb * bk, bk)
        z = [_dot_nt(q[h], k_ref[pl.ds(start, bk), hs[h]]) for h in heads]
        log_beta = [jnp.minimum(z[h], 0.0) - jnp.log(1.0 + jnp.exp(-jnp.abs(z[h]))) for h in heads]
        log_keep = [log_beta[h] - z[h] for h in heads]
        if masked:
            before = (start + col) < (i * bq + row)
            log_keep = [jnp.where(before, log_keep[h], 0.0) for h in heads]
        split = [_split2(log_keep[h]) for h in heads]
        within = [_dot(split[h][0], upper) + _dot(split[h][1], upper) for h in heads]
        a = [jnp.exp(log_beta[h] + (carry[h] + within[h])) for h in heads]
        if masked:
            a = [jnp.where(before, a[h], 0.0) for h in heads]
        acc = [acc[h] + _dot(a[h].astype(BF16), v_ref[pl.ds(start, bk), hs[h]]) for h in heads]
        carry = [carry[h] + jnp.sum(log_keep[h], axis=-1, keepdims=True) for h in heads]
        return carry, acc

    def alive_of(carry):
        m = jnp.max(carry[0])
        for h in heads[1:]:
            m = jnp.maximum(m, jnp.max(carry[h]))
        return m

    carry = [jnp.zeros((bq, 1), F32) for _ in heads]
    acc = [jnp.zeros((bq, dh), F32) for _ in heads]
    nd = bq // bk
    first = i * nd
    for d in range(nd - 1, -1, -1):
        carry, acc = block(first + d, carry, acc, True)

    def cond(st):
        return jnp.logical_and(st[0] >= 0, st[1] > SB_DEAD_LOG)

    def body(st):
        kb, _, carry, acc = st
        carry, acc = block(kb, list(carry), list(acc), False)
        return kb - 1, alive_of(carry), tuple(carry), tuple(acc)

    st = lax.while_loop(cond, body, (first - 1, alive_of(carry), tuple(carry), tuple(acc)))
    acc = st[3]
    for h in heads:
        o_ref[:, hs[h]] = _rms_rows(acc[h], g_ref[h:h + 1, :]).astype(o_ref.dtype)


def sb_attention(zb, gains):
    s = zb.shape[0]
    whole = lambda c: pl.BlockSpec((s, SB_WIDTH), lambda i: (0, c // SB_WIDTH),
                                   pipeline_mode=pl.Buffered(1))
    return pl.pallas_call(
        _sb_attn_kernel,
        grid=(s // SB_BQ,),
        in_specs=[
            pl.BlockSpec((SB_BQ, SB_WIDTH), lambda i: (i, COL_SBQ // SB_WIDTH)),
            whole(COL_SBK),
            whole(COL_SBV),
            pl.BlockSpec((SB_HEADS, SB_HEAD_DIM), lambda i: (0, 0)),
        ],
        out_specs=pl.BlockSpec((SB_BQ, SB_WIDTH), lambda i: (i, 0)),
        out_shape=jax.ShapeDtypeStruct((s, SB_WIDTH), BF16),
        compiler_params=_cparams(("arbitrary",)),
        name="sb_attention",
    )(zb, zb, zb, gains)


def _mla_prep_kernel(cq_ref, ckv_ref, rope_ref, tab_ref, qn_ref, kvn_ref, wq_ref, wkv_ref,
                     q_ref, k_ref, v_ref):
    scale = (MLA_NOPE_DIM + MLA_ROPE_DIM) ** -0.5
    hq = _rms_rows(cq_ref[...], qn_ref[...]).astype(BF16)
    hkv = _rms_rows(ckv_ref[...], kvn_ref[...]).astype(BF16)
    cos2 = tab_ref[:, :128]
    sin2 = tab_ref[:, 128:]
    kr = rope_ref[...]
    k_pe = kr * cos2 + pltpu.roll(kr, 64, 1) * sin2
    k_pe = k_pe.astype(BF16)
    qall = _dot(hq, wq_ref[...])
    kvall = _dot(hkv, wkv_ref[...])
    for h in range(MLA_HEADS):
        qh = qall[:, h * 384:(h + 1) * 384]
        q_ref[h, :, :128] = (qh[:, :128] * scale).astype(BF16)
        q_ref[h, :, 128:] = ((qh[:, 128:256] * cos2 + qh[:, 256:384] * sin2) * scale).astype(BF16)
        k_ref[h, :, :128] = kvall[:, h * 256:h * 256 + 128].astype(BF16)
        k_ref[h, :, 128:] = k_pe
        v_ref[h, :, :128] = kvall[:, h * 256 + 128:(h + 1) * 256].astype(BF16)
        v_ref[h, :, 128:] = jnp.ones((kr.shape[0], 128), BF16)


def mla_prep(z, tab, q_norm, kv_norm, wq, wkv, *, ts):
    s = z.shape[0]
    return pl.pallas_call(
        _mla_prep_kernel,
        grid=(s // ts,),
        in_specs=[
            pl.BlockSpec((ts, MLA_Q_LORA), lambda i: (i, COL_CQ // MLA_Q_LORA)),
            pl.BlockSpec((ts, MLA_KV_LORA), lambda i: (i, COL_CKV // MLA_KV_LORA)),
            pl.BlockSpec((ts, 128), lambda i: (i, COL_ROPE // 128)),
            pl.BlockSpec((ts, 256), lambda i: (i, 0)),
            pl.BlockSpec((1, MLA_Q_LORA), lambda i: (0, 0)),
            pl.BlockSpec((1, MLA_KV_LORA), lambda i: (0, 0)),
            pl.BlockSpec(wq.shape, lambda i: (0, 0)),
            pl.BlockSpec(wkv.shape, lambda i: (0, 0)),
        ],
        out_specs=[
            pl.BlockSpec((MLA_HEADS, ts, MLA_QK_PAD), lambda i: (0, i, 0)),
            pl.BlockSpec((MLA_HEADS, ts, MLA_QK_PAD), lambda i: (0, i, 0)),
            pl.BlockSpec((MLA_HEADS, ts, 2 * MLA_V_DIM), lambda i: (0, i, 0)),
        ],
        out_shape=[
            jax.ShapeDtypeStruct((MLA_HEADS, s, MLA_QK_PAD), BF16),
            jax.ShapeDtypeStruct((MLA_HEADS, s, MLA_QK_PAD), BF16),
            jax.ShapeDtypeStruct((MLA_HEADS, s, 2 * MLA_V_DIM), BF16),
        ],
        compiler_params=_cparams(("parallel",)),
        name="mla_prep",
    )(z, z, z, tab, q_norm.reshape(1, -1), kv_norm.reshape(1, -1), wq, wkv)


MLA_BQ = 256
MLA_BK = 512
MLA_HEADS_PER_STEP = 4


def _mla_attn_kernel(q_ref, k_ref, v_ref, g_ref, o_ref):
    i = pl.program_id(1)
    bq, bk = MLA_BQ, MLA_BK
    heads = range(MLA_HEADS_PER_STEP)
    q = [q_ref[h] for h in heads]
    nd = bk // bq

    def block(kb, m, acc, masked):
        start = pl.multiple_of(kb * bk, bk)
        sc = [_dot_nt(q[h], k_ref[h, pl.ds(start, bk), :]) for h in heads]
        if masked:
            row = lax.broadcasted_iota(jnp.int32, (bq, bk), 0) + i * bq
            col = lax.broadcasted_iota(jnp.int32, (bq, bk), 1) + start
            sc = [jnp.where(col <= row, sc[h], -1e30) for h in heads]
        m_new = [jnp.maximum(m[h], jnp.max(sc[h], axis=-1, keepdims=True)) for h in heads]
        for h in heads:
            alpha = jnp.exp(m[h] - m_new[h])
            p = jnp.exp(sc[h] - m_new[h])
            acc[h] = acc[h] * alpha + _dot(p.astype(BF16), v_ref[h, pl.ds(start, bk), :])
        return m_new, acc

    def body(kb, st):
        m, acc = block(kb, list(st[0]), list(st[1]), False)
        return tuple(m), tuple(acc)

    m0 = tuple(jnp.full((bq, 1), -1e30, F32) for _ in heads)
    a0 = tuple(jnp.zeros((bq, 2 * MLA_V_DIM), F32) for _ in heads)
    last = i // nd
    m, acc = lax.fori_loop(0, last, body, (m0, a0))
    m, acc = block(last, list(m), list(acc), True)
    for h in heads:
        y = acc[h][:, :MLA_V_DIM] / acc[h][:, MLA_V_DIM:]
        o_ref[:, h * MLA_V_DIM:(h + 1) * MLA_V_DIM] = _rms_rows(y, g_ref[h:h + 1, :]).astype(o_ref.dtype)


def mla_attention(q, k, v, gains):
    s = q.shape[1]
    hps = MLA_HEADS_PER_STEP
    whole = lambda n: pl.BlockSpec((hps, s, n), lambda g, i: (g, 0, 0), pipeline_mode=pl.Buffered(1))
    return pl.pallas_call(
        _mla_attn_kernel,
        grid=(MLA_HEADS // hps, s // MLA_BQ),
        in_specs=[
            pl.BlockSpec((hps, MLA_BQ, MLA_QK_PAD), lambda g, i: (g, i, 0)),
            whole(MLA_QK_PAD),
            whole(2 * MLA_V_DIM),
            pl.BlockSpec((hps, MLA_V_DIM), lambda g, i: (g, 0)),
        ],
        out_specs=pl.BlockSpec((MLA_BQ, hps * MLA_V_DIM), lambda g, i: (i, g)),
        out_shape=jax.ShapeDtypeStruct((s, MLA_WIDTH), BF16),
        compiler_params=_cparams(("parallel", "arbitrary")),
        name="mla_attention",
    )(q, k, v, gains)


def _head_sum_matrix():
    r = lax.broadcasted_iota(jnp.int32, (128, 128), 0) // RWKV_HEAD_DIM
    c = lax.broadcasted_iota(jnp.int32, (128, 128), 1) // RWKV_HEAD_DIM
    return jnp.where(r == c, 1.0, 0.0).astype(BF16)


def _head_sums(x, ones_bd):
    parts = []
    for j in range(x.shape[1] // 128):
        parts.append(_exact_lhs_dot_rhs(x[:, j * 128:(j + 1) * 128], ones_bd))
    return jnp.concatenate(parts, axis=-1)


def _exact_lhs_dot_rhs(x, m_bf16):
    hi, mid, lo = _split3(x)
    return _dot(hi, m_bf16) + (_dot(mid, m_bf16) + _dot(lo, m_bf16))


def _rwkv_prep_kernel(r_ref, k_ref, v_ref, lora_ref, xg_ref, mu_ref, mul_ref, mug_ref,
                      w0_ref, w2_ref, a0_ref, a2_ref, g2_ref, kk_ref, ka_ref,
                      ro_ref, lw_ref, ko_ref, vo_ref, kko_ref, bo_ref, go_ref,
                      prev_ref, prevl_ref):
    ts = r_ref.shape[0]

    @pl.when(pl.program_id(0) == 0)
    def _():
        prev_ref[...] = jnp.zeros_like(prev_ref)
        prevl_ref[...] = jnp.zeros_like(prevl_ref)

    first_row = lax.broadcasted_iota(jnp.int32, (ts, 1), 0) == 0

    def shift_mix(x, prev_row, mu):
        x_prev = jnp.where(first_row, prev_row, pltpu.roll(x, 1, 0))
        return x + (x_prev - x) * mu

    r_in, k_in, v_in = r_ref[...], k_ref[...], v_ref[...]
    lora_in, xg_in = lora_ref[...], xg_ref[...]
    r = shift_mix(r_in, prev_ref[0:1, :], mu_ref[0:1, :])
    k = shift_mix(k_in, prev_ref[1:2, :], mu_ref[1:2, :])
    v = shift_mix(v_in, prev_ref[2:3, :], mu_ref[2:3, :])
    lora = shift_mix(lora_in, prevl_ref[0:1, :], mul_ref[...])
    xg = shift_mix(xg_in, prevl_ref[1:2, :], mug_ref[...])
    prev_ref[0:1, :] = r_in[ts - 1:ts, :]
    prev_ref[1:2, :] = k_in[ts - 1:ts, :]
    prev_ref[2:3, :] = v_in[ts - 1:ts, :]
    prevl_ref[0:1, :] = lora_in[ts - 1:ts, :]
    prevl_ref[1:2, :] = xg_in[ts - 1:ts, :]

    dw = w0_ref[...] + _dot(jnp.tanh(lora).astype(BF16), w2_ref[...])
    log_w = -jax.nn.softplus(-dw) - 0.5
    lw_ref[...] = -jnp.exp(log_w)
    a = jax.nn.sigmoid(a0_ref[...] + _dot(lora.astype(BF16), a2_ref[...]))
    go_ref[...] = _dot(jax.nn.sigmoid(xg).astype(BF16), g2_ref[...])

    kk = k * kk_ref[...]
    ss = _head_sums(kk * kk, _head_sum_matrix())
    kk = kk / jnp.maximum(jnp.sqrt(ss), 1e-12)
    ro_ref[...] = r
    vo_ref[...] = v
    ko_ref[...] = k * (1.0 + (a - 1.0) * ka_ref[...])
    kko_ref[...] = kk
    bo_ref[...] = kk * a


def rwkv_prep(z, p, *, ts):
    s = z.shape[0]
    w = RWKV_WIDTH
    row = lambda n: pl.BlockSpec((1, n), lambda i: (0, 0))
    big = pl.BlockSpec((ts, w), lambda i: (i, 0))
    out = jax.ShapeDtypeStruct((s, w), F32)
    return pl.pallas_call(
        _rwkv_prep_kernel,
        grid=(s // ts,),
        in_specs=[
            pl.BlockSpec((ts, w), lambda i: (i, COL_R // w)),
            pl.BlockSpec((ts, w), lambda i: (i, COL_K // w)),
            pl.BlockSpec((ts, w), lambda i: (i, COL_V // w)),
            pl.BlockSpec((ts, 128), lambda i: (i, COL_LORA // 128)),
            pl.BlockSpec((ts, 128), lambda i: (i, COL_XG // 128)),
            pl.BlockSpec((3, w), lambda i: (0, 0)),
            row(128), row(128),
            row(w), pl.BlockSpec((128, w), lambda i: (0, 0)),
            row(w), pl.BlockSpec((128, w), lambda i: (0, 0)),
            pl.BlockSpec((128, w), lambda i: (0, 0)),
            row(w), row(w),
        ],
        out_specs=[big] * 7,
        out_shape=[out] * 7,
        scratch_shapes=[pltpu.VMEM((8, w), F32), pltpu.VMEM((8, 128), F32)],
        compiler_params=_cparams(("arbitrary",)),
        name="rwkv_prep",
    )(z, z, z, z, z, p["mu_rkv"], p["mu_lora"], p["mu_g"], p["w0"], p["w2"], p["a0"], p["a2"],
      p["g2"], p["k_k"], p["k_a"])


def _rwkv_chunk_kernel(r_ref, lw_ref, k_ref, v_ref, kk_ref, b_ref, g_ref, rk_ref, lnw_ref, lnb_ref,
                       o_ref, st_ref, y_ref):
    c = RWKV_CHUNK
    n = RWKV_HEAD_DIM
    nh = r_ref.shape[1] // n
    chunks = range(r_ref.shape[0] // c)

    @pl.when(pl.program_id(1) == 0)
    def _():
        st_ref[...] = jnp.zeros_like(st_ref)

    ri = lax.broadcasted_iota(jnp.int32, (c, c), 0)
    ci = lax.broadcasted_iota(jnp.int32, (c, c), 1)
    tril_ones = jnp.where(ri >= ci, 1.0, 0.0).astype(BF16)

    gh = RWKV_GROUP
    gw = gh * n
    groups = range(nh // gh)
    gs = [slice(g * gw, (g + 1) * gw) for g in groups]
    same_head = (lax.broadcasted_iota(jnp.int32, (gw, gw), 0) // n
                 == lax.broadcasted_iota(jnp.int32, (gw, gw), 1) // n)
    zero_b = jnp.zeros((gw, gw), BF16)

    def bd(y):
        yb = y.astype(BF16)
        return jnp.where(same_head, jnp.concatenate([yb] * gh, axis=0), zero_b)

    def mmb(x, y_bd):
        return _dot(x.astype(BF16), y_bd)

    ri2 = lax.broadcasted_iota(jnp.int32, (2 * c, gw), 0)
    ci2 = lax.broadcasted_iota(jnp.int32, (2 * c, gw), 1) % c
    causal2 = ci2 <= jnp.where(ri2 < c, ri2 - 1, ri2 - c)
    rc = lax.broadcasted_iota(jnp.int32, (c, gw), 0)
    cc = lax.broadcasted_iota(jnp.int32, (c, gw), 1) % c
    eye_g = jnp.where(rc == cc, 1.0, 0.0).astype(F32)
    d16_g = (rc // 16) == (cc // 16)
    d32_g = (rc // 32) == (cc // 32)
    lo_g = jnp.logical_and(d32_g, jnp.logical_not(d16_g))
    colhead = lax.broadcasted_iota(jnp.int32, (n, gw), 1) // n

    p_end, v_c, lhs, b_t, k_t, dec = [], [], [], [], [], []
    for ch in chunks:
        rows = slice(ch * c, (ch + 1) * c)
        lw = lw_ref[rows, :]
        cum = _exact_lhs_dot(tril_ones, lw)
        p_incl = jnp.exp(cum)
        p_inv = jnp.exp(-cum)
        pe = p_incl[c - 1:c, :]
        bt = b_ref[rows, :] * p_inv
        kt = k_ref[rows, :] * p_inv
        p_end.append(pe)
        v_c.append(v_ref[rows, :])
        b_t.append(bt)
        k_t.append(kt)
        lhs.append(jnp.concatenate([kk_ref[rows, :] * jnp.exp(cum - lw), r_ref[rows, :] * p_incl],
                                   axis=0).astype(BF16))
        dec.append(jnp.concatenate([bt * pe, kt * pe], axis=0).astype(BF16))

    cg = [(ch, g) for ch in chunks for g in groups]
    ab = {q: jnp.where(causal2, _dot_nt(lhs[q[0]][:, gs[q[1]]], bd(b_t[q[0]][:, gs[q[1]]])), 0.0)
          for q in cg}
    ak = {q: jnp.where(causal2, _dot_nt(lhs[q[0]][:, gs[q[1]]], bd(k_t[q[0]][:, gs[q[1]]])), 0.0)
          for q in cg}
    akv = {q: mmb(ak[q], bd(v_c[q[0]][:, gs[q[1]]])) for q in cg}
    a_ub = {q: ab[q][:c] for q in cg}
    ld = {q: jnp.where(d16_g, a_ub[q], 0.0) for q in cg}
    x = {q: eye_g - ld[q] for q in cg}
    pw = {q: mmb(ld[q], bd(ld[q])) for q in cg}
    x = {q: x[q] + mmb(x[q], bd(pw[q])) for q in cg}
    pw = {q: mmb(pw[q], bd(pw[q])) for q in cg}
    x = {q: x[q] + mmb(x[q], bd(pw[q])) for q in cg}
    pw = {q: mmb(pw[q], bd(pw[q])) for q in cg}
    x = {q: x[q] + mmb(x[q], bd(pw[q])) for q in cg}
    t = {q: mmb(x[q], bd(jnp.where(lo_g, a_ub[q], 0.0))) for q in cg}
    x = {q: x[q] - mmb(t[q], bd(x[q])) for q in cg}
    t = {q: mmb(x[q], bd(jnp.where(d32_g, 0.0, a_ub[q]))) for q in cg}
    x = {q: x[q] - mmb(t[q], bd(x[q])) for q in cg}
    u0 = {q: -mmb(x[q], bd(akv[q][:c])) for q in cg}
    wk = {q: -mmb(x[q], bd(lhs[q[0]][:c, gs[q[1]]])) for q in cg}
    lhs2 = {q: jnp.concatenate([wk[q].astype(BF16), lhs[q[0]][c:, gs[q[1]]]], axis=0) for q in cg}

    st = [st_ref[:, gs[g]] for g in groups]
    for ch in chunks:
        rows = slice(ch * c, (ch + 1) * c)
        through = [_dot_nt(lhs2[(ch, g)], bd(st[g])) for g in groups]
        u = [through[g][:c] + u0[(ch, g)] for g in groups]
        for g in groups:
            y_ref[rows, gs[g]] = through[g][c:] + akv[(ch, g)][c:] + mmb(ab[(ch, g)][c:], bd(u[g]))
        for g in groups:
            uv = jnp.concatenate([u[g], v_c[ch][:, gs[g]]], axis=0).astype(BF16)
            cross = _dot_tn(uv, dec[ch][:, gs[g]])
            upd = st[g] * p_end[ch][:, gs[g]]
            for h in range(gh):
                upd = upd + jnp.where(colhead == h, cross[h * n:(h + 1) * n], 0.0)
            st[g] = upd
    for g in groups:
        st_ref[:, gs[g]] = st[g]

    ones_bd = _head_sum_matrix()
    y = y_ref[...]
    v = v_ref[...]
    mean = _head_sums(y, ones_bd) * (1.0 / n)
    yc = y - mean
    var = _head_sums(yc * yc, ones_bd) * (1.0 / n)
    yn = yc * lax.rsqrt(var + RWKV_GN_EPS) * lnw_ref[...] + lnb_ref[...]
    bonus = _head_sums(r_ref[...] * k_ref[...] * rk_ref[...], ones_bd)
    o_ref[...] = ((yn + bonus * v) * g_ref[...]).astype(o_ref.dtype)


def rwkv_chunk(r, lw, k, v, kk, b, g, p):
    s, w = r.shape
    rows = RWKV_CHUNK * RWKV_CHUNKS_PER_STEP
    big = pl.BlockSpec((rows, w), lambda hg, t: (t, hg))
    row = pl.BlockSpec((1, w), lambda hg, t: (0, hg))
    return pl.pallas_call(
        _rwkv_chunk_kernel,
        grid=(1, s // rows),
        in_specs=[big] * 7 + [row] * 3,
        out_specs=big,
        out_shape=jax.ShapeDtypeStruct((s, w), BF16),
        scratch_shapes=[
            pltpu.VMEM((RWKV_HEAD_DIM, w), F32),
            pltpu.VMEM((rows, w), F32),
        ],
        compiler_params=_cparams(("parallel", "arbitrary")),
        name="rwkv_chunk",
    )(r, lw, k, v, kk, b, g, p["r_k"], p["ln_w"], p["ln_b"])


def _swap_halves(w):
    half = w.shape[-1] // 2
    return jnp.concatenate([w[..., half:], w[..., :half]], axis=-1)


def _prep_w_in(w_in):
    sizes = [SB_WIDTH] * 3 + [MLA_Q_LORA, MLA_KV_LORA, MLA_ROPE_DIM] + [RWKV_WIDTH] * 3 + [
        RWKV_DECAY_LORA, RWKV_A_LORA, RWKV_GATE_LORA]
    idx = [int(i) for i in np.cumsum(sizes)[:-1]]
    sbq, sbk, sbv, cq, ckv, krope, r, k, v, xw, xa, xg = jnp.split(w_in, idx, axis=-1)
    sbq = sbq * (SB_HEAD_DIM ** -0.5)
    cols = [r, k, v, sbq, sbk, sbv, cq, ckv, xw, xa, xg, krope, _swap_halves(krope)]
    used = sum(c.shape[-1] for c in cols)
    cols.append(jnp.zeros(w_in.shape[:-1] + (Z_COLS - used,), w_in.dtype))
    return jnp.concatenate(cols, axis=-1).astype(BF16)


def _prep_mla_w(w_uq, w_ukv):
    q = w_uq.reshape(MLA_Q_LORA, MLA_HEADS, MLA_NOPE_DIM + MLA_ROPE_DIM)
    nope, pe = q[..., :MLA_NOPE_DIM], q[..., MLA_NOPE_DIM:]
    zpad = jnp.zeros_like(pe)
    wq = jnp.concatenate([nope, pe, zpad, _swap_halves(pe), zpad], axis=-1)
    return wq.reshape(MLA_Q_LORA, MLA_HEADS * 384).astype(BF16), w_ukv.astype(BF16)


def _rope_table(positions):
    half = MLA_ROPE_DIM // 2
    inv_freq = ROPE_THETA ** (-jnp.arange(half, dtype=F32) / half)
    ang = positions.astype(F32)[:, None] * inv_freq
    cos, sin = jnp.cos(ang), jnp.sin(ang)
    z = jnp.zeros((positions.shape[0], MLA_ROPE_DIM), F32)
    return jnp.concatenate([cos, cos, z, -sin, sin, z], axis=-1)


def _pick_tile(n, candidates):
    for c in candidates:
        if n % c == 0:
            return c
    raise ValueError(f"no tile for {n}")


def kernel(x, positions, ffn1_norm, ffn1_gate, ffn1_up, ffn1_down, mix_norm, w_in, mla_q_norm,
           mla_w_uq, mla_kv_norm, mla_w_ukv, rwkv_mu, rwkv_w0, rwkv_w2, rwkv_a0, rwkv_a2, rwkv_g2,
           rwkv_k_k, rwkv_k_a, rwkv_r_k, rwkv_ln_w, rwkv_ln_b, sb_out_norm, mla_out_norm, w_out,
           ffn2_norm, ffn2_gate, ffn2_up, ffn2_down, final_norm):
    bsz, s, d = x.shape
    assert bsz == 1 and d == SB_WIDTH + MLA_WIDTH + RWKV_WIDTH
    depth = w_in.shape[0]
    d_ff = ffn1_gate.shape[-1]
    tm = _pick_tile(s, (1024, 512, 256, 128))
    tf = _pick_tile(d_ff, (512, 256, 128))
    x = x[0]
    tab = _rope_table(positions[0])
    w = RWKV_WIDTH
    zl = jnp.zeros((RWKV_DECAY_LORA, w), F32)
    ffn_w = [(wg, wu, wd.astype(BF16))
             for wg, wu, wd in ((ffn1_gate, ffn1_up, ffn1_down), (ffn2_gate, ffn2_up, ffn2_down))]
    w_in_b = _prep_w_in(w_in)
    w_out_b = w_out.astype(BF16)

    def ffn(x, g, ws, l):
        act = ffn_up(x, g, ws[0], ws[1], l, tm=tm, tn=tf)
        return ffn_down(x, act, ws[2], l, tm=tm, tn=512)

    for l in range(depth):
        x = ffn(x, ffn1_norm[l], ffn_w[0], l)

        z, zb = norm_proj(x, mix_norm[l], w_in_b, l, tm=tm, tn=768)

        y_sb = sb_attention(zb, sb_out_norm[l])

        wq, wkv = _prep_mla_w(mla_w_uq[l], mla_w_ukv[l])
        q, k, v = mla_prep(z, tab, mla_q_norm[l], mla_kv_norm[l], wq, wkv, ts=_pick_tile(s, (256, 128)))
        y_mla = mla_attention(q, k, v, mla_out_norm[l])

        mu = rwkv_mu[l]
        p = {
            "mu_rkv": mu[:3 * w].reshape(3, w),
            "mu_lora": mu[3 * w:3 * w + 128].reshape(1, 128),
            "mu_g": mu[3 * w + 128:].reshape(1, 128),
            "w0": rwkv_w0[l].reshape(1, w),
            "w2": jnp.concatenate([rwkv_w2[l], zl], axis=0).astype(BF16),
            "a0": rwkv_a0[l].reshape(1, w),
            "a2": jnp.concatenate([zl, rwkv_a2[l]], axis=0).astype(BF16),
            "g2": rwkv_g2[l].astype(BF16),
            "k_k": rwkv_k_k[l].reshape(1, w),
            "k_a": rwkv_k_a[l].reshape(1, w),
            "r_k": rwkv_r_k[l].reshape(1, w),
            "ln_w": rwkv_ln_w[l].reshape(1, w),
            "ln_b": rwkv_ln_b[l].reshape(1, w),
        }
        r_, lw_, k_, v_, kk_, b_, g_ = rwkv_prep(z, p, ts=_pick_tile(s, (256, 128)))
        y_rwkv = rwkv_chunk(r_, lw_, k_, v_, kk_, b_, g_, p)

        x = out_proj(x, y_sb, y_mla, y_rwkv, w_out_b, l, tm=_pick_tile(s, (512, 256, 128)))

        x = ffn(x, ffn2_norm[l], ffn_w[1], l)

    return final_rmsnorm(x, final_norm, tm=_pick_tile(s, (256, 128)))[None]
```

```python
import functools

import jax
import jax.numpy as jnp
import numpy as np
from jax import lax
from jax.experimental import pallas as pl
from jax.experimental.pallas import tpu as pltpu

F32 = jnp.float32
BF16 = jnp.bfloat16

NORM_EPS = 1e-6
SB_HEADS = 4
SB_HEAD_DIM = 128
SB_WIDTH = SB_HEADS * SB_HEAD_DIM
MLA_HEADS = 4
MLA_NOPE_DIM = 128
MLA_ROPE_DIM = 64
MLA_V_DIM = 128
MLA_Q_LORA = 512
MLA_KV_LORA = 256
MLA_WIDTH = MLA_HEADS * MLA_V_DIM
MLA_QK_PAD = 256
ROPE_THETA = 10000.0
RWKV_HEADS = 16
RWKV_HEAD_DIM = 64
RWKV_WIDTH = RWKV_HEADS * RWKV_HEAD_DIM
RWKV_DECAY_LORA = 64
RWKV_A_LORA = 64
RWKV_GATE_LORA = 128
RWKV_GN_EPS = 64e-5
RWKV_CHUNK = 64
RWKV_GROUP = 4
RWKV_CHUNKS_PER_STEP = 4

COL_R, COL_K, COL_V = 0, 1024, 2048
COL_CQ, COL_CKV = 3072, 3584
COL_LORA, COL_XG, COL_ROPE = 3840, 3968, 4096
Z_F32_COLS = 4608
COL_SBQ, COL_SBK, COL_SBV = 0, 512, 1024
Z_B16_COLS = 1536
Z_TILE = 768

VMEM_LIMIT = 48 * 1024 * 1024


def _cparams(sem):
    return pltpu.CompilerParams(dimension_semantics=sem, vmem_limit_bytes=VMEM_LIMIT)


def _dot(a, b):
    return jnp.dot(a, b, preferred_element_type=F32)


def _dot_nt(a, b):
    return lax.dot_general(a, b, (((1,), (1,)), ((), ())), preferred_element_type=F32)


def _dot_tn(a, b):
    return lax.dot_general(a, b, (((0,), (0,)), ((), ())), preferred_element_type=F32)


def _split3(x):
    hi = x.astype(BF16)
    r1 = x - hi.astype(F32)
    mid = r1.astype(BF16)
    lo = (r1 - mid.astype(F32)).astype(BF16)
    return hi, mid, lo


def _split2(x):
    hi = x.astype(BF16)
    lo = (x - hi.astype(F32)).astype(BF16)
    return hi, lo


def _mm3(dot, a, b):
    ah, al = _split2(a)
    bh, bl = _split2(b)
    return dot(ah, bh) + (dot(ah, bl) + dot(al, bh))


def _mm(dot, a, b, passes):
    if passes == 1:
        return dot(a.astype(BF16), b.astype(BF16))
    return _mm3(dot, a, b)


def _exact_lhs_dot(m_bf16, x):
    hi, mid, lo = _split3(x)
    return _dot(m_bf16, hi) + (_dot(m_bf16, mid) + _dot(m_bf16, lo))


def _rms_rows(x, g):
    ms = jnp.mean(x * x, axis=-1, keepdims=True)
    return x * lax.rsqrt(ms + NORM_EPS) * g


def _norm_proj_kernel(x_ref, g_ref, w_ref, o_ref, ob_ref, h_ref):
    @pl.when(pl.program_id(1) == 0)
    def _():
        h_ref[...] = _rms_rows(x_ref[...], g_ref[...]).astype(BF16)

    z = _dot(h_ref[...], w_ref[...])
    nf = Z_F32_COLS // Z_TILE

    @pl.when(pl.program_id(1) < nf)
    def _():
        o_ref[...] = z

    @pl.when(pl.program_id(1) >= nf)
    def _():
        ob_ref[...] = z.astype(BF16)


def norm_proj(x, g, w, l, *, tm):
    s, d = x.shape
    tn = Z_TILE
    nf, nb = Z_F32_COLS // tn, Z_B16_COLS // tn
    return pl.pallas_call(
        _norm_proj_kernel,
        grid=(s // tm, nf + nb),
        in_specs=[
            pl.BlockSpec((tm, d), lambda i, j: (i, 0)),
            pl.BlockSpec((1, d), lambda i, j: (0, 0)),
            pl.BlockSpec((None, d, tn), lambda i, j: (l, 0, j)),
        ],
        out_specs=[
            pl.BlockSpec((tm, tn), lambda i, j: (i, jnp.minimum(j, nf - 1))),
            pl.BlockSpec((tm, tn), lambda i, j: (i, jnp.maximum(j - nf, 0))),
        ],
        out_shape=[jax.ShapeDtypeStruct((s, Z_F32_COLS), F32),
                   jax.ShapeDtypeStruct((s, Z_B16_COLS), BF16)],
        scratch_shapes=[pltpu.VMEM((tm, d), BF16)],
        compiler_params=_cparams(("parallel", "arbitrary")),
        name="norm_proj",
    )(x, g.reshape(1, d), w)


def _ffn_up_kernel(x_ref, g_ref, wg_ref, wu_ref, o_ref, h_ref):
    @pl.when(pl.program_id(1) == 0)
    def _():
        h_ref[...] = _rms_rows(x_ref[...], g_ref[...]).astype(BF16)

    h = h_ref[...]
    a = _dot(h, wg_ref[...].astype(BF16))
    u = _dot(h, wu_ref[...].astype(BF16))
    o_ref[...] = (a * jax.nn.sigmoid(a) * u).astype(o_ref.dtype)


def ffn_up(x, g, wg, wu, l, *, tm, tn):
    s, d = x.shape
    f = wg.shape[2]
    return pl.pallas_call(
        _ffn_up_kernel,
        grid=(s // tm, f // tn),
        in_specs=[
            pl.BlockSpec((tm, d), lambda i, j: (i, 0)),
            pl.BlockSpec((1, d), lambda i, j: (0, 0)),
            pl.BlockSpec((None, d, tn), lambda i, j: (l, 0, j)),
            pl.BlockSpec((None, d, tn), lambda i, j: (l, 0, j)),
        ],
        out_specs=pl.BlockSpec((tm, tn), lambda i, j: (i, j)),
        out_shape=jax.ShapeDtypeStruct((s, f), BF16),
        scratch_shapes=[pltpu.VMEM((tm, d), BF16)],
        compiler_params=_cparams(("parallel", "arbitrary")),
        name="ffn_up",
    )(x, g.reshape(1, d), wg, wu)


def _ffn_down_kernel(x_ref, a_ref, w_ref, o_ref):
    o_ref[...] = x_ref[...] + 0.5 * _dot(a_ref[...], w_ref[...])


def ffn_down(x, act, wd, l, *, tm, tn):
    s, d = x.shape
    f = act.shape[1]
    return pl.pallas_call(
        _ffn_down_kernel,
        grid=(s // tm, d // tn),
        in_specs=[
            pl.BlockSpec((tm, tn), lambda i, j: (i, j)),
            pl.BlockSpec((tm, f), lambda i, j: (i, 0)),
            pl.BlockSpec((None, f, tn), lambda i, j: (l, 0, j)),
        ],
        out_specs=pl.BlockSpec((tm, tn), lambda i, j: (i, j)),
        out_shape=jax.ShapeDtypeStruct((s, d), F32),
        compiler_params=_cparams(("parallel", "arbitrary")),
        name="ffn_down",
    )(x, act, wd)


def _out_proj_kernel(x_ref, a_ref, b_ref, c_ref, wa_ref, wb_ref, wc_ref, o_ref):
    acc = _dot(a_ref[...], wa_ref[...])
    acc += _dot(b_ref[...], wb_ref[...])
    acc += _dot(c_ref[...], wc_ref[...])
    o_ref[...] = x_ref[...] + acc


def out_proj(x, y_sb, y_mla, y_rwkv, w_out, l, *, tm):
    s, d = x.shape
    assert SB_WIDTH == MLA_WIDTH and RWKV_WIDTH == SB_WIDTH + MLA_WIDTH
    wspec = lambda rows, blk: pl.BlockSpec((None, rows, d), lambda i: (l, blk, 0),
                                           pipeline_mode=pl.Buffered(1))
    return pl.pallas_call(
        _out_proj_kernel,
        grid=(s // tm,),
        in_specs=[
            pl.BlockSpec((tm, d), lambda i: (i, 0)),
            pl.BlockSpec((tm, SB_WIDTH), lambda i: (i, 0)),
            pl.BlockSpec((tm, MLA_WIDTH), lambda i: (i, 0)),
            pl.BlockSpec((tm, RWKV_WIDTH), lambda i: (i, 0)),
            wspec(SB_WIDTH, 0),
            wspec(MLA_WIDTH, 1),
            wspec(RWKV_WIDTH, 1),
        ],
        out_specs=pl.BlockSpec((tm, d), lambda i: (i, 0)),
        out_shape=jax.ShapeDtypeStruct((s, d), F32),
        compiler_params=_cparams(("parallel",)),
        name="out_proj",
    )(x, y_sb, y_mla, y_rwkv, w_out, w_out, w_out)


def _final_norm_kernel(x_ref, g_ref, o_ref):
    o_ref[...] = _rms_rows(x_ref[...], g_ref[...])


def final_rmsnorm(x, g, *, tm):
    s, d = x.shape
    return pl.pallas_call(
        _final_norm_kernel,
        grid=(s // tm,),
        in_specs=[pl.BlockSpec((tm, d), lambda i: (i, 0)), pl.BlockSpec((1, d), lambda i: (0, 0))],
        out_specs=pl.BlockSpec((tm, d), lambda i: (i, 0)),
        out_shape=jax.ShapeDtypeStruct((s, d), F32),
        compiler_params=_cparams(("parallel",)),
        name="final_norm",
    )(x, g.reshape(1, d))


SB_BQ = 256
SB_BK = 128
SB_DEAD_LOG = -104.0


def _sb_attn_kernel(q_ref, k_ref, v_ref, g_ref, o_ref):
    i = pl.program_id(0)
    bq, bk, dh = SB_BQ, SB_BK, SB_HEAD_DIM
    heads = range(SB_HEADS)
    hs = [slice(h * dh, (h + 1) * dh) for h in heads]
    q = [q_ref[:, hs[h]] for h in heads]
    row = lax.broadcasted_iota(jnp.int32, (bq, bk), 0)
    col = lax.broadcasted_iota(jnp.int32, (bq, bk), 1)
    kr = lax.broadcasted_iota(jnp.int32, (bk, bk), 0)
    kc = lax.broadcasted_iota(jnp.int32, (bk, bk), 1)
    upper = jnp.where(kr > kc, 1.0, 0.0).astype(BF16)

    def block(kb, carry, acc, masked):
        start = pl.multiple_of(kb * bk, bk)
        z = [_dot_nt(q[h], k_ref[pl.ds(start, bk), hs[h]]) for h in heads]
        log_beta = [jnp.minimum(z[h], 0.0) - jnp.log(1.0 + jnp.exp(-jnp.abs(z[h]))) for h in heads]
        log_keep = [log_beta[h] - z[h] for h in heads]
        if masked:
            before = (start + col) < (i * bq + row)
            log_keep = [jnp.where(before, log_keep[h], 0.0) for h in heads]
        split = [_split2(log_keep[h]) for h in heads]
        within = [_dot(split[h][0], upper) + _dot(split[h][1], upper) for h in heads]
        a = [jnp.exp(log_beta[h] + (carry[h] + within[h])) for h in heads]
        if masked:
            a = [jnp.where(before, a[h], 0.0) for h in heads]
        acc = [acc[h] + _dot(a[h].astype(BF16), v_ref[pl.ds(start, bk), hs[h]]) for h in heads]
        carry = [carry[h] + jnp.sum(log_keep[h], axis=-1, keepdims=True) for h in heads]
        return carry, acc

    def alive_of(carry):
        m = jnp.max(carry[0])
        for h in heads[1:]:
            m = jnp.maximum(m, jnp.max(carry[h]))
        return m

    carry = [jnp.zeros((bq, 1), F32) for _ in heads]
    acc = [jnp.zeros((bq, dh), F32) for _ in heads]
    nd = bq // bk
    first = i * nd
    for d in range(nd - 1, -1, -1):
        carry, acc = block(first + d, carry, acc, True)

    def cond(st):
        return jnp.logical_and(st[0] >= 0, st[1] > SB_DEAD_LOG)

    def body(st):
        kb, _, carry, acc = st
        carry, acc = block(kb, list(carry), list(acc), False)
        return kb - 1, alive_of(carry), tuple(carry), tuple(acc)

    st = lax.while_loop(cond, body, (first - 1, alive_of(carry), tuple(carry), tuple(acc)))
    acc = st[3]
    for h in heads:
        o_ref[:, hs[h]] = _rms_rows(acc[h], g_ref[h:h + 1, :]).astype(o_ref.dtype)


def sb_attention(zb, gains):
    s = zb.shape[0]
    whole = lambda c: pl.BlockSpec((s, SB_WIDTH), lambda i: (0, c // SB_WIDTH),
                                   pipeline_mode=pl.Buffered(1))
    return pl.pallas_call(
        _sb_attn_kernel,
        grid=(s // SB_BQ,),
        in_specs=[
            pl.BlockSpec((SB_BQ, SB_WIDTH), lambda i: (i, COL_SBQ // SB_WIDTH)),
            whole(COL_SBK),
            whole(COL_SBV),
            pl.BlockSpec((SB_HEADS, SB_HEAD_DIM), lambda i: (0, 0)),
        ],
        out_specs=pl.BlockSpec((SB_BQ, SB_WIDTH), lambda i: (i, 0)),
        out_shape=jax.ShapeDtypeStruct((s, SB_WIDTH), BF16),
        compiler_params=_cparams(("arbitrary",)),
        name="sb_attention",
    )(zb, zb, zb, gains)


def _mla_prep_kernel(cq_ref, ckv_ref, rope_ref, tab_ref, qn_ref, kvn_ref, wq_ref, wkv_ref,
                     q_ref, k_ref, v_ref):
    scale = (MLA_NOPE_DIM + MLA_ROPE_DIM) ** -0.5
    hq = _rms_rows(cq_ref[...], qn_ref[...]).astype(BF16)
    hkv = _rms_rows(ckv_ref[...], kvn_ref[...]).astype(BF16)
    cos2 = tab_ref[:, :128]
    sin2 = tab_ref[:, 128:]
    kr = rope_ref[...]
    k_pe = kr * cos2 + pltpu.roll(kr, 64, 1) * sin2
    k_pe = k_pe.astype(BF16)
    qall = _dot(hq, wq_ref[...])
    kvall = _dot(hkv, wkv_ref[...])
    for h in range(MLA_HEADS):
        qh = qall[:, h * 384:(h + 1) * 384]
        q_ref[h, :, :128] = (qh[:, :128] * scale).astype(BF16)
        q_ref[h, :, 128:] = ((qh[:, 128:256] * cos2 + qh[:, 256:384] * sin2) * scale).astype(BF16)
        k_ref[h, :, :128] = kvall[:, h * 256:h * 256 + 128].astype(BF16)
        k_ref[h, :, 128:] = k_pe
        v_ref[h, :, :128] = kvall[:, h * 256 + 128:(h + 1) * 256].astype(BF16)
        v_ref[h, :, 128:] = jnp.ones((kr.shape[0], 128), BF16)


def mla_prep(z, tab, q_norm, kv_norm, wq, wkv, *, ts):
    s = z.shape[0]
    return pl.pallas_call(
        _mla_prep_kernel,
        grid=(s // ts,),
        in_specs=[
            pl.BlockSpec((ts, MLA_Q_LORA), lambda i: (i, COL_CQ // MLA_Q_LORA)),
            pl.BlockSpec((ts, MLA_KV_LORA), lambda i: (i, COL_CKV // MLA_KV_LORA)),
            pl.BlockSpec((ts, 128), lambda i: (i, COL_ROPE // 128)),
            pl.BlockSpec((ts, 256), lambda i: (i, 0)),
            pl.BlockSpec((1, MLA_Q_LORA), lambda i: (0, 0)),
            pl.BlockSpec((1, MLA_KV_LORA), lambda i: (0, 0)),
            pl.BlockSpec(wq.shape, lambda i: (0, 0)),
            pl.BlockSpec(wkv.shape, lambda i: (0, 0)),
        ],
        out_specs=[
            pl.BlockSpec((MLA_HEADS, ts, MLA_QK_PAD), lambda i: (0, i, 0)),
            pl.BlockSpec((MLA_HEADS, ts, MLA_QK_PAD), lambda i: (0, i, 0)),
            pl.BlockSpec((MLA_HEADS, ts, 2 * MLA_V_DIM), lambda i: (0, i, 0)),
        ],
        out_shape=[
            jax.ShapeDtypeStruct((MLA_HEADS, s, MLA_QK_PAD), BF16),
            jax.ShapeDtypeStruct((MLA_HEADS, s, MLA_QK_PAD), BF16),
            jax.ShapeDtypeStruct((MLA_HEADS, s, 2 * MLA_V_DIM), BF16),
        ],
        compiler_params=_cparams(("parallel",)),
        name="mla_prep",
    )(z, z, z, tab, q_norm.reshape(1, -1), kv_norm.reshape(1, -1), wq, wkv)


MLA_BQ = 256
MLA_BK = 512
MLA_HEADS_PER_STEP = 4


def _mla_attn_kernel(q_ref, k_ref, v_ref, g_ref, o_ref):
    i = pl.program_id(1)
    bq, bk = MLA_BQ, MLA_BK
    heads = range(MLA_HEADS_PER_STEP)
    q = [q_ref[h] for h in heads]
    nd = bk // bq

    def block(kb, m, acc, masked):
        start = pl.multiple_of(kb * bk, bk)
        sc = [_dot_nt(q[h], k_ref[h, pl.ds(start, bk), :]) for h in heads]
        if masked:
            row = lax.broadcasted_iota(jnp.int32, (bq, bk), 0) + i * bq
            col = lax.broadcasted_iota(jnp.int32, (bq, bk), 1) + start
            sc = [jnp.where(col <= row, sc[h], -1e30) for h in heads]
        m_new = [jnp.maximum(m[h], jnp.max(sc[h], axis=-1, keepdims=True)) for h in heads]
        for h in heads:
            alpha = jnp.exp(m[h] - m_new[h])
            p = jnp.exp(sc[h] - m_new[h])
            acc[h] = acc[h] * alpha + _dot(p.astype(BF16), v_ref[h, pl.ds(start, bk), :])
        return m_new, acc

    def body(kb, st):
        m, acc = block(kb, list(st[0]), list(st[1]), False)
        return tuple(m), tuple(acc)

    m0 = tuple(jnp.full((bq, 1), -1e30, F32) for _ in heads)
    a0 = tuple(jnp.zeros((bq, 2 * MLA_V_DIM), F32) for _ in heads)
    last = i // nd
    m, acc = lax.fori_loop(0, last, body, (m0, a0))
    m, acc = block(last, list(m), list(acc), True)
    for h in heads:
        y = acc[h][:, :MLA_V_DIM] / acc[h][:, MLA_V_DIM:]
        o_ref[:, h * MLA_V_DIM:(h + 1) * MLA_V_DIM] = _rms_rows(y, g_ref[h:h + 1, :]).astype(o_ref.dtype)


def mla_attention(q, k, v, gains):
    s = q.shape[1]
    hps = MLA_HEADS_PER_STEP
    whole = lambda n: pl.BlockSpec((hps, s, n), lambda g, i: (g, 0, 0), pipeline_mode=pl.Buffered(1))
    return pl.pallas_call(
        _mla_attn_kernel,
        grid=(MLA_HEADS // hps, s // MLA_BQ),
        in_specs=[
            pl.BlockSpec((hps, MLA_BQ, MLA_QK_PAD), lambda g, i: (g, i, 0)),
            whole(MLA_QK_PAD),
            whole(2 * MLA_V_DIM),
            pl.BlockSpec((hps, MLA_V_DIM), lambda g, i: (g, 0)),
        ],
        out_specs=pl.BlockSpec((MLA_BQ, hps * MLA_V_DIM), lambda g, i: (i, g)),
        out_shape=jax.ShapeDtypeStruct((s, MLA_WIDTH), BF16),
        compiler_params=_cparams(("parallel", "arbitrary")),
        name="mla_attention",
    )(q, k, v, gains)


def _head_sum_matrix():
    r = lax.broadcasted_iota(jnp.int32, (128, 128), 0) // RWKV_HEAD_DIM
    c = lax.broadcasted_iota(jnp.int32, (128, 128), 1) // RWKV_HEAD_DIM
    return jnp.where(r == c, 1.0, 0.0).astype(BF16)


def _head_sums(x, ones_bd):
    parts = []
    for j in range(x.shape[1] // 128):
        parts.append(_exact_lhs_dot_rhs(x[:, j * 128:(j + 1) * 128], ones_bd))
    return jnp.concatenate(parts, axis=-1)


def _exact_lhs_dot_rhs(x, m_bf16):
    hi, mid, lo = _split3(x)
    return _dot(hi, m_bf16) + (_dot(mid, m_bf16) + _dot(lo, m_bf16))


def _rwkv_prep_kernel(r_ref, k_ref, v_ref, lora_ref, xg_ref, mu_ref, mul_ref, mug_ref,
                      w0_ref, w2_ref, a0_ref, a2_ref, g2_ref, kk_ref, ka_ref,
                      ro_ref, lw_ref, ko_ref, vo_ref, kko_ref, bo_ref, go_ref,
                      prev_ref, prevl_ref):
    ts = r_ref.shape[0]

    @pl.when(pl.program_id(0) == 0)
    def _():
        prev_ref[...] = jnp.zeros_like(prev_ref)
        prevl_ref[...] = jnp.zeros_like(prevl_ref)

    first_row = lax.broadcasted_iota(jnp.int32, (ts, 1), 0) == 0

    def shift_mix(x, prev_row, mu):
        x_prev = jnp.where(first_row, prev_row, pltpu.roll(x, 1, 0))
        return x + (x_prev - x) * mu

    r_in, k_in, v_in = r_ref[...], k_ref[...], v_ref[...]
    lora_in, xg_in = lora_ref[...], xg_ref[...]
    r = shift_mix(r_in, prev_ref[0:1, :], mu_ref[0:1, :])
    k = shift_mix(k_in, prev_ref[1:2, :], mu_ref[1:2, :])
    v = shift_mix(v_in, prev_ref[2:3, :], mu_ref[2:3, :])
    lora = shift_mix(lora_in, prevl_ref[0:1, :], mul_ref[...])
    xg = shift_mix(xg_in, prevl_ref[1:2, :], mug_ref[...])
    prev_ref[0:1, :] = r_in[ts - 1:ts, :]
    prev_ref[1:2, :] = k_in[ts - 1:ts, :]
    prev_ref[2:3, :] = v_in[ts - 1:ts, :]
    prevl_ref[0:1, :] = lora_in[ts - 1:ts, :]
    prevl_ref[1:2, :] = xg_in[ts - 1:ts, :]

    dw = w0_ref[...] + _dot(jnp.tanh(lora).astype(BF16), w2_ref[...])
    log_w = -jax.nn.softplus(-dw) - 0.5
    lw_ref[...] = -jnp.exp(log_w)
    a = jax.nn.sigmoid(a0_ref[...] + _dot(lora.astype(BF16), a2_ref[...]))
    go_ref[...] = _dot(jax.nn.sigmoid(xg).astype(BF16), g2_ref[...])

    kk = k * kk_ref[...]
    ss = _head_sums(kk * kk, _head_sum_matrix())
    kk = kk / jnp.maximum(jnp.sqrt(ss), 1e-12)
    ro_ref[...] = r
    vo_ref[...] = v
    ko_ref[...] = k * (1.0 + (a - 1.0) * ka_ref[...])
    kko_ref[...] = kk
    bo_ref[...] = kk * a


def rwkv_prep(z, p, *, ts):
    s = z.shape[0]
    w = RWKV_WIDTH
    row = lambda n: pl.BlockSpec((1, n), lambda i: (0, 0))
    big = pl.BlockSpec((ts, w), lambda i: (i, 0))
    out = jax.ShapeDtypeStruct((s, w), F32)
    return pl.pallas_call(
        _rwkv_prep_kernel,
        grid=(s // ts,),
        in_specs=[
            pl.BlockSpec((ts, w), lambda i: (i, COL_R // w)),
            pl.BlockSpec((ts, w), lambda i: (i, COL_K // w)),
            pl.BlockSpec((ts, w), lambda i: (i, COL_V // w)),
            pl.BlockSpec((ts, 128), lambda i: (i, COL_LORA // 128)),
            pl.BlockSpec((ts, 128), lambda i: (i, COL_XG // 128)),
            pl.BlockSpec((3, w), lambda i: (0, 0)),
            row(128), row(128),
            row(w), pl.BlockSpec((128, w), lambda i: (0, 0)),
            row(w), pl.BlockSpec((128, w), lambda i: (0, 0)),
            pl.BlockSpec((128, w), lambda i: (0, 0)),
            row(w), row(w),
        ],
        out_specs=[big] * 7,
        out_shape=[out] * 7,
        scratch_shapes=[pltpu.VMEM((8, w), F32), pltpu.VMEM((8, 128), F32)],
        compiler_params=_cparams(("arbitrary",)),
        name="rwkv_prep",
    )(z, z, z, z, z, p["mu_rkv"], p["mu_lora"], p["mu_g"], p["w0"], p["w2"], p["a0"], p["a2"],
      p["g2"], p["k_k"], p["k_a"])


def _rwkv_chunk_kernel(r_ref, lw_ref, k_ref, v_ref, kk_ref, b_ref, g_ref, rk_ref, lnw_ref, lnb_ref,
                       o_ref, st_ref, y_ref):
    c = RWKV_CHUNK
    n = RWKV_HEAD_DIM
    nh = r_ref.shape[1] // n
    chunks = range(r_ref.shape[0] // c)

    @pl.when(pl.program_id(1) == 0)
    def _():
        st_ref[...] = jnp.zeros_like(st_ref)

    ri = lax.broadcasted_iota(jnp.int32, (c, c), 0)
    ci = lax.broadcasted_iota(jnp.int32, (c, c), 1)
    tril_ones = jnp.where(ri >= ci, 1.0, 0.0).astype(BF16)

    gh = RWKV_GROUP
    gw = gh * n
    groups = range(nh // gh)
    gs = [slice(g * gw, (g + 1) * gw) for g in groups]
    same_head = (lax.broadcasted_iota(jnp.int32, (gw, gw), 0) // n
                 == lax.broadcasted_iota(jnp.int32, (gw, gw), 1) // n)
    zero_b = jnp.zeros((gw, gw), BF16)

    def bd(y):
        yb = y.astype(BF16)
        return jnp.where(same_head, jnp.concatenate([yb] * gh, axis=0), zero_b)

    def mmb(x, y_bd):
        return _dot(x.astype(BF16), y_bd)

    ri2 = lax.broadcasted_iota(jnp.int32, (2 * c, gw), 0)
    ci2 = lax.broadcasted_iota(jnp.int32, (2 * c, gw), 1) % c
    causal2 = ci2 <= jnp.where(ri2 < c, ri2 - 1, ri2 - c)
    rc = lax.broadcasted_iota(jnp.int32, (c, gw), 0)
    cc = lax.broadcasted_iota(jnp.int32, (c, gw), 1) % c
    eye_g = jnp.where(rc == cc, 1.0, 0.0).astype(F32)
    d16_g = (rc // 16) == (cc // 16)
    d32_g = (rc // 32) == (cc // 32)
    lo_g = jnp.logical_and(d32_g, jnp.logical_not(d16_g))
    colhead = lax.broadcasted_iota(jnp.int32, (n, gw), 1) // n

    p_end, v_c, lhs, b_t, k_t, dec = [], [], [], [], [], []
    for ch in chunks:
        rows = slice(ch * c, (ch + 1) * c)
        lw = lw_ref[rows, :]
        cum = _exact_lhs_dot(tril_ones, lw)
        p_incl = jnp.exp(cum)
        p_inv = jnp.exp(-cum)
        pe = p_incl[c - 1:c, :]
        bt = b_ref[rows, :] * p_inv
        kt = k_ref[rows, :] * p_inv
        p_end.append(pe)
        v_c.append(v_ref[rows, :])
        b_t.append(bt)
        k_t.append(kt)
        lhs.append(jnp.concatenate([kk_ref[rows, :] * jnp.exp(cum - lw), r_ref[rows, :] * p_incl],
                                   axis=0).astype(BF16))
        dec.append(jnp.concatenate([bt * pe, kt * pe], axis=0).astype(BF16))

    cg = [(ch, g) for ch in chunks for g in groups]
    ab = {q: jnp.where(causal2, _dot_nt(lhs[q[0]][:, gs[q[1]]], bd(b_t[q[0]][:, gs[q[1]]])), 0.0)
          for q in cg}
    ak = {q: jnp.where(causal2, _dot_nt(lhs[q[0]][:, gs[q[1]]], bd(k_t[q[0]][:, gs[q[1]]])), 0.0)
          for q in cg}
    akv = {q: mmb(ak[q], bd(v_c[q[0]][:, gs[q[1]]])) for q in cg}
    a_ub = {q: ab[q][:c] for q in cg}
    ld = {q: jnp.where(d16_g, a_ub[q], 0.0) for q in cg}
    x = {q: eye_g - ld[q] for q in cg}
    pw = {q: mmb(ld[q], bd(ld[q])) for q in cg}
    x = {q: x[q] + mmb(x[q], bd(pw[q])) for q in cg}
    pw = {q: mmb(pw[q], bd(pw[q])) for q in cg}
    x = {q: x[q] + mmb(x[q], bd(pw[q])) for q in cg}
    pw = {q: mmb(pw[q], bd(pw[q])) for q in cg}
    x = {q: x[q] + mmb(x[q], bd(pw[q])) for q in cg}
    t = {q: mmb(x[q], bd(jnp.where(lo_g, a_ub[q], 0.0))) for q in cg}
    x = {q: x[q] - mmb(t[q], bd(x[q])) for q in cg}
    t = {q: mmb(x[q], bd(jnp.where(d32_g, 0.0, a_ub[q]))) for q in cg}
    x = {q: x[q] - mmb(t[q], bd(x[q])) for q in cg}
    u0 = {q: -mmb(x[q], bd(akv[q][:c])) for q in cg}
    wk = {q: -mmb(x[q], bd(lhs[q[0]][:c, gs[q[1]]])) for q in cg}
    lhs2 = {q: jnp.concatenate([wk[q].astype(BF16), lhs[q[0]][c:, gs[q[1]]]], axis=0) for q in cg}

    st = [st_ref[:, gs[g]] for g in groups]
    for ch in chunks:
        rows = slice(ch * c, (ch + 1) * c)
        through = [_dot_nt(lhs2[(ch, g)], bd(st[g])) for g in groups]
        u = [through[g][:c] + u0[(ch, g)] for g in groups]
        for g in groups:
            y_ref[rows, gs[g]] = through[g][c:] + akv[(ch, g)][c:] + mmb(ab[(ch, g)][c:], bd(u[g]))
        for g in groups:
            uv = jnp.concatenate([u[g], v_c[ch][:, gs[g]]], axis=0).astype(BF16)
            cross = _dot_tn(uv, dec[ch][:, gs[g]])
            upd = st[g] * p_end[ch][:, gs[g]]
            for h in range(gh):
                upd = upd + jnp.where(colhead == h, cross[h * n:(h + 1) * n], 0.0)
            st[g] = upd
    for g in groups:
        st_ref[:, gs[g]] = st[g]

    ones_bd = _head_sum_matrix()
    y = y_ref[...]
    v = v_ref[...]
    mean = _head_sums(y, ones_bd) * (1.0 / n)
    yc = y - mean
    var = _head_sums(yc * yc, ones_bd) * (1.0 / n)
    yn = yc * lax.rsqrt(var + RWKV_GN_EPS) * lnw_ref[...] + lnb_ref[...]
    bonus = _head_sums(r_ref[...] * k_ref[...] * rk_ref[...], ones_bd)
    o_ref[...] = ((yn + bonus * v) * g_ref[...]).astype(o_ref.dtype)


def rwkv_chunk(r, lw, k, v, kk, b, g, p):
    s, w = r.shape
    rows = RWKV_CHUNK * RWKV_CHUNKS_PER_STEP
    big = pl.BlockSpec((rows, w), lambda hg, t: (t, hg))
    row = pl.BlockSpec((1, w), lambda hg, t: (0, hg))
    return pl.pallas_call(
        _rwkv_chunk_kernel,
        grid=(1, s // rows),
        in_specs=[big] * 7 + [row] * 3,
        out_specs=big,
        out_shape=jax.ShapeDtypeStruct((s, w), BF16),
        scratch_shapes=[
            pltpu.VMEM((RWKV_HEAD_DIM, w), F32),
            pltpu.VMEM((rows, w), F32),
        ],
        compiler_params=_cparams(("parallel", "arbitrary")),
        name="rwkv_chunk",
    )(r, lw, k, v, kk, b, g, p["r_k"], p["ln_w"], p["ln_b"])


def _swap_halves(w):
    half = w.shape[-1] // 2
    return jnp.concatenate([w[..., half:], w[..., :half]], axis=-1)


def _prep_w_in(w_in):
    sizes = [SB_WIDTH] * 3 + [MLA_Q_LORA, MLA_KV_LORA, MLA_ROPE_DIM] + [RWKV_WIDTH] * 3 + [
        RWKV_DECAY_LORA, RWKV_A_LORA, RWKV_GATE_LORA]
    idx = [int(i) for i in np.cumsum(sizes)[:-1]]
    sbq, sbk, sbv, cq, ckv, krope, r, k, v, xw, xa, xg = jnp.split(w_in, idx, axis=-1)
    sbq = sbq * (SB_HEAD_DIM ** -0.5)
    cols = [r, k, v, cq, ckv, xw, xa, xg, krope, _swap_halves(krope)]
    used = sum(c.shape[-1] for c in cols)
    cols.append(jnp.zeros(w_in.shape[:-1] + (Z_F32_COLS - used,), w_in.dtype))
    cols += [sbq, sbk, sbv]
    return jnp.concatenate(cols, axis=-1).astype(BF16)


def _prep_mla_w(w_uq, w_ukv):
    q = w_uq.reshape(MLA_Q_LORA, MLA_HEADS, MLA_NOPE_DIM + MLA_ROPE_DIM)
    nope, pe = q[..., :MLA_NOPE_DIM], q[..., MLA_NOPE_DIM:]
    zpad = jnp.zeros_like(pe)
    wq = jnp.concatenate([nope, pe, zpad, _swap_halves(pe), zpad], axis=-1)
    return wq.reshape(MLA_Q_LORA, MLA_HEADS * 384).astype(BF16), w_ukv.astype(BF16)


def _rope_table(positions):
    half = MLA_ROPE_DIM // 2
    inv_freq = ROPE_THETA ** (-jnp.arange(half, dtype=F32) / half)
    ang = positions.astype(F32)[:, None] * inv_freq
    cos, sin = jnp.cos(ang), jnp.sin(ang)
    z = jnp.zeros((positions.shape[0], MLA_ROPE_DIM), F32)
    return jnp.concatenate([cos, cos, z, -sin, sin, z], axis=-1)


def _pick_tile(n, candidates):
    for c in candidates:
        if n % c == 0:
            return c
    raise ValueError(f"no tile for {n}")


def kernel(x, positions, ffn1_norm, ffn1_gate, ffn1_up, ffn1_down, mix_norm, w_in, mla_q_norm,
           mla_w_uq, mla_kv_norm, mla_w_ukv, rwkv_mu, rwkv_w0, rwkv_w2, rwkv_a0, rwkv_a2, rwkv_g2,
           rwkv_k_k, rwkv_k_a, rwkv_r_k, rwkv_ln_w, rwkv_ln_b, sb_out_norm, mla_out_norm, w_out,
           ffn2_norm, ffn2_gate, ffn2_up, ffn2_down, final_norm):
    bsz, s, d = x.shape
    assert bsz == 1 and d == SB_WIDTH + MLA_WIDTH + RWKV_WIDTH
    depth = w_in.shape[0]
    d_ff = ffn1_gate.shape[-1]
    tm = _pick_tile(s, (1024, 512, 256, 128))
    tf = _pick_tile(d_ff, (512, 256, 128))
    x = x[0]
    tab = _rope_table(positions[0])
    w = RWKV_WIDTH
    zl = jnp.zeros((RWKV_DECAY_LORA, w), F32)
    ffn_w = [(wg, wu, wd.astype(BF16))
             for wg, wu, wd in ((ffn1_gate, ffn1_up, ffn1_down), (ffn2_gate, ffn2_up, ffn2_down))]
    w_in_b = _prep_w_in(w_in)
    w_out_b = w_out.astype(BF16)

    def ffn(x, g, ws, l):
        act = ffn_up(x, g, ws[0], ws[1], l, tm=tm, tn=tf)
        return ffn_down(x, act, ws[2], l, tm=tm, tn=512)

    for l in range(depth):
        x = ffn(x, ffn1_norm[l], ffn_w[0], l)

        z, zb = norm_proj(x, mix_norm[l], w_in_b, l, tm=tm)

        y_sb = sb_attention(zb, sb_out_norm[l])

        wq, wkv = _prep_mla_w(mla_w_uq[l], mla_w_ukv[l])
        q, k, v = mla_prep(z, tab, mla_q_norm[l], mla_kv_norm[l], wq, wkv, ts=_pick_tile(s, (256, 128)))
        y_mla = mla_attention(q, k, v, mla_out_norm[l])

        mu = rwkv_mu[l]
        p = {
            "mu_rkv": mu[:3 * w].reshape(3, w),
            "mu_lora": mu[3 * w:3 * w + 128].reshape(1, 128),
            "mu_g": mu[3 * w + 128:].reshape(1, 128),
            "w0": rwkv_w0[l].reshape(1, w),
            "w2": jnp.concatenate([rwkv_w2[l], zl], axis=0).astype(BF16),
            "a0": rwkv_a0[l].reshape(1, w),
            "a2": jnp.concatenate([zl, rwkv_a2[l]], axis=0).astype(BF16),
            "g2": rwkv_g2[l].astype(BF16),
            "k_k": rwkv_k_k[l].reshape(1, w),
            "k_a": rwkv_k_a[l].reshape(1, w),
            "r_k": rwkv_r_k[l].reshape(1, w),
            "ln_w": rwkv_ln_w[l].reshape(1, w),
            "ln_b": rwkv_ln_b[l].reshape(1, w),
        }
        r_, lw_, k_, v_, kk_, b_, g_ = rwkv_prep(z, p, ts=_pick_tile(s, (256, 128)))
        y_rwkv = rwkv_chunk(r_, lw_, k_, v_, kk_, b_, g_, p)

        x = out_proj(x, y_sb, y_mla, y_rwkv, w_out_b, l, tm=_pick_tile(s, (512, 256, 128)))

        x = ffn(x, ffn2_norm[l], ffn_w[1], l)

    return final_rmsnorm(x, final_norm, tm=_pick_tile(s, (256, 128)))[None]
```

```python
import functools

import jax
import jax.numpy as jnp
import numpy as np
from jax import lax
from jax.experimental import pallas as pl
from jax.experimental.pallas import tpu as pltpu

F32 = jnp.float32
BF16 = jnp.bfloat16

NORM_EPS = 1e-6
SB_HEADS = 4
SB_HEAD_DIM = 128
SB_WIDTH = SB_HEADS * SB_HEAD_DIM
MLA_HEADS = 4
MLA_NOPE_DIM = 128
MLA_ROPE_DIM = 64
MLA_V_DIM = 128
MLA_Q_LORA = 512
MLA_KV_LORA = 256
MLA_WIDTH = MLA_HEADS * MLA_V_DIM
MLA_QK_PAD = 256
ROPE_THETA = 10000.0
RWKV_HEADS = 16
RWKV_HEAD_DIM = 64
RWKV_WIDTH = RWKV_HEADS * RWKV_HEAD_DIM
RWKV_DECAY_LORA = 64
RWKV_A_LORA = 64
RWKV_GATE_LORA = 128
RWKV_GN_EPS = 64e-5
RWKV_CHUNK = 64
RWKV_GROUP = 4
RWKV_CHUNKS_PER_STEP = 4

COL_R, COL_K, COL_V = 0, 1024, 2048
COL_CQ, COL_CKV = 3072, 3584
COL_LORA, COL_XG, COL_ROPE = 3840, 3968, 4096
Z_F32_COLS = 4608
COL_SBQ, COL_SBK, COL_SBV = 0, 512, 1024
Z_B16_COLS = 1536
Z_TILE = 768

VMEM_LIMIT = 48 * 1024 * 1024


def _cparams(sem):
    return pltpu.CompilerParams(dimension_semantics=sem, vmem_limit_bytes=VMEM_LIMIT)


def _dot(a, b):
    return jnp.dot(a, b, preferred_element_type=F32)


def _dot_nt(a, b):
    return lax.dot_general(a, b, (((1,), (1,)), ((), ())), preferred_element_type=F32)


def _dot_tn(a, b):
    return lax.dot_general(a, b, (((0,), (0,)), ((), ())), preferred_element_type=F32)


def _split3(x):
    hi = x.astype(BF16)
    r1 = x - hi.astype(F32)
    mid = r1.astype(BF16)
    lo = (r1 - mid.astype(F32)).astype(BF16)
    return hi, mid, lo


def _split2(x):
    hi = x.astype(BF16)
    lo = (x - hi.astype(F32)).astype(BF16)
    return hi, lo


def _mm3(dot, a, b):
    ah, al = _split2(a)
    bh, bl = _split2(b)
    return dot(ah, bh) + (dot(ah, bl) + dot(al, bh))


def _mm(dot, a, b, passes):
    if passes == 1:
        return dot(a.astype(BF16), b.astype(BF16))
    return _mm3(dot, a, b)


def _exact_lhs_dot(m_bf16, x):
    hi, mid, lo = _split3(x)
    return _dot(m_bf16, hi) + (_dot(m_bf16, mid) + _dot(m_bf16, lo))


def _rms_rows(x, g):
    ms = jnp.mean(x * x, axis=-1, keepdims=True)
    return x * lax.rsqrt(ms + NORM_EPS) * g


def _norm_proj_kernel(x_ref, g_ref, w_ref, o_ref, ob_ref, h_ref):
    @pl.when(pl.program_id(1) == 0)
    def _():
        h_ref[...] = _rms_rows(x_ref[...], g_ref[...]).astype(BF16)

    z = _dot_nt(h_ref[...], w_ref[...])
    nf = Z_F32_COLS // Z_TILE

    @pl.when(pl.program_id(1) < nf)
    def _():
        o_ref[...] = z

    @pl.when(pl.program_id(1) >= nf)
    def _():
        ob_ref[...] = z.astype(BF16)


def norm_proj(x, g, w, l, *, tm):
    s, d = x.shape
    tn = Z_TILE
    nf, nb = Z_F32_COLS // tn, Z_B16_COLS // tn
    return pl.pallas_call(
        _norm_proj_kernel,
        grid=(s // tm, nf + nb),
        in_specs=[
            pl.BlockSpec((tm, d), lambda i, j: (i, 0)),
            pl.BlockSpec((1, d), lambda i, j: (0, 0)),
            pl.BlockSpec((None, tn, d), lambda i, j: (l, j, 0)),
        ],
        out_specs=[
            pl.BlockSpec((tm, tn), lambda i, j: (i, jnp.minimum(j, nf - 1))),
            pl.BlockSpec((tm, tn), lambda i, j: (i, jnp.maximum(j - nf, 0))),
        ],
        out_shape=[jax.ShapeDtypeStruct((s, Z_F32_COLS), F32),
                   jax.ShapeDtypeStruct((s, Z_B16_COLS), BF16)],
        scratch_shapes=[pltpu.VMEM((tm, d), BF16)],
        compiler_params=_cparams(("parallel", "arbitrary")),
        name="norm_proj",
    )(x, g.reshape(1, d), w)


def _ffn_up_kernel(x_ref, g_ref, wg_ref, wu_ref, o_ref, h_ref):
    @pl.when(pl.program_id(1) == 0)
    def _():
        h_ref[...] = _rms_rows(x_ref[...], g_ref[...]).astype(BF16)

    h = h_ref[...]
    a = _dot(h, wg_ref[...].astype(BF16))
    u = _dot(h, wu_ref[...].astype(BF16))
    o_ref[...] = (a * jax.nn.sigmoid(a) * u).astype(o_ref.dtype)


def ffn_up(x, g, wg, wu, l, *, tm, tn):
    s, d = x.shape
    f = wg.shape[2]
    return pl.pallas_call(
        _ffn_up_kernel,
        grid=(s // tm, f // tn),
        in_specs=[
            pl.BlockSpec((tm, d), lambda i, j: (i, 0)),
            pl.BlockSpec((1, d), lambda i, j: (0, 0)),
            pl.BlockSpec((None, d, tn), lambda i, j: (l, 0, j)),
            pl.BlockSpec((None, d, tn), lambda i, j: (l, 0, j)),
        ],
        out_specs=pl.BlockSpec((tm, tn), lambda i, j: (i, j)),
        out_shape=jax.ShapeDtypeStruct((s, f), BF16),
        scratch_shapes=[pltpu.VMEM((tm, d), BF16)],
        compiler_params=_cparams(("parallel", "arbitrary")),
        name="ffn_up",
    )(x, g.reshape(1, d), wg, wu)


def _ffn_down_kernel(x_ref, a_ref, w_ref, o_ref):
    o_ref[...] = x_ref[...] + 0.5 * _dot(a_ref[...], w_ref[...])


def ffn_down(x, act, wd, l, *, tm, tn):
    s, d = x.shape
    f = act.shape[1]
    return pl.pallas_call(
        _ffn_down_kernel,
        grid=(s // tm, d // tn),
        in_specs=[
            pl.BlockSpec((tm, tn), lambda i, j: (i, j)),
            pl.BlockSpec((tm, f), lambda i, j: (i, 0)),
            pl.BlockSpec((None, f, tn), lambda i, j: (l, 0, j)),
        ],
        out_specs=pl.BlockSpec((tm, tn), lambda i, j: (i, j)),
        out_shape=jax.ShapeDtypeStruct((s, d), F32),
        compiler_params=_cparams(("parallel", "arbitrary")),
        name="ffn_down",
    )(x, act, wd)


def _out_proj_kernel(x_ref, a_ref, b_ref, c_ref, wa_ref, wb_ref, wc_ref, o_ref):
    acc = _dot(a_ref[...], wa_ref[...])
    acc += _dot(b_ref[...], wb_ref[...])
    acc += _dot(c_ref[...], wc_ref[...])
    o_ref[...] = x_ref[...] + acc


def out_proj(x, y_sb, y_mla, y_rwkv, w_out, l, *, tm):
    s, d = x.shape
    assert SB_WIDTH == MLA_WIDTH and RWKV_WIDTH == SB_WIDTH + MLA_WIDTH
    wspec = lambda rows, blk: pl.BlockSpec((None, rows, d), lambda i: (l, blk, 0),
                                           pipeline_mode=pl.Buffered(1))
    return pl.pallas_call(
        _out_proj_kernel,
        grid=(s // tm,),
        in_specs=[
            pl.BlockSpec((tm, d), lambda i: (i, 0)),
            pl.BlockSpec((tm, SB_WIDTH), lambda i: (i, 0)),
            pl.BlockSpec((tm, MLA_WIDTH), lambda i: (i, 0)),
            pl.BlockSpec((tm, RWKV_WIDTH), lambda i: (i, 0)),
            wspec(SB_WIDTH, 0),
            wspec(MLA_WIDTH, 1),
            wspec(RWKV_WIDTH, 1),
        ],
        out_specs=pl.BlockSpec((tm, d), lambda i: (i, 0)),
        out_shape=jax.ShapeDtypeStruct((s, d), F32),
        compiler_params=_cparams(("parallel",)),
        name="out_proj",
    )(x, y_sb, y_mla, y_rwkv, w_out, w_out, w_out)


def _final_norm_kernel(x_ref, g_ref, o_ref):
    o_ref[...] = _rms_rows(x_ref[...], g_ref[...])


def final_rmsnorm(x, g, *, tm):
    s, d = x.shape
    return pl.pallas_call(
        _final_norm_kernel,
        grid=(s // tm,),
        in_specs=[pl.BlockSpec((tm, d), lambda i: (i, 0)), pl.BlockSpec((1, d), lambda i: (0, 0))],
        out_specs=pl.BlockSpec((tm, d), lambda i: (i, 0)),
        out_shape=jax.ShapeDtypeStruct((s, d), F32),
        compiler_params=_cparams(("parallel",)),
        name="final_norm",
    )(x, g.reshape(1, d))


SB_BQ = 256
SB_BK = 128
SB_DEAD_LOG = -104.0


def _sb_attn_kernel(q_ref, k_ref, v_ref, g_ref, o_ref):
    i = pl.program_id(0)
    bq, bk, dh = SB_BQ, SB_BK, SB_HEAD_DIM
    heads = range(SB_HEADS)
    hs = [slice(h * dh, (h + 1) * dh) for h in heads]
    q = [q_ref[:, hs[h]] for h in heads]
    row = lax.broadcasted_iota(jnp.int32, (bq, bk), 0)
    col = lax.broadcasted_iota(jnp.int32, (bq, bk), 1)
    kr = lax.broadcasted_iota(jnp.int32, (bk, bk), 0)
    kc = lax.broadcasted_iota(jnp.int32, (bk, bk), 1)
    upper = jnp.where(kr > kc, 1.0, 0.0).astype(BF16)

    def block(kb, carry, acc, masked):
        start = pl.multiple_of(kb * bk, bk)
        z = [_dot_nt(q[h], k_ref[pl.ds(start, bk), hs[h]]) for h in heads]
        log_beta = [jnp.minimum(z[h], 0.0) - jnp.log(1.0 + jnp.exp(-jnp.abs(z[h]))) for h in heads]
        log_keep = [log_beta[h] - z[h] for h in heads]
        if masked:
            before = (start + col) < (i * bq + row)
            log_keep = [jnp.where(before, log_keep[h], 0.0) for h in heads]
        split = [_split2(log_keep[h]) for h in heads]
        within = [_dot(split[h][0], upper) + _dot(split[h][1], upper) for h in heads]
        a = [jnp.exp(log_beta[h] + (carry[h] + within[h])) for h in heads]
        if masked:
            a = [jnp.where(before, a[h], 0.0) for h in heads]
        acc = [acc[h] + _dot(a[h].astype(BF16), v_ref[pl.ds(start, bk), hs[h]]) for h in heads]
        carry = [carry[h] + jnp.sum(log_keep[h], axis=-1, keepdims=True) for h in heads]
        return carry, acc

    def alive_of(carry):
        m = jnp.max(carry[0])
        for h in heads[1:]:
            m = jnp.maximum(m, jnp.max(carry[h]))
        return m

    carry = [jnp.zeros((bq, 1), F32) for _ in heads]
    acc = [jnp.zeros((bq, dh), F32) for _ in heads]
    nd = bq // bk
    first = i * nd
    for d in range(nd - 1, -1, -1):
        carry, acc = block(first + d, carry, acc, True)

    def cond(st):
        return jnp.logical_and(st[0] >= 0, st[1] > SB_DEAD_LOG)

    def body(st):
        kb, _, carry, acc = st
        carry, acc = block(kb, list(carry), list(acc), False)
        return kb - 1, alive_of(carry), tuple(carry), tuple(acc)

    st = lax.while_loop(cond, body, (first - 1, alive_of(carry), tuple(carry), tuple(acc)))
    acc = st[3]
    for h in heads:
        o_ref[:, hs[h]] = _rms_rows(acc[h], g_ref[h:h + 1, :]).astype(o_ref.dtype)


def sb_attention(zb, gains):
    s = zb.shape[0]
    whole = lambda c: pl.BlockSpec((s, SB_WIDTH), lambda i: (0, c // SB_WIDTH),
                                   pipeline_mode=pl.Buffered(1))
    return pl.pallas_call(
        _sb_attn_kernel,
        grid=(s // SB_BQ,),
        in_specs=[
            pl.BlockSpec((SB_BQ, SB_WIDTH), lambda i: (i, COL_SBQ // SB_WIDTH)),
            whole(COL_SBK),
            whole(COL_SBV),
            pl.BlockSpec((SB_HEADS, SB_HEAD_DIM), lambda i: (0, 0)),
        ],
        out_specs=pl.BlockSpec((SB_BQ, SB_WIDTH), lambda i: (i, 0)),
        out_shape=jax.ShapeDtypeStruct((s, SB_WIDTH), BF16),
        compiler_params=_cparams(("arbitrary",)),
        name="sb_attention",
    )(zb, zb, zb, gains)


def _mla_prep_kernel(cq_ref, ckv_ref, rope_ref, tab_ref, qn_ref, kvn_ref, wq_ref, wkv_ref,
                     q_ref, k_ref, v_ref):
    scale = (MLA_NOPE_DIM + MLA_ROPE_DIM) ** -0.5
    hq = _rms_rows(cq_ref[...], qn_ref[...]).astype(BF16)
    hkv = _rms_rows(ckv_ref[...], kvn_ref[...]).astype(BF16)
    cos2 = tab_ref[:, :128]
    sin2 = tab_ref[:, 128:]
    kr = rope_ref[...]
    k_pe = kr * cos2 + pltpu.roll(kr, 64, 1) * sin2
    k_pe = k_pe.astype(BF16)
    qall = _dot(hq, wq_ref[...])
    kvall = _dot(hkv, wkv_ref[...])
    for h in range(MLA_HEADS):
        qh = qall[:, h * 384:(h + 1) * 384]
        q_ref[h, :, :128] = (qh[:, :128] * scale).astype(BF16)
        q_ref[h, :, 128:] = ((qh[:, 128:256] * cos2 + qh[:, 256:384] * sin2) * scale).astype(BF16)
        k_ref[h, :, :128] = kvall[:, h * 256:h * 256 + 128].astype(BF16)
        k_ref[h, :, 128:] = k_pe
        v_ref[h, :, :128] = kvall[:, h * 256 + 128:(h + 1) * 256].astype(BF16)
        v_ref[h, :, 128:] = jnp.ones((kr.shape[0], 128), BF16)


def mla_prep(z, tab, q_norm, kv_norm, wq, wkv, *, ts):
    s = z.shape[0]
    return pl.pallas_call(
        _mla_prep_kernel,
        grid=(s // ts,),
        in_specs=[
            pl.BlockSpec((ts, MLA_Q_LORA), lambda i: (i, COL_CQ // MLA_Q_LORA)),
            pl.BlockSpec((ts, MLA_KV_LORA), lambda i: (i, COL_CKV // MLA_KV_LORA)),
            pl.BlockSpec((ts, 128), lambda i: (i, COL_ROPE // 128)),
            pl.BlockSpec((ts, 256), lambda i: (i, 0)),
            pl.BlockSpec((1, MLA_Q_LORA), lambda i: (0, 0)),
            pl.BlockSpec((1, MLA_KV_LORA), lambda i: (0, 0)),
            pl.BlockSpec(wq.shape, lambda i: (0, 0)),
            pl.BlockSpec(wkv.shape, lambda i: (0, 0)),
        ],
        out_specs=[
            pl.BlockSpec((MLA_HEADS, ts, MLA_QK_PAD), lambda i: (0, i, 0)),
            pl.BlockSpec((MLA_HEADS, ts, MLA_QK_PAD), lambda i: (0, i, 0)),
            pl.BlockSpec((MLA_HEADS, ts, 2 * MLA_V_DIM), lambda i: (0, i, 0)),
        ],
        out_shape=[
            jax.ShapeDtypeStruct((MLA_HEADS, s, MLA_QK_PAD), BF16),
            jax.ShapeDtypeStruct((MLA_HEADS, s, MLA_QK_PAD), BF16),
            jax.ShapeDtypeStruct((MLA_HEADS, s, 2 * MLA_V_DIM), BF16),
        ],
        compiler_params=_cparams(("parallel",)),
        name="mla_prep",
    )(z, z, z, tab, q_norm.reshape(1, -1), kv_norm.reshape(1, -1), wq, wkv)


MLA_BQ = 256
MLA_BK = 512
MLA_HEADS_PER_STEP = 4


def _mla_attn_kernel(q_ref, k_ref, v_ref, g_ref, o_ref):
    i = pl.program_id(1)
    bq, bk = MLA_BQ, MLA_BK
    heads = range(MLA_HEADS_PER_STEP)
    q = [q_ref[h] for h in heads]
    nd = bk // bq

    def block(kb, m, acc, masked):
        start = pl.multiple_of(kb * bk, bk)
        sc = [_dot_nt(q[h], k_ref[h, pl.ds(start, bk), :]) for h in heads]
        if masked:
            row = lax.broadcasted_iota(jnp.int32, (bq, bk), 0) + i * bq
            col = lax.broadcasted_iota(jnp.int32, (bq, bk), 1) + start
            sc = [jnp.where(col <= row, sc[h], -1e30) for h in heads]
        m_new = [jnp.maximum(m[h], jnp.max(sc[h], axis=-1, keepdims=True)) for h in heads]
        for h in heads:
            alpha = jnp.exp(m[h] - m_new[h])
            p = jnp.exp(sc[h] - m_new[h])
            acc[h] = acc[h] * alpha + _dot(p.astype(BF16), v_ref[h, pl.ds(start, bk), :])
        return m_new, acc

    def body(kb, st):
        m, acc = block(kb, list(st[0]), list(st[1]), False)
        return tuple(m), tuple(acc)

    m0 = tuple(jnp.full((bq, 1), -1e30, F32) for _ in heads)
    a0 = tuple(jnp.zeros((bq, 2 * MLA_V_DIM), F32) for _ in heads)
    last = i // nd
    m, acc = lax.fori_loop(0, last, body, (m0, a0))
    m, acc = block(last, list(m), list(acc), True)
    for h in heads:
        y = acc[h][:, :MLA_V_DIM] / acc[h][:, MLA_V_DIM:]
        o_ref[:, h * MLA_V_DIM:(h + 1) * MLA_V_DIM] = _rms_rows(y, g_ref[h:h + 1, :]).astype(o_ref.dtype)


def mla_attention(q, k, v, gains):
    s = q.shape[1]
    hps = MLA_HEADS_PER_STEP
    whole = lambda n: pl.BlockSpec((hps, s, n), lambda g, i: (g, 0, 0), pipeline_mode=pl.Buffered(1))
    return pl.pallas_call(
        _mla_attn_kernel,
        grid=(MLA_HEADS // hps, s // MLA_BQ),
        in_specs=[
            pl.BlockSpec((hps, MLA_BQ, MLA_QK_PAD), lambda g, i: (g, i, 0)),
            whole(MLA_QK_PAD),
            whole(2 * MLA_V_DIM),
            pl.BlockSpec((hps, MLA_V_DIM), lambda g, i: (g, 0)),
        ],
        out_specs=pl.BlockSpec((MLA_BQ, hps * MLA_V_DIM), lambda g, i: (i, g)),
        out_shape=jax.ShapeDtypeStruct((s, MLA_WIDTH), BF16),
        compiler_params=_cparams(("parallel", "arbitrary")),
        name="mla_attention",
    )(q, k, v, gains)


def _head_sum_matrix():
    r = lax.broadcasted_iota(jnp.int32, (128, 128), 0) // RWKV_HEAD_DIM
    c = lax.broadcasted_iota(jnp.int32, (128, 128), 1) // RWKV_HEAD_DIM
    return jnp.where(r == c, 1.0, 0.0).astype(BF16)


def _head_sums(x, ones_bd):
    parts = []
    for j in range(x.shape[1] // 128):
        parts.append(_exact_lhs_dot_rhs(x[:, j * 128:(j + 1) * 128], ones_bd))
    return jnp.concatenate(parts, axis=-1)


def _exact_lhs_dot_rhs(x, m_bf16):
    hi, mid, lo = _split3(x)
    return _dot(hi, m_bf16) + (_dot(mid, m_bf16) + _dot(lo, m_bf16))


def _rwkv_prep_kernel(r_ref, k_ref, v_ref, lora_ref, xg_ref, mu_ref, mul_ref, mug_ref,
                      w0_ref, w2_ref, a0_ref, a2_ref, g2_ref, kk_ref, ka_ref,
                      ro_ref, lw_ref, ko_ref, vo_ref, kko_ref, bo_ref, go_ref,
                      prev_ref, prevl_ref):
    ts = r_ref.shape[0]

    @pl.when(pl.program_id(0) == 0)
    def _():
        prev_ref[...] = jnp.zeros_like(prev_ref)
        prevl_ref[...] = jnp.zeros_like(prevl_ref)

    first_row = lax.broadcasted_iota(jnp.int32, (ts, 1), 0) == 0

    def shift_mix(x, prev_row, mu):
        x_prev = jnp.where(first_row, prev_row, pltpu.roll(x, 1, 0))
        return x + (x_prev - x) * mu

    r_in, k_in, v_in = r_ref[...], k_ref[...], v_ref[...]
    lora_in, xg_in = lora_ref[...], xg_ref[...]
    r = shift_mix(r_in, prev_ref[0:1, :], mu_ref[0:1, :])
    k = shift_mix(k_in, prev_ref[1:2, :], mu_ref[1:2, :])
    v = shift_mix(v_in, prev_ref[2:3, :], mu_ref[2:3, :])
    lora = shift_mix(lora_in, prevl_ref[0:1, :], mul_ref[...])
    xg = shift_mix(xg_in, prevl_ref[1:2, :], mug_ref[...])
    prev_ref[0:1, :] = r_in[ts - 1:ts, :]
    prev_ref[1:2, :] = k_in[ts - 1:ts, :]
    prev_ref[2:3, :] = v_in[ts - 1:ts, :]
    prevl_ref[0:1, :] = lora_in[ts - 1:ts, :]
    prevl_ref[1:2, :] = xg_in[ts - 1:ts, :]

    dw = w0_ref[...] + _dot(jnp.tanh(lora).astype(BF16), w2_ref[...])
    log_w = -jax.nn.softplus(-dw) - 0.5
    lw_ref[...] = -jnp.exp(log_w)
    a = jax.nn.sigmoid(a0_ref[...] + _dot(lora.astype(BF16), a2_ref[...]))
    go_ref[...] = _dot(jax.nn.sigmoid(xg).astype(BF16), g2_ref[...])

    kk = k * kk_ref[...]
    ss = _head_sums(kk * kk, _head_sum_matrix())
    kk = kk / jnp.maximum(jnp.sqrt(ss), 1e-12)
    ro_ref[...] = r
    vo_ref[...] = v
    ko_ref[...] = k * (1.0 + (a - 1.0) * ka_ref[...])
    kko_ref[...] = kk
    bo_ref[...] = kk * a


def rwkv_prep(z, p, *, ts):
    s = z.shape[0]
    w = RWKV_WIDTH
    row = lambda n: pl.BlockSpec((1, n), lambda i: (0, 0))
    big = pl.BlockSpec((ts, w), lambda i: (i, 0))
    out = jax.ShapeDtypeStruct((s, w), F32)
    return pl.pallas_call(
        _rwkv_prep_kernel,
        grid=(s // ts,),
        in_specs=[
            pl.BlockSpec((ts, w), lambda i: (i, COL_R // w)),
            pl.BlockSpec((ts, w), lambda i: (i, COL_K // w)),
            pl.BlockSpec((ts, w), lambda i: (i, COL_V // w)),
            pl.BlockSpec((ts, 128), lambda i: (i, COL_LORA // 128)),
            pl.BlockSpec((ts, 128), lambda i: (i, COL_XG // 128)),
            pl.BlockSpec((3, w), lambda i: (0, 0)),
            row(128), row(128),
            row(w), pl.BlockSpec((128, w), lambda i: (0, 0)),
            row(w), pl.BlockSpec((128, w), lambda i: (0, 0)),
            pl.BlockSpec((128, w), lambda i: (0, 0)),
            row(w), row(w),
        ],
        out_specs=[big] * 7,
        out_shape=[out] * 7,
        scratch_shapes=[pltpu.VMEM((8, w), F32), pltpu.VMEM((8, 128), F32)],
        compiler_params=_cparams(("arbitrary",)),
        name="rwkv_prep",
    )(z, z, z, z, z, p["mu_rkv"], p["mu_lora"], p["mu_g"], p["w0"], p["w2"], p["a0"], p["a2"],
      p["g2"], p["k_k"], p["k_a"])


def _rwkv_chunk_kernel(r_ref, lw_ref, k_ref, v_ref, kk_ref, b_ref, g_ref, rk_ref, lnw_ref, lnb_ref,
                       o_ref, st_ref, y_ref):
    c = RWKV_CHUNK
    n = RWKV_HEAD_DIM
    nh = r_ref.shape[1] // n
    chunks = range(r_ref.shape[0] // c)

    @pl.when(pl.program_id(1) == 0)
    def _():
        st_ref[...] = jnp.zeros_like(st_ref)

    ri = lax.broadcasted_iota(jnp.int32, (c, c), 0)
    ci = lax.broadcasted_iota(jnp.int32, (c, c), 1)
    tril_ones = jnp.where(ri >= ci, 1.0, 0.0).astype(BF16)

    gh = RWKV_GROUP
    gw = gh * n
    groups = range(nh // gh)
    gs = [slice(g * gw, (g + 1) * gw) for g in groups]
    same_head = (lax.broadcasted_iota(jnp.int32, (gw, gw), 0) // n
                 == lax.broadcasted_iota(jnp.int32, (gw, gw), 1) // n)
    zero_b = jnp.zeros((gw, gw), BF16)

    def bd(y):
        yb = y.astype(BF16)
        return jnp.where(same_head, jnp.concatenate([yb] * gh, axis=0), zero_b)

    def mmb(x, y_bd):
        return _dot(x.astype(BF16), y_bd)

    ri2 = lax.broadcasted_iota(jnp.int32, (2 * c, gw), 0)
    ci2 = lax.broadcasted_iota(jnp.int32, (2 * c, gw), 1) % c
    causal2 = ci2 <= jnp.where(ri2 < c, ri2 - 1, ri2 - c)
    rc = lax.broadcasted_iota(jnp.int32, (c, gw), 0)
    cc = lax.broadcasted_iota(jnp.int32, (c, gw), 1) % c
    eye_g = jnp.where(rc == cc, 1.0, 0.0).astype(F32)
    d16_g = (rc // 16) == (cc // 16)
    d32_g = (rc // 32) == (cc // 32)
    lo_g = jnp.logical_and(d32_g, jnp.logical_not(d16_g))
    colhead = lax.broadcasted_iota(jnp.int32, (n, gw), 1) // n

    p_end, v_c, lhs, b_t, k_t, dec = [], [], [], [], [], []
    for ch in chunks:
        rows = slice(ch * c, (ch + 1) * c)
        lw = lw_ref[rows, :]
        cum = _exact_lhs_dot(tril_ones, lw)
        p_incl = jnp.exp(cum)
        p_inv = jnp.exp(-cum)
        pe = p_incl[c - 1:c, :]
        bt = b_ref[rows, :] * p_inv
        kt = k_ref[rows, :] * p_inv
        p_end.append(pe)
        v_c.append(v_ref[rows, :])
        b_t.append(bt)
        k_t.append(kt)
        lhs.append(jnp.concatenate([kk_ref[rows, :] * jnp.exp(cum - lw), r_ref[rows, :] * p_incl],
                                   axis=0).astype(BF16))
        dec.append(jnp.concatenate([bt * pe, kt * pe], axis=0).astype(BF16))

    cg = [(ch, g) for ch in chunks for g in groups]
    ab = {q: jnp.where(causal2, _dot_nt(lhs[q[0]][:, gs[q[1]]], bd(b_t[q[0]][:, gs[q[1]]])), 0.0)
          for q in cg}
    ak = {q: jnp.where(causal2, _dot_nt(lhs[q[0]][:, gs[q[1]]], bd(k_t[q[0]][:, gs[q[1]]])), 0.0)
          for q in cg}
    akv = {q: mmb(ak[q], bd(v_c[q[0]][:, gs[q[1]]])) for q in cg}
    a_ub = {q: ab[q][:c] for q in cg}
    ld = {q: jnp.where(d16_g, a_ub[q], 0.0) for q in cg}
    x = {q: eye_g - ld[q] for q in cg}
    pw = {q: mmb(ld[q], bd(ld[q])) for q in cg}
    x = {q: x[q] + mmb(x[q], bd(pw[q])) for q in cg}
    pw = {q: mmb(pw[q], bd(pw[q])) for q in cg}
    x = {q: x[q] + mmb(x[q], bd(pw[q])) for q in cg}
    pw = {q: mmb(pw[q], bd(pw[q])) for q in cg}
    x = {q: x[q] + mmb(x[q], bd(pw[q])) for q in cg}
    t = {q: mmb(x[q], bd(jnp.where(lo_g, a_ub[q], 0.0))) for q in cg}
    x = {q: x[q] - mmb(t[q], bd(x[q])) for q in cg}
    t = {q: mmb(x[q], bd(jnp.where(d32_g, 0.0, a_ub[q]))) for q in cg}
    x = {q: x[q] - mmb(t[q], bd(x[q])) for q in cg}
    u0 = {q: -mmb(x[q], bd(akv[q][:c])) for q in cg}
    wk = {q: -mmb(x[q], bd(lhs[q[0]][:c, gs[q[1]]])) for q in cg}
    lhs2 = {q: jnp.concatenate([wk[q].astype(BF16), lhs[q[0]][c:, gs[q[1]]]], axis=0) for q in cg}

    st = [st_ref[:, gs[g]] for g in groups]
    for ch in chunks:
        rows = slice(ch * c, (ch + 1) * c)
        through = [_dot_nt(lhs2[(ch, g)], bd(st[g])) for g in groups]
        u = [through[g][:c] + u0[(ch, g)] for g in groups]
        for g in groups:
            y_ref[rows, gs[g]] = through[g][c:] + akv[(ch, g)][c:] + mmb(ab[(ch, g)][c:], bd(u[g]))
        for g in groups:
            uv = jnp.concatenate([u[g], v_c[ch][:, gs[g]]], axis=0).astype(BF16)
            cross = _dot_tn(uv, dec[ch][:, gs[g]])
            upd = st[g] * p_end[ch][:, gs[g]]
            for h in range(gh):
                upd = upd + jnp.where(colhead == h, cross[h * n:(h + 1) * n], 0.0)
            st[g] = upd
    for g in groups:
        st_ref[:, gs[g]] = st[g]

    ones_bd = _head_sum_matrix()
    y = y_ref[...]
    v = v_ref[...]
    mean = _head_sums(y, ones_bd) * (1.0 / n)
    yc = y - mean
    var = _head_sums(yc * yc, ones_bd) * (1.0 / n)
    yn = yc * lax.rsqrt(var + RWKV_GN_EPS) * lnw_ref[...] + lnb_ref[...]
    bonus = _head_sums(r_ref[...] * k_ref[...] * rk_ref[...], ones_bd)
    o_ref[...] = ((yn + bonus * v) * g_ref[...]).astype(o_ref.dtype)


def rwkv_chunk(r, lw, k, v, kk, b, g, p):
    s, w = r.shape
    rows = RWKV_CHUNK * RWKV_CHUNKS_PER_STEP
    big = pl.BlockSpec((rows, w), lambda hg, t: (t, hg))
    row = pl.BlockSpec((1, w), lambda hg, t: (0, hg))
    return pl.pallas_call(
        _rwkv_chunk_kernel,
        grid=(1, s // rows),
        in_specs=[big] * 7 + [row] * 3,
        out_specs=big,
        out_shape=jax.ShapeDtypeStruct((s, w), BF16),
        scratch_shapes=[
            pltpu.VMEM((RWKV_HEAD_DIM, w), F32),
            pltpu.VMEM((rows, w), F32),
        ],
        compiler_params=_cparams(("parallel", "arbitrary")),
        name="rwkv_chunk",
    )(r, lw, k, v, kk, b, g, p["r_k"], p["ln_w"], p["ln_b"])


def _swap_halves(w):
    half = w.shape[-1] // 2
    return jnp.concatenate([w[..., half:], w[..., :half]], axis=-1)


def _prep_w_in(w_in):
    wt = jnp.swapaxes(w_in, 1, 2)
    sizes = [SB_WIDTH] * 3 + [MLA_Q_LORA, MLA_KV_LORA, MLA_ROPE_DIM] + [RWKV_WIDTH] * 3 + [
        RWKV_DECAY_LORA, RWKV_A_LORA, RWKV_GATE_LORA]
    idx = [int(i) for i in np.cumsum(sizes)[:-1]]
    sbq, sbk, sbv, cq, ckv, krope, r, k, v, xw, xa, xg = jnp.split(wt, idx, axis=1)
    sbq = sbq * (SB_HEAD_DIM ** -0.5)
    half = MLA_ROPE_DIM // 2
    krope_sw = jnp.concatenate([krope[:, half:], krope[:, :half]], axis=1)
    rows = [r, k, v, cq, ckv, xw, xa, xg, krope, krope_sw]
    used = sum(t.shape[1] for t in rows)
    rows.append(jnp.zeros((wt.shape[0], Z_F32_COLS - used, wt.shape[2]), wt.dtype))
    rows += [sbq, sbk, sbv]
    return jnp.concatenate(rows, axis=1).astype(BF16)


def _prep_mla_w(w_uq, w_ukv):
    q = w_uq.reshape(MLA_Q_LORA, MLA_HEADS, MLA_NOPE_DIM + MLA_ROPE_DIM)
    nope, pe = q[..., :MLA_NOPE_DIM], q[..., MLA_NOPE_DIM:]
    zpad = jnp.zeros_like(pe)
    wq = jnp.concatenate([nope, pe, zpad, _swap_halves(pe), zpad], axis=-1)
    return wq.reshape(MLA_Q_LORA, MLA_HEADS * 384).astype(BF16), w_ukv.astype(BF16)


def _rope_table(positions):
    half = MLA_ROPE_DIM // 2
    inv_freq = ROPE_THETA ** (-jnp.arange(half, dtype=F32) / half)
    ang = positions.astype(F32)[:, None] * inv_freq
    cos, sin = jnp.cos(ang), jnp.sin(ang)
    z = jnp.zeros((positions.shape[0], MLA_ROPE_DIM), F32)
    return jnp.concatenate([cos, cos, z, -sin, sin, z], axis=-1)


def _pick_tile(n, candidates):
    for c in candidates:
        if n % c == 0:
            return c
    raise ValueError(f"no tile for {n}")


def kernel(x, positions, ffn1_norm, ffn1_gate, ffn1_up, ffn1_down, mix_norm, w_in, mla_q_norm,
           mla_w_uq, mla_kv_norm, mla_w_ukv, rwkv_mu, rwkv_w0, rwkv_w2, rwkv_a0, rwkv_a2, rwkv_g2,
           rwkv_k_k, rwkv_k_a, rwkv_r_k, rwkv_ln_w, rwkv_ln_b, sb_out_norm, mla_out_norm, w_out,
           ffn2_norm, ffn2_gate, ffn2_up, ffn2_down, final_norm):
    bsz, s, d = x.shape
    assert bsz == 1 and d == SB_WIDTH + MLA_WIDTH + RWKV_WIDTH
    depth = w_in.shape[0]
    d_ff = ffn1_gate.shape[-1]
    tm = _pick_tile(s, (1024, 512, 256, 128))
    tf = _pick_tile(d_ff, (512, 256, 128))
    x = x[0]
    tab = _rope_table(positions[0])
    w = RWKV_WIDTH
    zl = jnp.zeros((RWKV_DECAY_LORA, w), F32)
    ffn_w = [(wg, wu, wd.astype(BF16))
             for wg, wu, wd in ((ffn1_gate, ffn1_up, ffn1_down), (ffn2_gate, ffn2_up, ffn2_down))]
    w_in_b = _prep_w_in(w_in)
    w_out_b = w_out.astype(BF16)

    def ffn(x, g, ws, l):
        act = ffn_up(x, g, ws[0], ws[1], l, tm=tm, tn=tf)
        return ffn_down(x, act, ws[2], l, tm=tm, tn=512)

    for l in range(depth):
        x = ffn(x, ffn1_norm[l], ffn_w[0], l)

        z, zb = norm_proj(x, mix_norm[l], w_in_b, l, tm=tm)

        y_sb = sb_attention(zb, sb_out_norm[l])

        wq, wkv = _prep_mla_w(mla_w_uq[l], mla_w_ukv[l])
        q, k, v = mla_prep(z, tab, mla_q_norm[l], mla_kv_norm[l], wq, wkv, ts=_pick_tile(s, (256, 128)))
        y_mla = mla_attention(q, k, v, mla_out_norm[l])

        mu = rwkv_mu[l]
        p = {
            "mu_rkv": mu[:3 * w].reshape(3, w),
            "mu_lora": mu[3 * w:3 * w + 128].reshape(1, 128),
            "mu_g": mu[3 * w + 128:].reshape(1, 128),
            "w0": rwkv_w0[l].reshape(1, w),
            "w2": jnp.concatenate([rwkv_w2[l], zl], axis=0).astype(BF16),
            "a0": rwkv_a0[l].reshape(1, w),
            "a2": jnp.concatenate([zl, rwkv_a2[l]], axis=0).astype(BF16),
            "g2": rwkv_g2[l].astype(BF16),
            "k_k": rwkv_k_k[l].reshape(1, w),
            "k_a": rwkv_k_a[l].reshape(1, w),
            "r_k": rwkv_r_k[l].reshape(1, w),
            "ln_w": rwkv_ln_w[l].reshape(1, w),
            "ln_b": rwkv_ln_b[l].reshape(1, w),
        }
        r_, lw_, k_, v_, kk_, b_, g_ = rwkv_prep(z, p, ts=_pick_tile(s, (256, 128)))
        y_rwkv = rwkv_chunk(r_, lw_, k_, v_, kk_, b_, g_, p)

        x = out_proj(x, y_sb, y_mla, y_rwkv, w_out_b, l, tm=_pick_tile(s, (512, 256, 128)))

        x = ffn(x, ffn2_norm[l], ffn_w[1], l)

    return final_rmsnorm(x, final_norm, tm=_pick_tile(s, (256, 128)))[None]
```

```python
import functools

import jax
import jax.numpy as jnp
import numpy as np
from jax import lax
from jax.experimental import pallas as pl
from jax.experimental.pallas import tpu as pltpu

F32 = jnp.float32
BF16 = jnp.bfloat16

NORM_EPS = 1e-6
SB_HEADS = 4
SB_HEAD_DIM = 128
SB_WIDTH = SB_HEADS * SB_HEAD_DIM
MLA_HEADS = 4
MLA_NOPE_DIM = 128
MLA_ROPE_DIM = 64
MLA_V_DIM = 128
MLA_Q_LORA = 512
MLA_KV_LORA = 256
MLA_WIDTH = MLA_HEADS * MLA_V_DIM
MLA_QK_PAD = 256
ROPE_THETA = 10000.0
RWKV_HEADS = 16
RWKV_HEAD_DIM = 64
RWKV_WIDTH = RWKV_HEADS * RWKV_HEAD_DIM
RWKV_DECAY_LORA = 64
RWKV_A_LORA = 64
RWKV_GATE_LORA = 128
RWKV_GN_EPS = 64e-5
RWKV_CHUNK = 64
RWKV_GROUP = 4
RWKV_CHUNKS_PER_STEP = 4

COL_R, COL_K, COL_V = 0, 1024, 2048
COL_CQ, COL_CKV = 3072, 3584
COL_LORA, COL_XG, COL_ROPE = 3840, 3968, 4096
Z_F32_COLS = 4608
COL_SBQ, COL_SBK, COL_SBV = 0, 512, 1024
Z_B16_COLS = 1536
Z_TILE = 768

VMEM_LIMIT = 48 * 1024 * 1024


def _cparams(sem):
    return pltpu.CompilerParams(dimension_semantics=sem, vmem_limit_bytes=VMEM_LIMIT)


def _dot(a, b):
    return jnp.dot(a, b, preferred_element_type=F32)


def _dot_nt(a, b):
    return lax.dot_general(a, b, (((1,), (1,)), ((), ())), preferred_element_type=F32)


def _dot_tn(a, b):
    return lax.dot_general(a, b, (((0,), (0,)), ((), ())), preferred_element_type=F32)


def _split3(x):
    hi = x.astype(BF16)
    r1 = x - hi.astype(F32)
    mid = r1.astype(BF16)
    lo = (r1 - mid.astype(F32)).astype(BF16)
    return hi, mid, lo


def _split2(x):
    hi = x.astype(BF16)
    lo = (x - hi.astype(F32)).astype(BF16)
    return hi, lo


def _mm3(dot, a, b):
    ah, al = _split2(a)
    bh, bl = _split2(b)
    return dot(ah, bh) + (dot(ah, bl) + dot(al, bh))


def _mm(dot, a, b, passes):
    if passes == 1:
        return dot(a.astype(BF16), b.astype(BF16))
    return _mm3(dot, a, b)


def _exact_lhs_dot(m_bf16, x):
    hi, mid, lo = _split3(x)
    return _dot(m_bf16, hi) + (_dot(m_bf16, mid) + _dot(m_bf16, lo))


def _rms_rows(x, g):
    ms = jnp.mean(x * x, axis=-1, keepdims=True)
    return x * lax.rsqrt(ms + NORM_EPS) * g


def _norm_proj_kernel(x_ref, g_ref, w_ref, o_ref, ob_ref, h_ref):
    @pl.when(pl.program_id(1) == 0)
    def _():
        h_ref[...] = _rms_rows(x_ref[...], g_ref[...]).astype(BF16)

    z = _dot_nt(h_ref[...], w_ref[...])
    nf = Z_F32_COLS // Z_TILE

    @pl.when(pl.program_id(1) < nf)
    def _():
        o_ref[...] = z

    @pl.when(pl.program_id(1) >= nf)
    def _():
        ob_ref[...] = z.astype(BF16)


def norm_proj(x, g, w, l, *, tm):
    s, d = x.shape
    tn = Z_TILE
    nf, nb = Z_F32_COLS // tn, Z_B16_COLS // tn
    return pl.pallas_call(
        _norm_proj_kernel,
        grid=(s // tm, nf + nb),
        in_specs=[
            pl.BlockSpec((tm, d), lambda i, j: (i, 0)),
            pl.BlockSpec((1, d), lambda i, j: (0, 0)),
            pl.BlockSpec((None, tn, d), lambda i, j: (l, j, 0)),
        ],
        out_specs=[
            pl.BlockSpec((tm, tn), lambda i, j: (i, jnp.minimum(j, nf - 1))),
            pl.BlockSpec((tm, tn), lambda i, j: (i, jnp.maximum(j - nf, 0))),
        ],
        out_shape=[jax.ShapeDtypeStruct((s, Z_F32_COLS), F32),
                   jax.ShapeDtypeStruct((s, Z_B16_COLS), BF16)],
        scratch_shapes=[pltpu.VMEM((tm, d), BF16)],
        compiler_params=_cparams(("parallel", "arbitrary")),
        name="norm_proj",
    )(x, g.reshape(1, d), w)


def _ffn_up_kernel(x_ref, g_ref, wg_ref, wu_ref, o_ref, h_ref):
    @pl.when(pl.program_id(1) == 0)
    def _():
        h_ref[...] = _rms_rows(x_ref[...], g_ref[...]).astype(BF16)

    h = h_ref[...]
    a = _dot(h, wg_ref[...].astype(BF16))
    u = _dot(h, wu_ref[...].astype(BF16))
    o_ref[...] = (a * jax.nn.sigmoid(a) * u).astype(o_ref.dtype)


def ffn_up(x, g, wg, wu, l, *, tm, tn):
    s, d = x.shape
    f = wg.shape[2]
    return pl.pallas_call(
        _ffn_up_kernel,
        grid=(s // tm, f // tn),
        in_specs=[
            pl.BlockSpec((tm, d), lambda i, j: (i, 0)),
            pl.BlockSpec((1, d), lambda i, j: (0, 0)),
            pl.BlockSpec((None, d, tn), lambda i, j: (l, 0, j)),
            pl.BlockSpec((None, d, tn), lambda i, j: (l, 0, j)),
        ],
        out_specs=pl.BlockSpec((tm, tn), lambda i, j: (i, j)),
        out_shape=jax.ShapeDtypeStruct((s, f), BF16),
        scratch_shapes=[pltpu.VMEM((tm, d), BF16)],
        compiler_params=_cparams(("parallel", "arbitrary")),
        name="ffn_up",
    )(x, g.reshape(1, d), wg, wu)


def _ffn_down_kernel(x_ref, a_ref, w_ref, o_ref):
    o_ref[...] = x_ref[...] + 0.5 * _dot(a_ref[...], w_ref[...])


def ffn_down(x, act, wd, l, *, tm, tn):
    s, d = x.shape
    f = act.shape[1]
    return pl.pallas_call(
        _ffn_down_kernel,
        grid=(s // tm, d // tn),
        in_specs=[
            pl.BlockSpec((tm, tn), lambda i, j: (i, j)),
            pl.BlockSpec((tm, f), lambda i, j: (i, 0)),
            pl.BlockSpec((None, f, tn), lambda i, j: (l, 0, j)),
        ],
        out_specs=pl.BlockSpec((tm, tn), lambda i, j: (i, j)),
        out_shape=jax.ShapeDtypeStruct((s, d), F32),
        compiler_params=_cparams(("parallel", "arbitrary")),
        name="ffn_down",
    )(x, act, wd)


def _out_proj_kernel(x_ref, a_ref, b_ref, c_ref, wa_ref, wb_ref, wc_ref, o_ref):
    acc = _dot(a_ref[...], wa_ref[...])
    acc += _dot(b_ref[...], wb_ref[...])
    acc += _dot(c_ref[...], wc_ref[...])
    o_ref[...] = x_ref[...] + acc


def out_proj(x, y_sb, y_mla, y_rwkv, w_out, l, *, tm):
    s, d = x.shape
    assert SB_WIDTH == MLA_WIDTH and RWKV_WIDTH == SB_WIDTH + MLA_WIDTH
    wspec = lambda rows, blk: pl.BlockSpec((None, rows, d), lambda i: (l, blk, 0),
                                           pipeline_mode=pl.Buffered(1))
    return pl.pallas_call(
        _out_proj_kernel,
        grid=(s // tm,),
        in_specs=[
            pl.BlockSpec((tm, d), lambda i: (i, 0)),
            pl.BlockSpec((tm, SB_WIDTH), lambda i: (i, 0)),
            pl.BlockSpec((tm, MLA_WIDTH), lambda i: (i, 0)),
            pl.BlockSpec((tm, RWKV_WIDTH), lambda i: (i, 0)),
            wspec(SB_WIDTH, 0),
            wspec(MLA_WIDTH, 1),
            wspec(RWKV_WIDTH, 1),
        ],
        out_specs=pl.BlockSpec((tm, d), lambda i: (i, 0)),
        out_shape=jax.ShapeDtypeStruct((s, d), F32),
        compiler_params=_cparams(("parallel",)),
        name="out_proj",
    )(x, y_sb, y_mla, y_rwkv, w_out, w_out, w_out)


def _final_norm_kernel(x_ref, g_ref, o_ref):
    o_ref[...] = _rms_rows(x_ref[...], g_ref[...])


def final_rmsnorm(x, g, *, tm):
    s, d = x.shape
    return pl.pallas_call(
        _final_norm_kernel,
        grid=(s // tm,),
        in_specs=[pl.BlockSpec((tm, d), lambda i: (i, 0)), pl.BlockSpec((1, d), lambda i: (0, 0))],
        out_specs=pl.BlockSpec((tm, d), lambda i: (i, 0)),
        out_shape=jax.ShapeDtypeStruct((s, d), F32),
        compiler_params=_cparams(("parallel",)),
        name="final_norm",
    )(x, g.reshape(1, d))


SB_BQ = 256
SB_BK = 128
SB_DEAD_LOG = -104.0


def _sb_attn_kernel(q_ref, k_ref, v_ref, g_ref, o_ref):
    i = pl.program_id(0)
    bq, bk, dh = SB_BQ, SB_BK, SB_HEAD_DIM
    heads = range(SB_HEADS)
    hs = [slice(h * dh, (h + 1) * dh) for h in heads]
    q = [q_ref[:, hs[h]] for h in heads]
    row = lax.broadcasted_iota(jnp.int32, (bq, bk), 0)
    col = lax.broadcasted_iota(jnp.int32, (bq, bk), 1)
    kr = lax.broadcasted_iota(jnp.int32, (bk, bk), 0)
    kc = lax.broadcasted_iota(jnp.int32, (bk, bk), 1)
    upper = jnp.where(kr > kc, 1.0, 0.0).astype(BF16)

    def block(kb, carry, acc, masked):
        start = pl.multiple_of(kb * bk, bk)
        z = [_dot_nt(q[h], k_ref[pl.ds(start, bk), hs[h]]) for h in heads]
        log_beta = [jnp.minimum(z[h], 0.0) - jnp.log(1.0 + jnp.exp(-jnp.abs(z[h]))) for h in heads]
        log_keep = [log_beta[h] - z[h] for h in heads]
        if masked:
            before = (start + col) < (i * bq + row)
            log_keep = [jnp.where(before, log_keep[h], 0.0) for h in heads]
        split = [_split2(log_keep[h]) for h in heads]
        within = [_dot(split[h][0], upper) + _dot(split[h][1], upper) for h in heads]
        a = [jnp.exp(log_beta[h] + (carry[h] + within[h])) for h in heads]
        if masked:
            a = [jnp.where(before, a[h], 0.0) for h in heads]
        acc = [acc[h] + _dot(a[h].astype(BF16), v_ref[pl.ds(start, bk), hs[h]]) for h in heads]
        carry = [carry[h] + jnp.sum(log_keep[h], axis=-1, keepdims=True) for h in heads]
        return carry, acc

    def alive_of(carry):
        m = jnp.max(carry[0])
        for h in heads[1:]:
            m = jnp.maximum(m, jnp.max(carry[h]))
        return m

    carry = [jnp.zeros((bq, 1), F32) for _ in heads]
    acc = [jnp.zeros((bq, dh), F32) for _ in heads]
    nd = bq // bk
    first = i * nd
    for d in range(nd - 1, -1, -1):
        carry, acc = block(first + d, carry, acc, True)

    def cond(st):
        return jnp.logical_and(st[0] >= 0, st[1] > SB_DEAD_LOG)

    def body(st):
        kb, _, carry, acc = st
        carry, acc = block(kb, list(carry), list(acc), False)
        return kb - 1, alive_of(carry), tuple(carry), tuple(acc)

    st = lax.while_loop(cond, body, (first - 1, alive_of(carry), tuple(carry), tuple(acc)))
    acc = st[3]
    for h in heads:
        o_ref[:, hs[h]] = _rms_rows(acc[h], g_ref[h:h + 1, :]).astype(o_ref.dtype)


def sb_attention(zb, gains):
    s = zb.shape[0]
    whole = lambda c: pl.BlockSpec((s, SB_WIDTH), lambda i: (0, c // SB_WIDTH),
                                   pipeline_mode=pl.Buffered(1))
    return pl.pallas_call(
        _sb_attn_kernel,
        grid=(s // SB_BQ,),
        in_specs=[
            pl.BlockSpec((SB_BQ, SB_WIDTH), lambda i: (i, COL_SBQ // SB_WIDTH)),
            whole(COL_SBK),
            whole(COL_SBV),
            pl.BlockSpec((SB_HEADS, SB_HEAD_DIM), lambda i: (0, 0)),
        ],
        out_specs=pl.BlockSpec((SB_BQ, SB_WIDTH), lambda i: (i, 0)),
        out_shape=jax.ShapeDtypeStruct((s, SB_WIDTH), BF16),
        compiler_params=_cparams(("arbitrary",)),
        name="sb_attention",
    )(zb, zb, zb, gains)


def _mla_prep_kernel(cq_ref, ckv_ref, rope_ref, tab_ref, qn_ref, kvn_ref, wq_ref, wkv_ref,
                     q_ref, k_ref, v_ref):
    scale = (MLA_NOPE_DIM + MLA_ROPE_DIM) ** -0.5
    hq = _rms_rows(cq_ref[...], qn_ref[...]).astype(BF16)
    hkv = _rms_rows(ckv_ref[...], kvn_ref[...]).astype(BF16)
    cos2 = tab_ref[:, :128]
    sin2 = tab_ref[:, 128:]
    kr = rope_ref[...]
    k_pe = kr * cos2 + pltpu.roll(kr, 64, 1) * sin2
    k_pe = k_pe.astype(BF16)
    qall = _dot(hq, wq_ref[...])
    kvall = _dot(hkv, wkv_ref[...])
    for h in range(MLA_HEADS):
        qh = qall[:, h * 384:(h + 1) * 384]
        q_ref[h, :, :128] = (qh[:, :128] * scale).astype(BF16)
        q_ref[h, :, 128:] = ((qh[:, 128:256] * cos2 + qh[:, 256:384] * sin2) * scale).astype(BF16)
        k_ref[h, :, :128] = kvall[:, h * 256:h * 256 + 128].astype(BF16)
        k_ref[h, :, 128:] = k_pe
        v_ref[h, :, :128] = kvall[:, h * 256 + 128:(h + 1) * 256].astype(BF16)
        v_ref[h, :, 128:] = jnp.ones((kr.shape[0], 128), BF16)


def mla_prep(z, tab, q_norm, kv_norm, wq, wkv, *, ts):
    s = z.shape[0]
    return pl.pallas_call(
        _mla_prep_kernel,
        grid=(s // ts,),
        in_specs=[
            pl.BlockSpec((ts, MLA_Q_LORA), lambda i: (i, COL_CQ // MLA_Q_LORA)),
            pl.BlockSpec((ts, MLA_KV_LORA), lambda i: (i, COL_CKV // MLA_KV_LORA)),
            pl.BlockSpec((ts, 128), lambda i: (i, COL_ROPE // 128)),
            pl.BlockSpec((ts, 256), lambda i: (i, 0)),
            pl.BlockSpec((1, MLA_Q_LORA), lambda i: (0, 0)),
            pl.BlockSpec((1, MLA_KV_LORA), lambda i: (0, 0)),
            pl.BlockSpec(wq.shape, lambda i: (0, 0)),
            pl.BlockSpec(wkv.shape, lambda i: (0, 0)),
        ],
        out_specs=[
            pl.BlockSpec((MLA_HEADS, ts, MLA_QK_PAD), lambda i: (0, i, 0)),
            pl.BlockSpec((MLA_HEADS, ts, MLA_QK_PAD), lambda i: (0, i, 0)),
            pl.BlockSpec((MLA_HEADS, ts, 2 * MLA_V_DIM), lambda i: (0, i, 0)),
        ],
        out_shape=[
            jax.ShapeDtypeStruct((MLA_HEADS, s, MLA_QK_PAD), BF16),
            jax.ShapeDtypeStruct((MLA_HEADS, s, MLA_QK_PAD), BF16),
            jax.ShapeDtypeStruct((MLA_HEADS, s, 2 * MLA_V_DIM), BF16),
        ],
        compiler_params=_cparams(("parallel",)),
        name="mla_prep",
    )(z, z, z, tab, q_norm.reshape(1, -1), kv_norm.reshape(1, -1), wq, wkv)


MLA_BQ = 256
MLA_BK = 512
MLA_HEADS_PER_STEP = 4


def _mla_attn_kernel(q_ref, k_ref, v_ref, g_ref, o_ref):
    i = pl.program_id(1)
    bq, bk = MLA_BQ, MLA_BK
    heads = range(MLA_HEADS_PER_STEP)
    q = [q_ref[h] for h in heads]
    nd = bk // bq

    def block(kb, m, acc, masked):
        start = pl.multiple_of(kb * bk, bk)
        sc = [_dot_nt(q[h], k_ref[h, pl.ds(start, bk), :]) for h in heads]
        if masked:
            row = lax.broadcasted_iota(jnp.int32, (bq, bk), 0) + i * bq
            col = lax.broadcasted_iota(jnp.int32, (bq, bk), 1) + start
            sc = [jnp.where(col <= row, sc[h], -1e30) for h in heads]
        m_new = [jnp.maximum(m[h], jnp.max(sc[h], axis=-1, keepdims=True)) for h in heads]
        for h in heads:
            alpha = jnp.exp(m[h] - m_new[h])
            p = jnp.exp(sc[h] - m_new[h])
            acc[h] = acc[h] * alpha + _dot(p.astype(BF16), v_ref[h, pl.ds(start, bk), :])
        return m_new, acc

    def body(kb, st):
        m, acc = block(kb, list(st[0]), list(st[1]), False)
        return tuple(m), tuple(acc)

    m0 = tuple(jnp.full((bq, 1), -1e30, F32) for _ in heads)
    a0 = tuple(jnp.zeros((bq, 2 * MLA_V_DIM), F32) for _ in heads)
    last = i // nd
    m, acc = lax.fori_loop(0, last, body, (m0, a0))
    m, acc = block(last, list(m), list(acc), True)
    for h in heads:
        y = acc[h][:, :MLA_V_DIM] / acc[h][:, MLA_V_DIM:]
        o_ref[:, h * MLA_V_DIM:(h + 1) * MLA_V_DIM] = _rms_rows(y, g_ref[h:h + 1, :]).astype(o_ref.dtype)


def mla_attention(q, k, v, gains):
    s = q.shape[1]
    hps = MLA_HEADS_PER_STEP
    whole = lambda n: pl.BlockSpec((hps, s, n), lambda g, i: (g, 0, 0), pipeline_mode=pl.Buffered(1))
    return pl.pallas_call(
        _mla_attn_kernel,
        grid=(MLA_HEADS // hps, s // MLA_BQ),
        in_specs=[
            pl.BlockSpec((hps, MLA_BQ, MLA_QK_PAD), lambda g, i: (g, i, 0)),
            whole(MLA_QK_PAD),
            whole(2 * MLA_V_DIM),
            pl.BlockSpec((hps, MLA_V_DIM), lambda g, i: (g, 0)),
        ],
        out_specs=pl.BlockSpec((MLA_BQ, hps * MLA_V_DIM), lambda g, i: (i, g)),
        out_shape=jax.ShapeDtypeStruct((s, MLA_WIDTH), BF16),
        compiler_params=_cparams(("parallel", "arbitrary")),
        name="mla_attention",
    )(q, k, v, gains)


def _head_sum_matrix():
    r = lax.broadcasted_iota(jnp.int32, (128, 128), 0) // RWKV_HEAD_DIM
    c = lax.broadcasted_iota(jnp.int32, (128, 128), 1) // RWKV_HEAD_DIM
    return jnp.where(r == c, 1.0, 0.0).astype(BF16)


def _head_sums(x, ones_bd):
    parts = []
    for j in range(x.shape[1] // 128):
        hi, lo = _split2(x[:, j * 128:(j + 1) * 128])
        parts.append(_dot(hi, ones_bd) + _dot(lo, ones_bd))
    return jnp.concatenate(parts, axis=-1)


def _rwkv_prep_kernel(r_ref, k_ref, v_ref, lora_ref, xg_ref, mu_ref, mul_ref, mug_ref,
                      w0_ref, w2_ref, a0_ref, a2_ref, g2_ref, kk_ref, ka_ref,
                      ro_ref, lw_ref, ko_ref, vo_ref, kko_ref, bo_ref, go_ref,
                      prev_ref, prevl_ref):
    ts = r_ref.shape[0]

    @pl.when(pl.program_id(0) == 0)
    def _():
        prev_ref[...] = jnp.zeros_like(prev_ref)
        prevl_ref[...] = jnp.zeros_like(prevl_ref)

    first_row = lax.broadcasted_iota(jnp.int32, (ts, 1), 0) == 0

    def shift_mix(x, prev_row, mu):
        x_prev = jnp.where(first_row, prev_row, pltpu.roll(x, 1, 0))
        return x + (x_prev - x) * mu

    r_in, k_in, v_in = r_ref[...], k_ref[...], v_ref[...]
    lora_in, xg_in = lora_ref[...], xg_ref[...]
    r = shift_mix(r_in, prev_ref[0:1, :], mu_ref[0:1, :])
    k = shift_mix(k_in, prev_ref[1:2, :], mu_ref[1:2, :])
    v = shift_mix(v_in, prev_ref[2:3, :], mu_ref[2:3, :])
    lora = shift_mix(lora_in, prevl_ref[0:1, :], mul_ref[...])
    xg = shift_mix(xg_in, prevl_ref[1:2, :], mug_ref[...])
    prev_ref[0:1, :] = r_in[ts - 1:ts, :]
    prev_ref[1:2, :] = k_in[ts - 1:ts, :]
    prev_ref[2:3, :] = v_in[ts - 1:ts, :]
    prevl_ref[0:1, :] = lora_in[ts - 1:ts, :]
    prevl_ref[1:2, :] = xg_in[ts - 1:ts, :]

    dw = w0_ref[...] + _dot(jnp.tanh(lora).astype(BF16), w2_ref[...])
    log_w = jnp.minimum(dw, 0.0) - jnp.log(1.0 + jnp.exp(-jnp.abs(dw))) - 0.5
    lw_ref[...] = -jnp.exp(log_w)
    a = jax.nn.sigmoid(a0_ref[...] + _dot(lora.astype(BF16), a2_ref[...]))
    go_ref[...] = _dot(jax.nn.sigmoid(xg).astype(BF16), g2_ref[...])

    kk = k * kk_ref[...]
    ss = _head_sums(kk * kk, _head_sum_matrix())
    kk = kk / jnp.maximum(jnp.sqrt(ss), 1e-12)
    ro_ref[...] = r
    vo_ref[...] = v
    ko_ref[...] = k * (1.0 + (a - 1.0) * ka_ref[...])
    kko_ref[...] = kk
    bo_ref[...] = kk * a


def _rwkv_chunk_kernel(r_ref, lw_ref, k_ref, v_ref, kk_ref, b_ref, g_ref, rk_ref, lnw_ref, lnb_ref,
                       o_ref, st_ref, y_ref):
    c = RWKV_CHUNK
    n = RWKV_HEAD_DIM
    nh = r_ref.shape[1] // n
    chunks = range(r_ref.shape[0] // c)

    @pl.when(pl.program_id(0) == 0)
    def _():
        st_ref[...] = jnp.zeros_like(st_ref)

    ri = lax.broadcasted_iota(jnp.int32, (c, c), 0)
    ci = lax.broadcasted_iota(jnp.int32, (c, c), 1)
    tril_ones = jnp.where(ri >= ci, 1.0, 0.0).astype(BF16)

    gh = RWKV_GROUP
    gw = gh * n
    groups = range(nh // gh)
    gs = [slice(g * gw, (g + 1) * gw) for g in groups]
    same_head = (lax.broadcasted_iota(jnp.int32, (gw, gw), 0) // n
                 == lax.broadcasted_iota(jnp.int32, (gw, gw), 1) // n)
    zero_b = jnp.zeros((gw, gw), BF16)

    def bd(y):
        yb = y.astype(BF16)
        return jnp.where(same_head, jnp.concatenate([yb] * gh, axis=0), zero_b)

    def mmb(x, y_bd):
        return _dot(x.astype(BF16), y_bd)

    ri2 = lax.broadcasted_iota(jnp.int32, (2 * c, gw), 0)
    ci2 = lax.broadcasted_iota(jnp.int32, (2 * c, gw), 1) % c
    causal2 = ci2 <= jnp.where(ri2 < c, ri2 - 1, ri2 - c)
    rc = lax.broadcasted_iota(jnp.int32, (c, gw), 0)
    cc = lax.broadcasted_iota(jnp.int32, (c, gw), 1) % c
    eye_g = jnp.where(rc == cc, 1.0, 0.0).astype(F32)
    d16_g = (rc // 16) == (cc // 16)
    d32_g = (rc // 32) == (cc // 32)
    lo_g = jnp.logical_and(d32_g, jnp.logical_not(d16_g))
    colhead = lax.broadcasted_iota(jnp.int32, (n, gw), 1) // n

    p_end, v_c, lhs, b_t, k_t, dec = [], [], [], [], [], []
    for ch in chunks:
        rows = slice(ch * c, (ch + 1) * c)
        lw = lw_ref[rows, :]
        cum = _exact_lhs_dot(tril_ones, lw)
        p_incl = jnp.exp(cum)
        p_inv = jnp.exp(-cum)
        pe = p_incl[c - 1:c, :]
        bt = b_ref[rows, :] * p_inv
        kt = k_ref[rows, :] * p_inv
        p_end.append(pe)
        v_c.append(v_ref[rows, :])
        b_t.append(bt)
        k_t.append(kt)
        lhs.append(jnp.concatenate([kk_ref[rows, :] * jnp.exp(cum - lw), r_ref[rows, :] * p_incl],
                                   axis=0).astype(BF16))
        dec.append(jnp.concatenate([bt * pe, kt * pe], axis=0).astype(BF16))

    cg = [(ch, g) for ch in chunks for g in groups]
    ab = {q: jnp.where(causal2, _dot_nt(lhs[q[0]][:, gs[q[1]]], bd(b_t[q[0]][:, gs[q[1]]])), 0.0)
          for q in cg}
    ak = {q: jnp.where(causal2, _dot_nt(lhs[q[0]][:, gs[q[1]]], bd(k_t[q[0]][:, gs[q[1]]])), 0.0)
          for q in cg}
    akv = {q: mmb(ak[q], bd(v_c[q[0]][:, gs[q[1]]])) for q in cg}
    a_ub = {q: ab[q][:c] for q in cg}
    ld = {q: jnp.where(d16_g, a_ub[q], 0.0) for q in cg}
    x = {q: eye_g - ld[q] for q in cg}
    pw = {q: mmb(ld[q], bd(ld[q])) for q in cg}
    x = {q: x[q] + mmb(x[q], bd(pw[q])) for q in cg}
    pw = {q: mmb(pw[q], bd(pw[q])) for q in cg}
    x = {q: x[q] + mmb(x[q], bd(pw[q])) for q in cg}
    pw = {q: mmb(pw[q], bd(pw[q])) for q in cg}
    x = {q: x[q] + mmb(x[q], bd(pw[q])) for q in cg}
    t = {q: mmb(x[q], bd(jnp.where(lo_g, a_ub[q], 0.0))) for q in cg}
    x = {q: x[q] - mmb(t[q], bd(x[q])) for q in cg}
    t = {q: mmb(x[q], bd(jnp.where(d32_g, 0.0, a_ub[q]))) for q in cg}
    x = {q: x[q] - mmb(t[q], bd(x[q])) for q in cg}
    u0 = {q: -mmb(x[q], bd(akv[q][:c])) for q in cg}
    wk = {q: -mmb(x[q], bd(lhs[q[0]][:c, gs[q[1]]])) for q in cg}
    lhs2 = {q: jnp.concatenate([wk[q].astype(BF16), lhs[q[0]][c:, gs[q[1]]]], axis=0) for q in cg}

    st = [st_ref[:, gs[g]] for g in groups]
    for ch in chunks:
        rows = slice(ch * c, (ch + 1) * c)
        through = [_dot_nt(lhs2[(ch, g)], bd(st[g])) for g in groups]
        u = [through[g][:c] + u0[(ch, g)] for g in groups]
        for g in groups:
            y_ref[rows, gs[g]] = through[g][c:] + akv[(ch, g)][c:] + mmb(ab[(ch, g)][c:], bd(u[g]))
        for g in groups:
            uv = jnp.concatenate([u[g], v_c[ch][:, gs[g]]], axis=0).astype(BF16)
            cross = _dot_tn(uv, dec[ch][:, gs[g]])
            upd = st[g] * p_end[ch][:, gs[g]]
            for h in range(gh):
                upd = upd + jnp.where(colhead == h, cross[h * n:(h + 1) * n], 0.0)
            st[g] = upd
    for g in groups:
        st_ref[:, gs[g]] = st[g]

    ones_bd = _head_sum_matrix()
    y = y_ref[...]
    v = v_ref[...]
    mean = _head_sums(y, ones_bd) * (1.0 / n)
    yc = y - mean
    var = _head_sums(yc * yc, ones_bd) * (1.0 / n)
    yn = yc * lax.rsqrt(var + RWKV_GN_EPS) * lnw_ref[...] + lnb_ref[...]
    bonus = _head_sums(r_ref[...] * k_ref[...] * rk_ref[...], ones_bd)
    o_ref[...] = ((yn + bonus * v) * g_ref[...]).astype(o_ref.dtype)


def _rwkv_kernel(zr_ref, zk_ref, zv_ref, zl_ref, zg_ref, mu_ref, mul_ref, mug_ref, w0_ref, w2_ref,
                 a0_ref, a2_ref, g2_ref, kk_ref, ka_ref, rk_ref, lnw_ref, lnb_ref,
                 o_ref, st_ref, y_ref, prev_ref, prevl_ref, *staged):
    _rwkv_prep_kernel(zr_ref, zk_ref, zv_ref, zl_ref, zg_ref, mu_ref, mul_ref, mug_ref, w0_ref, w2_ref,
                      a0_ref, a2_ref, g2_ref, kk_ref, ka_ref, *staged, prev_ref, prevl_ref)
    _rwkv_chunk_kernel(*staged, rk_ref, lnw_ref, lnb_ref, o_ref, st_ref, y_ref)


def rwkv_mix(z, p):
    s = z.shape[0]
    w = RWKV_WIDTH
    rows = RWKV_CHUNK * RWKV_CHUNKS_PER_STEP
    row = lambda n: pl.BlockSpec((1, n), lambda i: (0, 0))
    lora_w = pl.BlockSpec((128, w), lambda i: (0, 0))
    return pl.pallas_call(
        _rwkv_kernel,
        grid=(s // rows,),
        in_specs=[
            pl.BlockSpec((rows, w), lambda i: (i, COL_R // w)),
            pl.BlockSpec((rows, w), lambda i: (i, COL_K // w)),
            pl.BlockSpec((rows, w), lambda i: (i, COL_V // w)),
            pl.BlockSpec((rows, 128), lambda i: (i, COL_LORA // 128)),
            pl.BlockSpec((rows, 128), lambda i: (i, COL_XG // 128)),
            pl.BlockSpec((3, w), lambda i: (0, 0)),
            row(128), row(128),
            row(w), lora_w,
            row(w), lora_w,
            lora_w,
            row(w), row(w),
            row(w), row(w), row(w),
        ],
        out_specs=pl.BlockSpec((rows, w), lambda i: (i, 0)),
        out_shape=jax.ShapeDtypeStruct((s, w), BF16),
        scratch_shapes=[
            pltpu.VMEM((RWKV_HEAD_DIM, w), F32),
            pltpu.VMEM((rows, w), F32),
            pltpu.VMEM((8, w), F32), pltpu.VMEM((8, 128), F32),
        ] + [pltpu.VMEM((rows, w), F32)] * 7,
        compiler_params=_cparams(("arbitrary",)),
        name="rwkv_mix",
    )(z, z, z, z, z, p["mu_rkv"], p["mu_lora"], p["mu_g"], p["w0"], p["w2"], p["a0"], p["a2"],
      p["g2"], p["k_k"], p["k_a"], p["r_k"], p["ln_w"], p["ln_b"])


def _swap_halves(w):
    half = w.shape[-1] // 2
    return jnp.concatenate([w[..., half:], w[..., :half]], axis=-1)


def _prep_w_in(w_in):
    wt = jnp.swapaxes(w_in, 1, 2)
    sizes = [SB_WIDTH] * 3 + [MLA_Q_LORA, MLA_KV_LORA, MLA_ROPE_DIM] + [RWKV_WIDTH] * 3 + [
        RWKV_DECAY_LORA, RWKV_A_LORA, RWKV_GATE_LORA]
    idx = [int(i) for i in np.cumsum(sizes)[:-1]]
    sbq, sbk, sbv, cq, ckv, krope, r, k, v, xw, xa, xg = jnp.split(wt, idx, axis=1)
    sbq = sbq * (SB_HEAD_DIM ** -0.5)
    half = MLA_ROPE_DIM // 2
    krope_sw = jnp.concatenate([krope[:, half:], krope[:, :half]], axis=1)
    rows = [r, k, v, cq, ckv, xw, xa, xg, krope, krope_sw]
    used = sum(t.shape[1] for t in rows)
    rows.append(jnp.zeros((wt.shape[0], Z_F32_COLS - used, wt.shape[2]), wt.dtype))
    rows += [sbq, sbk, sbv]
    return jnp.concatenate(rows, axis=1).astype(BF16)


def _prep_mla_w(w_uq, w_ukv):
    q = w_uq.reshape(MLA_Q_LORA, MLA_HEADS, MLA_NOPE_DIM + MLA_ROPE_DIM)
    nope, pe = q[..., :MLA_NOPE_DIM], q[..., MLA_NOPE_DIM:]
    zpad = jnp.zeros_like(pe)
    wq = jnp.concatenate([nope, pe, zpad, _swap_halves(pe), zpad], axis=-1)
    return wq.reshape(MLA_Q_LORA, MLA_HEADS * 384).astype(BF16), w_ukv.astype(BF16)


def _rope_table(positions):
    half = MLA_ROPE_DIM // 2
    inv_freq = ROPE_THETA ** (-jnp.arange(half, dtype=F32) / half)
    ang = positions.astype(F32)[:, None] * inv_freq
    cos, sin = jnp.cos(ang), jnp.sin(ang)
    z = jnp.zeros((positions.shape[0], MLA_ROPE_DIM), F32)
    return jnp.concatenate([cos, cos, z, -sin, sin, z], axis=-1)


def _pick_tile(n, candidates):
    for c in candidates:
        if n % c == 0:
            return c
    raise ValueError(f"no tile for {n}")


def kernel(x, positions, ffn1_norm, ffn1_gate, ffn1_up, ffn1_down, mix_norm, w_in, mla_q_norm,
           mla_w_uq, mla_kv_norm, mla_w_ukv, rwkv_mu, rwkv_w0, rwkv_w2, rwkv_a0, rwkv_a2, rwkv_g2,
           rwkv_k_k, rwkv_k_a, rwkv_r_k, rwkv_ln_w, rwkv_ln_b, sb_out_norm, mla_out_norm, w_out,
           ffn2_norm, ffn2_gate, ffn2_up, ffn2_down, final_norm):
    bsz, s, d = x.shape
    assert bsz == 1 and d == SB_WIDTH + MLA_WIDTH + RWKV_WIDTH
    depth = w_in.shape[0]
    d_ff = ffn1_gate.shape[-1]
    tm = _pick_tile(s, (1024, 512, 256, 128))
    tf = _pick_tile(d_ff, (512, 256, 128))
    x = x[0]
    tab = _rope_table(positions[0])
    w = RWKV_WIDTH
    zl = jnp.zeros((RWKV_DECAY_LORA, w), F32)
    ffn_w = [(wg, wu, wd.astype(BF16))
             for wg, wu, wd in ((ffn1_gate, ffn1_up, ffn1_down), (ffn2_gate, ffn2_up, ffn2_down))]
    w_in_b = _prep_w_in(w_in)
    w_out_b = w_out.astype(BF16)

    def ffn(x, g, ws, l):
        act = ffn_up(x, g, ws[0], ws[1], l, tm=tm, tn=tf)
        return ffn_down(x, act, ws[2], l, tm=tm, tn=512)

    for l in range(depth):
        x = ffn(x, ffn1_norm[l], ffn_w[0], l)

        z, zb = norm_proj(x, mix_norm[l], w_in_b, l, tm=tm)

        y_sb = sb_attention(zb, sb_out_norm[l])

        wq, wkv = _prep_mla_w(mla_w_uq[l], mla_w_ukv[l])
        q, k, v = mla_prep(z, tab, mla_q_norm[l], mla_kv_norm[l], wq, wkv, ts=_pick_tile(s, (256, 128)))
        y_mla = mla_attention(q, k, v, mla_out_norm[l])

        mu = rwkv_mu[l]
        p = {
            "mu_rkv": mu[:3 * w].reshape(3, w),
            "mu_lora": mu[3 * w:3 * w + 128].reshape(1, 128),
            "mu_g": mu[3 * w + 128:].reshape(1, 128),
            "w0": rwkv_w0[l].reshape(1, w),
            "w2": jnp.concatenate([rwkv_w2[l], zl], axis=0).astype(BF16),
            "a0": rwkv_a0[l].reshape(1, w),
            "a2": jnp.concatenate([zl, rwkv_a2[l]], axis=0).astype(BF16),
            "g2": rwkv_g2[l].astype(BF16),
            "k_k": rwkv_k_k[l].reshape(1, w),
            "k_a": rwkv_k_a[l].reshape(1, w),
            "r_k": rwkv_r_k[l].reshape(1, w),
            "ln_w": rwkv_ln_w[l].reshape(1, w),
            "ln_b": rwkv_ln_b[l].reshape(1, w),
        }
        y_rwkv = rwkv_mix(z, p)

        x = out_proj(x, y_sb, y_mla, y_rwkv, w_out_b, l, tm=_pick_tile(s, (512, 256, 128)))

        x = ffn(x, ffn2_norm[l], ffn_w[1], l)

    return final_rmsnorm(x, final_norm, tm=_pick_tile(s, (256, 128)))[None]
```

```python
import functools

import jax
import jax.numpy as jnp
import numpy as np
from jax import lax
from jax.experimental import pallas as pl
from jax.experimental.pallas import tpu as pltpu

F32 = jnp.float32
BF16 = jnp.bfloat16

NORM_EPS = 1e-6
SB_HEADS = 4
SB_HEAD_DIM = 128
SB_WIDTH = SB_HEADS * SB_HEAD_DIM
MLA_HEADS = 4
MLA_NOPE_DIM = 128
MLA_ROPE_DIM = 64
MLA_V_DIM = 128
MLA_Q_LORA = 512
MLA_KV_LORA = 256
MLA_WIDTH = MLA_HEADS * MLA_V_DIM
MLA_QK_PAD = 256
ROPE_THETA = 10000.0
RWKV_HEADS = 16
RWKV_HEAD_DIM = 64
RWKV_WIDTH = RWKV_HEADS * RWKV_HEAD_DIM
RWKV_DECAY_LORA = 64
RWKV_A_LORA = 64
RWKV_GATE_LORA = 128
RWKV_GN_EPS = 64e-5
RWKV_CHUNK = 64
RWKV_GROUP = 4
RWKV_CHUNKS_PER_STEP = 4

COL_R, COL_K, COL_V = 0, 1024, 2048
COL_CQ, COL_CKV = 3072, 3584
COL_LORA, COL_XG, COL_ROPE = 3840, 3968, 4096
Z_F32_COLS = 4608
COL_SBQ, COL_SBK, COL_SBV = 0, 512, 1024
Z_B16_COLS = 1536
Z_TILE = 768

VMEM_LIMIT = 48 * 1024 * 1024


def _cparams(sem):
    return pltpu.CompilerParams(dimension_semantics=sem, vmem_limit_bytes=VMEM_LIMIT)


def _dot(a, b):
    return jnp.dot(a, b, preferred_element_type=F32)


def _dot_nt(a, b):
    return lax.dot_general(a, b, (((1,), (1,)), ((), ())), preferred_element_type=F32)


def _dot_tn(a, b):
    return lax.dot_general(a, b, (((0,), (0,)), ((), ())), preferred_element_type=F32)


def _split3(x):
    hi = x.astype(BF16)
    r1 = x - hi.astype(F32)
    mid = r1.astype(BF16)
    lo = (r1 - mid.astype(F32)).astype(BF16)
    return hi, mid, lo


def _split2(x):
    hi = x.astype(BF16)
    lo = (x - hi.astype(F32)).astype(BF16)
    return hi, lo


def _mm3(dot, a, b):
    ah, al = _split2(a)
    bh, bl = _split2(b)
    return dot(ah, bh) + (dot(ah, bl) + dot(al, bh))


def _mm(dot, a, b, passes):
    if passes == 1:
        return dot(a.astype(BF16), b.astype(BF16))
    return _mm3(dot, a, b)


def _exact_lhs_dot(m_bf16, x):
    hi, mid, lo = _split3(x)
    return _dot(m_bf16, hi) + (_dot(m_bf16, mid) + _dot(m_bf16, lo))


def _rms_rows(x, g):
    ms = jnp.mean(x * x, axis=-1, keepdims=True)
    return x * lax.rsqrt(ms + NORM_EPS) * g


def _norm_proj_kernel(x_ref, g_ref, w_ref, o_ref, ob_ref, h_ref):
    @pl.when(pl.program_id(1) == 0)
    def _():
        h_ref[...] = _rms_rows(x_ref[...], g_ref[...]).astype(BF16)

    z = _dot_nt(h_ref[...], w_ref[...])
    nf = Z_F32_COLS // Z_TILE

    @pl.when(pl.program_id(1) < nf)
    def _():
        o_ref[...] = z

    @pl.when(pl.program_id(1) >= nf)
    def _():
        ob_ref[...] = z.astype(BF16)


def norm_proj(x, g, w, l, *, tm):
    s, d = x.shape
    tn = Z_TILE
    nf, nb = Z_F32_COLS // tn, Z_B16_COLS // tn
    return pl.pallas_call(
        _norm_proj_kernel,
        grid=(s // tm, nf + nb),
        in_specs=[
            pl.BlockSpec((tm, d), lambda i, j: (i, 0)),
            pl.BlockSpec((1, d), lambda i, j: (0, 0)),
            pl.BlockSpec((None, tn, d), lambda i, j: (l, j, 0)),
        ],
        out_specs=[
            pl.BlockSpec((tm, tn), lambda i, j: (i, jnp.minimum(j, nf - 1))),
            pl.BlockSpec((tm, tn), lambda i, j: (i, jnp.maximum(j - nf, 0))),
        ],
        out_shape=[jax.ShapeDtypeStruct((s, Z_F32_COLS), F32),
                   jax.ShapeDtypeStruct((s, Z_B16_COLS), BF16)],
        scratch_shapes=[pltpu.VMEM((tm, d), BF16)],
        compiler_params=_cparams(("parallel", "arbitrary")),
        name="norm_proj",
    )(x, g.reshape(1, d), w)


def _ffn_up_kernel(x_ref, g_ref, wg_ref, wu_ref, o_ref, h_ref):
    @pl.when(pl.program_id(1) == 0)
    def _():
        h_ref[...] = _rms_rows(x_ref[...], g_ref[...]).astype(BF16)

    h = h_ref[...]
    a = _dot(h, wg_ref[...].astype(BF16))
    u = _dot(h, wu_ref[...].astype(BF16))
    o_ref[...] = (a * jax.nn.sigmoid(a) * u).astype(o_ref.dtype)


def ffn_up(x, g, wg, wu, l, *, tm, tn):
    s, d = x.shape
    f = wg.shape[2]
    return pl.pallas_call(
        _ffn_up_kernel,
        grid=(s // tm, f // tn),
        in_specs=[
            pl.BlockSpec((tm, d), lambda i, j: (i, 0)),
            pl.BlockSpec((1, d), lambda i, j: (0, 0)),
            pl.BlockSpec((None, d, tn), lambda i, j: (l, 0, j)),
            pl.BlockSpec((None, d, tn), lambda i, j: (l, 0, j)),
        ],
        out_specs=pl.BlockSpec((tm, tn), lambda i, j: (i, j)),
        out_shape=jax.ShapeDtypeStruct((s, f), BF16),
        scratch_shapes=[pltpu.VMEM((tm, d), BF16)],
        compiler_params=_cparams(("parallel", "arbitrary")),
        name="ffn_up",
    )(x, g.reshape(1, d), wg, wu)


def _ffn_down_kernel(x_ref, a_ref, w_ref, o_ref):
    o_ref[...] = x_ref[...] + 0.5 * _dot(a_ref[...], w_ref[...])


def ffn_down(x, act, wd, l, *, tm, tn):
    s, d = x.shape
    f = act.shape[1]
    return pl.pallas_call(
        _ffn_down_kernel,
        grid=(s // tm, d // tn),
        in_specs=[
            pl.BlockSpec((tm, tn), lambda i, j: (i, j)),
            pl.BlockSpec((tm, f), lambda i, j: (i, 0)),
            pl.BlockSpec((None, f, tn), lambda i, j: (l, 0, j)),
        ],
        out_specs=pl.BlockSpec((tm, tn), lambda i, j: (i, j)),
        out_shape=jax.ShapeDtypeStruct((s, d), F32),
        compiler_params=_cparams(("parallel", "arbitrary")),
        name="ffn_down",
    )(x, act, wd)


def _out_proj_kernel(x_ref, a_ref, b_ref, c_ref, wa_ref, wb_ref, wc_ref, o_ref):
    acc = _dot(a_ref[...], wa_ref[...])
    acc += _dot(b_ref[...], wb_ref[...])
    acc += _dot(c_ref[...], wc_ref[...])
    o_ref[...] = x_ref[...] + acc


def out_proj(x, y_sb, y_mla, y_rwkv, w_out, l, *, tm):
    s, d = x.shape
    assert SB_WIDTH == MLA_WIDTH and RWKV_WIDTH == SB_WIDTH + MLA_WIDTH
    wspec = lambda rows, blk: pl.BlockSpec((None, rows, d), lambda i: (l, blk, 0),
                                           pipeline_mode=pl.Buffered(1))
    return pl.pallas_call(
        _out_proj_kernel,
        grid=(s // tm,),
        in_specs=[
            pl.BlockSpec((tm, d), lambda i: (i, 0)),
            pl.BlockSpec((tm, SB_WIDTH), lambda i: (i, 0)),
            pl.BlockSpec((tm, MLA_WIDTH), lambda i: (i, 0)),
            pl.BlockSpec((tm, RWKV_WIDTH), lambda i: (i, 0)),
            wspec(SB_WIDTH, 0),
            wspec(MLA_WIDTH, 1),
            wspec(RWKV_WIDTH, 1),
        ],
        out_specs=pl.BlockSpec((tm, d), lambda i: (i, 0)),
        out_shape=jax.ShapeDtypeStruct((s, d), F32),
        compiler_params=_cparams(("parallel",)),
        name="out_proj",
    )(x, y_sb, y_mla, y_rwkv, w_out, w_out, w_out)


def _final_norm_kernel(x_ref, g_ref, o_ref):
    o_ref[...] = _rms_rows(x_ref[...], g_ref[...])


def final_rmsnorm(x, g, *, tm):
    s, d = x.shape
    return pl.pallas_call(
        _final_norm_kernel,
        grid=(s // tm,),
        in_specs=[pl.BlockSpec((tm, d), lambda i: (i, 0)), pl.BlockSpec((1, d), lambda i: (0, 0))],
        out_specs=pl.BlockSpec((tm, d), lambda i: (i, 0)),
        out_shape=jax.ShapeDtypeStruct((s, d), F32),
        compiler_params=_cparams(("parallel",)),
        name="final_norm",
    )(x, g.reshape(1, d))


SB_BQ = 256
SB_BK = 128
SB_DEAD_LOG = -104.0


def _sb_attn_kernel(q_ref, k_ref, v_ref, g_ref, o_ref):
    i = pl.program_id(0)
    bq, bk, dh = SB_BQ, SB_BK, SB_HEAD_DIM
    heads = range(SB_HEADS)
    hs = [slice(h * dh, (h + 1) * dh) for h in heads]
    q = [q_ref[:, hs[h]] for h in heads]
    row = lax.broadcasted_iota(jnp.int32, (bq, bk), 0)
    col = lax.broadcasted_iota(jnp.int32, (bq, bk), 1)
    kr = lax.broadcasted_iota(jnp.int32, (bk, bk), 0)
    kc = lax.broadcasted_iota(jnp.int32, (bk, bk), 1)
    upper = jnp.where(kr > kc, 1.0, 0.0).astype(BF16)

    def block(kb, carry, acc, masked):
        start = pl.multiple_of(kb * bk, bk)
        z = [_dot_nt(q[h], k_ref[pl.ds(start, bk), hs[h]]) for h in heads]
        log_beta = [jnp.minimum(z[h], 0.0) - jnp.log(1.0 + jnp.exp(-jnp.abs(z[h]))) for h in heads]
        log_keep = [log_beta[h] - z[h] for h in heads]
        if masked:
            before = (start + col) < (i * bq + row)
            log_keep = [jnp.where(before, log_keep[h], 0.0) for h in heads]
        split = [_split2(log_keep[h]) for h in heads]
        within = [_dot(split[h][0], upper) + _dot(split[h][1], upper) for h in heads]
        a = [jnp.exp(log_beta[h] + (carry[h] + within[h])) for h in heads]
        if masked:
            a = [jnp.where(before, a[h], 0.0) for h in heads]
        acc = [acc[h] + _dot(a[h].astype(BF16), v_ref[pl.ds(start, bk), hs[h]]) for h in heads]
        carry = [carry[h] + jnp.sum(log_keep[h], axis=-1, keepdims=True) for h in heads]
        return carry, acc

    def alive_of(carry):
        m = jnp.max(carry[0])
        for h in heads[1:]:
            m = jnp.maximum(m, jnp.max(carry[h]))
        return m

    carry = [jnp.zeros((bq, 1), F32) for _ in heads]
    acc = [jnp.zeros((bq, dh), F32) for _ in heads]
    nd = bq // bk
    first = i * nd
    for d in range(nd - 1, -1, -1):
        carry, acc = block(first + d, carry, acc, True)

    def cond(st):
        return jnp.logical_and(st[0] >= 0, st[1] > SB_DEAD_LOG)

    def body(st):
        kb, _, carry, acc = st
        carry, acc = block(kb, list(carry), list(acc), False)
        return kb - 1, alive_of(carry), tuple(carry), tuple(acc)

    st = lax.while_loop(cond, body, (first - 1, alive_of(carry), tuple(carry), tuple(acc)))
    acc = st[3]
    for h in heads:
        o_ref[:, hs[h]] = _rms_rows(acc[h], g_ref[h:h + 1, :]).astype(o_ref.dtype)


def sb_attention(zb, gains):
    s = zb.shape[0]
    whole = lambda c: pl.BlockSpec((s, SB_WIDTH), lambda i: (0, c // SB_WIDTH),
                                   pipeline_mode=pl.Buffered(1))
    return pl.pallas_call(
        _sb_attn_kernel,
        grid=(s // SB_BQ,),
        in_specs=[
            pl.BlockSpec((SB_BQ, SB_WIDTH), lambda i: (i, COL_SBQ // SB_WIDTH)),
            whole(COL_SBK),
            whole(COL_SBV),
            pl.BlockSpec((SB_HEADS, SB_HEAD_DIM), lambda i: (0, 0)),
        ],
        out_specs=pl.BlockSpec((SB_BQ, SB_WIDTH), lambda i: (i, 0)),
        out_shape=jax.ShapeDtypeStruct((s, SB_WIDTH), BF16),
        compiler_params=_cparams(("arbitrary",)),
        name="sb_attention",
    )(zb, zb, zb, gains)


def _mla_prep_kernel(cq_ref, ckv_ref, rope_ref, tab_ref, qn_ref, kvn_ref, wq_ref, wkv_ref,
                     q_ref, k_ref, v_ref):
    scale = (MLA_NOPE_DIM + MLA_ROPE_DIM) ** -0.5
    hq = _rms_rows(cq_ref[...], qn_ref[...]).astype(BF16)
    hkv = _rms_rows(ckv_ref[...], kvn_ref[...]).astype(BF16)
    cos2 = tab_ref[:, :128]
    sin2 = tab_ref[:, 128:]
    kr = rope_ref[...]
    k_pe = kr * cos2 + pltpu.roll(kr, 64, 1) * sin2
    k_pe = k_pe.astype(BF16)
    qall = _dot(hq, wq_ref[...])
    kvall = _dot(hkv, wkv_ref[...])
    for h in range(MLA_HEADS):
        qh = qall[:, h * 384:(h + 1) * 384]
        q_ref[h, :, :128] = (qh[:, :128] * scale).astype(BF16)
        q_ref[h, :, 128:] = ((qh[:, 128:256] * cos2 + qh[:, 256:384] * sin2) * scale).astype(BF16)
        k_ref[h, :, :128] = kvall[:, h * 256:h * 256 + 128].astype(BF16)
        k_ref[h, :, 128:] = k_pe
        v_ref[h, :, :128] = kvall[:, h * 256 + 128:(h + 1) * 256].astype(BF16)
        v_ref[h, :, 128:] = jnp.ones((kr.shape[0], 128), BF16)


def mla_prep(z, tab, q_norm, kv_norm, wq, wkv, *, ts):
    s = z.shape[0]
    return pl.pallas_call(
        _mla_prep_kernel,
        grid=(s // ts,),
        in_specs=[
            pl.BlockSpec((ts, MLA_Q_LORA), lambda i: (i, COL_CQ // MLA_Q_LORA)),
            pl.BlockSpec((ts, MLA_KV_LORA), lambda i: (i, COL_CKV // MLA_KV_LORA)),
            pl.BlockSpec((ts, 128), lambda i: (i, COL_ROPE // 128)),
            pl.BlockSpec((ts, 256), lambda i: (i, 0)),
            pl.BlockSpec((1, MLA_Q_LORA), lambda i: (0, 0)),
            pl.BlockSpec((1, MLA_KV_LORA), lambda i: (0, 0)),
            pl.BlockSpec(wq.shape, lambda i: (0, 0)),
            pl.BlockSpec(wkv.shape, lambda i: (0, 0)),
        ],
        out_specs=[
            pl.BlockSpec((MLA_HEADS, ts, MLA_QK_PAD), lambda i: (0, i, 0)),
            pl.BlockSpec((MLA_HEADS, ts, MLA_QK_PAD), lambda i: (0, i, 0)),
            pl.BlockSpec((MLA_HEADS, ts, 2 * MLA_V_DIM), lambda i: (0, i, 0)),
        ],
        out_shape=[
            jax.ShapeDtypeStruct((MLA_HEADS, s, MLA_QK_PAD), BF16),
            jax.ShapeDtypeStruct((MLA_HEADS, s, MLA_QK_PAD), BF16),
            jax.ShapeDtypeStruct((MLA_HEADS, s, 2 * MLA_V_DIM), BF16),
        ],
        compiler_params=_cparams(("parallel",)),
        name="mla_prep",
    )(z, z, z, tab, q_norm.reshape(1, -1), kv_norm.reshape(1, -1), wq, wkv)


MLA_BQ = 256
MLA_BK = 1024
MLA_HEADS_PER_STEP = 4


def _mla_attn_kernel(q_ref, k_ref, v_ref, g_ref, o_ref):
    i = pl.program_id(1)
    bq, bk = MLA_BQ, MLA_BK
    heads = range(MLA_HEADS_PER_STEP)
    q = [q_ref[h] for h in heads]
    nd = bk // bq

    def block(kb, m, acc, masked):
        start = pl.multiple_of(kb * bk, bk)
        sc = [_dot_nt(q[h], k_ref[h, pl.ds(start, bk), :]) for h in heads]
        if masked:
            row = lax.broadcasted_iota(jnp.int32, (bq, bk), 0) + i * bq
            col = lax.broadcasted_iota(jnp.int32, (bq, bk), 1) + start
            sc = [jnp.where(col <= row, sc[h], -1e30) for h in heads]
        m_new = [jnp.maximum(m[h], jnp.max(sc[h], axis=-1, keepdims=True)) for h in heads]
        for h in heads:
            alpha = jnp.exp(m[h] - m_new[h])
            p = jnp.exp(sc[h] - m_new[h])
            acc[h] = acc[h] * alpha + _dot(p.astype(BF16), v_ref[h, pl.ds(start, bk), :])
        return m_new, acc

    def body(kb, st):
        m, acc = block(kb, list(st[0]), list(st[1]), False)
        return tuple(m), tuple(acc)

    m0 = tuple(jnp.full((bq, 1), -1e30, F32) for _ in heads)
    a0 = tuple(jnp.zeros((bq, 2 * MLA_V_DIM), F32) for _ in heads)
    last = i // nd
    m, acc = lax.fori_loop(0, last, body, (m0, a0))
    m, acc = block(last, list(m), list(acc), True)
    for h in heads:
        y = acc[h][:, :MLA_V_DIM] / acc[h][:, MLA_V_DIM:]
        o_ref[:, h * MLA_V_DIM:(h + 1) * MLA_V_DIM] = _rms_rows(y, g_ref[h:h + 1, :]).astype(o_ref.dtype)


def mla_attention(q, k, v, gains):
    s = q.shape[1]
    hps = MLA_HEADS_PER_STEP
    whole = lambda n: pl.BlockSpec((hps, s, n), lambda g, i: (g, 0, 0), pipeline_mode=pl.Buffered(1))
    return pl.pallas_call(
        _mla_attn_kernel,
        grid=(MLA_HEADS // hps, s // MLA_BQ),
        in_specs=[
            pl.BlockSpec((hps, MLA_BQ, MLA_QK_PAD), lambda g, i: (g, i, 0)),
            whole(MLA_QK_PAD),
            whole(2 * MLA_V_DIM),
            pl.BlockSpec((hps, MLA_V_DIM), lambda g, i: (g, 0)),
        ],
        out_specs=pl.BlockSpec((MLA_BQ, hps * MLA_V_DIM), lambda g, i: (i, g)),
        out_shape=jax.ShapeDtypeStruct((s, MLA_WIDTH), BF16),
        compiler_params=_cparams(("parallel", "arbitrary")),
        name="mla_attention",
    )(q, k, v, gains)


def _head_sum_matrix():
    r = lax.broadcasted_iota(jnp.int32, (128, 128), 0) // RWKV_HEAD_DIM
    c = lax.broadcasted_iota(jnp.int32, (128, 128), 1) // RWKV_HEAD_DIM
    return jnp.where(r == c, 1.0, 0.0).astype(BF16)


def _head_sums(x, ones_bd):
    parts = []
    for j in range(x.shape[1] // 128):
        hi, lo = _split2(x[:, j * 128:(j + 1) * 128])
        parts.append(_dot(hi, ones_bd) + _dot(lo, ones_bd))
    return jnp.concatenate(parts, axis=-1)


def _rwkv_prep_kernel(r_ref, k_ref, v_ref, lora_ref, xg_ref, mu_ref, mul_ref, mug_ref,
                      w0_ref, w2_ref, a0_ref, a2_ref, g2_ref, kk_ref, ka_ref,
                      ro_ref, lw_ref, ko_ref, vo_ref, kko_ref, bo_ref, go_ref,
                      prev_ref, prevl_ref):
    ts = r_ref.shape[0]

    @pl.when(pl.program_id(0) == 0)
    def _():
        prev_ref[...] = jnp.zeros_like(prev_ref)
        prevl_ref[...] = jnp.zeros_like(prevl_ref)

    first_row = lax.broadcasted_iota(jnp.int32, (ts, 1), 0) == 0

    def shift_mix(x, prev_row, mu):
        x_prev = jnp.where(first_row, prev_row, pltpu.roll(x, 1, 0))
        return x + (x_prev - x) * mu

    r_in, k_in, v_in = r_ref[...], k_ref[...], v_ref[...]
    lora_in, xg_in = lora_ref[...], xg_ref[...]
    r = shift_mix(r_in, prev_ref[0:1, :], mu_ref[0:1, :])
    k = shift_mix(k_in, prev_ref[1:2, :], mu_ref[1:2, :])
    v = shift_mix(v_in, prev_ref[2:3, :], mu_ref[2:3, :])
    lora = shift_mix(lora_in, prevl_ref[0:1, :], mul_ref[...])
    xg = shift_mix(xg_in, prevl_ref[1:2, :], mug_ref[...])
    prev_ref[0:1, :] = r_in[ts - 1:ts, :]
    prev_ref[1:2, :] = k_in[ts - 1:ts, :]
    prev_ref[2:3, :] = v_in[ts - 1:ts, :]
    prevl_ref[0:1, :] = lora_in[ts - 1:ts, :]
    prevl_ref[1:2, :] = xg_in[ts - 1:ts, :]

    dw = w0_ref[...] + _dot(jnp.tanh(lora).astype(BF16), w2_ref[...])
    log_w = jnp.minimum(dw, 0.0) - jnp.log(1.0 + jnp.exp(-jnp.abs(dw))) - 0.5
    lw_ref[...] = -jnp.exp(log_w)
    a = jax.nn.sigmoid(a0_ref[...] + _dot(lora.astype(BF16), a2_ref[...]))
    go_ref[...] = _dot(jax.nn.sigmoid(xg).astype(BF16), g2_ref[...])

    kk = k * kk_ref[...]
    ss = _head_sums(kk * kk, _head_sum_matrix())
    kk = kk / jnp.maximum(jnp.sqrt(ss), 1e-12)
    ro_ref[...] = r
    vo_ref[...] = v
    ko_ref[...] = k * (1.0 + (a - 1.0) * ka_ref[...])
    kko_ref[...] = kk
    bo_ref[...] = kk * a


def _rwkv_chunk_kernel(r_ref, lw_ref, k_ref, v_ref, kk_ref, b_ref, g_ref, rk_ref, lnw_ref, lnb_ref,
                       o_ref, st_ref, y_ref):
    c = RWKV_CHUNK
    n = RWKV_HEAD_DIM
    nh = r_ref.shape[1] // n
    chunks = range(r_ref.shape[0] // c)

    @pl.when(pl.program_id(0) == 0)
    def _():
        st_ref[...] = jnp.zeros_like(st_ref)

    ri = lax.broadcasted_iota(jnp.int32, (c, c), 0)
    ci = lax.broadcasted_iota(jnp.int32, (c, c), 1)
    tril_ones = jnp.where(ri >= ci, 1.0, 0.0).astype(BF16)

    gh = RWKV_GROUP
    gw = gh * n
    groups = range(nh // gh)
    gs = [slice(g * gw, (g + 1) * gw) for g in groups]
    same_head = (lax.broadcasted_iota(jnp.int32, (gw, gw), 0) // n
                 == lax.broadcasted_iota(jnp.int32, (gw, gw), 1) // n)
    zero_b = jnp.zeros((gw, gw), BF16)

    def bd(y):
        yb = y.astype(BF16)
        return jnp.where(same_head, jnp.concatenate([yb] * gh, axis=0), zero_b)

    def mmb(x, y_bd):
        return _dot(x.astype(BF16), y_bd)

    ri2 = lax.broadcasted_iota(jnp.int32, (2 * c, gw), 0)
    ci2 = lax.broadcasted_iota(jnp.int32, (2 * c, gw), 1) % c
    causal2 = ci2 <= jnp.where(ri2 < c, ri2 - 1, ri2 - c)
    rc = lax.broadcasted_iota(jnp.int32, (c, gw), 0)
    cc = lax.broadcasted_iota(jnp.int32, (c, gw), 1) % c
    eye_g = jnp.where(rc == cc, 1.0, 0.0).astype(F32)
    d16_g = (rc // 16) == (cc // 16)
    d32_g = (rc // 32) == (cc // 32)
    lo_g = jnp.logical_and(d32_g, jnp.logical_not(d16_g))
    colhead = lax.broadcasted_iota(jnp.int32, (n, gw), 1) // n

    p_end, v_c, lhs, b_t, k_t, dec = [], [], [], [], [], []
    for ch in chunks:
        rows = slice(ch * c, (ch + 1) * c)
        lw = lw_ref[rows, :]
        cum = _exact_lhs_dot(tril_ones, lw)
        p_incl = jnp.exp(cum)
        p_inv = jnp.exp(-cum)
        pe = p_incl[c - 1:c, :]
        bt = b_ref[rows, :] * p_inv
        kt = k_ref[rows, :] * p_inv
        p_end.append(pe)
        v_c.append(v_ref[rows, :])
        b_t.append(bt)
        k_t.append(kt)
        lhs.append(jnp.concatenate([kk_ref[rows, :] * jnp.exp(cum - lw), r_ref[rows, :] * p_incl],
                                   axis=0).astype(BF16))
        dec.append(jnp.concatenate([bt * pe, kt * pe], axis=0).astype(BF16))

    cg = [(ch, g) for ch in chunks for g in groups]
    ab = {q: jnp.where(causal2, _dot_nt(lhs[q[0]][:, gs[q[1]]], bd(b_t[q[0]][:, gs[q[1]]])), 0.0)
          for q in cg}
    ak = {q: jnp.where(causal2, _dot_nt(lhs[q[0]][:, gs[q[1]]], bd(k_t[q[0]][:, gs[q[1]]])), 0.0)
          for q in cg}
    akv = {q: mmb(ak[q], bd(v_c[q[0]][:, gs[q[1]]])) for q in cg}
    a_ub = {q: ab[q][:c] for q in cg}
    ld = {q: jnp.where(d16_g, a_ub[q], 0.0) for q in cg}
    x = {q: eye_g - ld[q] for q in cg}
    pw = {q: mmb(ld[q], bd(ld[q])) for q in cg}
    x = {q: x[q] + mmb(x[q], bd(pw[q])) for q in cg}
    pw = {q: mmb(pw[q], bd(pw[q])) for q in cg}
    x = {q: x[q] + mmb(x[q], bd(pw[q])) for q in cg}
    pw = {q: mmb(pw[q], bd(pw[q])) for q in cg}
    x = {q: x[q] + mmb(x[q], bd(pw[q])) for q in cg}
    t = {q: mmb(x[q], bd(jnp.where(lo_g, a_ub[q], 0.0))) for q in cg}
    x = {q: x[q] - mmb(t[q], bd(x[q])) for q in cg}
    t = {q: mmb(x[q], bd(jnp.where(d32_g, 0.0, a_ub[q]))) for q in cg}
    x = {q: x[q] - mmb(t[q], bd(x[q])) for q in cg}
    u0 = {q: -mmb(x[q], bd(akv[q][:c])) for q in cg}
    wk = {q: -mmb(x[q], bd(lhs[q[0]][:c, gs[q[1]]])) for q in cg}
    lhs2 = {q: jnp.concatenate([wk[q].astype(BF16), lhs[q[0]][c:, gs[q[1]]]], axis=0) for q in cg}

    st = [st_ref[:, gs[g]] for g in groups]
    for ch in chunks:
        rows = slice(ch * c, (ch + 1) * c)
        through = [_dot_nt(lhs2[(ch, g)], bd(st[g])) for g in groups]
        u = [through[g][:c] + u0[(ch, g)] for g in groups]
        for g in groups:
            y_ref[rows, gs[g]] = through[g][c:] + akv[(ch, g)][c:] + mmb(ab[(ch, g)][c:], bd(u[g]))
        for g in groups:
            uv = jnp.concatenate([u[g], v_c[ch][:, gs[g]]], axis=0).astype(BF16)
            cross = _dot_tn(uv, dec[ch][:, gs[g]])
            upd = st[g] * p_end[ch][:, gs[g]]
            for h in range(gh):
                upd = upd + jnp.where(colhead == h, cross[h * n:(h + 1) * n], 0.0)
            st[g] = upd
    for g in groups:
        st_ref[:, gs[g]] = st[g]

    ones_bd = _head_sum_matrix()
    y = y_ref[...]
    v = v_ref[...]
    mean = _head_sums(y, ones_bd) * (1.0 / n)
    yc = y - mean
    var = _head_sums(yc * yc, ones_bd) * (1.0 / n)
    yn = yc * lax.rsqrt(var + RWKV_GN_EPS) * lnw_ref[...] + lnb_ref[...]
    bonus = _head_sums(r_ref[...] * k_ref[...] * rk_ref[...], ones_bd)
    o_ref[...] = ((yn + bonus * v) * g_ref[...]).astype(o_ref.dtype)


def _rwkv_kernel(zr_ref, zk_ref, zv_ref, zl_ref, zg_ref, mu_ref, mul_ref, mug_ref, w0_ref, w2_ref,
                 a0_ref, a2_ref, g2_ref, kk_ref, ka_ref, rk_ref, lnw_ref, lnb_ref,
                 o_ref, st_ref, y_ref, prev_ref, prevl_ref, *staged):
    _rwkv_prep_kernel(zr_ref, zk_ref, zv_ref, zl_ref, zg_ref, mu_ref, mul_ref, mug_ref, w0_ref, w2_ref,
                      a0_ref, a2_ref, g2_ref, kk_ref, ka_ref, *staged, prev_ref, prevl_ref)
    _rwkv_chunk_kernel(*staged, rk_ref, lnw_ref, lnb_ref, o_ref, st_ref, y_ref)


def rwkv_mix(z, p):
    s = z.shape[0]
    w = RWKV_WIDTH
    rows = RWKV_CHUNK * RWKV_CHUNKS_PER_STEP
    row = lambda n: pl.BlockSpec((1, n), lambda i: (0, 0))
    lora_w = pl.BlockSpec((128, w), lambda i: (0, 0))
    return pl.pallas_call(
        _rwkv_kernel,
        grid=(s // rows,),
        in_specs=[
            pl.BlockSpec((rows, w), lambda i: (i, COL_R // w)),
            pl.BlockSpec((rows, w), lambda i: (i, COL_K // w)),
            pl.BlockSpec((rows, w), lambda i: (i, COL_V // w)),
            pl.BlockSpec((rows, 128), lambda i: (i, COL_LORA // 128)),
            pl.BlockSpec((rows, 128), lambda i: (i, COL_XG // 128)),
            pl.BlockSpec((3, w), lambda i: (0, 0)),
            row(128), row(128),
            row(w), lora_w,
            row(w), lora_w,
            lora_w,
            row(w), row(w),
            row(w), row(w), row(w),
        ],
        out_specs=pl.BlockSpec((rows, w), lambda i: (i, 0)),
        out_shape=jax.ShapeDtypeStruct((s, w), BF16),
        scratch_shapes=[
            pltpu.VMEM((RWKV_HEAD_DIM, w), F32),
            pltpu.VMEM((rows, w), F32),
            pltpu.VMEM((8, w), F32), pltpu.VMEM((8, 128), F32),
        ] + [pltpu.VMEM((rows, w), F32)] * 7,
        compiler_params=_cparams(("arbitrary",)),
        name="rwkv_mix",
    )(z, z, z, z, z, p["mu_rkv"], p["mu_lora"], p["mu_g"], p["w0"], p["w2"], p["a0"], p["a2"],
      p["g2"], p["k_k"], p["k_a"], p["r_k"], p["ln_w"], p["ln_b"])


def _swap_halves(w):
    half = w.shape[-1] // 2
    return jnp.concatenate([w[..., half:], w[..., :half]], axis=-1)


def _prep_w_in(w_in):
    wt = jnp.swapaxes(w_in, 1, 2)
    sizes = [SB_WIDTH] * 3 + [MLA_Q_LORA, MLA_KV_LORA, MLA_ROPE_DIM] + [RWKV_WIDTH] * 3 + [
        RWKV_DECAY_LORA, RWKV_A_LORA, RWKV_GATE_LORA]
    idx = [int(i) for i in np.cumsum(sizes)[:-1]]
    sbq, sbk, sbv, cq, ckv, krope, r, k, v, xw, xa, xg = jnp.split(wt, idx, axis=1)
    sbq = sbq * (SB_HEAD_DIM ** -0.5)
    half = MLA_ROPE_DIM // 2
    krope_sw = jnp.concatenate([krope[:, half:], krope[:, :half]], axis=1)
    rows = [r, k, v, cq, ckv, xw, xa, xg, krope, krope_sw]
    used = sum(t.shape[1] for t in rows)
    rows.append(jnp.zeros((wt.shape[0], Z_F32_COLS - used, wt.shape[2]), wt.dtype))
    rows += [sbq, sbk, sbv]
    return jnp.concatenate(rows, axis=1).astype(BF16)


def _prep_mla_w(w_uq, w_ukv):
    q = w_uq.reshape(MLA_Q_LORA, MLA_HEADS, MLA_NOPE_DIM + MLA_ROPE_DIM)
    nope, pe = q[..., :MLA_NOPE_DIM], q[..., MLA_NOPE_DIM:]
    zpad = jnp.zeros_like(pe)
    wq = jnp.concatenate([nope, pe, zpad, _swap_halves(pe), zpad], axis=-1)
    return wq.reshape(MLA_Q_LORA, MLA_HEADS * 384).astype(BF16), w_ukv.astype(BF16)


def _rope_table(positions):
    half = MLA_ROPE_DIM // 2
    inv_freq = ROPE_THETA ** (-jnp.arange(half, dtype=F32) / half)
    ang = positions.astype(F32)[:, None] * inv_freq
    cos, sin = jnp.cos(ang), jnp.sin(ang)
    z = jnp.zeros((positions.shape[0], MLA_ROPE_DIM), F32)
    return jnp.concatenate([cos, cos, z, -sin, sin, z], axis=-1)


def _pick_tile(n, candidates):
    for c in candidates:
        if n % c == 0:
            return c
    raise ValueError(f"no tile for {n}")


def kernel(x, positions, ffn1_norm, ffn1_gate, ffn1_up, ffn1_down, mix_norm, w_in, mla_q_norm,
           mla_w_uq, mla_kv_norm, mla_w_ukv, rwkv_mu, rwkv_w0, rwkv_w2, rwkv_a0, rwkv_a2, rwkv_g2,
           rwkv_k_k, rwkv_k_a, rwkv_r_k, rwkv_ln_w, rwkv_ln_b, sb_out_norm, mla_out_norm, w_out,
           ffn2_norm, ffn2_gate, ffn2_up, ffn2_down, final_norm):
    bsz, s, d = x.shape
    assert bsz == 1 and d == SB_WIDTH + MLA_WIDTH + RWKV_WIDTH
    depth = w_in.shape[0]
    d_ff = ffn1_gate.shape[-1]
    tm = _pick_tile(s, (1024, 512, 256, 128))
    tf = _pick_tile(d_ff, (512, 256, 128))
    x = x[0]
    tab = _rope_table(positions[0])
    w = RWKV_WIDTH
    zl = jnp.zeros((RWKV_DECAY_LORA, w), F32)
    ffn_w = [(wg, wu, wd.astype(BF16))
             for wg, wu, wd in ((ffn1_gate, ffn1_up, ffn1_down), (ffn2_gate, ffn2_up, ffn2_down))]
    w_in_b = _prep_w_in(w_in)
    w_out_b = w_out.astype(BF16)

    def ffn(x, g, ws, l):
        act = ffn_up(x, g, ws[0], ws[1], l, tm=tm, tn=tf)
        return ffn_down(x, act, ws[2], l, tm=tm, tn=512)

    for l in range(depth):
        x = ffn(x, ffn1_norm[l], ffn_w[0], l)

        z, zb = norm_proj(x, mix_norm[l], w_in_b, l, tm=tm)

        y_sb = sb_attention(zb, sb_out_norm[l])

        wq, wkv = _prep_mla_w(mla_w_uq[l], mla_w_ukv[l])
        q, k, v = mla_prep(z, tab, mla_q_norm[l], mla_kv_norm[l], wq, wkv, ts=_pick_tile(s, (256, 128)))
        y_mla = mla_attention(q, k, v, mla_out_norm[l])

        mu = rwkv_mu[l]
        p = {
            "mu_rkv": mu[:3 * w].reshape(3, w),
            "mu_lora": mu[3 * w:3 * w + 128].reshape(1, 128),
            "mu_g": mu[3 * w + 128:].reshape(1, 128),
            "w0": rwkv_w0[l].reshape(1, w),
            "w2": jnp.concatenate([rwkv_w2[l], zl], axis=0).astype(BF16),
            "a0": rwkv_a0[l].reshape(1, w),
            "a2": jnp.concatenate([zl, rwkv_a2[l]], axis=0).astype(BF16),
            "g2": rwkv_g2[l].astype(BF16),
            "k_k": rwkv_k_k[l].reshape(1, w),
            "k_a": rwkv_k_a[l].reshape(1, w),
            "r_k": rwkv_r_k[l].reshape(1, w),
            "ln_w": rwkv_ln_w[l].reshape(1, w),
            "ln_b": rwkv_ln_b[l].reshape(1, w),
        }
        y_rwkv = rwkv_mix(z, p)

        x = out_proj(x, y_sb, y_mla, y_rwkv, w_out_b, l, tm=_pick_tile(s, (512, 256, 128)))

        x = ffn(x, ffn2_norm[l], ffn_w[1], l)

    return final_rmsnorm(x, final_norm, tm=_pick_tile(s, (256, 128)))[None]
```

```python
import jax
import jax.numpy as jnp
import numpy as np
from jax import lax
from jax.experimental import pallas as pl
from jax.experimental.pallas import tpu as pltpu

F32 = jnp.float32
BF16 = jnp.bfloat16

NORM_EPS = 1e-6
SB_HEADS = 4
SB_HEAD_DIM = 128
SB_WIDTH = SB_HEADS * SB_HEAD_DIM
MLA_HEADS = 4
MLA_NOPE_DIM = 128
MLA_ROPE_DIM = 64
MLA_V_DIM = 128
MLA_Q_LORA = 512
MLA_KV_LORA = 256
MLA_WIDTH = MLA_HEADS * MLA_V_DIM
MLA_QK_PAD = 256
ROPE_THETA = 10000.0
RWKV_HEADS = 16
RWKV_HEAD_DIM = 64
RWKV_WIDTH = RWKV_HEADS * RWKV_HEAD_DIM
RWKV_DECAY_LORA = 64
RWKV_A_LORA = 64
RWKV_GATE_LORA = 128
RWKV_GN_EPS = 64e-5
RWKV_CHUNK = 64
RWKV_GROUP = 4
RWKV_CHUNKS_PER_STEP = 4

COL_R, COL_K, COL_V = 0, 1024, 2048
COL_CQ, COL_CKV = 3072, 3584
COL_LORA, COL_XG, COL_ROPE = 3840, 3968, 4096
Z_F32_COLS = 4608
COL_SBQ, COL_SBK, COL_SBV = 0, 512, 1024
Z_B16_COLS = 1536
Z_TILE = 768

VMEM_LIMIT = 48 * 1024 * 1024


def _cparams(sem):
    return pltpu.CompilerParams(dimension_semantics=sem, vmem_limit_bytes=VMEM_LIMIT)


def _dot(a, b):
    return jnp.dot(a, b, preferred_element_type=F32)


def _dot_nt(a, b):
    return lax.dot_general(a, b, (((1,), (1,)), ((), ())), preferred_element_type=F32)


def _dot_tn(a, b):
    return lax.dot_general(a, b, (((0,), (0,)), ((), ())), preferred_element_type=F32)


def _split3(x):
    hi = x.astype(BF16)
    r1 = x - hi.astype(F32)
    mid = r1.astype(BF16)
    lo = (r1 - mid.astype(F32)).astype(BF16)
    return hi, mid, lo


def _split2(x):
    hi = x.astype(BF16)
    lo = (x - hi.astype(F32)).astype(BF16)
    return hi, lo


def _exact_lhs_dot(m_bf16, x):
    hi, mid, lo = _split3(x)
    return _dot(m_bf16, hi) + (_dot(m_bf16, mid) + _dot(m_bf16, lo))


def _rms_rows(x, g):
    ms = jnp.mean(x * x, axis=-1, keepdims=True)
    return x * lax.rsqrt(ms + NORM_EPS) * g


def _norm_proj_kernel(x_ref, g_ref, w_ref, o_ref, ob_ref, h_ref):
    @pl.when(pl.program_id(1) == 0)
    def _():
        h_ref[...] = _rms_rows(x_ref[...], g_ref[...]).astype(BF16)

    z = _dot_nt(h_ref[...], w_ref[...])
    nf = Z_F32_COLS // Z_TILE

    @pl.when(pl.program_id(1) < nf)
    def _():
        o_ref[...] = z

    @pl.when(pl.program_id(1) >= nf)
    def _():
        ob_ref[...] = z.astype(BF16)


def norm_proj(x, g, w, l, *, tm):
    s, d = x.shape
    tn = Z_TILE
    nf, nb = Z_F32_COLS // tn, Z_B16_COLS // tn
    return pl.pallas_call(
        _norm_proj_kernel,
        grid=(s // tm, nf + nb),
        in_specs=[
            pl.BlockSpec((tm, d), lambda i, j: (i, 0)),
            pl.BlockSpec((1, d), lambda i, j: (0, 0)),
            pl.BlockSpec((None, tn, d), lambda i, j: (l, j, 0)),
        ],
        out_specs=[
            pl.BlockSpec((tm, tn), lambda i, j: (i, jnp.minimum(j, nf - 1))),
            pl.BlockSpec((tm, tn), lambda i, j: (i, jnp.maximum(j - nf, 0))),
        ],
        out_shape=[jax.ShapeDtypeStruct((s, Z_F32_COLS), F32),
                   jax.ShapeDtypeStruct((s, Z_B16_COLS), BF16)],
        scratch_shapes=[pltpu.VMEM((tm, d), BF16)],
        compiler_params=_cparams(("parallel", "arbitrary")),
        name="norm_proj",
    )(x, g.reshape(1, d), w)


def _ffn_up_kernel(x_ref, g_ref, wg_ref, wu_ref, o_ref, h_ref):
    @pl.when(pl.program_id(1) == 0)
    def _():
        h_ref[...] = _rms_rows(x_ref[...], g_ref[...]).astype(BF16)

    h = h_ref[...]
    a = _dot(h, wg_ref[...].astype(BF16))
    u = _dot(h, wu_ref[...].astype(BF16))
    o_ref[...] = (a * jax.nn.sigmoid(a) * u).astype(o_ref.dtype)


def ffn_up(x, g, wg, wu, l, *, tm, tn):
    s, d = x.shape
    f = wg.shape[2]
    return pl.pallas_call(
        _ffn_up_kernel,
        grid=(s // tm, f // tn),
        in_specs=[
            pl.BlockSpec((tm, d), lambda i, j: (i, 0)),
            pl.BlockSpec((1, d), lambda i, j: (0, 0)),
            pl.BlockSpec((None, d, tn), lambda i, j: (l, 0, j)),
            pl.BlockSpec((None, d, tn), lambda i, j: (l, 0, j)),
        ],
        out_specs=pl.BlockSpec((tm, tn), lambda i, j: (i, j)),
        out_shape=jax.ShapeDtypeStruct((s, f), BF16),
        scratch_shapes=[pltpu.VMEM((tm, d), BF16)],
        compiler_params=_cparams(("parallel", "arbitrary")),
        name="ffn_up",
    )(x, g.reshape(1, d), wg, wu)


def _ffn_down_kernel(x_ref, a_ref, w_ref, o_ref):
    o_ref[...] = x_ref[...] + 0.5 * _dot(a_ref[...], w_ref[...])


def ffn_down(x, act, wd, l, *, tm, tn):
    s, d = x.shape
    f = act.shape[1]
    return pl.pallas_call(
        _ffn_down_kernel,
        grid=(s // tm, d // tn),
        in_specs=[
            pl.BlockSpec((tm, tn), lambda i, j: (i, j)),
            pl.BlockSpec((tm, f), lambda i, j: (i, 0)),
            pl.BlockSpec((None, f, tn), lambda i, j: (l, 0, j)),
        ],
        out_specs=pl.BlockSpec((tm, tn), lambda i, j: (i, j)),
        out_shape=jax.ShapeDtypeStruct((s, d), F32),
        compiler_params=_cparams(("parallel", "arbitrary")),
        name="ffn_down",
    )(x, act, wd)


def _out_proj_kernel(x_ref, a_ref, b_ref, c_ref, wa_ref, wb_ref, wc_ref, o_ref):
    acc = _dot(a_ref[...], wa_ref[...])
    acc += _dot(b_ref[...], wb_ref[...])
    acc += _dot(c_ref[...], wc_ref[...])
    o_ref[...] = x_ref[...] + acc


def out_proj(x, y_sb, y_mla, y_rwkv, w_out, l, *, tm):
    s, d = x.shape
    assert SB_WIDTH == MLA_WIDTH and RWKV_WIDTH == SB_WIDTH + MLA_WIDTH
    wspec = lambda rows, blk: pl.BlockSpec((None, rows, d), lambda i: (l, blk, 0),
                                           pipeline_mode=pl.Buffered(1))
    return pl.pallas_call(
        _out_proj_kernel,
        grid=(s // tm,),
        in_specs=[
            pl.BlockSpec((tm, d), lambda i: (i, 0)),
            pl.BlockSpec((tm, SB_WIDTH), lambda i: (i, 0)),
            pl.BlockSpec((tm, MLA_WIDTH), lambda i: (i, 0)),
            pl.BlockSpec((tm, RWKV_WIDTH), lambda i: (i, 0)),
            wspec(SB_WIDTH, 0),
            wspec(MLA_WIDTH, 1),
            wspec(RWKV_WIDTH, 1),
        ],
        out_specs=pl.BlockSpec((tm, d), lambda i: (i, 0)),
        out_shape=jax.ShapeDtypeStruct((s, d), F32),
        compiler_params=_cparams(("parallel",)),
        name="out_proj",
    )(x, y_sb, y_mla, y_rwkv, w_out, w_out, w_out)


def _final_norm_kernel(x_ref, g_ref, o_ref):
    o_ref[...] = _rms_rows(x_ref[...], g_ref[...])


def final_rmsnorm(x, g, *, tm):
    s, d = x.shape
    return pl.pallas_call(
        _final_norm_kernel,
        grid=(s // tm,),
        in_specs=[pl.BlockSpec((tm, d), lambda i: (i, 0)), pl.BlockSpec((1, d), lambda i: (0, 0))],
        out_specs=pl.BlockSpec((tm, d), lambda i: (i, 0)),
        out_shape=jax.ShapeDtypeStruct((s, d), F32),
        compiler_params=_cparams(("parallel",)),
        name="final_norm",
    )(x, g.reshape(1, d))


SB_BQ = 256
SB_BK = 128
SB_DEAD_LOG = -104.0


def _sb_attn_kernel(q_ref, k_ref, v_ref, g_ref, o_ref):
    i = pl.program_id(0)
    bq, bk, dh = SB_BQ, SB_BK, SB_HEAD_DIM
    heads = range(SB_HEADS)
    hs = [slice(h * dh, (h + 1) * dh) for h in heads]
    q = [q_ref[:, hs[h]] for h in heads]
    row = lax.broadcasted_iota(jnp.int32, (bq, bk), 0)
    col = lax.broadcasted_iota(jnp.int32, (bq, bk), 1)
    kr = lax.broadcasted_iota(jnp.int32, (bk, bk), 0)
    kc = lax.broadcasted_iota(jnp.int32, (bk, bk), 1)
    upper = jnp.where(kr > kc, 1.0, 0.0).astype(BF16)

    def block(kb, carry, acc, masked):
        start = pl.multiple_of(kb * bk, bk)
        z = [_dot_nt(q[h], k_ref[pl.ds(start, bk), hs[h]]) for h in heads]
        log_beta = [jnp.minimum(z[h], 0.0) - jnp.log(1.0 + jnp.exp(-jnp.abs(z[h]))) for h in heads]
        log_keep = [log_beta[h] - z[h] for h in heads]
        if masked:
            before = (start + col) < (i * bq + row)
            log_keep = [jnp.where(before, log_keep[h], 0.0) for h in heads]
        split = [_split2(log_keep[h]) for h in heads]
        within = [_dot(split[h][0], upper) + _dot(split[h][1], upper) for h in heads]
        a = [jnp.exp(log_beta[h] + (carry[h] + within[h])) for h in heads]
        if masked:
            a = [jnp.where(before, a[h], 0.0) for h in heads]
        acc = [acc[h] + _dot(a[h].astype(BF16), v_ref[pl.ds(start, bk), hs[h]]) for h in heads]
        carry = [carry[h] + jnp.sum(log_keep[h], axis=-1, keepdims=True) for h in heads]
        return carry, acc

    def alive_of(carry):
        m = jnp.max(carry[0])
        for h in heads[1:]:
            m = jnp.maximum(m, jnp.max(carry[h]))
        return m

    carry = [jnp.zeros((bq, 1), F32) for _ in heads]
    acc = [jnp.zeros((bq, dh), F32) for _ in heads]
    nd = bq // bk
    first = i * nd
    for d in range(nd - 1, -1, -1):
        carry, acc = block(first + d, carry, acc, True)

    def cond(st):
        return jnp.logical_and(st[0] >= 0, st[1] > SB_DEAD_LOG)

    def body(st):
        kb, _, carry, acc = st
        carry, acc = block(kb, list(carry), list(acc), False)
        return kb - 1, alive_of(carry), tuple(carry), tuple(acc)

    st = lax.while_loop(cond, body, (first - 1, alive_of(carry), tuple(carry), tuple(acc)))
    acc = st[3]
    for h in heads:
        o_ref[:, hs[h]] = _rms_rows(acc[h], g_ref[h:h + 1, :]).astype(o_ref.dtype)


def sb_attention(zb, gains):
    s = zb.shape[0]
    assert s % SB_BQ == 0 and SB_BQ % SB_BK == 0
    whole = lambda c: pl.BlockSpec((s, SB_WIDTH), lambda i: (0, c // SB_WIDTH),
                                   pipeline_mode=pl.Buffered(1))
    return pl.pallas_call(
        _sb_attn_kernel,
        grid=(s // SB_BQ,),
        in_specs=[
            pl.BlockSpec((SB_BQ, SB_WIDTH), lambda i: (i, COL_SBQ // SB_WIDTH)),
            whole(COL_SBK),
            whole(COL_SBV),
            pl.BlockSpec((SB_HEADS, SB_HEAD_DIM), lambda i: (0, 0)),
        ],
        out_specs=pl.BlockSpec((SB_BQ, SB_WIDTH), lambda i: (i, 0)),
        out_shape=jax.ShapeDtypeStruct((s, SB_WIDTH), BF16),
        compiler_params=_cparams(("arbitrary",)),
        name="sb_attention",
    )(zb, zb, zb, gains)


def _mla_prep_kernel(cq_ref, ckv_ref, rope_ref, tab_ref, qn_ref, kvn_ref, wq_ref, wkv_ref,
                     q_ref, k_ref, v_ref):
    scale = (MLA_NOPE_DIM + MLA_ROPE_DIM) ** -0.5
    hq = _rms_rows(cq_ref[...], qn_ref[...]).astype(BF16)
    hkv = _rms_rows(ckv_ref[...], kvn_ref[...]).astype(BF16)
    cos2 = tab_ref[:, :128]
    sin2 = tab_ref[:, 128:]
    kr = rope_ref[...]
    k_pe = kr * cos2 + pltpu.roll(kr, 64, 1) * sin2
    k_pe = k_pe.astype(BF16)
    qall = _dot(hq, wq_ref[...])
    kvall = _dot(hkv, wkv_ref[...])
    for h in range(MLA_HEADS):
        qh = qall[:, h * 384:(h + 1) * 384]
        q_ref[h, :, :128] = (qh[:, :128] * scale).astype(BF16)
        q_ref[h, :, 128:] = ((qh[:, 128:256] * cos2 + qh[:, 256:384] * sin2) * scale).astype(BF16)
        k_ref[h, :, :128] = kvall[:, h * 256:h * 256 + 128].astype(BF16)
        k_ref[h, :, 128:] = k_pe
        v_ref[h, :, :128] = kvall[:, h * 256 + 128:(h + 1) * 256].astype(BF16)
        v_ref[h, :, 128:] = jnp.ones((kr.shape[0], 128), BF16)


def mla_prep(z, tab, q_norm, kv_norm, wq, wkv, *, ts):
    s = z.shape[0]
    return pl.pallas_call(
        _mla_prep_kernel,
        grid=(s // ts,),
        in_specs=[
            pl.BlockSpec((ts, MLA_Q_LORA), lambda i: (i, COL_CQ // MLA_Q_LORA)),
            pl.BlockSpec((ts, MLA_KV_LORA), lambda i: (i, COL_CKV // MLA_KV_LORA)),
            pl.BlockSpec((ts, 128), lambda i: (i, COL_ROPE // 128)),
            pl.BlockSpec((ts, 256), lambda i: (i, 0)),
            pl.BlockSpec((1, MLA_Q_LORA), lambda i: (0, 0)),
            pl.BlockSpec((1, MLA_KV_LORA), lambda i: (0, 0)),
            pl.BlockSpec(wq.shape, lambda i: (0, 0)),
            pl.BlockSpec(wkv.shape, lambda i: (0, 0)),
        ],
        out_specs=[
            pl.BlockSpec((MLA_HEADS, ts, MLA_QK_PAD), lambda i: (0, i, 0)),
            pl.BlockSpec((MLA_HEADS, ts, MLA_QK_PAD), lambda i: (0, i, 0)),
            pl.BlockSpec((MLA_HEADS, ts, 2 * MLA_V_DIM), lambda i: (0, i, 0)),
        ],
        out_shape=[
            jax.ShapeDtypeStruct((MLA_HEADS, s, MLA_QK_PAD), BF16),
            jax.ShapeDtypeStruct((MLA_HEADS, s, MLA_QK_PAD), BF16),
            jax.ShapeDtypeStruct((MLA_HEADS, s, 2 * MLA_V_DIM), BF16),
        ],
        compiler_params=_cparams(("parallel",)),
        name="mla_prep",
    )(z, z, z, tab, q_norm.reshape(1, -1), kv_norm.reshape(1, -1), wq, wkv)


MLA_BQ = 256
MLA_BK = 1024
MLA_HEADS_PER_STEP = 4


def _mla_attn_kernel(q_ref, k_ref, v_ref, g_ref, o_ref):
    i = pl.program_id(1)
    bq, bk = MLA_BQ, MLA_BK
    heads = range(MLA_HEADS_PER_STEP)
    q = [q_ref[h] for h in heads]
    nd = bk // bq

    def block(kb, m, acc, width, masked):
        start = pl.multiple_of(kb * bk, bk)
        sc = [_dot_nt(q[h], k_ref[h, pl.ds(start, width), :]) for h in heads]
        if masked:
            row = lax.broadcasted_iota(jnp.int32, (bq, width), 0) + i * bq
            col = lax.broadcasted_iota(jnp.int32, (bq, width), 1) + start
            sc = [jnp.where(col <= row, sc[h], -1e30) for h in heads]
        m_new = [jnp.maximum(m[h], jnp.max(sc[h], axis=-1, keepdims=True)) for h in heads]
        for h in heads:
            alpha = jnp.exp(m[h] - m_new[h])
            p = jnp.exp(sc[h] - m_new[h])
            acc[h] = acc[h] * alpha + _dot(p.astype(BF16), v_ref[h, pl.ds(start, width), :])
        return m_new, acc

    def body(kb, st):
        m, acc = block(kb, list(st[0]), list(st[1]), bk, False)
        return tuple(m), tuple(acc)

    m0 = tuple(jnp.full((bq, 1), -1e30, F32) for _ in heads)
    a0 = tuple(jnp.zeros((bq, 2 * MLA_V_DIM), F32) for _ in heads)
    last = i // nd
    m, acc = lax.fori_loop(0, last, body, (m0, a0))

    for r in range(nd):
        @pl.when(i % nd == r)
        def _(r=r):
            _, acc_r = block(last, list(m), list(acc), (r + 1) * bq, True)
            for h in heads:
                y = acc_r[h][:, :MLA_V_DIM] / acc_r[h][:, MLA_V_DIM:]
                o_ref[:, h * MLA_V_DIM:(h + 1) * MLA_V_DIM] = _rms_rows(
                    y, g_ref[h:h + 1, :]).astype(o_ref.dtype)


def mla_attention(q, k, v, gains):
    s = q.shape[1]
    assert s % MLA_BK == 0 and MLA_BK % MLA_BQ == 0
    hps = MLA_HEADS_PER_STEP
    whole = lambda n: pl.BlockSpec((hps, s, n), lambda g, i: (g, 0, 0), pipeline_mode=pl.Buffered(1))
    return pl.pallas_call(
        _mla_attn_kernel,
        grid=(MLA_HEADS // hps, s // MLA_BQ),
        in_specs=[
            pl.BlockSpec((hps, MLA_BQ, MLA_QK_PAD), lambda g, i: (g, i, 0)),
            whole(MLA_QK_PAD),
            whole(2 * MLA_V_DIM),
            pl.BlockSpec((hps, MLA_V_DIM), lambda g, i: (g, 0)),
        ],
        out_specs=pl.BlockSpec((MLA_BQ, hps * MLA_V_DIM), lambda g, i: (i, g)),
        out_shape=jax.ShapeDtypeStruct((s, MLA_WIDTH), BF16),
        compiler_params=_cparams(("parallel", "arbitrary")),
        name="mla_attention",
    )(q, k, v, gains)


def _head_sum_matrix():
    r = lax.broadcasted_iota(jnp.int32, (128, 128), 0) // RWKV_HEAD_DIM
    c = lax.broadcasted_iota(jnp.int32, (128, 128), 1) // RWKV_HEAD_DIM
    return jnp.where(r == c, 1.0, 0.0).astype(BF16)


def _head_sums(x, ones_bd):
    parts = []
    for j in range(x.shape[1] // 128):
        hi, lo = _split2(x[:, j * 128:(j + 1) * 128])
        parts.append(_dot(hi, ones_bd) + _dot(lo, ones_bd))
    return jnp.concatenate(parts, axis=-1)


def _rwkv_prep_kernel(r_ref, k_ref, v_ref, lora_ref, xg_ref, mu_ref, mul_ref, mug_ref,
                      w0_ref, w2_ref, a0_ref, a2_ref, g2_ref, kk_ref, ka_ref,
                      ro_ref, lw_ref, ko_ref, vo_ref, kko_ref, bo_ref, go_ref,
                      prev_ref, prevl_ref):
    ts = r_ref.shape[0]

    @pl.when(pl.program_id(0) == 0)
    def _():
        prev_ref[...] = jnp.zeros_like(prev_ref)
        prevl_ref[...] = jnp.zeros_like(prevl_ref)

    first_row = lax.broadcasted_iota(jnp.int32, (ts, 1), 0) == 0

    def shift_mix(x, prev_row, mu):
        x_prev = jnp.where(first_row, prev_row, pltpu.roll(x, 1, 0))
        return x + (x_prev - x) * mu

    r_in, k_in, v_in = r_ref[...], k_ref[...], v_ref[...]
    lora_in, xg_in = lora_ref[...], xg_ref[...]
    r = shift_mix(r_in, prev_ref[0:1, :], mu_ref[0:1, :])
    k = shift_mix(k_in, prev_ref[1:2, :], mu_ref[1:2, :])
    v = shift_mix(v_in, prev_ref[2:3, :], mu_ref[2:3, :])
    lora = shift_mix(lora_in, prevl_ref[0:1, :], mul_ref[...])
    xg = shift_mix(xg_in, prevl_ref[1:2, :], mug_ref[...])
    prev_ref[0:1, :] = r_in[ts - 1:ts, :]
    prev_ref[1:2, :] = k_in[ts - 1:ts, :]
    prev_ref[2:3, :] = v_in[ts - 1:ts, :]
    prevl_ref[0:1, :] = lora_in[ts - 1:ts, :]
    prevl_ref[1:2, :] = xg_in[ts - 1:ts, :]

    dw = w0_ref[...] + _dot(jnp.tanh(lora).astype(BF16), w2_ref[...])
    log_w = jnp.minimum(dw, 0.0) - jnp.log(1.0 + jnp.exp(-jnp.abs(dw))) - 0.5
    lw_ref[...] = -jnp.exp(log_w)
    a = jax.nn.sigmoid(a0_ref[...] + _dot(lora.astype(BF16), a2_ref[...]))
    go_ref[...] = _dot(jax.nn.sigmoid(xg).astype(BF16), g2_ref[...])

    kk = k * kk_ref[...]
    ss = _head_sums(kk * kk, _head_sum_matrix())
    kk = kk / jnp.maximum(jnp.sqrt(ss), 1e-12)
    ro_ref[...] = r
    vo_ref[...] = v
    ko_ref[...] = k * (1.0 + (a - 1.0) * ka_ref[...])
    kko_ref[...] = kk
    bo_ref[...] = kk * a


def _rwkv_chunk_kernel(r_ref, lw_ref, k_ref, v_ref, kk_ref, b_ref, g_ref, rk_ref, lnw_ref, lnb_ref,
                       o_ref, st_ref, y_ref):
    c = RWKV_CHUNK
    n = RWKV_HEAD_DIM
    nh = r_ref.shape[1] // n
    chunks = range(r_ref.shape[0] // c)

    @pl.when(pl.program_id(0) == 0)
    def _():
        st_ref[...] = jnp.zeros_like(st_ref)

    ri = lax.broadcasted_iota(jnp.int32, (c, c), 0)
    ci = lax.broadcasted_iota(jnp.int32, (c, c), 1)
    tril_ones = jnp.where(ri >= ci, 1.0, 0.0).astype(BF16)

    gh = RWKV_GROUP
    gw = gh * n
    groups = range(nh // gh)
    gs = [slice(g * gw, (g + 1) * gw) for g in groups]
    same_head = (lax.broadcasted_iota(jnp.int32, (gw, gw), 0) // n
                 == lax.broadcasted_iota(jnp.int32, (gw, gw), 1) // n)
    zero_b = jnp.zeros((gw, gw), BF16)

    def bd(y):
        yb = y.astype(BF16)
        return jnp.where(same_head, jnp.concatenate([yb] * gh, axis=0), zero_b)

    def mmb(x, y_bd):
        return _dot(x.astype(BF16), y_bd)

    ri2 = lax.broadcasted_iota(jnp.int32, (2 * c, gw), 0)
    ci2 = lax.broadcasted_iota(jnp.int32, (2 * c, gw), 1) % c
    causal2 = ci2 <= jnp.where(ri2 < c, ri2 - 1, ri2 - c)
    rc = lax.broadcasted_iota(jnp.int32, (c, gw), 0)
    cc = lax.broadcasted_iota(jnp.int32, (c, gw), 1) % c
    eye_g = jnp.where(rc == cc, 1.0, 0.0).astype(F32)
    d16_g = (rc // 16) == (cc // 16)
    d32_g = (rc // 32) == (cc // 32)
    lo_g = jnp.logical_and(d32_g, jnp.logical_not(d16_g))
    colhead = lax.broadcasted_iota(jnp.int32, (n, gw), 1) // n

    p_end, v_c, lhs, b_t, k_t, dec = [], [], [], [], [], []
    for ch in chunks:
        rows = slice(ch * c, (ch + 1) * c)
        lw = lw_ref[rows, :]
        cum = _exact_lhs_dot(tril_ones, lw)
        p_incl = jnp.exp(cum)
        p_inv = jnp.exp(-cum)
        pe = p_incl[c - 1:c, :]
        bt = b_ref[rows, :] * p_inv
        kt = k_ref[rows, :] * p_inv
        p_end.append(pe)
        v_c.append(v_ref[rows, :])
        b_t.append(bt)
        k_t.append(kt)
        lhs.append(jnp.concatenate([kk_ref[rows, :] * jnp.exp(cum - lw), r_ref[rows, :] * p_incl],
                                   axis=0).astype(BF16))
        dec.append(jnp.concatenate([bt * pe, kt * pe], axis=0).astype(BF16))

    cg = [(ch, g) for ch in chunks for g in groups]
    ab = {q: jnp.where(causal2, _dot_nt(lhs[q[0]][:, gs[q[1]]], bd(b_t[q[0]][:, gs[q[1]]])), 0.0)
          for q in cg}
    ak = {q: jnp.where(causal2, _dot_nt(lhs[q[0]][:, gs[q[1]]], bd(k_t[q[0]][:, gs[q[1]]])), 0.0)
          for q in cg}
    akv = {q: mmb(ak[q], bd(v_c[q[0]][:, gs[q[1]]])) for q in cg}
    a_ub = {q: ab[q][:c] for q in cg}
    ld = {q: jnp.where(d16_g, a_ub[q], 0.0) for q in cg}
    x = {q: eye_g - ld[q] for q in cg}
    pw = {q: mmb(ld[q], bd(ld[q])) for q in cg}
    x = {q: x[q] + mmb(x[q], bd(pw[q])) for q in cg}
    pw = {q: mmb(pw[q], bd(pw[q])) for q in cg}
    x = {q: x[q] + mmb(x[q], bd(pw[q])) for q in cg}
    pw = {q: mmb(pw[q], bd(pw[q])) for q in cg}
    x = {q: x[q] + mmb(x[q], bd(pw[q])) for q in cg}
    t = {q: mmb(x[q], bd(jnp.where(lo_g, a_ub[q], 0.0))) for q in cg}
    x = {q: x[q] - mmb(t[q], bd(x[q])) for q in cg}
    t = {q: mmb(x[q], bd(jnp.where(d32_g, 0.0, a_ub[q]))) for q in cg}
    x = {q: x[q] - mmb(t[q], bd(x[q])) for q in cg}
    u0 = {q: -mmb(x[q], bd(akv[q][:c])) for q in cg}
    wk = {q: -mmb(x[q], bd(lhs[q[0]][:c, gs[q[1]]])) for q in cg}
    lhs2 = {q: jnp.concatenate([wk[q].astype(BF16), lhs[q[0]][c:, gs[q[1]]]], axis=0) for q in cg}

    st = [st_ref[:, gs[g]] for g in groups]
    for ch in chunks:
        rows = slice(ch * c, (ch + 1) * c)
        through = [_dot_nt(lhs2[(ch, g)], bd(st[g])) for g in groups]
        u = [through[g][:c] + u0[(ch, g)] for g in groups]
        for g in groups:
            y_ref[rows, gs[g]] = through[g][c:] + akv[(ch, g)][c:] + mmb(ab[(ch, g)][c:], bd(u[g]))
        for g in groups:
            uv = jnp.concatenate([u[g], v_c[ch][:, gs[g]]], axis=0).astype(BF16)
            cross = _dot_tn(uv, dec[ch][:, gs[g]])
            upd = st[g] * p_end[ch][:, gs[g]]
            for h in range(gh):
                upd = upd + jnp.where(colhead == h, cross[h * n:(h + 1) * n], 0.0)
            st[g] = upd
    for g in groups:
        st_ref[:, gs[g]] = st[g]

    ones_bd = _head_sum_matrix()
    y = y_ref[...]
    v = v_ref[...]
    mean = _head_sums(y, ones_bd) * (1.0 / n)
    yc = y - mean
    var = _head_sums(yc * yc, ones_bd) * (1.0 / n)
    yn = yc * lax.rsqrt(var + RWKV_GN_EPS) * lnw_ref[...] + lnb_ref[...]
    bonus = _head_sums(r_ref[...] * k_ref[...] * rk_ref[...], ones_bd)
    o_ref[...] = ((yn + bonus * v) * g_ref[...]).astype(o_ref.dtype)


def _rwkv_kernel(zr_ref, zk_ref, zv_ref, zl_ref, zg_ref, mu_ref, mul_ref, mug_ref, w0_ref, w2_ref,
                 a0_ref, a2_ref, g2_ref, kk_ref, ka_ref, rk_ref, lnw_ref, lnb_ref,
                 o_ref, st_ref, y_ref, prev_ref, prevl_ref, *staged):
    _rwkv_prep_kernel(zr_ref, zk_ref, zv_ref, zl_ref, zg_ref, mu_ref, mul_ref, mug_ref, w0_ref, w2_ref,
                      a0_ref, a2_ref, g2_ref, kk_ref, ka_ref, *staged, prev_ref, prevl_ref)
    _rwkv_chunk_kernel(*staged, rk_ref, lnw_ref, lnb_ref, o_ref, st_ref, y_ref)


def rwkv_mix(z, p):
    s = z.shape[0]
    w = RWKV_WIDTH
    rows = RWKV_CHUNK * RWKV_CHUNKS_PER_STEP
    assert s % rows == 0
    row = lambda n: pl.BlockSpec((1, n), lambda i: (0, 0))
    lora_w = pl.BlockSpec((128, w), lambda i: (0, 0))
    return pl.pallas_call(
        _rwkv_kernel,
        grid=(s // rows,),
        in_specs=[
            pl.BlockSpec((rows, w), lambda i: (i, COL_R // w)),
            pl.BlockSpec((rows, w), lambda i: (i, COL_K // w)),
            pl.BlockSpec((rows, w), lambda i: (i, COL_V // w)),
            pl.BlockSpec((rows, 128), lambda i: (i, COL_LORA // 128)),
            pl.BlockSpec((rows, 128), lambda i: (i, COL_XG // 128)),
            pl.BlockSpec((3, w), lambda i: (0, 0)),
            row(128), row(128),
            row(w), lora_w,
            row(w), lora_w,
            lora_w,
            row(w), row(w),
            row(w), row(w), row(w),
        ],
        out_specs=pl.BlockSpec((rows, w), lambda i: (i, 0)),
        out_shape=jax.ShapeDtypeStruct((s, w), BF16),
        scratch_shapes=[
            pltpu.VMEM((RWKV_HEAD_DIM, w), F32),
            pltpu.VMEM((rows, w), F32),
            pltpu.VMEM((8, w), F32), pltpu.VMEM((8, 128), F32),
        ] + [pltpu.VMEM((rows, w), F32)] * 7,
        compiler_params=_cparams(("arbitrary",)),
        name="rwkv_mix",
    )(z, z, z, z, z, p["mu_rkv"], p["mu_lora"], p["mu_g"], p["w0"], p["w2"], p["a0"], p["a2"],
      p["g2"], p["k_k"], p["k_a"], p["r_k"], p["ln_w"], p["ln_b"])


def _swap_halves(w):
    half = w.shape[-1] // 2
    return jnp.concatenate([w[..., half:], w[..., :half]], axis=-1)


def _prep_w_in(w_in):
    wt = jnp.swapaxes(w_in, 1, 2)
    sizes = [SB_WIDTH] * 3 + [MLA_Q_LORA, MLA_KV_LORA, MLA_ROPE_DIM] + [RWKV_WIDTH] * 3 + [
        RWKV_DECAY_LORA, RWKV_A_LORA, RWKV_GATE_LORA]
    idx = [int(i) for i in np.cumsum(sizes)[:-1]]
    sbq, sbk, sbv, cq, ckv, krope, r, k, v, xw, xa, xg = jnp.split(wt, idx, axis=1)
    sbq = sbq * (SB_HEAD_DIM ** -0.5)
    half = MLA_ROPE_DIM // 2
    krope_sw = jnp.concatenate([krope[:, half:], krope[:, :half]], axis=1)
    rows = [r, k, v, cq, ckv, xw, xa, xg, krope, krope_sw]
    used = sum(t.shape[1] for t in rows)
    rows.append(jnp.zeros((wt.shape[0], Z_F32_COLS - used, wt.shape[2]), wt.dtype))
    rows += [sbq, sbk, sbv]
    return jnp.concatenate(rows, axis=1).astype(BF16)


def _prep_mla_w(w_uq, w_ukv):
    q = w_uq.reshape(MLA_Q_LORA, MLA_HEADS, MLA_NOPE_DIM + MLA_ROPE_DIM)
    nope, pe = q[..., :MLA_NOPE_DIM], q[..., MLA_NOPE_DIM:]
    zpad = jnp.zeros_like(pe)
    wq = jnp.concatenate([nope, pe, zpad, _swap_halves(pe), zpad], axis=-1)
    return wq.reshape(MLA_Q_LORA, MLA_HEADS * 384).astype(BF16), w_ukv.astype(BF16)


def _rope_table(positions):
    half = MLA_ROPE_DIM // 2
    inv_freq = ROPE_THETA ** (-jnp.arange(half, dtype=F32) / half)
    ang = positions.astype(F32)[:, None] * inv_freq
    cos, sin = jnp.cos(ang), jnp.sin(ang)
    z = jnp.zeros((positions.shape[0], MLA_ROPE_DIM), F32)
    return jnp.concatenate([cos, cos, z, -sin, sin, z], axis=-1)


def _pick_tile(n, candidates):
    for c in candidates:
        if n % c == 0:
            return c
    raise ValueError(f"no tile for {n}")


def kernel(x, positions, ffn1_norm, ffn1_gate, ffn1_up, ffn1_down, mix_norm, w_in, mla_q_norm,
           mla_w_uq, mla_kv_norm, mla_w_ukv, rwkv_mu, rwkv_w0, rwkv_w2, rwkv_a0, rwkv_a2, rwkv_g2,
           rwkv_k_k, rwkv_k_a, rwkv_r_k, rwkv_ln_w, rwkv_ln_b, sb_out_norm, mla_out_norm, w_out,
           ffn2_norm, ffn2_gate, ffn2_up, ffn2_down, final_norm):
    bsz, s, d = x.shape
    assert bsz == 1 and d == SB_WIDTH + MLA_WIDTH + RWKV_WIDTH
    depth = w_in.shape[0]
    d_ff = ffn1_gate.shape[-1]
    tm = _pick_tile(s, (1024, 512, 256, 128))
    tf = _pick_tile(d_ff, (512, 256, 128))
    x = x[0]
    tab = _rope_table(positions[0])
    w = RWKV_WIDTH
    zl = jnp.zeros((RWKV_DECAY_LORA, w), F32)
    ffn_w = [(wg, wu, wd.astype(BF16))
             for wg, wu, wd in ((ffn1_gate, ffn1_up, ffn1_down), (ffn2_gate, ffn2_up, ffn2_down))]
    w_in_b = _prep_w_in(w_in)
    w_out_b = w_out.astype(BF16)

    def ffn(x, g, ws, l):
        act = ffn_up(x, g, ws[0], ws[1], l, tm=tm, tn=tf)
        return ffn_down(x, act, ws[2], l, tm=tm, tn=512)

    for l in range(depth):
        x = ffn(x, ffn1_norm[l], ffn_w[0], l)

        z, zb = norm_proj(x, mix_norm[l], w_in_b, l, tm=tm)

        y_sb = sb_attention(zb, sb_out_norm[l])

        wq, wkv = _prep_mla_w(mla_w_uq[l], mla_w_ukv[l])
        q, k, v = mla_prep(z, tab, mla_q_norm[l], mla_kv_norm[l], wq, wkv, ts=_pick_tile(s, (512, 256, 128)))
        y_mla = mla_attention(q, k, v, mla_out_norm[l])

        mu = rwkv_mu[l]
        p = {
            "mu_rkv": mu[:3 * w].reshape(3, w),
            "mu_lora": mu[3 * w:3 * w + 128].reshape(1, 128),
            "mu_g": mu[3 * w + 128:].reshape(1, 128),
            "w0": rwkv_w0[l].reshape(1, w),
            "w2": jnp.concatenate([rwkv_w2[l], zl], axis=0).astype(BF16),
            "a0": rwkv_a0[l].reshape(1, w),
            "a2": jnp.concatenate([zl, rwkv_a2[l]], axis=0).astype(BF16),
            "g2": rwkv_g2[l].astype(BF16),
            "k_k": rwkv_k_k[l].reshape(1, w),
            "k_a": rwkv_k_a[l].reshape(1, w),
            "r_k": rwkv_r_k[l].reshape(1, w),
            "ln_w": rwkv_ln_w[l].reshape(1, w),
            "ln_b": rwkv_ln_b[l].reshape(1, w),
        }
        y_rwkv = rwkv_mix(z, p)

        x = out_proj(x, y_sb, y_mla, y_rwkv, w_out_b, l, tm=_pick_tile(s, (512, 256, 128)))

        x = ffn(x, ffn2_norm[l], ffn_w[1], l)

    return final_rmsnorm(x, final_norm, tm=_pick_tile(s, (256, 128)))[None]
```

```python
import jax
import jax.numpy as jnp
import numpy as np
from jax import lax
from jax.experimental import pallas as pl
from jax.experimental.pallas import tpu as pltpu

F32 = jnp.float32
BF16 = jnp.bfloat16

NORM_EPS = 1e-6
SB_HEADS = 4
SB_HEAD_DIM = 128
SB_WIDTH = SB_HEADS * SB_HEAD_DIM
MLA_HEADS = 4
MLA_NOPE_DIM = 128
MLA_ROPE_DIM = 64
MLA_V_DIM = 128
MLA_Q_LORA = 512
MLA_KV_LORA = 256
MLA_WIDTH = MLA_HEADS * MLA_V_DIM
MLA_QK_PAD = 256
ROPE_THETA = 10000.0
RWKV_HEADS = 16
RWKV_HEAD_DIM = 64
RWKV_WIDTH = RWKV_HEADS * RWKV_HEAD_DIM
RWKV_DECAY_LORA = 64
RWKV_A_LORA = 64
RWKV_GATE_LORA = 128
RWKV_GN_EPS = 64e-5
RWKV_CHUNK = 64
RWKV_GROUP = 4
RWKV_CHUNKS_PER_STEP = 4

COL_R, COL_K, COL_V = 0, 1024, 2048
COL_CQ, COL_CKV = 3072, 3584
COL_LORA, COL_XG, COL_ROPE = 3840, 3968, 4096
Z_F32_COLS = 4608
COL_SBQ, COL_SBK, COL_SBV = 0, 512, 1024
Z_B16_COLS = 1536
Z_TILE = 768

VMEM_LIMIT = 48 * 1024 * 1024


def _cparams(sem):
    return pltpu.CompilerParams(dimension_semantics=sem, vmem_limit_bytes=VMEM_LIMIT)


def _dot(a, b):
    return jnp.dot(a, b, preferred_element_type=F32)


def _dot_nt(a, b):
    return lax.dot_general(a, b, (((1,), (1,)), ((), ())), preferred_element_type=F32)


def _dot_tn(a, b):
    return lax.dot_general(a, b, (((0,), (0,)), ((), ())), preferred_element_type=F32)


def _split3(x):
    hi = x.astype(BF16)
    r1 = x - hi.astype(F32)
    mid = r1.astype(BF16)
    lo = (r1 - mid.astype(F32)).astype(BF16)
    return hi, mid, lo


def _split2(x):
    hi = x.astype(BF16)
    lo = (x - hi.astype(F32)).astype(BF16)
    return hi, lo


def _exact_lhs_dot(m_bf16, x):
    hi, mid, lo = _split3(x)
    return _dot(m_bf16, hi) + (_dot(m_bf16, mid) + _dot(m_bf16, lo))


def _rms_rows(x, g):
    ms = jnp.mean(x * x, axis=-1, keepdims=True)
    return x * lax.rsqrt(ms + NORM_EPS) * g


def _norm_proj_kernel(x_ref, g_ref, w_ref, o_ref, ob_ref, h_ref):
    @pl.when(pl.program_id(1) == 0)
    def _():
        h_ref[...] = _rms_rows(x_ref[...], g_ref[...]).astype(BF16)

    z = _dot_nt(h_ref[...], w_ref[...])
    nf = Z_F32_COLS // Z_TILE

    @pl.when(pl.program_id(1) < nf)
    def _():
        o_ref[...] = z

    @pl.when(pl.program_id(1) >= nf)
    def _():
        ob_ref[...] = z.astype(BF16)


def norm_proj(x, g, w, l, *, tm):
    s, d = x.shape
    tn = Z_TILE
    nf, nb = Z_F32_COLS // tn, Z_B16_COLS // tn
    return pl.pallas_call(
        _norm_proj_kernel,
        grid=(s // tm, nf + nb),
        in_specs=[
            pl.BlockSpec((tm, d), lambda i, j: (i, 0)),
            pl.BlockSpec((1, d), lambda i, j: (0, 0)),
            pl.BlockSpec((None, tn, d), lambda i, j: (l, j, 0)),
        ],
        out_specs=[
            pl.BlockSpec((tm, tn), lambda i, j: (i, jnp.minimum(j, nf - 1))),
            pl.BlockSpec((tm, tn), lambda i, j: (i, jnp.maximum(j - nf, 0))),
        ],
        out_shape=[jax.ShapeDtypeStruct((s, Z_F32_COLS), F32),
                   jax.ShapeDtypeStruct((s, Z_B16_COLS), BF16)],
        scratch_shapes=[pltpu.VMEM((tm, d), BF16)],
        compiler_params=_cparams(("parallel", "arbitrary")),
        name="norm_proj",
    )(x, g.reshape(1, d), w)


def _ffn_up_kernel(x_ref, g_ref, wg_ref, wu_ref, o_ref, h_ref):
    @pl.when(pl.program_id(1) == 0)
    def _():
        h_ref[...] = _rms_rows(x_ref[...], g_ref[...]).astype(BF16)

    h = h_ref[...]
    a = _dot(h, wg_ref[...].astype(BF16))
    u = _dot(h, wu_ref[...].astype(BF16))
    o_ref[...] = (a * jax.nn.sigmoid(a) * u).astype(o_ref.dtype)


def ffn_up(x, g, wg, wu, l, *, tm, tn):
    s, d = x.shape
    f = wg.shape[2]
    return pl.pallas_call(
        _ffn_up_kernel,
        grid=(s // tm, f // tn),
        in_specs=[
            pl.BlockSpec((tm, d), lambda i, j: (i, 0)),
            pl.BlockSpec((1, d), lambda i, j: (0, 0)),
            pl.BlockSpec((None, d, tn), lambda i, j: (l, 0, j)),
            pl.BlockSpec((None, d, tn), lambda i, j: (l, 0, j)),
        ],
        out_specs=pl.BlockSpec((tm, tn), lambda i, j: (i, j)),
        out_shape=jax.ShapeDtypeStruct((s, f), BF16),
        scratch_shapes=[pltpu.VMEM((tm, d), BF16)],
        compiler_params=_cparams(("parallel", "arbitrary")),
        name="ffn_up",
    )(x, g.reshape(1, d), wg, wu)


def _ffn_down_kernel(x_ref, a_ref, w_ref, o_ref):
    o_ref[...] = x_ref[...] + 0.5 * _dot(a_ref[...], w_ref[...])


def ffn_down(x, act, wd, l, *, tm, tn):
    s, d = x.shape
    f = act.shape[1]
    return pl.pallas_call(
        _ffn_down_kernel,
        grid=(s // tm, d // tn),
        in_specs=[
            pl.BlockSpec((tm, tn), lambda i, j: (i, j)),
            pl.BlockSpec((tm, f), lambda i, j: (i, 0)),
            pl.BlockSpec((None, f, tn), lambda i, j: (l, 0, j)),
        ],
        out_specs=pl.BlockSpec((tm, tn), lambda i, j: (i, j)),
        out_shape=jax.ShapeDtypeStruct((s, d), F32),
        compiler_params=_cparams(("parallel", "arbitrary")),
        name="ffn_down",
    )(x, act, wd)


def _out_proj_kernel(x_ref, a_ref, b_ref, c_ref, wa_ref, wb_ref, wc_ref, o_ref):
    acc = _dot(a_ref[...], wa_ref[...])
    acc += _dot(b_ref[...], wb_ref[...])
    acc += _dot(c_ref[...], wc_ref[...])
    o_ref[...] = x_ref[...] + acc


def out_proj(x, y_sb, y_mla, y_rwkv, w_out, l, *, tm):
    s, d = x.shape
    assert SB_WIDTH == MLA_WIDTH and RWKV_WIDTH == SB_WIDTH + MLA_WIDTH
    wspec = lambda rows, blk: pl.BlockSpec((None, rows, d), lambda i: (l, blk, 0),
                                           pipeline_mode=pl.Buffered(1))
    return pl.pallas_call(
        _out_proj_kernel,
        grid=(s // tm,),
        in_specs=[
            pl.BlockSpec((tm, d), lambda i: (i, 0)),
            pl.BlockSpec((tm, SB_WIDTH), lambda i: (i, 0)),
            pl.BlockSpec((tm, MLA_WIDTH), lambda i: (i, 0)),
            pl.BlockSpec((tm, RWKV_WIDTH), lambda i: (i, 0)),
            wspec(SB_WIDTH, 0),
            wspec(MLA_WIDTH, 1),
            wspec(RWKV_WIDTH, 1),
        ],
        out_specs=pl.BlockSpec((tm, d), lambda i: (i, 0)),
        out_shape=jax.ShapeDtypeStruct((s, d), F32),
        compiler_params=_cparams(("parallel",)),
        name="out_proj",
    )(x, y_sb, y_mla, y_rwkv, w_out, w_out, w_out)


def _final_norm_kernel(x_ref, g_ref, o_ref):
    o_ref[...] = _rms_rows(x_ref[...], g_ref[...])


def final_rmsnorm(x, g, *, tm):
    s, d = x.shape
    return pl.pallas_call(
        _final_norm_kernel,
        grid=(s // tm,),
        in_specs=[pl.BlockSpec((tm, d), lambda i: (i, 0)), pl.BlockSpec((1, d), lambda i: (0, 0))],
        out_specs=pl.BlockSpec((tm, d), lambda i: (i, 0)),
        out_shape=jax.ShapeDtypeStruct((s, d), F32),
        compiler_params=_cparams(("parallel",)),
        name="final_norm",
    )(x, g.reshape(1, d))


SB_BQ = 256
SB_BK = 128
SB_DEAD_LOG = -104.0


def _sb_attn_kernel(q_ref, k_ref, v_ref, g_ref, o_ref):
    i = pl.program_id(0)
    bq, bk, dh = SB_BQ, SB_BK, SB_HEAD_DIM
    heads = range(SB_HEADS)
    hs = [slice(h * dh, (h + 1) * dh) for h in heads]
    q = [q_ref[:, hs[h]] for h in heads]
    row = lax.broadcasted_iota(jnp.int32, (bq, bk), 0)
    col = lax.broadcasted_iota(jnp.int32, (bq, bk), 1)
    kr = lax.broadcasted_iota(jnp.int32, (bk, bk), 0)
    kc = lax.broadcasted_iota(jnp.int32, (bk, bk), 1)
    upper = jnp.where(kr > kc, 1.0, 0.0).astype(BF16)

    def block(kb, carry, acc, masked):
        start = pl.multiple_of(kb * bk, bk)
        z = [_dot_nt(q[h], k_ref[pl.ds(start, bk), hs[h]]) for h in heads]
        log_beta = [jnp.minimum(z[h], 0.0) - jnp.log(1.0 + jnp.exp(-jnp.abs(z[h]))) for h in heads]
        log_keep = [log_beta[h] - z[h] for h in heads]
        if masked:
            before = (start + col) < (i * bq + row)
            log_keep = [jnp.where(before, log_keep[h], 0.0) for h in heads]
        split = [_split2(log_keep[h]) for h in heads]
        within = [_dot(split[h][0], upper) + _dot(split[h][1], upper) for h in heads]
        a = [jnp.exp(log_beta[h] + (carry[h] + within[h])) for h in heads]
        if masked:
            a = [jnp.where(before, a[h], 0.0) for h in heads]
        acc = [acc[h] + _dot(a[h].astype(BF16), v_ref[pl.ds(start, bk), hs[h]]) for h in heads]
        carry = [carry[h] + jnp.sum(log_keep[h], axis=-1, keepdims=True) for h in heads]
        return carry, acc

    def alive_of(carry):
        m = jnp.max(carry[0])
        for h in heads[1:]:
            m = jnp.maximum(m, jnp.max(carry[h]))
        return m

    carry = [jnp.zeros((bq, 1), F32) for _ in heads]
    acc = [jnp.zeros((bq, dh), F32) for _ in heads]
    nd = bq // bk
    first = i * nd
    for d in range(nd - 1, -1, -1):
        carry, acc = block(first + d, carry, acc, True)

    def cond(st):
        return jnp.logical_and(st[0] >= 0, st[1] > SB_DEAD_LOG)

    def body(st):
        kb, _, carry, acc = st
        carry, acc = block(kb, list(carry), list(acc), False)
        return kb - 1, alive_of(carry), tuple(carry), tuple(acc)

    st = lax.while_loop(cond, body, (first - 1, alive_of(carry), tuple(carry), tuple(acc)))
    acc = st[3]
    for h in heads:
        o_ref[:, hs[h]] = _rms_rows(acc[h], g_ref[h:h + 1, :]).astype(o_ref.dtype)


def sb_attention(zb, gains):
    s = zb.shape[0]
    assert s % SB_BQ == 0 and SB_BQ % SB_BK == 0
    whole = lambda c: pl.BlockSpec((s, SB_WIDTH), lambda i: (0, c // SB_WIDTH),
                                   pipeline_mode=pl.Buffered(1))
    return pl.pallas_call(
        _sb_attn_kernel,
        grid=(s // SB_BQ,),
        in_specs=[
            pl.BlockSpec((SB_BQ, SB_WIDTH), lambda i: (i, COL_SBQ // SB_WIDTH)),
            whole(COL_SBK),
            whole(COL_SBV),
            pl.BlockSpec((SB_HEADS, SB_HEAD_DIM), lambda i: (0, 0)),
        ],
        out_specs=pl.BlockSpec((SB_BQ, SB_WIDTH), lambda i: (i, 0)),
        out_shape=jax.ShapeDtypeStruct((s, SB_WIDTH), BF16),
        compiler_params=_cparams(("arbitrary",)),
        name="sb_attention",
    )(zb, zb, zb, gains)


def _mla_prep_kernel(cq_ref, ckv_ref, rope_ref, tab_ref, qn_ref, kvn_ref, wq_ref, wkv_ref,
                     q_ref, k_ref, v_ref):
    scale = (MLA_NOPE_DIM + MLA_ROPE_DIM) ** -0.5
    hq = _rms_rows(cq_ref[...], qn_ref[...]).astype(BF16)
    hkv = _rms_rows(ckv_ref[...], kvn_ref[...]).astype(BF16)
    cos2 = tab_ref[:, :128]
    sin2 = tab_ref[:, 128:]
    kr = rope_ref[...]
    k_pe = kr * cos2 + pltpu.roll(kr, 64, 1) * sin2
    k_pe = k_pe.astype(BF16)
    qall = _dot(hq, wq_ref[...])
    kvall = _dot(hkv, wkv_ref[...])
    for h in range(MLA_HEADS):
        qh = qall[:, h * 384:(h + 1) * 384]
        q_ref[h, :, :128] = (qh[:, :128] * scale).astype(BF16)
        q_ref[h, :, 128:] = ((qh[:, 128:256] * cos2 + qh[:, 256:384] * sin2) * scale).astype(BF16)
        k_ref[h, :, :128] = kvall[:, h * 256:h * 256 + 128].astype(BF16)
        k_ref[h, :, 128:] = k_pe
        v_ref[h, :, :128] = kvall[:, h * 256 + 128:(h + 1) * 256].astype(BF16)
        v_ref[h, :, 128:] = jnp.ones((kr.shape[0], 128), BF16)


def mla_prep(z, tab, q_norm, kv_norm, wq, wkv, *, ts):
    s = z.shape[0]
    return pl.pallas_call(
        _mla_prep_kernel,
        grid=(s // ts,),
        in_specs=[
            pl.BlockSpec((ts, MLA_Q_LORA), lambda i: (i, COL_CQ // MLA_Q_LORA)),
            pl.BlockSpec((ts, MLA_KV_LORA), lambda i: (i, COL_CKV // MLA_KV_LORA)),
            pl.BlockSpec((ts, 128), lambda i: (i, COL_ROPE // 128)),
            pl.BlockSpec((ts, 256), lambda i: (i, 0)),
            pl.BlockSpec((1, MLA_Q_LORA), lambda i: (0, 0)),
            pl.BlockSpec((1, MLA_KV_LORA), lambda i: (0, 0)),
            pl.BlockSpec(wq.shape, lambda i: (0, 0)),
            pl.BlockSpec(wkv.shape, lambda i: (0, 0)),
        ],
        out_specs=[
            pl.BlockSpec((MLA_HEADS, ts, MLA_QK_PAD), lambda i: (0, i, 0)),
            pl.BlockSpec((MLA_HEADS, ts, MLA_QK_PAD), lambda i: (0, i, 0)),
            pl.BlockSpec((MLA_HEADS, ts, 2 * MLA_V_DIM), lambda i: (0, i, 0)),
        ],
        out_shape=[
            jax.ShapeDtypeStruct((MLA_HEADS, s, MLA_QK_PAD), BF16),
            jax.ShapeDtypeStruct((MLA_HEADS, s, MLA_QK_PAD), BF16),
            jax.ShapeDtypeStruct((MLA_HEADS, s, 2 * MLA_V_DIM), BF16),
        ],
        compiler_params=_cparams(("parallel",)),
        name="mla_prep",
    )(z, z, z, tab, q_norm.reshape(1, -1), kv_norm.reshape(1, -1), wq, wkv)


MLA_BQ = 256
MLA_BK = 1024
MLA_HEADS_PER_STEP = 4


def _mla_attn_kernel(q_ref, k_ref, v_ref, g_ref, o_ref):
    i = pl.program_id(1)
    bq, bk = MLA_BQ, MLA_BK
    heads = range(MLA_HEADS_PER_STEP)
    q = [q_ref[h] for h in heads]
    nd = bk // bq

    def block(kb, m, acc, width, masked):
        start = pl.multiple_of(kb * bk, bk)
        sc = [_dot_nt(q[h], k_ref[h, pl.ds(start, width), :]) for h in heads]
        if masked:
            row = lax.broadcasted_iota(jnp.int32, (bq, width), 0) + i * bq
            col = lax.broadcasted_iota(jnp.int32, (bq, width), 1) + start
            sc = [jnp.where(col <= row, sc[h], -1e30) for h in heads]
        m_new = [jnp.maximum(m[h], jnp.max(sc[h], axis=-1, keepdims=True)) for h in heads]
        for h in heads:
            alpha = jnp.exp(m[h] - m_new[h])
            p = jnp.exp(sc[h] - m_new[h])
            acc[h] = acc[h] * alpha + _dot(p.astype(BF16), v_ref[h, pl.ds(start, width), :])
        return m_new, acc

    def body(kb, st):
        m, acc = block(kb, list(st[0]), list(st[1]), bk, False)
        return tuple(m), tuple(acc)

    m0 = tuple(jnp.full((bq, 1), -1e30, F32) for _ in heads)
    a0 = tuple(jnp.zeros((bq, 2 * MLA_V_DIM), F32) for _ in heads)
    last = i // nd
    m, acc = lax.fori_loop(0, last, body, (m0, a0))

    for r in range(nd):
        @pl.when(i % nd == r)
        def _(r=r):
            _, acc_r = block(last, list(m), list(acc), (r + 1) * bq, True)
            for h in heads:
                y = acc_r[h][:, :MLA_V_DIM] / acc_r[h][:, MLA_V_DIM:]
                o_ref[:, h * MLA_V_DIM:(h + 1) * MLA_V_DIM] = _rms_rows(
                    y, g_ref[h:h + 1, :]).astype(o_ref.dtype)


def mla_attention(q, k, v, gains):
    s = q.shape[1]
    assert s % MLA_BK == 0 and MLA_BK % MLA_BQ == 0
    hps = MLA_HEADS_PER_STEP
    whole = lambda n: pl.BlockSpec((hps, s, n), lambda g, i: (g, 0, 0), pipeline_mode=pl.Buffered(1))
    return pl.pallas_call(
        _mla_attn_kernel,
        grid=(MLA_HEADS // hps, s // MLA_BQ),
        in_specs=[
            pl.BlockSpec((hps, MLA_BQ, MLA_QK_PAD), lambda g, i: (g, i, 0)),
            whole(MLA_QK_PAD),
            whole(2 * MLA_V_DIM),
            pl.BlockSpec((hps, MLA_V_DIM), lambda g, i: (g, 0)),
        ],
        out_specs=pl.BlockSpec((MLA_BQ, hps * MLA_V_DIM), lambda g, i: (i, g)),
        out_shape=jax.ShapeDtypeStruct((s, MLA_WIDTH), BF16),
        compiler_params=_cparams(("parallel", "arbitrary")),
        name="mla_attention",
    )(q, k, v, gains)


def _head_sum_matrix():
    r = lax.broadcasted_iota(jnp.int32, (128, 128), 0) // RWKV_HEAD_DIM
    c = lax.broadcasted_iota(jnp.int32, (128, 128), 1) // RWKV_HEAD_DIM
    return jnp.where(r == c, 1.0, 0.0).astype(BF16)


def _head_sums(x, ones_bd):
    parts = []
    for j in range(x.shape[1] // 128):
        hi, lo = _split2(x[:, j * 128:(j + 1) * 128])
        parts.append(_dot(hi, ones_bd) + _dot(lo, ones_bd))
    return jnp.concatenate(parts, axis=-1)


def _rwkv_prep_kernel(r_ref, k_ref, v_ref, lora_ref, xg_ref, mu_ref, mul_ref, mug_ref,
                      w0_ref, w2_ref, a0_ref, a2_ref, g2_ref, kk_ref, ka_ref,
                      ro_ref, lw_ref, ko_ref, vo_ref, kko_ref, bo_ref, go_ref,
                      prev_ref, prevl_ref):
    ts = r_ref.shape[0]

    @pl.when(pl.program_id(0) == 0)
    def _():
        prev_ref[...] = jnp.zeros_like(prev_ref)
        prevl_ref[...] = jnp.zeros_like(prevl_ref)

    first_row = lax.broadcasted_iota(jnp.int32, (ts, 1), 0) == 0

    def shift_mix(x, prev_row, mu):
        x_prev = jnp.where(first_row, prev_row, pltpu.roll(x, 1, 0))
        return x + (x_prev - x) * mu

    r_in, k_in, v_in = r_ref[...], k_ref[...], v_ref[...]
    lora_in, xg_in = lora_ref[...], xg_ref[...]
    r = shift_mix(r_in, prev_ref[0:1, :], mu_ref[0:1, :])
    k = shift_mix(k_in, prev_ref[1:2, :], mu_ref[1:2, :])
    v = shift_mix(v_in, prev_ref[2:3, :], mu_ref[2:3, :])
    lora = shift_mix(lora_in, prevl_ref[0:1, :], mul_ref[...])
    xg = shift_mix(xg_in, prevl_ref[1:2, :], mug_ref[...])
    prev_ref[0:1, :] = r_in[ts - 1:ts, :]
    prev_ref[1:2, :] = k_in[ts - 1:ts, :]
    prev_ref[2:3, :] = v_in[ts - 1:ts, :]
    prevl_ref[0:1, :] = lora_in[ts - 1:ts, :]
    prevl_ref[1:2, :] = xg_in[ts - 1:ts, :]

    dw = w0_ref[...] + _dot(jnp.tanh(lora).astype(BF16), w2_ref[...])
    log_w = jnp.minimum(dw, 0.0) - jnp.log(1.0 + jnp.exp(-jnp.abs(dw))) - 0.5
    lw_ref[...] = -jnp.exp(log_w)
    a = jax.nn.sigmoid(a0_ref[...] + _dot(lora.astype(BF16), a2_ref[...]))
    go_ref[...] = _dot(jax.nn.sigmoid(xg).astype(BF16), g2_ref[...])

    kk = k * kk_ref[...]
    ss = _head_sums(kk * kk, _head_sum_matrix())
    kk = kk / jnp.maximum(jnp.sqrt(ss), 1e-12)
    ro_ref[...] = r
    vo_ref[...] = v
    ko_ref[...] = k * (1.0 + (a - 1.0) * ka_ref[...])
    kko_ref[...] = kk
    bo_ref[...] = kk * a


def _rwkv_chunk_kernel(r_ref, lw_ref, k_ref, v_ref, kk_ref, b_ref, g_ref, rk_ref, lnw_ref, lnb_ref,
                       o_ref, st_ref, y_ref):
    c = RWKV_CHUNK
    n = RWKV_HEAD_DIM
    nh = r_ref.shape[1] // n
    chunks = range(r_ref.shape[0] // c)

    @pl.when(pl.program_id(0) == 0)
    def _():
        st_ref[...] = jnp.zeros_like(st_ref)

    ri = lax.broadcasted_iota(jnp.int32, (c, c), 0)
    ci = lax.broadcasted_iota(jnp.int32, (c, c), 1)
    tril_ones = jnp.where(ri >= ci, 1.0, 0.0).astype(BF16)

    gh = RWKV_GROUP
    gw = gh * n
    groups = range(nh // gh)
    gs = [slice(g * gw, (g + 1) * gw) for g in groups]
    same_head = (lax.broadcasted_iota(jnp.int32, (gw, gw), 0) // n
                 == lax.broadcasted_iota(jnp.int32, (gw, gw), 1) // n)
    zero_b = jnp.zeros((gw, gw), BF16)

    def bd(y):
        yb = y.astype(BF16)
        return jnp.where(same_head, jnp.concatenate([yb] * gh, axis=0), zero_b)

    def mmb(x, y_bd):
        return _dot(x.astype(BF16), y_bd)

    ri2 = lax.broadcasted_iota(jnp.int32, (2 * c, gw), 0)
    ci2 = lax.broadcasted_iota(jnp.int32, (2 * c, gw), 1) % c
    causal2 = ci2 <= jnp.where(ri2 < c, ri2 - 1, ri2 - c)
    rc = lax.broadcasted_iota(jnp.int32, (c, gw), 0)
    cc = lax.broadcasted_iota(jnp.int32, (c, gw), 1) % c
    eye_g = jnp.where(rc == cc, 1.0, 0.0).astype(F32)
    d16_g = (rc // 16) == (cc // 16)
    d32_g = (rc // 32) == (cc // 32)
    lo_g = jnp.logical_and(d32_g, jnp.logical_not(d16_g))
    colhead = lax.broadcasted_iota(jnp.int32, (n, gw), 1) // n

    p_end, v_c, lhs, b_t, k_t, dec = [], [], [], [], [], []
    for ch in chunks:
        rows = slice(ch * c, (ch + 1) * c)
        lw = lw_ref[rows, :]
        cum = _exact_lhs_dot(tril_ones, lw)
        p_incl = jnp.exp(cum)
        p_inv = jnp.exp(-cum)
        pe = p_incl[c - 1:c, :]
        bt = b_ref[rows, :] * p_inv
        kt = k_ref[rows, :] * p_inv
        p_end.append(pe)
        v_c.append(v_ref[rows, :])
        b_t.append(bt)
        k_t.append(kt)
        lhs.append(jnp.concatenate([kk_ref[rows, :] * jnp.exp(cum - lw), r_ref[rows, :] * p_incl],
                                   axis=0).astype(BF16))
        dec.append(jnp.concatenate([bt * pe, kt * pe], axis=0).astype(BF16))

    cg = [(ch, g) for ch in chunks for g in groups]
    ab = {q: jnp.where(causal2, _dot_nt(lhs[q[0]][:, gs[q[1]]], bd(b_t[q[0]][:, gs[q[1]]])), 0.0)
          for q in cg}
    ak = {q: jnp.where(causal2, _dot_nt(lhs[q[0]][:, gs[q[1]]], bd(k_t[q[0]][:, gs[q[1]]])), 0.0)
          for q in cg}
    akv = {q: mmb(ak[q], bd(v_c[q[0]][:, gs[q[1]]])) for q in cg}
    a_ub = {q: ab[q][:c] for q in cg}
    ld = {q: jnp.where(d16_g, a_ub[q], 0.0) for q in cg}
    x = {q: eye_g - ld[q] for q in cg}
    pw = {q: mmb(ld[q], bd(ld[q])) for q in cg}
    x = {q: x[q] + mmb(x[q], bd(pw[q])) for q in cg}
    pw = {q: mmb(pw[q], bd(pw[q])) for q in cg}
    x = {q: x[q] + mmb(x[q], bd(pw[q])) for q in cg}
    pw = {q: mmb(pw[q], bd(pw[q])) for q in cg}
    x = {q: x[q] + mmb(x[q], bd(pw[q])) for q in cg}
    t = {q: mmb(x[q], bd(jnp.where(lo_g, a_ub[q], 0.0))) for q in cg}
    x = {q: x[q] - mmb(t[q], bd(x[q])) for q in cg}
    t = {q: mmb(x[q], bd(jnp.where(d32_g, 0.0, a_ub[q]))) for q in cg}
    x = {q: x[q] - mmb(t[q], bd(x[q])) for q in cg}
    u0 = {q: -mmb(x[q], bd(akv[q][:c])) for q in cg}
    wk = {q: -mmb(x[q], bd(lhs[q[0]][:c, gs[q[1]]])) for q in cg}
    lhs2 = {q: jnp.concatenate([wk[q].astype(BF16), lhs[q[0]][c:, gs[q[1]]]], axis=0) for q in cg}

    st = [st_ref[:, gs[g]] for g in groups]
    for ch in chunks:
        rows = slice(ch * c, (ch + 1) * c)
        through = [_dot_nt(lhs2[(ch, g)], bd(st[g])) for g in groups]
        u = [through[g][:c] + u0[(ch, g)] for g in groups]
        for g in groups:
            y_ref[rows, gs[g]] = through[g][c:] + akv[(ch, g)][c:] + mmb(ab[(ch, g)][c:], bd(u[g]))
        for g in groups:
            uv = jnp.concatenate([u[g], v_c[ch][:, gs[g]]], axis=0).astype(BF16)
            cross = _dot_tn(uv, dec[ch][:, gs[g]])
            upd = st[g] * p_end[ch][:, gs[g]]
            for h in range(gh):
                upd = upd + jnp.where(colhead == h, cross[h * n:(h + 1) * n], 0.0)
            st[g] = upd
    for g in groups:
        st_ref[:, gs[g]] = st[g]

    ones_bd = _head_sum_matrix()
    y = y_ref[...]
    v = v_ref[...]
    mean = _head_sums(y, ones_bd) * (1.0 / n)
    yc = y - mean
    var = _head_sums(yc * yc, ones_bd) * (1.0 / n)
    yn = yc * lax.rsqrt(var + RWKV_GN_EPS) * lnw_ref[...] + lnb_ref[...]
    bonus = _head_sums(r_ref[...] * k_ref[...] * rk_ref[...], ones_bd)
    o_ref[...] = ((yn + bonus * v) * g_ref[...]).astype(o_ref.dtype)


def _rwkv_kernel(zr_ref, zk_ref, zv_ref, zl_ref, zg_ref, mu_ref, mul_ref, mug_ref, w0_ref, w2_ref,
                 a0_ref, a2_ref, g2_ref, kk_ref, ka_ref, rk_ref, lnw_ref, lnb_ref,
                 o_ref, st_ref, y_ref, prev_ref, prevl_ref, *staged):
    _rwkv_prep_kernel(zr_ref, zk_ref, zv_ref, zl_ref, zg_ref, mu_ref, mul_ref, mug_ref, w0_ref, w2_ref,
                      a0_ref, a2_ref, g2_ref, kk_ref, ka_ref, *staged, prev_ref, prevl_ref)
    _rwkv_chunk_kernel(*staged, rk_ref, lnw_ref, lnb_ref, o_ref, st_ref, y_ref)


def rwkv_mix(z, p):
    s = z.shape[0]
    w = RWKV_WIDTH
    rows = RWKV_CHUNK * RWKV_CHUNKS_PER_STEP
    assert s % rows == 0
    row = lambda n: pl.BlockSpec((1, n), lambda i: (0, 0))
    lora_w = pl.BlockSpec((128, w), lambda i: (0, 0))
    return pl.pallas_call(
        _rwkv_kernel,
        grid=(s // rows,),
        in_specs=[
            pl.BlockSpec((rows, w), lambda i: (i, COL_R // w)),
            pl.BlockSpec((rows, w), lambda i: (i, COL_K // w)),
            pl.BlockSpec((rows, w), lambda i: (i, COL_V // w)),
            pl.BlockSpec((rows, 128), lambda i: (i, COL_LORA // 128)),
            pl.BlockSpec((rows, 128), lambda i: (i, COL_XG // 128)),
            pl.BlockSpec((3, w), lambda i: (0, 0)),
            row(128), row(128),
            row(w), lora_w,
            row(w), lora_w,
            lora_w,
            row(w), row(w),
            row(w), row(w), row(w),
        ],
        out_specs=pl.BlockSpec((rows, w), lambda i: (i, 0)),
        out_shape=jax.ShapeDtypeStruct((s, w), BF16),
        scratch_shapes=[
            pltpu.VMEM((RWKV_HEAD_DIM, w), F32),
            pltpu.VMEM((rows, w), F32),
            pltpu.VMEM((8, w), F32), pltpu.VMEM((8, 128), F32),
        ] + [pltpu.VMEM((rows, w), F32)] * 7,
        compiler_params=_cparams(("arbitrary",)),
        name="rwkv_mix",
    )(z, z, z, z, z, p["mu_rkv"], p["mu_lora"], p["mu_g"], p["w0"], p["w2"], p["a0"], p["a2"],
      p["g2"], p["k_k"], p["k_a"], p["r_k"], p["ln_w"], p["ln_b"])


def _swap_halves(w):
    half = w.shape[-1] // 2
    return jnp.concatenate([w[..., half:], w[..., :half]], axis=-1)


def _prep_w_in(w_in):
    wt = jnp.swapaxes(w_in, 1, 2)
    sizes = [SB_WIDTH] * 3 + [MLA_Q_LORA, MLA_KV_LORA, MLA_ROPE_DIM] + [RWKV_WIDTH] * 3 + [
        RWKV_DECAY_LORA, RWKV_A_LORA, RWKV_GATE_LORA]
    idx = [int(i) for i in np.cumsum(sizes)[:-1]]
    sbq, sbk, sbv, cq, ckv, krope, r, k, v, xw, xa, xg = jnp.split(wt, idx, axis=1)
    sbq = sbq * (SB_HEAD_DIM ** -0.5)
    half = MLA_ROPE_DIM // 2
    krope_sw = jnp.concatenate([krope[:, half:], krope[:, :half]], axis=1)
    rows = [r, k, v, cq, ckv, xw, xa, xg, krope, krope_sw]
    used = sum(t.shape[1] for t in rows)
    rows.append(jnp.zeros((wt.shape[0], Z_F32_COLS - used, wt.shape[2]), wt.dtype))
    rows += [sbq, sbk, sbv]
    return jnp.concatenate(rows, axis=1).astype(BF16)


def _prep_mla_w(w_uq, w_ukv):
    q = w_uq.reshape(MLA_Q_LORA, MLA_HEADS, MLA_NOPE_DIM + MLA_ROPE_DIM)
    nope, pe = q[..., :MLA_NOPE_DIM], q[..., MLA_NOPE_DIM:]
    zpad = jnp.zeros_like(pe)
    wq = jnp.concatenate([nope, pe, zpad, _swap_halves(pe), zpad], axis=-1)
    return wq.reshape(MLA_Q_LORA, MLA_HEADS * 384).astype(BF16), w_ukv.astype(BF16)


def _rope_table(positions):
    half = MLA_ROPE_DIM // 2
    inv_freq = ROPE_THETA ** (-jnp.arange(half, dtype=F32) / half)
    ang = positions.astype(F32)[:, None] * inv_freq
    cos, sin = jnp.cos(ang), jnp.sin(ang)
    z = jnp.zeros((positions.shape[0], MLA_ROPE_DIM), F32)
    return jnp.concatenate([cos, cos, z, -sin, sin, z], axis=-1)


def _pick_tile(n, candidates):
    for c in candidates:
        if n % c == 0:
            return c
    raise ValueError(f"no tile for {n}")


def kernel(x, positions, ffn1_norm, ffn1_gate, ffn1_up, ffn1_down, mix_norm, w_in, mla_q_norm,
           mla_w_uq, mla_kv_norm, mla_w_ukv, rwkv_mu, rwkv_w0, rwkv_w2, rwkv_a0, rwkv_a2, rwkv_g2,
           rwkv_k_k, rwkv_k_a, rwkv_r_k, rwkv_ln_w, rwkv_ln_b, sb_out_norm, mla_out_norm, w_out,
           ffn2_norm, ffn2_gate, ffn2_up, ffn2_down, final_norm):
    bsz, s, d = x.shape
    assert bsz == 1 and d == SB_WIDTH + MLA_WIDTH + RWKV_WIDTH
    depth = w_in.shape[0]
    d_ff = ffn1_gate.shape[-1]
    tm = _pick_tile(s, (1024, 512, 256, 128))
    tf = _pick_tile(d_ff, (512, 256, 128))
    tn_down = 512
    tm_io = _pick_tile(s, (512, 256, 128))
    x = x[0]
    tab = _rope_table(positions[0])
    w = RWKV_WIDTH
    zl = jnp.zeros((RWKV_DECAY_LORA, w), F32)
    ffn_w = [(wg, wu, wd.astype(BF16))
             for wg, wu, wd in ((ffn1_gate, ffn1_up, ffn1_down), (ffn2_gate, ffn2_up, ffn2_down))]
    w_in_b = _prep_w_in(w_in)
    w_out_b = w_out.astype(BF16)

    def ffn(x, g, ws, l):
        act = ffn_up(x, g, ws[0], ws[1], l, tm=tm, tn=tf)
        return ffn_down(x, act, ws[2], l, tm=tm, tn=tn_down)

    for l in range(depth):
        x = ffn(x, ffn1_norm[l], ffn_w[0], l)

        z, zb = norm_proj(x, mix_norm[l], w_in_b, l, tm=tm)

        y_sb = sb_attention(zb, sb_out_norm[l])

        wq, wkv = _prep_mla_w(mla_w_uq[l], mla_w_ukv[l])
        q, k, v = mla_prep(z, tab, mla_q_norm[l], mla_kv_norm[l], wq, wkv, ts=tm_io)
        y_mla = mla_attention(q, k, v, mla_out_norm[l])

        mu = rwkv_mu[l]
        p = {
            "mu_rkv": mu[:3 * w].reshape(3, w),
            "mu_lora": mu[3 * w:3 * w + 128].reshape(1, 128),
            "mu_g": mu[3 * w + 128:].reshape(1, 128),
            "w0": rwkv_w0[l].reshape(1, w),
            "w2": jnp.concatenate([rwkv_w2[l], zl], axis=0).astype(BF16),
            "a0": rwkv_a0[l].reshape(1, w),
            "a2": jnp.concatenate([zl, rwkv_a2[l]], axis=0).astype(BF16),
            "g2": rwkv_g2[l].astype(BF16),
            "k_k": rwkv_k_k[l].reshape(1, w),
            "k_a": rwkv_k_a[l].reshape(1, w),
            "r_k": rwkv_r_k[l].reshape(1, w),
            "ln_w": rwkv_ln_w[l].reshape(1, w),
            "ln_b": rwkv_ln_b[l].reshape(1, w),
        }
        y_rwkv = rwkv_mix(z, p)

        x = out_proj(x, y_sb, y_mla, y_rwkv, w_out_b, l, tm=tm_io)

        x = ffn(x, ffn2_norm[l], ffn_w[1], l)

    return final_rmsnorm(x, final_norm, tm=tm_io)[None]
```

```python
import jax
import jax.numpy as jnp
import numpy as np
from jax import lax
from jax.experimental import pallas as pl
from jax.experimental.pallas import tpu as pltpu

F32 = jnp.float32
BF16 = jnp.bfloat16

NORM_EPS = 1e-6
SB_HEADS = 4
SB_HEAD_DIM = 128
SB_WIDTH = SB_HEADS * SB_HEAD_DIM
MLA_HEADS = 4
MLA_NOPE_DIM = 128
MLA_ROPE_DIM = 64
MLA_V_DIM = 128
MLA_Q_LORA = 512
MLA_KV_LORA = 256
MLA_WIDTH = MLA_HEADS * MLA_V_DIM
MLA_QK_PAD = 256
ROPE_THETA = 10000.0
RWKV_HEADS = 16
RWKV_HEAD_DIM = 64
RWKV_WIDTH = RWKV_HEADS * RWKV_HEAD_DIM
RWKV_DECAY_LORA = 64
RWKV_A_LORA = 64
RWKV_GATE_LORA = 128
RWKV_GN_EPS = 64e-5
RWKV_CHUNK = 64
RWKV_GROUP = 4
RWKV_CHUNKS_PER_STEP = 4

COL_R, COL_K, COL_V = 0, 1024, 2048
COL_CQ, COL_CKV = 3072, 3584
COL_LORA, COL_XG, COL_ROPE = 3840, 3968, 4096
Z_F32_COLS = 4608
COL_SBQ, COL_SBK, COL_SBV = 0, 512, 1024
Z_B16_COLS = 1536
Z_TILE = 768

VMEM_LIMIT = 48 * 1024 * 1024


def _cparams(sem):
    return pltpu.CompilerParams(dimension_semantics=sem, vmem_limit_bytes=VMEM_LIMIT)


def _dot(a, b):
    return jnp.dot(a, b, preferred_element_type=F32)


def _dot_nt(a, b):
    return lax.dot_general(a, b, (((1,), (1,)), ((), ())), preferred_element_type=F32)


def _dot_tn(a, b):
    return lax.dot_general(a, b, (((0,), (0,)), ((), ())), preferred_element_type=F32)


def _split3(x):
    hi = x.astype(BF16)
    r1 = x - hi.astype(F32)
    mid = r1.astype(BF16)
    lo = (r1 - mid.astype(F32)).astype(BF16)
    return hi, mid, lo


def _split2(x):
    hi = x.astype(BF16)
    lo = (x - hi.astype(F32)).astype(BF16)
    return hi, lo


def _exact_lhs_dot(m_bf16, x):
    hi, mid, lo = _split3(x)
    return _dot(m_bf16, hi) + (_dot(m_bf16, mid) + _dot(m_bf16, lo))


def _rms_rows(x, g):
    ms = jnp.mean(x * x, axis=-1, keepdims=True)
    return x * lax.rsqrt(ms + NORM_EPS) * g


def _norm_proj_kernel(x_ref, g_ref, w_ref, o_ref, ob_ref, h_ref):
    @pl.when(pl.program_id(1) == 0)
    def _():
        h_ref[...] = _rms_rows(x_ref[...], g_ref[...]).astype(BF16)

    z = _dot_nt(h_ref[...], w_ref[...])
    nf = Z_F32_COLS // Z_TILE

    @pl.when(pl.program_id(1) < nf)
    def _():
        o_ref[...] = z

    @pl.when(pl.program_id(1) >= nf)
    def _():
        ob_ref[...] = z.astype(BF16)


def norm_proj(x, g, w, l, *, tm):
    s, d = x.shape
    tn = Z_TILE
    nf, nb = Z_F32_COLS // tn, Z_B16_COLS // tn
    return pl.pallas_call(
        _norm_proj_kernel,
        grid=(s // tm, nf + nb),
        in_specs=[
            pl.BlockSpec((tm, d), lambda i, j: (i, 0)),
            pl.BlockSpec((1, d), lambda i, j: (0, 0)),
            pl.BlockSpec((None, tn, d), lambda i, j: (l, j, 0)),
        ],
        out_specs=[
            pl.BlockSpec((tm, tn), lambda i, j: (i, jnp.minimum(j, nf - 1))),
            pl.BlockSpec((tm, tn), lambda i, j: (i, jnp.maximum(j - nf, 0))),
        ],
        out_shape=[jax.ShapeDtypeStruct((s, Z_F32_COLS), F32),
                   jax.ShapeDtypeStruct((s, Z_B16_COLS), BF16)],
        scratch_shapes=[pltpu.VMEM((tm, d), BF16)],
        compiler_params=_cparams(("parallel", "arbitrary")),
        name="norm_proj",
    )(x, g.reshape(1, d), w)


def _ffn_up_kernel(x_ref, g_ref, wg_ref, wu_ref, o_ref, h_ref):
    @pl.when(pl.program_id(1) == 0)
    def _():
        h_ref[...] = _rms_rows(x_ref[...], g_ref[...]).astype(BF16)

    h = h_ref[...]
    a = _dot(h, wg_ref[...].astype(BF16))
    u = _dot(h, wu_ref[...].astype(BF16))
    o_ref[...] = (a * jax.nn.sigmoid(a) * u).astype(o_ref.dtype)


def ffn_up(x, g, wg, wu, l, *, tm, tn):
    s, d = x.shape
    f = wg.shape[2]
    return pl.pallas_call(
        _ffn_up_kernel,
        grid=(s // tm, f // tn),
        in_specs=[
            pl.BlockSpec((tm, d), lambda i, j: (i, 0)),
            pl.BlockSpec((1, d), lambda i, j: (0, 0)),
            pl.BlockSpec((None, d, tn), lambda i, j: (l, 0, j)),
            pl.BlockSpec((None, d, tn), lambda i, j: (l, 0, j)),
        ],
        out_specs=pl.BlockSpec((tm, tn), lambda i, j: (i, j)),
        out_shape=jax.ShapeDtypeStruct((s, f), BF16),
        scratch_shapes=[pltpu.VMEM((tm, d), BF16)],
        compiler_params=_cparams(("parallel", "arbitrary")),
        name="ffn_up",
    )(x, g.reshape(1, d), wg, wu)


def _ffn_down_kernel(x_ref, a_ref, w_ref, o_ref):
    o_ref[...] = x_ref[...] + 0.5 * _dot(a_ref[...], w_ref[...])


def ffn_down(x, act, wd, l, *, tm, tn):
    s, d = x.shape
    f = act.shape[1]
    return pl.pallas_call(
        _ffn_down_kernel,
        grid=(s // tm, d // tn),
        in_specs=[
            pl.BlockSpec((tm, tn), lambda i, j: (i, j)),
            pl.BlockSpec((tm, f), lambda i, j: (i, 0)),
            pl.BlockSpec((None, f, tn), lambda i, j: (l, 0, j)),
        ],
        out_specs=pl.BlockSpec((tm, tn), lambda i, j: (i, j)),
        out_shape=jax.ShapeDtypeStruct((s, d), F32),
        compiler_params=_cparams(("parallel", "arbitrary")),
        name="ffn_down",
    )(x, act, wd)


def _out_proj_kernel(x_ref, a_ref, b_ref, c_ref, wa_ref, wb_ref, wc_ref, o_ref):
    acc = _dot(a_ref[...], wa_ref[...])
    acc += _dot(b_ref[...], wb_ref[...])
    acc += _dot(c_ref[...], wc_ref[...])
    o_ref[...] = x_ref[...] + acc


def out_proj(x, y_sb, y_mla, y_rwkv, w_out, l, *, tm):
    s, d = x.shape
    assert SB_WIDTH == MLA_WIDTH and RWKV_WIDTH == SB_WIDTH + MLA_WIDTH
    wspec = lambda rows, blk: pl.BlockSpec((None, rows, d), lambda i: (l, blk, 0),
                                           pipeline_mode=pl.Buffered(1))
    return pl.pallas_call(
        _out_proj_kernel,
        grid=(s // tm,),
        in_specs=[
            pl.BlockSpec((tm, d), lambda i: (i, 0)),
            pl.BlockSpec((tm, SB_WIDTH), lambda i: (i, 0)),
            pl.BlockSpec((tm, MLA_WIDTH), lambda i: (i, 0)),
            pl.BlockSpec((tm, RWKV_WIDTH), lambda i: (i, 0)),
            wspec(SB_WIDTH, 0),
            wspec(MLA_WIDTH, 1),
            wspec(RWKV_WIDTH, 1),
        ],
        out_specs=pl.BlockSpec((tm, d), lambda i: (i, 0)),
        out_shape=jax.ShapeDtypeStruct((s, d), F32),
        compiler_params=_cparams(("parallel",)),
        name="out_proj",
    )(x, y_sb, y_mla, y_rwkv, w_out, w_out, w_out)


def _final_norm_kernel(x_ref, g_ref, o_ref):
    o_ref[...] = _rms_rows(x_ref[...], g_ref[...])


def final_rmsnorm(x, g, *, tm):
    s, d = x.shape
    return pl.pallas_call(
        _final_norm_kernel,
        grid=(s // tm,),
        in_specs=[pl.BlockSpec((tm, d), lambda i: (i, 0)), pl.BlockSpec((1, d), lambda i: (0, 0))],
        out_specs=pl.BlockSpec((tm, d), lambda i: (i, 0)),
        out_shape=jax.ShapeDtypeStruct((s, d), F32),
        compiler_params=_cparams(("parallel",)),
        name="final_norm",
    )(x, g.reshape(1, d))


SB_BQ = 256
SB_BK = 128
SB_DEAD_LOG = -104.0


def _sb_attn_kernel(q_ref, k_ref, v_ref, g_ref, o_ref):
    i = pl.program_id(0)
    bq, bk, dh = SB_BQ, SB_BK, SB_HEAD_DIM
    heads = range(SB_HEADS)
    hs = [slice(h * dh, (h + 1) * dh) for h in heads]
    q = [q_ref[:, hs[h]] for h in heads]
    row = lax.broadcasted_iota(jnp.int32, (bq, bk), 0)
    col = lax.broadcasted_iota(jnp.int32, (bq, bk), 1)
    kr = lax.broadcasted_iota(jnp.int32, (bk, bk), 0)
    kc = lax.broadcasted_iota(jnp.int32, (bk, bk), 1)
    upper = jnp.where(kr > kc, 1.0, 0.0).astype(BF16)

    def block(kb, carry, acc, masked):
        start = pl.multiple_of(kb * bk, bk)
        z = [_dot_nt(q[h], k_ref[pl.ds(start, bk), hs[h]]) for h in heads]
        log_beta = [jnp.minimum(z[h], 0.0) - jnp.log(1.0 + jnp.exp(-jnp.abs(z[h]))) for h in heads]
        log_keep = [log_beta[h] - z[h] for h in heads]
        if masked:
            before = (start + col) < (i * bq + row)
            log_keep = [jnp.where(before, log_keep[h], 0.0) for h in heads]
        split = [_split2(log_keep[h]) for h in heads]
        within = [_dot(split[h][0], upper) + _dot(split[h][1], upper) for h in heads]
        a = [jnp.exp(log_beta[h] + (carry[h] + within[h])) for h in heads]
        if masked:
            a = [jnp.where(before, a[h], 0.0) for h in heads]
        acc = [acc[h] + _dot(a[h].astype(BF16), v_ref[pl.ds(start, bk), hs[h]]) for h in heads]
        carry = [carry[h] + jnp.sum(log_keep[h], axis=-1, keepdims=True) for h in heads]
        return carry, acc

    def alive_of(carry):
        m = jnp.max(carry[0])
        for h in heads[1:]:
            m = jnp.maximum(m, jnp.max(carry[h]))
        return m

    carry = [jnp.zeros((bq, 1), F32) for _ in heads]
    acc = [jnp.zeros((bq, dh), F32) for _ in heads]
    nd = bq // bk
    first = i * nd
    for d in range(nd - 1, -1, -1):
        carry, acc = block(first + d, carry, acc, True)

    def cond(st):
        return jnp.logical_and(st[0] >= 0, st[1] > SB_DEAD_LOG)

    def body(st):
        kb, _, carry, acc = st
        carry, acc = block(kb, list(carry), list(acc), False)
        return kb - 1, alive_of(carry), tuple(carry), tuple(acc)

    st = lax.while_loop(cond, body, (first - 1, alive_of(carry), tuple(carry), tuple(acc)))
    acc = st[3]
    for h in heads:
        o_ref[:, hs[h]] = _rms_rows(acc[h], g_ref[h:h + 1, :]).astype(o_ref.dtype)


def sb_attention(zb, gains):
    s = zb.shape[0]
    assert s % SB_BQ == 0 and SB_BQ % SB_BK == 0
    whole = lambda c: pl.BlockSpec((s, SB_WIDTH), lambda i: (0, c // SB_WIDTH),
                                   pipeline_mode=pl.Buffered(1))
    return pl.pallas_call(
        _sb_attn_kernel,
        grid=(s // SB_BQ,),
        in_specs=[
            pl.BlockSpec((SB_BQ, SB_WIDTH), lambda i: (i, COL_SBQ // SB_WIDTH)),
            whole(COL_SBK),
            whole(COL_SBV),
            pl.BlockSpec((SB_HEADS, SB_HEAD_DIM), lambda i: (0, 0)),
        ],
        out_specs=pl.BlockSpec((SB_BQ, SB_WIDTH), lambda i: (i, 0)),
        out_shape=jax.ShapeDtypeStruct((s, SB_WIDTH), BF16),
        compiler_params=_cparams(("arbitrary",)),
        name="sb_attention",
    )(zb, zb, zb, gains)


def _mla_prep_kernel(cq_ref, ckv_ref, rope_ref, tab_ref, qn_ref, kvn_ref, wq_ref, wkv_ref,
                     q_ref, k_ref, v_ref):
    scale = float((MLA_NOPE_DIM + MLA_ROPE_DIM) ** -0.5 * np.log2(np.e))
    hq = _rms_rows(cq_ref[...], qn_ref[...]).astype(BF16)
    hkv = _rms_rows(ckv_ref[...], kvn_ref[...]).astype(BF16)
    cos2 = tab_ref[:, :128]
    sin2 = tab_ref[:, 128:]
    kr = rope_ref[...]
    k_pe = kr * cos2 + pltpu.roll(kr, 64, 1) * sin2
    k_pe = k_pe.astype(BF16)
    qall = _dot(hq, wq_ref[...])
    kvall = _dot(hkv, wkv_ref[...])
    for h in range(MLA_HEADS):
        qh = qall[:, h * 384:(h + 1) * 384]
        q_ref[h, :, :128] = (qh[:, :128] * scale).astype(BF16)
        q_ref[h, :, 128:] = ((qh[:, 128:256] * cos2 + qh[:, 256:384] * sin2) * scale).astype(BF16)
        k_ref[h, :, :128] = kvall[:, h * 256:h * 256 + 128].astype(BF16)
        k_ref[h, :, 128:] = k_pe
        v_ref[h, :, :128] = kvall[:, h * 256 + 128:(h + 1) * 256].astype(BF16)
        v_ref[h, :, 128:] = jnp.ones((kr.shape[0], 128), BF16)


def mla_prep(z, tab, q_norm, kv_norm, wq, wkv, *, ts):
    s = z.shape[0]
    return pl.pallas_call(
        _mla_prep_kernel,
        grid=(s // ts,),
        in_specs=[
            pl.BlockSpec((ts, MLA_Q_LORA), lambda i: (i, COL_CQ // MLA_Q_LORA)),
            pl.BlockSpec((ts, MLA_KV_LORA), lambda i: (i, COL_CKV // MLA_KV_LORA)),
            pl.BlockSpec((ts, 128), lambda i: (i, COL_ROPE // 128)),
            pl.BlockSpec((ts, 256), lambda i: (i, 0)),
            pl.BlockSpec((1, MLA_Q_LORA), lambda i: (0, 0)),
            pl.BlockSpec((1, MLA_KV_LORA), lambda i: (0, 0)),
            pl.BlockSpec(wq.shape, lambda i: (0, 0)),
            pl.BlockSpec(wkv.shape, lambda i: (0, 0)),
        ],
        out_specs=[
            pl.BlockSpec((MLA_HEADS, ts, MLA_QK_PAD), lambda i: (0, i, 0)),
            pl.BlockSpec((MLA_HEADS, ts, MLA_QK_PAD), lambda i: (0, i, 0)),
            pl.BlockSpec((MLA_HEADS, ts, 2 * MLA_V_DIM), lambda i: (0, i, 0)),
        ],
        out_shape=[
            jax.ShapeDtypeStruct((MLA_HEADS, s, MLA_QK_PAD), BF16),
            jax.ShapeDtypeStruct((MLA_HEADS, s, MLA_QK_PAD), BF16),
            jax.ShapeDtypeStruct((MLA_HEADS, s, 2 * MLA_V_DIM), BF16),
        ],
        compiler_params=_cparams(("parallel",)),
        name="mla_prep",
    )(z, z, z, tab, q_norm.reshape(1, -1), kv_norm.reshape(1, -1), wq, wkv)


MLA_BQ = 256
MLA_BK = 1024
MLA_HEADS_PER_STEP = 4


def _mla_attn_kernel(q_ref, k_ref, v_ref, g_ref, o_ref):
    i = pl.program_id(1)
    bq, bk = MLA_BQ, MLA_BK
    heads = range(MLA_HEADS_PER_STEP)
    q = [q_ref[h] for h in heads]
    nd = bk // bq

    def block(kb, m, acc, width, masked):
        start = pl.multiple_of(kb * bk, bk)
        sc = [_dot_nt(q[h], k_ref[h, pl.ds(start, width), :]) for h in heads]
        if masked:
            row = lax.broadcasted_iota(jnp.int32, (bq, width), 0) + i * bq
            col = lax.broadcasted_iota(jnp.int32, (bq, width), 1) + start
            sc = [jnp.where(col <= row, sc[h], -1e30) for h in heads]
        m_new = [jnp.maximum(m[h], jnp.max(sc[h], axis=-1, keepdims=True)) for h in heads]
        for h in heads:
            alpha = jnp.exp2(m[h] - m_new[h])
            p = jnp.exp2(sc[h] - m_new[h])
            acc[h] = acc[h] * alpha + _dot(p.astype(BF16), v_ref[h, pl.ds(start, width), :])
        return m_new, acc

    def body(kb, st):
        m, acc = block(kb, list(st[0]), list(st[1]), bk, False)
        return tuple(m), tuple(acc)

    m0 = tuple(jnp.full((bq, 1), -1e30, F32) for _ in heads)
    a0 = tuple(jnp.zeros((bq, 2 * MLA_V_DIM), F32) for _ in heads)
    last = i // nd
    m, acc = lax.fori_loop(0, last, body, (m0, a0))

    for r in range(nd):
        @pl.when(i % nd == r)
        def _(r=r):
            _, acc_r = block(last, list(m), list(acc), (r + 1) * bq, True)
            for h in heads:
                y = acc_r[h][:, :MLA_V_DIM] / acc_r[h][:, MLA_V_DIM:]
                o_ref[:, h * MLA_V_DIM:(h + 1) * MLA_V_DIM] = _rms_rows(
                    y, g_ref[h:h + 1, :]).astype(o_ref.dtype)


def mla_attention(q, k, v, gains):
    s = q.shape[1]
    assert s % MLA_BK == 0 and MLA_BK % MLA_BQ == 0
    hps = MLA_HEADS_PER_STEP
    whole = lambda n: pl.BlockSpec((hps, s, n), lambda g, i: (g, 0, 0), pipeline_mode=pl.Buffered(1))
    return pl.pallas_call(
        _mla_attn_kernel,
        grid=(MLA_HEADS // hps, s // MLA_BQ),
        in_specs=[
            pl.BlockSpec((hps, MLA_BQ, MLA_QK_PAD), lambda g, i: (g, i, 0)),
            whole(MLA_QK_PAD),
            whole(2 * MLA_V_DIM),
            pl.BlockSpec((hps, MLA_V_DIM), lambda g, i: (g, 0)),
        ],
        out_specs=pl.BlockSpec((MLA_BQ, hps * MLA_V_DIM), lambda g, i: (i, g)),
        out_shape=jax.ShapeDtypeStruct((s, MLA_WIDTH), BF16),
        compiler_params=_cparams(("parallel", "arbitrary")),
        name="mla_attention",
    )(q, k, v, gains)


def _head_sum_matrix():
    r = lax.broadcasted_iota(jnp.int32, (128, 128), 0) // RWKV_HEAD_DIM
    c = lax.broadcasted_iota(jnp.int32, (128, 128), 1) // RWKV_HEAD_DIM
    return jnp.where(r == c, 1.0, 0.0).astype(BF16)


def _head_sums(x, ones_bd):
    parts = []
    for j in range(x.shape[1] // 128):
        hi, lo = _split2(x[:, j * 128:(j + 1) * 128])
        parts.append(_dot(hi, ones_bd) + _dot(lo, ones_bd))
    return jnp.concatenate(parts, axis=-1)


def _rwkv_prep_kernel(r_ref, k_ref, v_ref, lora_ref, xg_ref, mu_ref, mul_ref, mug_ref,
                      w0_ref, w2_ref, a0_ref, a2_ref, g2_ref, kk_ref, ka_ref,
                      ro_ref, lw_ref, ko_ref, vo_ref, kko_ref, bo_ref, go_ref,
                      prev_ref, prevl_ref):
    ts = r_ref.shape[0]

    @pl.when(pl.program_id(0) == 0)
    def _():
        prev_ref[...] = jnp.zeros_like(prev_ref)
        prevl_ref[...] = jnp.zeros_like(prevl_ref)

    first_row = lax.broadcasted_iota(jnp.int32, (ts, 1), 0) == 0

    def shift_mix(x, prev_row, mu):
        x_prev = jnp.where(first_row, prev_row, pltpu.roll(x, 1, 0))
        return x + (x_prev - x) * mu

    r_in, k_in, v_in = r_ref[...], k_ref[...], v_ref[...]
    lora_in, xg_in = lora_ref[...], xg_ref[...]
    r = shift_mix(r_in, prev_ref[0:1, :], mu_ref[0:1, :])
    k = shift_mix(k_in, prev_ref[1:2, :], mu_ref[1:2, :])
    v = shift_mix(v_in, prev_ref[2:3, :], mu_ref[2:3, :])
    lora = shift_mix(lora_in, prevl_ref[0:1, :], mul_ref[...])
    xg = shift_mix(xg_in, prevl_ref[1:2, :], mug_ref[...])
    prev_ref[0:1, :] = r_in[ts - 1:ts, :]
    prev_ref[1:2, :] = k_in[ts - 1:ts, :]
    prev_ref[2:3, :] = v_in[ts - 1:ts, :]
    prevl_ref[0:1, :] = lora_in[ts - 1:ts, :]
    prevl_ref[1:2, :] = xg_in[ts - 1:ts, :]

    dw = w0_ref[...] + _dot(jnp.tanh(lora).astype(BF16), w2_ref[...])
    log_w = jnp.minimum(dw, 0.0) - jnp.log(1.0 + jnp.exp(-jnp.abs(dw))) - 0.5
    lw_ref[...] = -jnp.exp(log_w)
    a = jax.nn.sigmoid(a0_ref[...] + _dot(lora.astype(BF16), a2_ref[...]))
    go_ref[...] = _dot(jax.nn.sigmoid(xg).astype(BF16), g2_ref[...])

    kk = k * kk_ref[...]
    ss = _head_sums(kk * kk, _head_sum_matrix())
    kk = kk / jnp.maximum(jnp.sqrt(ss), 1e-12)
    ro_ref[...] = r
    vo_ref[...] = v
    ko_ref[...] = k * (1.0 + (a - 1.0) * ka_ref[...])
    kko_ref[...] = kk
    bo_ref[...] = kk * a


def _rwkv_chunk_kernel(r_ref, lw_ref, k_ref, v_ref, kk_ref, b_ref, g_ref, rk_ref, lnw_ref, lnb_ref,
                       o_ref, st_ref, y_ref):
    c = RWKV_CHUNK
    n = RWKV_HEAD_DIM
    nh = r_ref.shape[1] // n
    chunks = range(r_ref.shape[0] // c)

    @pl.when(pl.program_id(0) == 0)
    def _():
        st_ref[...] = jnp.zeros_like(st_ref)

    ri = lax.broadcasted_iota(jnp.int32, (c, c), 0)
    ci = lax.broadcasted_iota(jnp.int32, (c, c), 1)
    tril_ones = jnp.where(ri >= ci, 1.0, 0.0).astype(BF16)

    gh = RWKV_GROUP
    gw = gh * n
    groups = range(nh // gh)
    gs = [slice(g * gw, (g + 1) * gw) for g in groups]
    same_head = (lax.broadcasted_iota(jnp.int32, (gw, gw), 0) // n
                 == lax.broadcasted_iota(jnp.int32, (gw, gw), 1) // n)
    zero_b = jnp.zeros((gw, gw), BF16)

    def bd(y):
        yb = y.astype(BF16)
        return jnp.where(same_head, jnp.concatenate([yb] * gh, axis=0), zero_b)

    def mmb(x, y_bd):
        return _dot(x.astype(BF16), y_bd)

    ri2 = lax.broadcasted_iota(jnp.int32, (2 * c, gw), 0)
    ci2 = lax.broadcasted_iota(jnp.int32, (2 * c, gw), 1) % c
    causal2 = ci2 <= jnp.where(ri2 < c, ri2 - 1, ri2 - c)
    rc = lax.broadcasted_iota(jnp.int32, (c, gw), 0)
    cc = lax.broadcasted_iota(jnp.int32, (c, gw), 1) % c
    eye_g = jnp.where(rc == cc, 1.0, 0.0).astype(F32)
    d16_g = (rc // 16) == (cc // 16)
    d32_g = (rc // 32) == (cc // 32)
    lo_g = jnp.logical_and(d32_g, jnp.logical_not(d16_g))
    colhead = lax.broadcasted_iota(jnp.int32, (n, gw), 1) // n

    p_end, v_c, lhs, b_t, k_t, dec = [], [], [], [], [], []
    for ch in chunks:
        rows = slice(ch * c, (ch + 1) * c)
        lw = lw_ref[rows, :]
        cum = _exact_lhs_dot(tril_ones, lw)
        p_incl = jnp.exp(cum)
        p_inv = jnp.exp(-cum)
        pe = p_incl[c - 1:c, :]
        bt = b_ref[rows, :] * p_inv
        kt = k_ref[rows, :] * p_inv
        p_end.append(pe)
        v_c.append(v_ref[rows, :])
        b_t.append(bt)
        k_t.append(kt)
        lhs.append(jnp.concatenate([kk_ref[rows, :] * jnp.exp(cum - lw), r_ref[rows, :] * p_incl],
                                   axis=0).astype(BF16))
        dec.append(jnp.concatenate([bt * pe, kt * pe], axis=0).astype(BF16))

    cg = [(ch, g) for ch in chunks for g in groups]
    ab = {q: jnp.where(causal2, _dot_nt(lhs[q[0]][:, gs[q[1]]], bd(b_t[q[0]][:, gs[q[1]]])), 0.0)
          for q in cg}
    ak = {q: jnp.where(causal2, _dot_nt(lhs[q[0]][:, gs[q[1]]], bd(k_t[q[0]][:, gs[q[1]]])), 0.0)
          for q in cg}
    akv = {q: mmb(ak[q], bd(v_c[q[0]][:, gs[q[1]]])) for q in cg}
    a_ub = {q: ab[q][:c] for q in cg}
    ld = {q: jnp.where(d16_g, a_ub[q], 0.0) for q in cg}
    x = {q: eye_g - ld[q] for q in cg}
    pw = {q: mmb(ld[q], bd(ld[q])) for q in cg}
    x = {q: x[q] + mmb(x[q], bd(pw[q])) for q in cg}
    pw = {q: mmb(pw[q], bd(pw[q])) for q in cg}
    x = {q: x[q] + mmb(x[q], bd(pw[q])) for q in cg}
    pw = {q: mmb(pw[q], bd(pw[q])) for q in cg}
    x = {q: x[q] + mmb(x[q], bd(pw[q])) for q in cg}
    t = {q: mmb(x[q], bd(jnp.where(lo_g, a_ub[q], 0.0))) for q in cg}
    x = {q: x[q] - mmb(t[q], bd(x[q])) for q in cg}
    t = {q: mmb(x[q], bd(jnp.where(d32_g, 0.0, a_ub[q]))) for q in cg}
    x = {q: x[q] - mmb(t[q], bd(x[q])) for q in cg}
    u0 = {q: -mmb(x[q], bd(akv[q][:c])) for q in cg}
    wk = {q: -mmb(x[q], bd(lhs[q[0]][:c, gs[q[1]]])) for q in cg}
    lhs2 = {q: jnp.concatenate([wk[q].astype(BF16), lhs[q[0]][c:, gs[q[1]]]], axis=0) for q in cg}

    st = [st_ref[:, gs[g]] for g in groups]
    for ch in chunks:
        rows = slice(ch * c, (ch + 1) * c)
        through = [_dot_nt(lhs2[(ch, g)], bd(st[g])) for g in groups]
        u = [through[g][:c] + u0[(ch, g)] for g in groups]
        for g in groups:
            y_ref[rows, gs[g]] = through[g][c:] + akv[(ch, g)][c:] + mmb(ab[(ch, g)][c:], bd(u[g]))
        for g in groups:
            uv = jnp.concatenate([u[g], v_c[ch][:, gs[g]]], axis=0).astype(BF16)
            cross = _dot_tn(uv, dec[ch][:, gs[g]])
            upd = st[g] * p_end[ch][:, gs[g]]
            for h in range(gh):
                upd = upd + jnp.where(colhead == h, cross[h * n:(h + 1) * n], 0.0)
            st[g] = upd
    for g in groups:
        st_ref[:, gs[g]] = st[g]

    ones_bd = _head_sum_matrix()
    y = y_ref[...]
    v = v_ref[...]
    mean = _head_sums(y, ones_bd) * (1.0 / n)
    yc = y - mean
    var = _head_sums(yc * yc, ones_bd) * (1.0 / n)
    yn = yc * lax.rsqrt(var + RWKV_GN_EPS) * lnw_ref[...] + lnb_ref[...]
    bonus = _head_sums(r_ref[...] * k_ref[...] * rk_ref[...], ones_bd)
    o_ref[...] = ((yn + bonus * v) * g_ref[...]).astype(o_ref.dtype)


def _rwkv_kernel(zr_ref, zk_ref, zv_ref, zl_ref, zg_ref, mu_ref, mul_ref, mug_ref, w0_ref, w2_ref,
                 a0_ref, a2_ref, g2_ref, kk_ref, ka_ref, rk_ref, lnw_ref, lnb_ref,
                 o_ref, st_ref, y_ref, prev_ref, prevl_ref, *staged):
    _rwkv_prep_kernel(zr_ref, zk_ref, zv_ref, zl_ref, zg_ref, mu_ref, mul_ref, mug_ref, w0_ref, w2_ref,
                      a0_ref, a2_ref, g2_ref, kk_ref, ka_ref, *staged, prev_ref, prevl_ref)
    _rwkv_chunk_kernel(*staged, rk_ref, lnw_ref, lnb_ref, o_ref, st_ref, y_ref)


def rwkv_mix(z, p):
    s = z.shape[0]
    w = RWKV_WIDTH
    rows = RWKV_CHUNK * RWKV_CHUNKS_PER_STEP
    assert s % rows == 0
    row = lambda n: pl.BlockSpec((1, n), lambda i: (0, 0))
    lora_w = pl.BlockSpec((128, w), lambda i: (0, 0))
    return pl.pallas_call(
        _rwkv_kernel,
        grid=(s // rows,),
        in_specs=[
            pl.BlockSpec((rows, w), lambda i: (i, COL_R // w)),
            pl.BlockSpec((rows, w), lambda i: (i, COL_K // w)),
            pl.BlockSpec((rows, w), lambda i: (i, COL_V // w)),
            pl.BlockSpec((rows, 128), lambda i: (i, COL_LORA // 128)),
            pl.BlockSpec((rows, 128), lambda i: (i, COL_XG // 128)),
            pl.BlockSpec((3, w), lambda i: (0, 0)),
            row(128), row(128),
            row(w), lora_w,
            row(w), lora_w,
            lora_w,
            row(w), row(w),
            row(w), row(w), row(w),
        ],
        out_specs=pl.BlockSpec((rows, w), lambda i: (i, 0)),
        out_shape=jax.ShapeDtypeStruct((s, w), BF16),
        scratch_shapes=[
            pltpu.VMEM((RWKV_HEAD_DIM, w), F32),
            pltpu.VMEM((rows, w), F32),
            pltpu.VMEM((8, w), F32), pltpu.VMEM((8, 128), F32),
        ] + [pltpu.VMEM((rows, w), F32)] * 7,
        compiler_params=_cparams(("arbitrary",)),
        name="rwkv_mix",
    )(z, z, z, z, z, p["mu_rkv"], p["mu_lora"], p["mu_g"], p["w0"], p["w2"], p["a0"], p["a2"],
      p["g2"], p["k_k"], p["k_a"], p["r_k"], p["ln_w"], p["ln_b"])


def _swap_halves(w):
    half = w.shape[-1] // 2
    return jnp.concatenate([w[..., half:], w[..., :half]], axis=-1)


def _prep_w_in(w_in):
    wt = jnp.swapaxes(w_in, 1, 2)
    sizes = [SB_WIDTH] * 3 + [MLA_Q_LORA, MLA_KV_LORA, MLA_ROPE_DIM] + [RWKV_WIDTH] * 3 + [
        RWKV_DECAY_LORA, RWKV_A_LORA, RWKV_GATE_LORA]
    idx = [int(i) for i in np.cumsum(sizes)[:-1]]
    sbq, sbk, sbv, cq, ckv, krope, r, k, v, xw, xa, xg = jnp.split(wt, idx, axis=1)
    sbq = sbq * (SB_HEAD_DIM ** -0.5)
    half = MLA_ROPE_DIM // 2
    krope_sw = jnp.concatenate([krope[:, half:], krope[:, :half]], axis=1)
    rows = [r, k, v, cq, ckv, xw, xa, xg, krope, krope_sw]
    used = sum(t.shape[1] for t in rows)
    rows.append(jnp.zeros((wt.shape[0], Z_F32_COLS - used, wt.shape[2]), wt.dtype))
    rows += [sbq, sbk, sbv]
    return jnp.concatenate(rows, axis=1).astype(BF16)


def _prep_mla_w(w_uq, w_ukv):
    q = w_uq.reshape(MLA_Q_LORA, MLA_HEADS, MLA_NOPE_DIM + MLA_ROPE_DIM)
    nope, pe = q[..., :MLA_NOPE_DIM], q[..., MLA_NOPE_DIM:]
    zpad = jnp.zeros_like(pe)
    wq = jnp.concatenate([nope, pe, zpad, _swap_halves(pe), zpad], axis=-1)
    return wq.reshape(MLA_Q_LORA, MLA_HEADS * 384).astype(BF16), w_ukv.astype(BF16)


def _rope_table(positions):
    half = MLA_ROPE_DIM // 2
    inv_freq = ROPE_THETA ** (-jnp.arange(half, dtype=F32) / half)
    ang = positions.astype(F32)[:, None] * inv_freq
    cos, sin = jnp.cos(ang), jnp.sin(ang)
    z = jnp.zeros((positions.shape[0], MLA_ROPE_DIM), F32)
    return jnp.concatenate([cos, cos, z, -sin, sin, z], axis=-1)


def _pick_tile(n, candidates):
    for c in candidates:
        if n % c == 0:
            return c
    raise ValueError(f"no tile for {n}")


def kernel(x, positions, ffn1_norm, ffn1_gate, ffn1_up, ffn1_down, mix_norm, w_in, mla_q_norm,
           mla_w_uq, mla_kv_norm, mla_w_ukv, rwkv_mu, rwkv_w0, rwkv_w2, rwkv_a0, rwkv_a2, rwkv_g2,
           rwkv_k_k, rwkv_k_a, rwkv_r_k, rwkv_ln_w, rwkv_ln_b, sb_out_norm, mla_out_norm, w_out,
           ffn2_norm, ffn2_gate, ffn2_up, ffn2_down, final_norm):
    bsz, s, d = x.shape
    assert bsz == 1 and d == SB_WIDTH + MLA_WIDTH + RWKV_WIDTH
    depth = w_in.shape[0]
    d_ff = ffn1_gate.shape[-1]
    tm = _pick_tile(s, (1024, 512, 256, 128))
    tf = _pick_tile(d_ff, (512, 256, 128))
    tn_down = 512
    tm_io = _pick_tile(s, (512, 256, 128))
    x = x[0]
    tab = _rope_table(positions[0])
    w = RWKV_WIDTH
    zl = jnp.zeros((RWKV_DECAY_LORA, w), F32)
    ffn_w = [(wg, wu, wd.astype(BF16))
             for wg, wu, wd in ((ffn1_gate, ffn1_up, ffn1_down), (ffn2_gate, ffn2_up, ffn2_down))]
    w_in_b = _prep_w_in(w_in)
    w_out_b = w_out.astype(BF16)

    def ffn(x, g, ws, l):
        act = ffn_up(x, g, ws[0], ws[1], l, tm=tm, tn=tf)
        return ffn_down(x, act, ws[2], l, tm=tm, tn=tn_down)

    for l in range(depth):
        x = ffn(x, ffn1_norm[l], ffn_w[0], l)

        z, zb = norm_proj(x, mix_norm[l], w_in_b, l, tm=tm)

        y_sb = sb_attention(zb, sb_out_norm[l])

        wq, wkv = _prep_mla_w(mla_w_uq[l], mla_w_ukv[l])
        q, k, v = mla_prep(z, tab, mla_q_norm[l], mla_kv_norm[l], wq, wkv, ts=tm_io)
        y_mla = mla_attention(q, k, v, mla_out_norm[l])

        mu = rwkv_mu[l]
        p = {
            "mu_rkv": mu[:3 * w].reshape(3, w),
            "mu_lora": mu[3 * w:3 * w + 128].reshape(1, 128),
            "mu_g": mu[3 * w + 128:].reshape(1, 128),
            "w0": rwkv_w0[l].reshape(1, w),
            "w2": jnp.concatenate([rwkv_w2[l], zl], axis=0).astype(BF16),
            "a0": rwkv_a0[l].reshape(1, w),
            "a2": jnp.concatenate([zl, rwkv_a2[l]], axis=0).astype(BF16),
            "g2": rwkv_g2[l].astype(BF16),
            "k_k": rwkv_k_k[l].reshape(1, w),
            "k_a": rwkv_k_a[l].reshape(1, w),
            "r_k": rwkv_r_k[l].reshape(1, w),
            "ln_w": rwkv_ln_w[l].reshape(1, w),
            "ln_b": rwkv_ln_b[l].reshape(1, w),
        }
        y_rwkv = rwkv_mix(z, p)

        x = out_proj(x, y_sb, y_mla, y_rwkv, w_out_b, l, tm=tm_io)

        x = ffn(x, ffn2_norm[l], ffn_w[1], l)

    return final_rmsnorm(x, final_norm, tm=tm_io)[None]
```

```python
import jax
import jax.numpy as jnp
import numpy as np
from jax import lax
from jax.experimental import pallas as pl
from jax.experimental.pallas import tpu as pltpu

F32 = jnp.float32
BF16 = jnp.bfloat16

NORM_EPS = 1e-6
SB_HEADS = 4
SB_HEAD_DIM = 128
SB_WIDTH = SB_HEADS * SB_HEAD_DIM
MLA_HEADS = 4
MLA_NOPE_DIM = 128
MLA_ROPE_DIM = 64
MLA_V_DIM = 128
MLA_Q_LORA = 512
MLA_KV_LORA = 256
MLA_WIDTH = MLA_HEADS * MLA_V_DIM
MLA_QK_PAD = 256
ROPE_THETA = 10000.0
RWKV_HEADS = 16
RWKV_HEAD_DIM = 64
RWKV_WIDTH = RWKV_HEADS * RWKV_HEAD_DIM
RWKV_DECAY_LORA = 64
RWKV_A_LORA = 64
RWKV_GATE_LORA = 128
RWKV_GN_EPS = 64e-5
RWKV_CHUNK = 64
RWKV_GROUP = 4
RWKV_CHUNKS_PER_STEP = 4

COL_R, COL_K, COL_V = 0, 1024, 2048
COL_CQ, COL_CKV = 3072, 3584
COL_LORA, COL_XG, COL_ROPE = 3840, 3968, 4096
Z_F32_COLS = 4608
COL_SBQ, COL_SBK, COL_SBV = 0, 512, 1024
Z_B16_COLS = 1536
Z_TILE = 768

VMEM_LIMIT = 48 * 1024 * 1024


def _cparams(sem):
    return pltpu.CompilerParams(dimension_semantics=sem, vmem_limit_bytes=VMEM_LIMIT)


def _dot(a, b):
    return jnp.dot(a, b, preferred_element_type=F32)


def _dot_nt(a, b):
    return lax.dot_general(a, b, (((1,), (1,)), ((), ())), preferred_element_type=F32)


def _dot_tn(a, b):
    return lax.dot_general(a, b, (((0,), (0,)), ((), ())), preferred_element_type=F32)


def _split3(x):
    hi = x.astype(BF16)
    r1 = x - hi.astype(F32)
    mid = r1.astype(BF16)
    lo = (r1 - mid.astype(F32)).astype(BF16)
    return hi, mid, lo


def _split2(x):
    hi = x.astype(BF16)
    lo = (x - hi.astype(F32)).astype(BF16)
    return hi, lo


def _exact_lhs_dot(m_bf16, x):
    hi, mid, lo = _split3(x)
    return _dot(m_bf16, hi) + (_dot(m_bf16, mid) + _dot(m_bf16, lo))


def _rms_rows(x, g):
    ms = jnp.mean(x * x, axis=-1, keepdims=True)
    return x * lax.rsqrt(ms + NORM_EPS) * g


def _norm_proj_kernel(x_ref, g_ref, w_ref, o_ref, ob_ref, h_ref):
    @pl.when(pl.program_id(1) == 0)
    def _():
        h_ref[...] = _rms_rows(x_ref[...], g_ref[...]).astype(BF16)

    z = _dot_nt(h_ref[...], w_ref[...])
    nf = Z_F32_COLS // Z_TILE

    @pl.when(pl.program_id(1) < nf)
    def _():
        o_ref[...] = z

    @pl.when(pl.program_id(1) >= nf)
    def _():
        ob_ref[...] = z.astype(BF16)


def norm_proj(x, g, w, l, *, tm):
    s, d = x.shape
    tn = Z_TILE
    nf, nb = Z_F32_COLS // tn, Z_B16_COLS // tn
    return pl.pallas_call(
        _norm_proj_kernel,
        grid=(s // tm, nf + nb),
        in_specs=[
            pl.BlockSpec((tm, d), lambda i, j: (i, 0)),
            pl.BlockSpec((1, d), lambda i, j: (0, 0)),
            pl.BlockSpec((None, tn, d), lambda i, j: (l, j, 0)),
        ],
        out_specs=[
            pl.BlockSpec((tm, tn), lambda i, j: (i, jnp.minimum(j, nf - 1))),
            pl.BlockSpec((tm, tn), lambda i, j: (i, jnp.maximum(j - nf, 0))),
        ],
        out_shape=[jax.ShapeDtypeStruct((s, Z_F32_COLS), F32),
                   jax.ShapeDtypeStruct((s, Z_B16_COLS), BF16)],
        scratch_shapes=[pltpu.VMEM((tm, d), BF16)],
        compiler_params=_cparams(("parallel", "arbitrary")),
        name="norm_proj",
    )(x, g.reshape(1, d), w)


def _ffn_up_kernel(x_ref, g_ref, wg_ref, wu_ref, o_ref, h_ref):
    @pl.when(pl.program_id(1) == 0)
    def _():
        h_ref[...] = _rms_rows(x_ref[...], g_ref[...]).astype(BF16)

    h = h_ref[...]
    a = _dot(h, wg_ref[...].astype(BF16))
    u = _dot(h, wu_ref[...].astype(BF16))
    o_ref[...] = (a * jax.nn.sigmoid(a) * u).astype(o_ref.dtype)


def ffn_up(x, g, wg, wu, l, *, tm, tn):
    s, d = x.shape
    f = wg.shape[2]
    return pl.pallas_call(
        _ffn_up_kernel,
        grid=(s // tm, f // tn),
        in_specs=[
            pl.BlockSpec((tm, d), lambda i, j: (i, 0)),
            pl.BlockSpec((1, d), lambda i, j: (0, 0)),
            pl.BlockSpec((None, d, tn), lambda i, j: (l, 0, j)),
            pl.BlockSpec((None, d, tn), lambda i, j: (l, 0, j)),
        ],
        out_specs=pl.BlockSpec((tm, tn), lambda i, j: (i, j)),
        out_shape=jax.ShapeDtypeStruct((s, f), BF16),
        scratch_shapes=[pltpu.VMEM((tm, d), BF16)],
        compiler_params=_cparams(("parallel", "arbitrary")),
        name="ffn_up",
    )(x, g.reshape(1, d), wg, wu)


def _ffn_down_kernel(x_ref, a_ref, w_ref, o_ref):
    o_ref[...] = x_ref[...] + 0.5 * _dot(a_ref[...], w_ref[...])


def ffn_down(x, act, wd, l, *, tm, tn):
    s, d = x.shape
    f = act.shape[1]
    return pl.pallas_call(
        _ffn_down_kernel,
        grid=(s // tm, d // tn),
        in_specs=[
            pl.BlockSpec((tm, tn), lambda i, j: (i, j)),
            pl.BlockSpec((tm, f), lambda i, j: (i, 0)),
            pl.BlockSpec((None, f, tn), lambda i, j: (l, 0, j)),
        ],
        out_specs=pl.BlockSpec((tm, tn), lambda i, j: (i, j)),
        out_shape=jax.ShapeDtypeStruct((s, d), F32),
        compiler_params=_cparams(("parallel", "arbitrary")),
        name="ffn_down",
    )(x, act, wd)


def _out_proj_kernel(x_ref, a_ref, b_ref, c_ref, wa_ref, wb_ref, wc_ref, o_ref):
    acc = _dot(a_ref[...], wa_ref[...])
    acc += _dot(b_ref[...], wb_ref[...])
    acc += _dot(c_ref[...], wc_ref[...])
    o_ref[...] = x_ref[...] + acc


def out_proj(x, y_sb, y_mla, y_rwkv, w_out, l, *, tm):
    s, d = x.shape
    assert SB_WIDTH == MLA_WIDTH and RWKV_WIDTH == SB_WIDTH + MLA_WIDTH
    wspec = lambda rows, blk: pl.BlockSpec((None, rows, d), lambda i: (l, blk, 0),
                                           pipeline_mode=pl.Buffered(1))
    return pl.pallas_call(
        _out_proj_kernel,
        grid=(s // tm,),
        in_specs=[
            pl.BlockSpec((tm, d), lambda i: (i, 0)),
            pl.BlockSpec((tm, SB_WIDTH), lambda i: (i, 0)),
            pl.BlockSpec((tm, MLA_WIDTH), lambda i: (i, 0)),
            pl.BlockSpec((tm, RWKV_WIDTH), lambda i: (i, 0)),
            wspec(SB_WIDTH, 0),
            wspec(MLA_WIDTH, 1),
            wspec(RWKV_WIDTH, 1),
        ],
        out_specs=pl.BlockSpec((tm, d), lambda i: (i, 0)),
        out_shape=jax.ShapeDtypeStruct((s, d), F32),
        compiler_params=_cparams(("parallel",)),
        name="out_proj",
    )(x, y_sb, y_mla, y_rwkv, w_out, w_out, w_out)


def _final_norm_kernel(x_ref, g_ref, o_ref):
    o_ref[...] = _rms_rows(x_ref[...], g_ref[...])


def final_rmsnorm(x, g, *, tm):
    s, d = x.shape
    return pl.pallas_call(
        _final_norm_kernel,
        grid=(s // tm,),
        in_specs=[pl.BlockSpec((tm, d), lambda i: (i, 0)), pl.BlockSpec((1, d), lambda i: (0, 0))],
        out_specs=pl.BlockSpec((tm, d), lambda i: (i, 0)),
        out_shape=jax.ShapeDtypeStruct((s, d), F32),
        compiler_params=_cparams(("parallel",)),
        name="final_norm",
    )(x, g.reshape(1, d))


SB_BQ = 256
SB_BK = 128
SB_DEAD_LOG = -104.0


def _sb_attn_kernel(q_ref, k_ref, v_ref, g_ref, o_ref):
    i = pl.program_id(0)
    bq, bk, dh = SB_BQ, SB_BK, SB_HEAD_DIM
    heads = range(SB_HEADS)
    hs = [slice(h * dh, (h + 1) * dh) for h in heads]
    q = [q_ref[:, hs[h]] for h in heads]
    row = lax.broadcasted_iota(jnp.int32, (bq, bk), 0)
    col = lax.broadcasted_iota(jnp.int32, (bq, bk), 1)
    kr = lax.broadcasted_iota(jnp.int32, (bk, bk), 0)
    kc = lax.broadcasted_iota(jnp.int32, (bk, bk), 1)
    upper = jnp.where(kr > kc, 1.0, 0.0).astype(BF16)

    def block(kb, carry, acc, masked):
        start = pl.multiple_of(kb * bk, bk)
        z = [_dot_nt(q[h], k_ref[pl.ds(start, bk), hs[h]]) for h in heads]
        log_beta = [jnp.minimum(z[h], 0.0) - jnp.log(1.0 + jnp.exp(-jnp.abs(z[h]))) for h in heads]
        log_keep = [log_beta[h] - z[h] for h in heads]
        if masked:
            before = (start + col) < (i * bq + row)
            log_keep = [jnp.where(before, log_keep[h], 0.0) for h in heads]
        split = [_split2(log_keep[h]) for h in heads]
        within = [_dot(split[h][0], upper) + _dot(split[h][1], upper) for h in heads]
        a = [jnp.exp(log_beta[h] + (carry[h] + within[h])) for h in heads]
        if masked:
            a = [jnp.where(before, a[h], 0.0) for h in heads]
        acc = [acc[h] + _dot(a[h].astype(BF16), v_ref[pl.ds(start, bk), hs[h]]) for h in heads]
        carry = [carry[h] + jnp.sum(log_keep[h], axis=-1, keepdims=True) for h in heads]
        return carry, acc

    def alive_of(carry):
        m = jnp.max(carry[0])
        for h in heads[1:]:
            m = jnp.maximum(m, jnp.max(carry[h]))
        return m

    carry = [jnp.zeros((bq, 1), F32) for _ in heads]
    acc = [jnp.zeros((bq, dh), F32) for _ in heads]
    nd = bq // bk
    first = i * nd
    for d in range(nd - 1, -1, -1):
        carry, acc = block(first + d, carry, acc, True)

    def cond(st):
        return jnp.logical_and(st[0] >= 0, st[1] > SB_DEAD_LOG)

    def body(st):
        kb, _, carry, acc = st
        carry, acc = block(kb, list(carry), list(acc), False)
        return kb - 1, alive_of(carry), tuple(carry), tuple(acc)

    st = lax.while_loop(cond, body, (first - 1, alive_of(carry), tuple(carry), tuple(acc)))
    acc = st[3]
    for h in heads:
        o_ref[:, hs[h]] = _rms_rows(acc[h], g_ref[h:h + 1, :]).astype(o_ref.dtype)


def sb_attention(zb, gains):
    s = zb.shape[0]
    assert s % SB_BQ == 0 and SB_BQ % SB_BK == 0
    whole = lambda c: pl.BlockSpec((s, SB_WIDTH), lambda i: (0, c // SB_WIDTH),
                                   pipeline_mode=pl.Buffered(1))
    return pl.pallas_call(
        _sb_attn_kernel,
        grid=(s // SB_BQ,),
        in_specs=[
            pl.BlockSpec((SB_BQ, SB_WIDTH), lambda i: (i, COL_SBQ // SB_WIDTH)),
            whole(COL_SBK),
            whole(COL_SBV),
            pl.BlockSpec((SB_HEADS, SB_HEAD_DIM), lambda i: (0, 0)),
        ],
        out_specs=pl.BlockSpec((SB_BQ, SB_WIDTH), lambda i: (i, 0)),
        out_shape=jax.ShapeDtypeStruct((s, SB_WIDTH), BF16),
        compiler_params=_cparams(("arbitrary",)),
        name="sb_attention",
    )(zb, zb, zb, gains)


def _mla_prep_kernel(cq_ref, ckv_ref, rope_ref, tab_ref, qn_ref, kvn_ref, wq_ref, wkv_ref,
                     q_ref, k_ref, v_ref):
    scale = float((MLA_NOPE_DIM + MLA_ROPE_DIM) ** -0.5 * np.log2(np.e))
    hq = _rms_rows(cq_ref[...], qn_ref[...]).astype(BF16)
    hkv = _rms_rows(ckv_ref[...], kvn_ref[...]).astype(BF16)
    cos2 = tab_ref[:, :128]
    sin2 = tab_ref[:, 128:]
    kr = rope_ref[...]
    k_pe = kr * cos2 + pltpu.roll(kr, 64, 1) * sin2
    k_pe = k_pe.astype(BF16)
    qall = _dot(hq, wq_ref[...])
    kvall = _dot(hkv, wkv_ref[...])
    for h in range(MLA_HEADS):
        qh = qall[:, h * 384:(h + 1) * 384]
        q_ref[h, :, :128] = (qh[:, :128] * scale).astype(BF16)
        q_ref[h, :, 128:] = ((qh[:, 128:256] * cos2 + qh[:, 256:384] * sin2) * scale).astype(BF16)
        k_ref[h, :, :128] = kvall[:, h * 256:h * 256 + 128].astype(BF16)
        k_ref[h, :, 128:] = k_pe
        v_ref[h, :, :128] = kvall[:, h * 256 + 128:(h + 1) * 256].astype(BF16)
        v_ref[h, :, 128:] = jnp.ones((kr.shape[0], 128), BF16)


def mla_prep(z, tab, q_norm, kv_norm, wq, wkv, *, ts):
    s = z.shape[0]
    return pl.pallas_call(
        _mla_prep_kernel,
        grid=(s // ts,),
        in_specs=[
            pl.BlockSpec((ts, MLA_Q_LORA), lambda i: (i, COL_CQ // MLA_Q_LORA)),
            pl.BlockSpec((ts, MLA_KV_LORA), lambda i: (i, COL_CKV // MLA_KV_LORA)),
            pl.BlockSpec((ts, 128), lambda i: (i, COL_ROPE // 128)),
            pl.BlockSpec((ts, 256), lambda i: (i, 0)),
            pl.BlockSpec((1, MLA_Q_LORA), lambda i: (0, 0)),
            pl.BlockSpec((1, MLA_KV_LORA), lambda i: (0, 0)),
            pl.BlockSpec(wq.shape, lambda i: (0, 0)),
            pl.BlockSpec(wkv.shape, lambda i: (0, 0)),
        ],
        out_specs=[
            pl.BlockSpec((MLA_HEADS, ts, MLA_QK_PAD), lambda i: (0, i, 0)),
            pl.BlockSpec((MLA_HEADS, ts, MLA_QK_PAD), lambda i: (0, i, 0)),
            pl.BlockSpec((MLA_HEADS, ts, 2 * MLA_V_DIM), lambda i: (0, i, 0)),
        ],
        out_shape=[
            jax.ShapeDtypeStruct((MLA_HEADS, s, MLA_QK_PAD), BF16),
            jax.ShapeDtypeStruct((MLA_HEADS, s, MLA_QK_PAD), BF16),
            jax.ShapeDtypeStruct((MLA_HEADS, s, 2 * MLA_V_DIM), BF16),
        ],
        compiler_params=_cparams(("parallel",)),
        name="mla_prep",
    )(z, z, z, tab, q_norm.reshape(1, -1), kv_norm.reshape(1, -1), wq, wkv)


MLA_BQ = 256
MLA_BK = 1024
MLA_HEADS_PER_STEP = 4


def _mla_attn_kernel(q_ref, k_ref, v_ref, g_ref, o_ref):
    i = pl.program_id(1)
    bq, bk = MLA_BQ, MLA_BK
    heads = range(MLA_HEADS_PER_STEP)
    q = [q_ref[h] for h in heads]
    nd = bk // bq

    def block(kb, m, acc, width, masked):
        start = pl.multiple_of(kb * bk, bk)
        sc = [_dot_nt(q[h], k_ref[h, pl.ds(start, width), :]) for h in heads]
        if masked:
            row = lax.broadcasted_iota(jnp.int32, (bq, width), 0) + i * bq
            col = lax.broadcasted_iota(jnp.int32, (bq, width), 1) + start
            sc = [jnp.where(col <= row, sc[h], -1e30) for h in heads]
        m_new = [jnp.maximum(m[h], jnp.max(sc[h], axis=-1, keepdims=True)) for h in heads]
        for h in heads:
            alpha = jnp.exp2(m[h] - m_new[h])
            p = jnp.exp2(sc[h] - m_new[h])
            acc[h] = acc[h] * alpha + _dot(p.astype(BF16), v_ref[h, pl.ds(start, width), :])
        return m_new, acc

    def body(kb, st):
        m, acc = block(kb, list(st[0]), list(st[1]), bk, False)
        return tuple(m), tuple(acc)

    m0 = tuple(jnp.full((bq, 1), -1e30, F32) for _ in heads)
    a0 = tuple(jnp.zeros((bq, 2 * MLA_V_DIM), F32) for _ in heads)
    last = i // nd
    m, acc = lax.fori_loop(0, last, body, (m0, a0))

    for r in range(nd):
        @pl.when(i % nd == r)
        def _(r=r):
            _, acc_r = block(last, list(m), list(acc), (r + 1) * bq, True)
            for h in heads:
                y = acc_r[h][:, :MLA_V_DIM] / acc_r[h][:, MLA_V_DIM:]
                o_ref[:, h * MLA_V_DIM:(h + 1) * MLA_V_DIM] = _rms_rows(
                    y, g_ref[h:h + 1, :]).astype(o_ref.dtype)


def mla_attention(q, k, v, gains):
    s = q.shape[1]
    assert s % MLA_BK == 0 and MLA_BK % MLA_BQ == 0
    hps = MLA_HEADS_PER_STEP
    whole = lambda n: pl.BlockSpec((hps, s, n), lambda g, i: (g, 0, 0), pipeline_mode=pl.Buffered(1))
    return pl.pallas_call(
        _mla_attn_kernel,
        grid=(MLA_HEADS // hps, s // MLA_BQ),
        in_specs=[
            pl.BlockSpec((hps, MLA_BQ, MLA_QK_PAD), lambda g, i: (g, i, 0)),
            whole(MLA_QK_PAD),
            whole(2 * MLA_V_DIM),
            pl.BlockSpec((hps, MLA_V_DIM), lambda g, i: (g, 0)),
        ],
        out_specs=pl.BlockSpec((MLA_BQ, hps * MLA_V_DIM), lambda g, i: (i, g)),
        out_shape=jax.ShapeDtypeStruct((s, MLA_WIDTH), BF16),
        compiler_params=_cparams(("parallel", "arbitrary")),
        name="mla_attention",
    )(q, k, v, gains)


def _head_sum_matrix():
    r = lax.broadcasted_iota(jnp.int32, (128, 128), 0) // RWKV_HEAD_DIM
    c = lax.broadcasted_iota(jnp.int32, (128, 128), 1) // RWKV_HEAD_DIM
    return jnp.where(r == c, 1.0, 0.0).astype(BF16)


def _head_sums(x, ones_bd):
    parts = []
    for j in range(x.shape[1] // 128):
        hi, lo = _split2(x[:, j * 128:(j + 1) * 128])
        parts.append(_dot(hi, ones_bd) + _dot(lo, ones_bd))
    return jnp.concatenate(parts, axis=-1)


def _rwkv_prep_kernel(r_ref, k_ref, v_ref, lora_ref, xg_ref, mu_ref, mul_ref, mug_ref,
                      w0_ref, w2_ref, a0_ref, a2_ref, g2_ref, kk_ref, ka_ref,
                      ro_ref, lw_ref, ko_ref, vo_ref, kko_ref, bo_ref, go_ref,
                      prev_ref, prevl_ref):
    ts = r_ref.shape[0]

    @pl.when(pl.program_id(0) == 0)
    def _():
        prev_ref[...] = jnp.zeros_like(prev_ref)
        prevl_ref[...] = jnp.zeros_like(prevl_ref)

    first_row = lax.broadcasted_iota(jnp.int32, (ts, 1), 0) == 0

    def shift_mix(x, prev_row, mu):
        x_prev = jnp.where(first_row, prev_row, pltpu.roll(x, 1, 0))
        return x + (x_prev - x) * mu

    r_in, k_in, v_in = r_ref[...], k_ref[...], v_ref[...]
    lora_in, xg_in = lora_ref[...], xg_ref[...]
    r = shift_mix(r_in, prev_ref[0:1, :], mu_ref[0:1, :])
    k = shift_mix(k_in, prev_ref[1:2, :], mu_ref[1:2, :])
    v = shift_mix(v_in, prev_ref[2:3, :], mu_ref[2:3, :])
    lora = shift_mix(lora_in, prevl_ref[0:1, :], mul_ref[...])
    xg = shift_mix(xg_in, prevl_ref[1:2, :], mug_ref[...])
    prev_ref[0:1, :] = r_in[ts - 1:ts, :]
    prev_ref[1:2, :] = k_in[ts - 1:ts, :]
    prev_ref[2:3, :] = v_in[ts - 1:ts, :]
    prevl_ref[0:1, :] = lora_in[ts - 1:ts, :]
    prevl_ref[1:2, :] = xg_in[ts - 1:ts, :]

    dw = w0_ref[...] + _dot(jnp.tanh(lora).astype(BF16), w2_ref[...])
    log_w = jnp.minimum(dw, 0.0) - jnp.log(1.0 + jnp.exp(-jnp.abs(dw))) - 0.5
    lw_ref[...] = -jnp.exp(log_w)
    a = jax.nn.sigmoid(a0_ref[...] + _dot(lora.astype(BF16), a2_ref[...]))
    go_ref[...] = _dot(jax.nn.sigmoid(xg).astype(BF16), g2_ref[...])

    kk = k * kk_ref[...]
    ss = _head_sums(kk * kk, _head_sum_matrix())
    kk = kk * lax.rsqrt(jnp.maximum(ss, 1e-24))
    ro_ref[...] = r
    vo_ref[...] = v
    ko_ref[...] = k * (1.0 + (a - 1.0) * ka_ref[...])
    kko_ref[...] = kk
    bo_ref[...] = kk * a


def _rwkv_chunk_kernel(r_ref, lw_ref, k_ref, v_ref, kk_ref, b_ref, g_ref, rk_ref, lnw_ref, lnb_ref,
                       o_ref, st_ref, y_ref):
    c = RWKV_CHUNK
    n = RWKV_HEAD_DIM
    nh = r_ref.shape[1] // n
    chunks = range(r_ref.shape[0] // c)

    @pl.when(pl.program_id(0) == 0)
    def _():
        st_ref[...] = jnp.zeros_like(st_ref)

    ri = lax.broadcasted_iota(jnp.int32, (c, c), 0)
    ci = lax.broadcasted_iota(jnp.int32, (c, c), 1)
    tril_ones = jnp.where(ri >= ci, 1.0, 0.0).astype(BF16)

    gh = RWKV_GROUP
    gw = gh * n
    groups = range(nh // gh)
    gs = [slice(g * gw, (g + 1) * gw) for g in groups]
    same_head = (lax.broadcasted_iota(jnp.int32, (gw, gw), 0) // n
                 == lax.broadcasted_iota(jnp.int32, (gw, gw), 1) // n)
    zero_b = jnp.zeros((gw, gw), BF16)

    def bd(y):
        yb = y.astype(BF16)
        return jnp.where(same_head, jnp.concatenate([yb] * gh, axis=0), zero_b)

    def mmb(x, y_bd):
        return _dot(x.astype(BF16), y_bd)

    ri2 = lax.broadcasted_iota(jnp.int32, (2 * c, gw), 0)
    ci2 = lax.broadcasted_iota(jnp.int32, (2 * c, gw), 1) % c
    causal2 = ci2 <= jnp.where(ri2 < c, ri2 - 1, ri2 - c)
    rc = lax.broadcasted_iota(jnp.int32, (c, gw), 0)
    cc = lax.broadcasted_iota(jnp.int32, (c, gw), 1) % c
    eye_g = jnp.where(rc == cc, 1.0, 0.0).astype(F32)
    d16_g = (rc // 16) == (cc // 16)
    d32_g = (rc // 32) == (cc // 32)
    lo_g = jnp.logical_and(d32_g, jnp.logical_not(d16_g))
    colhead = lax.broadcasted_iota(jnp.int32, (n, gw), 1) // n

    p_end, v_c, lhs, b_t, k_t, dec = [], [], [], [], [], []
    for ch in chunks:
        rows = slice(ch * c, (ch + 1) * c)
        lw = lw_ref[rows, :]
        cum = _exact_lhs_dot(tril_ones, lw)
        p_incl = jnp.exp(cum)
        p_inv = jnp.exp(-cum)
        pe = p_incl[c - 1:c, :]
        bt = b_ref[rows, :] * p_inv
        kt = k_ref[rows, :] * p_inv
        p_end.append(pe)
        v_c.append(v_ref[rows, :])
        b_t.append(bt)
        k_t.append(kt)
        lhs.append(jnp.concatenate([kk_ref[rows, :] * jnp.exp(cum - lw), r_ref[rows, :] * p_incl],
                                   axis=0).astype(BF16))
        dec.append(jnp.concatenate([bt * pe, kt * pe], axis=0).astype(BF16))

    cg = [(ch, g) for ch in chunks for g in groups]
    ab = {q: jnp.where(causal2, _dot_nt(lhs[q[0]][:, gs[q[1]]], bd(b_t[q[0]][:, gs[q[1]]])), 0.0)
          for q in cg}
    ak = {q: jnp.where(causal2, _dot_nt(lhs[q[0]][:, gs[q[1]]], bd(k_t[q[0]][:, gs[q[1]]])), 0.0)
          for q in cg}
    akv = {q: mmb(ak[q], bd(v_c[q[0]][:, gs[q[1]]])) for q in cg}
    a_ub = {q: ab[q][:c] for q in cg}
    ld = {q: jnp.where(d16_g, a_ub[q], 0.0) for q in cg}
    x = {q: eye_g - ld[q] for q in cg}
    pw = {q: mmb(ld[q], bd(ld[q])) for q in cg}
    x = {q: x[q] + mmb(x[q], bd(pw[q])) for q in cg}
    pw = {q: mmb(pw[q], bd(pw[q])) for q in cg}
    x = {q: x[q] + mmb(x[q], bd(pw[q])) for q in cg}
    pw = {q: mmb(pw[q], bd(pw[q])) for q in cg}
    x = {q: x[q] + mmb(x[q], bd(pw[q])) for q in cg}
    t = {q: mmb(x[q], bd(jnp.where(lo_g, a_ub[q], 0.0))) for q in cg}
    x = {q: x[q] - mmb(t[q], bd(x[q])) for q in cg}
    t = {q: mmb(x[q], bd(jnp.where(d32_g, 0.0, a_ub[q]))) for q in cg}
    x = {q: x[q] - mmb(t[q], bd(x[q])) for q in cg}
    u0 = {q: -mmb(x[q], bd(akv[q][:c])) for q in cg}
    wk = {q: -mmb(x[q], bd(lhs[q[0]][:c, gs[q[1]]])) for q in cg}
    lhs2 = {q: jnp.concatenate([wk[q].astype(BF16), lhs[q[0]][c:, gs[q[1]]]], axis=0) for q in cg}

    st = [st_ref[:, gs[g]] for g in groups]
    for ch in chunks:
        rows = slice(ch * c, (ch + 1) * c)
        through = [_dot_nt(lhs2[(ch, g)], bd(st[g])) for g in groups]
        u = [through[g][:c] + u0[(ch, g)] for g in groups]
        for g in groups:
            y_ref[rows, gs[g]] = through[g][c:] + akv[(ch, g)][c:] + mmb(ab[(ch, g)][c:], bd(u[g]))
        for g in groups:
            uv = jnp.concatenate([u[g], v_c[ch][:, gs[g]]], axis=0).astype(BF16)
            cross = _dot_tn(uv, dec[ch][:, gs[g]])
            upd = st[g] * p_end[ch][:, gs[g]]
            for h in range(gh):
                upd = upd + jnp.where(colhead == h, cross[h * n:(h + 1) * n], 0.0)
            st[g] = upd
    for g in groups:
        st_ref[:, gs[g]] = st[g]

    ones_bd = _head_sum_matrix()
    y = y_ref[...]
    v = v_ref[...]
    mean = _head_sums(y, ones_bd) * (1.0 / n)
    yc = y - mean
    var = _head_sums(yc * yc, ones_bd) * (1.0 / n)
    yn = yc * lax.rsqrt(var + RWKV_GN_EPS) * lnw_ref[...] + lnb_ref[...]
    bonus = _head_sums(r_ref[...] * k_ref[...] * rk_ref[...], ones_bd)
    o_ref[...] = ((yn + bonus * v) * g_ref[...]).astype(o_ref.dtype)


def _rwkv_kernel(zr_ref, zk_ref, zv_ref, zl_ref, zg_ref, mu_ref, mul_ref, mug_ref, w0_ref, w2_ref,
                 a0_ref, a2_ref, g2_ref, kk_ref, ka_ref, rk_ref, lnw_ref, lnb_ref,
                 o_ref, st_ref, y_ref, prev_ref, prevl_ref, *staged):
    _rwkv_prep_kernel(zr_ref, zk_ref, zv_ref, zl_ref, zg_ref, mu_ref, mul_ref, mug_ref, w0_ref, w2_ref,
                      a0_ref, a2_ref, g2_ref, kk_ref, ka_ref, *staged, prev_ref, prevl_ref)
    _rwkv_chunk_kernel(*staged, rk_ref, lnw_ref, lnb_ref, o_ref, st_ref, y_ref)


def rwkv_mix(z, p):
    s = z.shape[0]
    w = RWKV_WIDTH
    rows = RWKV_CHUNK * RWKV_CHUNKS_PER_STEP
    assert s % rows == 0
    row = lambda n: pl.BlockSpec((1, n), lambda i: (0, 0))
    lora_w = pl.BlockSpec((128, w), lambda i: (0, 0))
    return pl.pallas_call(
        _rwkv_kernel,
        grid=(s // rows,),
        in_specs=[
            pl.BlockSpec((rows, w), lambda i: (i, COL_R // w)),
            pl.BlockSpec((rows, w), lambda i: (i, COL_K // w)),
            pl.BlockSpec((rows, w), lambda i: (i, COL_V // w)),
            pl.BlockSpec((rows, 128), lambda i: (i, COL_LORA // 128)),
            pl.BlockSpec((rows, 128), lambda i: (i, COL_XG // 128)),
            pl.BlockSpec((3, w), lambda i: (0, 0)),
            row(128), row(128),
            row(w), lora_w,
            row(w), lora_w,
            lora_w,
            row(w), row(w),
            row(w), row(w), row(w),
        ],
        out_specs=pl.BlockSpec((rows, w), lambda i: (i, 0)),
        out_shape=jax.ShapeDtypeStruct((s, w), BF16),
        scratch_shapes=[
            pltpu.VMEM((RWKV_HEAD_DIM, w), F32),
            pltpu.VMEM((rows, w), F32),
            pltpu.VMEM((8, w), F32), pltpu.VMEM((8, 128), F32),
        ] + [pltpu.VMEM((rows, w), F32)] * 7,
        compiler_params=_cparams(("arbitrary",)),
        name="rwkv_mix",
    )(z, z, z, z, z, p["mu_rkv"], p["mu_lora"], p["mu_g"], p["w0"], p["w2"], p["a0"], p["a2"],
      p["g2"], p["k_k"], p["k_a"], p["r_k"], p["ln_w"], p["ln_b"])


def _swap_halves(w):
    half = w.shape[-1] // 2
    return jnp.concatenate([w[..., half:], w[..., :half]], axis=-1)


def _prep_w_in(w_in):
    wt = jnp.swapaxes(w_in, 1, 2)
    sizes = [SB_WIDTH] * 3 + [MLA_Q_LORA, MLA_KV_LORA, MLA_ROPE_DIM] + [RWKV_WIDTH] * 3 + [
        RWKV_DECAY_LORA, RWKV_A_LORA, RWKV_GATE_LORA]
    idx = [int(i) for i in np.cumsum(sizes)[:-1]]
    sbq, sbk, sbv, cq, ckv, krope, r, k, v, xw, xa, xg = jnp.split(wt, idx, axis=1)
    sbq = sbq * (SB_HEAD_DIM ** -0.5)
    half = MLA_ROPE_DIM // 2
    krope_sw = jnp.concatenate([krope[:, half:], krope[:, :half]], axis=1)
    rows = [r, k, v, cq, ckv, xw, xa, xg, krope, krope_sw]
    used = sum(t.shape[1] for t in rows)
    rows.append(jnp.zeros((wt.shape[0], Z_F32_COLS - used, wt.shape[2]), wt.dtype))
    rows += [sbq, sbk, sbv]
    return jnp.concatenate(rows, axis=1).astype(BF16)


def _prep_mla_w(w_uq, w_ukv):
    q = w_uq.reshape(MLA_Q_LORA, MLA_HEADS, MLA_NOPE_DIM + MLA_ROPE_DIM)
    nope, pe = q[..., :MLA_NOPE_DIM], q[..., MLA_NOPE_DIM:]
    zpad = jnp.zeros_like(pe)
    wq = jnp.concatenate([nope, pe, zpad, _swap_halves(pe), zpad], axis=-1)
    return wq.reshape(MLA_Q_LORA, MLA_HEADS * 384).astype(BF16), w_ukv.astype(BF16)


def _rope_table(positions):
    half = MLA_ROPE_DIM // 2
    inv_freq = ROPE_THETA ** (-jnp.arange(half, dtype=F32) / half)
    ang = positions.astype(F32)[:, None] * inv_freq
    cos, sin = jnp.cos(ang), jnp.sin(ang)
    z = jnp.zeros((positions.shape[0], MLA_ROPE_DIM), F32)
    return jnp.concatenate([cos, cos, z, -sin, sin, z], axis=-1)


def _pick_tile(n, candidates):
    for c in candidates:
        if n % c == 0:
            return c
    raise ValueError(f"no tile for {n}")


def kernel(x, positions, ffn1_norm, ffn1_gate, ffn1_up, ffn1_down, mix_norm, w_in, mla_q_norm,
           mla_w_uq, mla_kv_norm, mla_w_ukv, rwkv_mu, rwkv_w0, rwkv_w2, rwkv_a0, rwkv_a2, rwkv_g2,
           rwkv_k_k, rwkv_k_a, rwkv_r_k, rwkv_ln_w, rwkv_ln_b, sb_out_norm, mla_out_norm, w_out,
           ffn2_norm, ffn2_gate, ffn2_up, ffn2_down, final_norm):
    bsz, s, d = x.shape
    assert bsz == 1 and d == SB_WIDTH + MLA_WIDTH + RWKV_WIDTH
    depth = w_in.shape[0]
    d_ff = ffn1_gate.shape[-1]
    tm = _pick_tile(s, (1024, 512, 256, 128))
    tf = _pick_tile(d_ff, (512, 256, 128))
    tn_down = 512
    tm_io = _pick_tile(s, (512, 256, 128))
    x = x[0]
    tab = _rope_table(positions[0])
    w = RWKV_WIDTH
    zl = jnp.zeros((RWKV_DECAY_LORA, w), F32)
    ffn_w = [(wg, wu, wd.astype(BF16))
             for wg, wu, wd in ((ffn1_gate, ffn1_up, ffn1_down), (ffn2_gate, ffn2_up, ffn2_down))]
    w_in_b = _prep_w_in(w_in)
    w_out_b = w_out.astype(BF16)

    def ffn(x, g, ws, l):
        act = ffn_up(x, g, ws[0], ws[1], l, tm=tm, tn=tf)
        return ffn_down(x, act, ws[2], l, tm=tm, tn=tn_down)

    for l in range(depth):
        x = ffn(x, ffn1_norm[l], ffn_w[0], l)

        z, zb = norm_proj(x, mix_norm[l], w_in_b, l, tm=tm)

        y_sb = sb_attention(zb, sb_out_norm[l])

        wq, wkv = _prep_mla_w(mla_w_uq[l], mla_w_ukv[l])
        q, k, v = mla_prep(z, tab, mla_q_norm[l], mla_kv_norm[l], wq, wkv, ts=tm_io)
        y_mla = mla_attention(q, k, v, mla_out_norm[l])

        mu = rwkv_mu[l]
        p = {
            "mu_rkv": mu[:3 * w].reshape(3, w),
            "mu_lora": mu[3 * w:3 * w + 128].reshape(1, 128),
            "mu_g": mu[3 * w + 128:].reshape(1, 128),
            "w0": rwkv_w0[l].reshape(1, w),
            "w2": jnp.concatenate([rwkv_w2[l], zl], axis=0).astype(BF16),
            "a0": rwkv_a0[l].reshape(1, w),
            "a2": jnp.concatenate([zl, rwkv_a2[l]], axis=0).astype(BF16),
            "g2": rwkv_g2[l].astype(BF16),
            "k_k": rwkv_k_k[l].reshape(1, w),
            "k_a": rwkv_k_a[l].reshape(1, w),
            "r_k": rwkv_r_k[l].reshape(1, w),
            "ln_w": rwkv_ln_w[l].reshape(1, w),
            "ln_b": rwkv_ln_b[l].reshape(1, w),
        }
        y_rwkv = rwkv_mix(z, p)

        x = out_proj(x, y_sb, y_mla, y_rwkv, w_out_b, l, tm=tm_io)

        x = ffn(x, ffn2_norm[l], ffn_w[1], l)

    return final_rmsnorm(x, final_norm, tm=tm_io)[None]
```

```python
import jax
import jax.numpy as jnp
import numpy as np
from jax import lax
from jax.experimental import pallas as pl
from jax.experimental.pallas import tpu as pltpu

F32 = jnp.float32
BF16 = jnp.bfloat16

NORM_EPS = 1e-6
SB_HEADS = 4
SB_HEAD_DIM = 128
SB_WIDTH = SB_HEADS * SB_HEAD_DIM
MLA_HEADS = 4
MLA_NOPE_DIM = 128
MLA_ROPE_DIM = 64
MLA_V_DIM = 128
MLA_Q_LORA = 512
MLA_KV_LORA = 256
MLA_WIDTH = MLA_HEADS * MLA_V_DIM
MLA_QK_PAD = 256
ROPE_THETA = 10000.0
RWKV_HEADS = 16
RWKV_HEAD_DIM = 64
RWKV_WIDTH = RWKV_HEADS * RWKV_HEAD_DIM
RWKV_DECAY_LORA = 64
RWKV_A_LORA = 64
RWKV_GATE_LORA = 128
RWKV_GN_EPS = 64e-5
RWKV_CHUNK = 64
RWKV_GROUP = 4
RWKV_CHUNKS_PER_STEP = 4

COL_R, COL_K, COL_V = 0, 1024, 2048
COL_CQ, COL_CKV = 3072, 3584
COL_LORA, COL_XG, COL_ROPE = 3840, 3968, 4096
Z_F32_COLS = 4608
COL_SBQ, COL_SBK, COL_SBV = 0, 512, 1024
Z_B16_COLS = 1536
Z_TILE = 768

VMEM_LIMIT = 48 * 1024 * 1024


def _cparams(sem):
    return pltpu.CompilerParams(dimension_semantics=sem, vmem_limit_bytes=VMEM_LIMIT)


def _dot(a, b):
    return jnp.dot(a, b, preferred_element_type=F32)


def _dot_nt(a, b):
    return lax.dot_general(a, b, (((1,), (1,)), ((), ())), preferred_element_type=F32)


def _dot_tn(a, b):
    return lax.dot_general(a, b, (((0,), (0,)), ((), ())), preferred_element_type=F32)


def _split3(x):
    hi = x.astype(BF16)
    r1 = x - hi.astype(F32)
    mid = r1.astype(BF16)
    lo = (r1 - mid.astype(F32)).astype(BF16)
    return hi, mid, lo


def _split2(x):
    hi = x.astype(BF16)
    lo = (x - hi.astype(F32)).astype(BF16)
    return hi, lo


def _exact_lhs_dot(m_bf16, x):
    hi, mid, lo = _split3(x)
    return _dot(m_bf16, hi) + (_dot(m_bf16, mid) + _dot(m_bf16, lo))


def _rms_rows(x, g):
    ms = jnp.mean(x * x, axis=-1, keepdims=True)
    return x * lax.rsqrt(ms + NORM_EPS) * g


def _norm_proj_kernel(x_ref, g_ref, w_ref, o_ref, ob_ref, h_ref):
    @pl.when(pl.program_id(1) == 0)
    def _():
        h_ref[...] = _rms_rows(x_ref[...], g_ref[...]).astype(BF16)

    z = _dot_nt(h_ref[...], w_ref[...])
    nf = Z_F32_COLS // Z_TILE

    @pl.when(pl.program_id(1) < nf)
    def _():
        o_ref[...] = z

    @pl.when(pl.program_id(1) >= nf)
    def _():
        ob_ref[...] = z.astype(BF16)


def norm_proj(x, g, w, l, *, tm):
    s, d = x.shape
    tn = Z_TILE
    nf, nb = Z_F32_COLS // tn, Z_B16_COLS // tn
    return pl.pallas_call(
        _norm_proj_kernel,
        grid=(s // tm, nf + nb),
        in_specs=[
            pl.BlockSpec((tm, d), lambda i, j: (i, 0)),
            pl.BlockSpec((1, d), lambda i, j: (0, 0)),
            pl.BlockSpec((None, tn, d), lambda i, j: (l, j, 0)),
        ],
        out_specs=[
            pl.BlockSpec((tm, tn), lambda i, j: (i, jnp.minimum(j, nf - 1))),
            pl.BlockSpec((tm, tn), lambda i, j: (i, jnp.maximum(j - nf, 0))),
        ],
        out_shape=[jax.ShapeDtypeStruct((s, Z_F32_COLS), F32),
                   jax.ShapeDtypeStruct((s, Z_B16_COLS), BF16)],
        scratch_shapes=[pltpu.VMEM((tm, d), BF16)],
        compiler_params=_cparams(("parallel", "arbitrary")),
        name="norm_proj",
    )(x, g.reshape(1, d), w)


def _ffn_up_kernel(x_ref, g_ref, wg_ref, wu_ref, o_ref, h_ref):
    @pl.when(pl.program_id(1) == 0)
    def _():
        h_ref[...] = _rms_rows(x_ref[...], g_ref[...]).astype(BF16)

    h = h_ref[...]
    a = _dot(h, wg_ref[...].astype(BF16))
    u = _dot(h, wu_ref[...].astype(BF16))
    o_ref[...] = (a * jax.nn.sigmoid(a) * u).astype(o_ref.dtype)


def ffn_up(x, g, wg, wu, l, *, tm, tn):
    s, d = x.shape
    f = wg.shape[2]
    return pl.pallas_call(
        _ffn_up_kernel,
        grid=(s // tm, f // tn),
        in_specs=[
            pl.BlockSpec((tm, d), lambda i, j: (i, 0)),
            pl.BlockSpec((1, d), lambda i, j: (0, 0)),
            pl.BlockSpec((None, d, tn), lambda i, j: (l, 0, j)),
            pl.BlockSpec((None, d, tn), lambda i, j: (l, 0, j)),
        ],
        out_specs=pl.BlockSpec((tm, tn), lambda i, j: (i, j)),
        out_shape=jax.ShapeDtypeStruct((s, f), BF16),
        scratch_shapes=[pltpu.VMEM((tm, d), BF16)],
        compiler_params=_cparams(("parallel", "arbitrary")),
        name="ffn_up",
    )(x, g.reshape(1, d), wg, wu)


def _ffn_down_kernel(x_ref, a_ref, w_ref, o_ref):
    o_ref[...] = x_ref[...] + 0.5 * _dot(a_ref[...], w_ref[...])


def ffn_down(x, act, wd, l, *, tm, tn):
    s, d = x.shape
    f = act.shape[1]
    return pl.pallas_call(
        _ffn_down_kernel,
        grid=(s // tm, d // tn),
        in_specs=[
            pl.BlockSpec((tm, tn), lambda i, j: (i, j)),
            pl.BlockSpec((tm, f), lambda i, j: (i, 0)),
            pl.BlockSpec((None, f, tn), lambda i, j: (l, 0, j)),
        ],
        out_specs=pl.BlockSpec((tm, tn), lambda i, j: (i, j)),
        out_shape=jax.ShapeDtypeStruct((s, d), F32),
        compiler_params=_cparams(("parallel", "arbitrary")),
        name="ffn_down",
    )(x, act, wd)


def _out_proj_kernel(x_ref, a_ref, b_ref, c_ref, wa_ref, wb_ref, wc_ref, o_ref):
    acc = _dot(a_ref[...], wa_ref[...])
    acc += _dot(b_ref[...], wb_ref[...])
    acc += _dot(c_ref[...], wc_ref[...])
    o_ref[...] = x_ref[...] + acc


def out_proj(x, y_sb, y_mla, y_rwkv, w_out, l, *, tm):
    s, d = x.shape
    assert SB_WIDTH == MLA_WIDTH and RWKV_WIDTH == SB_WIDTH + MLA_WIDTH
    wspec = lambda rows, blk: pl.BlockSpec((None, rows, d), lambda i: (l, blk, 0),
                                           pipeline_mode=pl.Buffered(1))
    return pl.pallas_call(
        _out_proj_kernel,
        grid=(s // tm,),
        in_specs=[
            pl.BlockSpec((tm, d), lambda i: (i, 0)),
            pl.BlockSpec((tm, SB_WIDTH), lambda i: (i, 0)),
            pl.BlockSpec((tm, MLA_WIDTH), lambda i: (i, 0)),
            pl.BlockSpec((tm, RWKV_WIDTH), lambda i: (i, 0)),
            wspec(SB_WIDTH, 0),
            wspec(MLA_WIDTH, 1),
            wspec(RWKV_WIDTH, 1),
        ],
        out_specs=pl.BlockSpec((tm, d), lambda i: (i, 0)),
        out_shape=jax.ShapeDtypeStruct((s, d), F32),
        compiler_params=_cparams(("parallel",)),
        name="out_proj",
    )(x, y_sb, y_mla, y_rwkv, w_out, w_out, w_out)


def _final_norm_kernel(x_ref, g_ref, o_ref):
    o_ref[...] = _rms_rows(x_ref[...], g_ref[...])


def final_rmsnorm(x, g, *, tm):
    s, d = x.shape
    return pl.pallas_call(
        _final_norm_kernel,
        grid=(s // tm,),
        in_specs=[pl.BlockSpec((tm, d), lambda i: (i, 0)), pl.BlockSpec((1, d), lambda i: (0, 0))],
        out_specs=pl.BlockSpec((tm, d), lambda i: (i, 0)),
        out_shape=jax.ShapeDtypeStruct((s, d), F32),
        compiler_params=_cparams(("parallel",)),
        name="final_norm",
    )(x, g.reshape(1, d))


SB_BQ = 256
SB_BK = 128
SB_DEAD_LOG = -104.0


def _sb_attn_kernel(q_ref, k_ref, v_ref, g_ref, o_ref):
    i = pl.program_id(0)
    bq, bk, dh = SB_BQ, SB_BK, SB_HEAD_DIM
    heads = range(SB_HEADS)
    hs = [slice(h * dh, (h + 1) * dh) for h in heads]
    q = [q_ref[:, hs[h]] for h in heads]
    row = lax.broadcasted_iota(jnp.int32, (bq, bk), 0)
    col = lax.broadcasted_iota(jnp.int32, (bq, bk), 1)
    kr = lax.broadcasted_iota(jnp.int32, (bk, bk), 0)
    kc = lax.broadcasted_iota(jnp.int32, (bk, bk), 1)
    upper = jnp.where(kr > kc, 1.0, 0.0).astype(BF16)

    def block(kbs, carry, acc, masked):
        starts = [pl.multiple_of(kb * bk, bk) for kb in kbs]
        pairs = [(t, h) for t in range(len(kbs)) for h in heads]
        z = {p: _dot_nt(q[p[1]], k_ref[pl.ds(starts[p[0]], bk), hs[p[1]]]) for p in pairs}
        log_beta = {p: jnp.minimum(z[p], 0.0) - jnp.log(1.0 + jnp.exp(-jnp.abs(z[p]))) for p in pairs}
        log_keep = {p: log_beta[p] - z[p] for p in pairs}
        if masked:
            before = [(starts[t] + col) < (i * bq + row) for t in range(len(kbs))]
            log_keep = {p: jnp.where(before[p[0]], log_keep[p], 0.0) for p in pairs}
        split = {p: _split2(log_keep[p]) for p in pairs}
        within = {p: _dot(split[p][0], upper) + _dot(split[p][1], upper) for p in pairs}
        rowsum = {p: jnp.sum(log_keep[p], axis=-1, keepdims=True) for p in pairs}
        for t in range(len(kbs)):
            a = [jnp.exp(log_beta[(t, h)] + (carry[h] + within[(t, h)])) for h in heads]
            if masked:
                a = [jnp.where(before[t], a[h], 0.0) for h in heads]
            acc = [acc[h] + _dot(a[h].astype(BF16), v_ref[pl.ds(starts[t], bk), hs[h]]) for h in heads]
            carry = [carry[h] + rowsum[(t, h)] for h in heads]
        return carry, acc

    def alive_of(carry):
        m = jnp.max(carry[0])
        for h in heads[1:]:
            m = jnp.maximum(m, jnp.max(carry[h]))
        return m

    carry = [jnp.zeros((bq, 1), F32) for _ in heads]
    acc = [jnp.zeros((bq, dh), F32) for _ in heads]
    nd = bq // bk
    first = i * nd
    carry, acc = block([first + d for d in range(nd - 1, -1, -1)], carry, acc, True)

    def cond(st):
        return jnp.logical_and(st[0] >= 0, st[1] > SB_DEAD_LOG)

    def body(st):
        kb, _, carry, acc = st
        carry, acc = block([kb - d for d in range(nd)], list(carry), list(acc), False)
        return kb - nd, alive_of(carry), tuple(carry), tuple(acc)

    st = lax.while_loop(cond, body, (first - 1, alive_of(carry), tuple(carry), tuple(acc)))
    acc = st[3]
    for h in heads:
        o_ref[:, hs[h]] = _rms_rows(acc[h], g_ref[h:h + 1, :]).astype(o_ref.dtype)


def sb_attention(zb, gains):
    s = zb.shape[0]
    assert s % SB_BQ == 0 and SB_BQ % SB_BK == 0
    whole = lambda c: pl.BlockSpec((s, SB_WIDTH), lambda i: (0, c // SB_WIDTH),
                                   pipeline_mode=pl.Buffered(1))
    return pl.pallas_call(
        _sb_attn_kernel,
        grid=(s // SB_BQ,),
        in_specs=[
            pl.BlockSpec((SB_BQ, SB_WIDTH), lambda i: (i, COL_SBQ // SB_WIDTH)),
            whole(COL_SBK),
            whole(COL_SBV),
            pl.BlockSpec((SB_HEADS, SB_HEAD_DIM), lambda i: (0, 0)),
        ],
        out_specs=pl.BlockSpec((SB_BQ, SB_WIDTH), lambda i: (i, 0)),
        out_shape=jax.ShapeDtypeStruct((s, SB_WIDTH), BF16),
        compiler_params=_cparams(("arbitrary",)),
        name="sb_attention",
    )(zb, zb, zb, gains)


def _mla_prep_kernel(cq_ref, ckv_ref, rope_ref, tab_ref, qn_ref, kvn_ref, wq_ref, wkv_ref,
                     q_ref, k_ref, v_ref):
    scale = float((MLA_NOPE_DIM + MLA_ROPE_DIM) ** -0.5 * np.log2(np.e))
    hq = _rms_rows(cq_ref[...], qn_ref[...]).astype(BF16)
    hkv = _rms_rows(ckv_ref[...], kvn_ref[...]).astype(BF16)
    cos2 = tab_ref[:, :128]
    sin2 = tab_ref[:, 128:]
    kr = rope_ref[...]
    k_pe = kr * cos2 + pltpu.roll(kr, 64, 1) * sin2
    k_pe = k_pe.astype(BF16)
    qall = _dot(hq, wq_ref[...])
    kvall = _dot(hkv, wkv_ref[...])
    for h in range(MLA_HEADS):
        qh = qall[:, h * 384:(h + 1) * 384]
        q_ref[h, :, :128] = (qh[:, :128] * scale).astype(BF16)
        q_ref[h, :, 128:] = ((qh[:, 128:256] * cos2 + qh[:, 256:384] * sin2) * scale).astype(BF16)
        k_ref[h, :, :128] = kvall[:, h * 256:h * 256 + 128].astype(BF16)
        k_ref[h, :, 128:] = k_pe
        v_ref[h, :, :128] = kvall[:, h * 256 + 128:(h + 1) * 256].astype(BF16)
        v_ref[h, :, 128:] = jnp.ones((kr.shape[0], 128), BF16)


def mla_prep(z, tab, q_norm, kv_norm, wq, wkv, *, ts):
    s = z.shape[0]
    return pl.pallas_call(
        _mla_prep_kernel,
        grid=(s // ts,),
        in_specs=[
            pl.BlockSpec((ts, MLA_Q_LORA), lambda i: (i, COL_CQ // MLA_Q_LORA)),
            pl.BlockSpec((ts, MLA_KV_LORA), lambda i: (i, COL_CKV // MLA_KV_LORA)),
            pl.BlockSpec((ts, 128), lambda i: (i, COL_ROPE // 128)),
            pl.BlockSpec((ts, 256), lambda i: (i, 0)),
            pl.BlockSpec((1, MLA_Q_LORA), lambda i: (0, 0)),
            pl.BlockSpec((1, MLA_KV_LORA), lambda i: (0, 0)),
            pl.BlockSpec(wq.shape, lambda i: (0, 0)),
            pl.BlockSpec(wkv.shape, lambda i: (0, 0)),
        ],
        out_specs=[
            pl.BlockSpec((MLA_HEADS, ts, MLA_QK_PAD), lambda i: (0, i, 0)),
            pl.BlockSpec((MLA_HEADS, ts, MLA_QK_PAD), lambda i: (0, i, 0)),
            pl.BlockSpec((MLA_HEADS, ts, 2 * MLA_V_DIM), lambda i: (0, i, 0)),
        ],
        out_shape=[
            jax.ShapeDtypeStruct((MLA_HEADS, s, MLA_QK_PAD), BF16),
            jax.ShapeDtypeStruct((MLA_HEADS, s, MLA_QK_PAD), BF16),
            jax.ShapeDtypeStruct((MLA_HEADS, s, 2 * MLA_V_DIM), BF16),
        ],
        compiler_params=_cparams(("parallel",)),
        name="mla_prep",
    )(z, z, z, tab, q_norm.reshape(1, -1), kv_norm.reshape(1, -1), wq, wkv)


MLA_BQ = 256
MLA_BK = 1024
MLA_HEADS_PER_STEP = 4


def _mla_attn_kernel(q_ref, k_ref, v_ref, g_ref, o_ref):
    i = pl.program_id(1)
    bq, bk = MLA_BQ, MLA_BK
    heads = range(MLA_HEADS_PER_STEP)
    q = [q_ref[h] for h in heads]
    nd = bk // bq

    def block(kb, m, acc, width, masked):
        start = pl.multiple_of(kb * bk, bk)
        sc = [_dot_nt(q[h], k_ref[h, pl.ds(start, width), :]) for h in heads]
        if masked:
            row = lax.broadcasted_iota(jnp.int32, (bq, width), 0) + i * bq
            col = lax.broadcasted_iota(jnp.int32, (bq, width), 1) + start
            sc = [jnp.where(col <= row, sc[h], -1e30) for h in heads]
        m_new = [jnp.maximum(m[h], jnp.max(sc[h], axis=-1, keepdims=True)) for h in heads]
        for h in heads:
            alpha = jnp.exp2(m[h] - m_new[h])
            p = jnp.exp2(sc[h] - m_new[h])
            acc[h] = acc[h] * alpha + _dot(p.astype(BF16), v_ref[h, pl.ds(start, width), :])
        return m_new, acc

    def body(kb, st):
        m, acc = block(kb, list(st[0]), list(st[1]), bk, False)
        return tuple(m), tuple(acc)

    m0 = tuple(jnp.full((bq, 1), -1e30, F32) for _ in heads)
    a0 = tuple(jnp.zeros((bq, 2 * MLA_V_DIM), F32) for _ in heads)
    last = i // nd
    m, acc = lax.fori_loop(0, last, body, (m0, a0))

    for r in range(nd):
        @pl.when(i % nd == r)
        def _(r=r):
            _, acc_r = block(last, list(m), list(acc), (r + 1) * bq, True)
            for h in heads:
                y = acc_r[h][:, :MLA_V_DIM] / acc_r[h][:, MLA_V_DIM:]
                o_ref[:, h * MLA_V_DIM:(h + 1) * MLA_V_DIM] = _rms_rows(
                    y, g_ref[h:h + 1, :]).astype(o_ref.dtype)


def mla_attention(q, k, v, gains):
    s = q.shape[1]
    assert s % MLA_BK == 0 and MLA_BK % MLA_BQ == 0
    hps = MLA_HEADS_PER_STEP
    whole = lambda n: pl.BlockSpec((hps, s, n), lambda g, i: (g, 0, 0), pipeline_mode=pl.Buffered(1))
    return pl.pallas_call(
        _mla_attn_kernel,
        grid=(MLA_HEADS // hps, s // MLA_BQ),
        in_specs=[
            pl.BlockSpec((hps, MLA_BQ, MLA_QK_PAD), lambda g, i: (g, i, 0)),
            whole(MLA_QK_PAD),
            whole(2 * MLA_V_DIM),
            pl.BlockSpec((hps, MLA_V_DIM), lambda g, i: (g, 0)),
        ],
        out_specs=pl.BlockSpec((MLA_BQ, hps * MLA_V_DIM), lambda g, i: (i, g)),
        out_shape=jax.ShapeDtypeStruct((s, MLA_WIDTH), BF16),
        compiler_params=_cparams(("parallel", "arbitrary")),
        name="mla_attention",
    )(q, k, v, gains)


def _head_sum_matrix():
    r = lax.broadcasted_iota(jnp.int32, (128, 128), 0) // RWKV_HEAD_DIM
    c = lax.broadcasted_iota(jnp.int32, (128, 128), 1) // RWKV_HEAD_DIM
    return jnp.where(r == c, 1.0, 0.0).astype(BF16)


def _head_sums(x, ones_bd):
    parts = []
    for j in range(x.shape[1] // 128):
        hi, lo = _split2(x[:, j * 128:(j + 1) * 128])
        parts.append(_dot(hi, ones_bd) + _dot(lo, ones_bd))
    return jnp.concatenate(parts, axis=-1)


def _rwkv_prep_kernel(r_ref, k_ref, v_ref, lora_ref, xg_ref, mu_ref, mul_ref, mug_ref,
                      w0_ref, w2_ref, a0_ref, a2_ref, g2_ref, kk_ref, ka_ref,
                      ro_ref, lw_ref, ko_ref, vo_ref, kko_ref, bo_ref, go_ref,
                      prev_ref, prevl_ref):
    ts = r_ref.shape[0]

    @pl.when(pl.program_id(0) == 0)
    def _():
        prev_ref[...] = jnp.zeros_like(prev_ref)
        prevl_ref[...] = jnp.zeros_like(prevl_ref)

    first_row = lax.broadcasted_iota(jnp.int32, (ts, 1), 0) == 0

    def shift_mix(x, prev_row, mu):
        x_prev = jnp.where(first_row, prev_row, pltpu.roll(x, 1, 0))
        return x + (x_prev - x) * mu

    r_in, k_in, v_in = r_ref[...], k_ref[...], v_ref[...]
    lora_in, xg_in = lora_ref[...], xg_ref[...]
    r = shift_mix(r_in, prev_ref[0:1, :], mu_ref[0:1, :])
    k = shift_mix(k_in, prev_ref[1:2, :], mu_ref[1:2, :])
    v = shift_mix(v_in, prev_ref[2:3, :], mu_ref[2:3, :])
    lora = shift_mix(lora_in, prevl_ref[0:1, :], mul_ref[...])
    xg = shift_mix(xg_in, prevl_ref[1:2, :], mug_ref[...])
    prev_ref[0:1, :] = r_in[ts - 1:ts, :]
    prev_ref[1:2, :] = k_in[ts - 1:ts, :]
    prev_ref[2:3, :] = v_in[ts - 1:ts, :]
    prevl_ref[0:1, :] = lora_in[ts - 1:ts, :]
    prevl_ref[1:2, :] = xg_in[ts - 1:ts, :]

    dw = w0_ref[...] + _dot(jnp.tanh(lora).astype(BF16), w2_ref[...])
    log_w = jnp.minimum(dw, 0.0) - jnp.log(1.0 + jnp.exp(-jnp.abs(dw))) - 0.5
    lw_ref[...] = -jnp.exp(log_w)
    a = jax.nn.sigmoid(a0_ref[...] + _dot(lora.astype(BF16), a2_ref[...]))
    go_ref[...] = _dot(jax.nn.sigmoid(xg).astype(BF16), g2_ref[...])

    kk = k * kk_ref[...]
    ss = _head_sums(kk * kk, _head_sum_matrix())
    kk = kk / jnp.maximum(jnp.sqrt(ss), 1e-12)
    ro_ref[...] = r
    vo_ref[...] = v
    ko_ref[...] = k * (1.0 + (a - 1.0) * ka_ref[...])
    kko_ref[...] = kk
    bo_ref[...] = kk * a


def _rwkv_chunk_kernel(r_ref, lw_ref, k_ref, v_ref, kk_ref, b_ref, g_ref, rk_ref, lnw_ref, lnb_ref,
                       o_ref, st_ref, y_ref):
    c = RWKV_CHUNK
    n = RWKV_HEAD_DIM
    nh = r_ref.shape[1] // n
    chunks = range(r_ref.shape[0] // c)

    @pl.when(pl.program_id(0) == 0)
    def _():
        st_ref[...] = jnp.zeros_like(st_ref)

    ri = lax.broadcasted_iota(jnp.int32, (c, c), 0)
    ci = lax.broadcasted_iota(jnp.int32, (c, c), 1)
    tril_ones = jnp.where(ri >= ci, 1.0, 0.0).astype(BF16)

    gh = RWKV_GROUP
    gw = gh * n
    groups = range(nh // gh)
    gs = [slice(g * gw, (g + 1) * gw) for g in groups]
    same_head = (lax.broadcasted_iota(jnp.int32, (gw, gw), 0) // n
                 == lax.broadcasted_iota(jnp.int32, (gw, gw), 1) // n)
    zero_b = jnp.zeros((gw, gw), BF16)

    def bd(y):
        yb = y.astype(BF16)
        return jnp.where(same_head, jnp.concatenate([yb] * gh, axis=0), zero_b)

    def mmb(x, y_bd):
        return _dot(x.astype(BF16), y_bd)

    ri2 = lax.broadcasted_iota(jnp.int32, (2 * c, gw), 0)
    ci2 = lax.broadcasted_iota(jnp.int32, (2 * c, gw), 1) % c
    causal2 = ci2 <= jnp.where(ri2 < c, ri2 - 1, ri2 - c)
    rc = lax.broadcasted_iota(jnp.int32, (c, gw), 0)
    cc = lax.broadcasted_iota(jnp.int32, (c, gw), 1) % c
    eye_g = jnp.where(rc == cc, 1.0, 0.0).astype(F32)
    d16_g = (rc // 16) == (cc // 16)
    d32_g = (rc // 32) == (cc // 32)
    lo_g = jnp.logical_and(d32_g, jnp.logical_not(d16_g))
    colhead = lax.broadcasted_iota(jnp.int32, (n, gw), 1) // n

    p_end, v_c, lhs, b_t, k_t, dec = [], [], [], [], [], []
    for ch in chunks:
        rows = slice(ch * c, (ch + 1) * c)
        lw = lw_ref[rows, :]
        cum = _exact_lhs_dot(tril_ones, lw)
        p_incl = jnp.exp(cum)
        p_inv = jnp.exp(-cum)
        pe = p_incl[c - 1:c, :]
        bt = b_ref[rows, :] * p_inv
        kt = k_ref[rows, :] * p_inv
        p_end.append(pe)
        v_c.append(v_ref[rows, :])
        b_t.append(bt)
        k_t.append(kt)
        lhs.append(jnp.concatenate([kk_ref[rows, :] * jnp.exp(cum - lw), r_ref[rows, :] * p_incl],
                                   axis=0).astype(BF16))
        dec.append(jnp.concatenate([bt * pe, kt * pe], axis=0).astype(BF16))

    cg = [(ch, g) for ch in chunks for g in groups]
    ab = {q: jnp.where(causal2, _dot_nt(lhs[q[0]][:, gs[q[1]]], bd(b_t[q[0]][:, gs[q[1]]])), 0.0)
          for q in cg}
    ak = {q: jnp.where(causal2, _dot_nt(lhs[q[0]][:, gs[q[1]]], bd(k_t[q[0]][:, gs[q[1]]])), 0.0)
          for q in cg}
    akv = {q: mmb(ak[q], bd(v_c[q[0]][:, gs[q[1]]])) for q in cg}
    a_ub = {q: ab[q][:c] for q in cg}
    ld = {q: jnp.where(d16_g, a_ub[q], 0.0) for q in cg}
    x = {q: eye_g - ld[q] for q in cg}
    pw = {q: mmb(ld[q], bd(ld[q])) for q in cg}
    x = {q: x[q] + mmb(x[q], bd(pw[q])) for q in cg}
    pw = {q: mmb(pw[q], bd(pw[q])) for q in cg}
    x = {q: x[q] + mmb(x[q], bd(pw[q])) for q in cg}
    pw = {q: mmb(pw[q], bd(pw[q])) for q in cg}
    x = {q: x[q] + mmb(x[q], bd(pw[q])) for q in cg}
    t = {q: mmb(x[q], bd(jnp.where(lo_g, a_ub[q], 0.0))) for q in cg}
    x = {q: x[q] - mmb(t[q], bd(x[q])) for q in cg}
    t = {q: mmb(x[q], bd(jnp.where(d32_g, 0.0, a_ub[q]))) for q in cg}
    x = {q: x[q] - mmb(t[q], bd(x[q])) for q in cg}
    u0 = {q: -mmb(x[q], bd(akv[q][:c])) for q in cg}
    wk = {q: -mmb(x[q], bd(lhs[q[0]][:c, gs[q[1]]])) for q in cg}
    lhs2 = {q: jnp.concatenate([wk[q].astype(BF16), lhs[q[0]][c:, gs[q[1]]]], axis=0) for q in cg}

    st = [st_ref[:, gs[g]] for g in groups]
    for ch in chunks:
        rows = slice(ch * c, (ch + 1) * c)
        through = [_dot_nt(lhs2[(ch, g)], bd(st[g])) for g in groups]
        u = [through[g][:c] + u0[(ch, g)] for g in groups]
        for g in groups:
            y_ref[rows, gs[g]] = through[g][c:] + akv[(ch, g)][c:] + mmb(ab[(ch, g)][c:], bd(u[g]))
        for g in groups:
            uv = jnp.concatenate([u[g], v_c[ch][:, gs[g]]], axis=0).astype(BF16)
            cross = _dot_tn(uv, dec[ch][:, gs[g]])
            upd = st[g] * p_end[ch][:, gs[g]]
            for h in range(gh):
                upd = upd + jnp.where(colhead == h, cross[h * n:(h + 1) * n], 0.0)
            st[g] = upd
    for g in groups:
        st_ref[:, gs[g]] = st[g]

    ones_bd = _head_sum_matrix()
    y = y_ref[...]
    v = v_ref[...]
    mean = _head_sums(y, ones_bd) * (1.0 / n)
    yc = y - mean
    var = _head_sums(yc * yc, ones_bd) * (1.0 / n)
    yn = yc * lax.rsqrt(var + RWKV_GN_EPS) * lnw_ref[...] + lnb_ref[...]
    bonus = _head_sums(r_ref[...] * k_ref[...] * rk_ref[...], ones_bd)
    o_ref[...] = ((yn + bonus * v) * g_ref[...]).astype(o_ref.dtype)


def _rwkv_kernel(zr_ref, zk_ref, zv_ref, zl_ref, zg_ref, mu_ref, mul_ref, mug_ref, w0_ref, w2_ref,
                 a0_ref, a2_ref, g2_ref, kk_ref, ka_ref, rk_ref, lnw_ref, lnb_ref,
                 o_ref, st_ref, y_ref, prev_ref, prevl_ref, *staged):
    _rwkv_prep_kernel(zr_ref, zk_ref, zv_ref, zl_ref, zg_ref, mu_ref, mul_ref, mug_ref, w0_ref, w2_ref,
                      a0_ref, a2_ref, g2_ref, kk_ref, ka_ref, *staged, prev_ref, prevl_ref)
    _rwkv_chunk_kernel(*staged, rk_ref, lnw_ref, lnb_ref, o_ref, st_ref, y_ref)


def rwkv_mix(z, p):
    s = z.shape[0]
    w = RWKV_WIDTH
    rows = RWKV_CHUNK * RWKV_CHUNKS_PER_STEP
    assert s % rows == 0
    row = lambda n: pl.BlockSpec((1, n), lambda i: (0, 0))
    lora_w = pl.BlockSpec((128, w), lambda i: (0, 0))
    return pl.pallas_call(
        _rwkv_kernel,
        grid=(s // rows,),
        in_specs=[
            pl.BlockSpec((rows, w), lambda i: (i, COL_R // w)),
            pl.BlockSpec((rows, w), lambda i: (i, COL_K // w)),
            pl.BlockSpec((rows, w), lambda i: (i, COL_V // w)),
            pl.BlockSpec((rows, 128), lambda i: (i, COL_LORA // 128)),
            pl.BlockSpec((rows, 128), lambda i: (i, COL_XG // 128)),
            pl.BlockSpec((3, w), lambda i: (0, 0)),
            row(128), row(128),
            row(w), lora_w,
            row(w), lora_w,
            lora_w,
            row(w), row(w),
            row(w), row(w), row(w),
        ],
        out_specs=pl.BlockSpec((rows, w), lambda i: (i, 0)),
        out_shape=jax.ShapeDtypeStruct((s, w), BF16),
        scratch_shapes=[
            pltpu.VMEM((RWKV_HEAD_DIM, w), F32),
            pltpu.VMEM((rows, w), F32),
            pltpu.VMEM((8, w), F32), pltpu.VMEM((8, 128), F32),
        ] + [pltpu.VMEM((rows, w), F32)] * 7,
        compiler_params=_cparams(("arbitrary",)),
        name="rwkv_mix",
    )(z, z, z, z, z, p["mu_rkv"], p["mu_lora"], p["mu_g"], p["w0"], p["w2"], p["a0"], p["a2"],
      p["g2"], p["k_k"], p["k_a"], p["r_k"], p["ln_w"], p["ln_b"])


def _swap_halves(w):
    half = w.shape[-1] // 2
    return jnp.concatenate([w[..., half:], w[..., :half]], axis=-1)


def _prep_w_in(w_in):
    wt = jnp.swapaxes(w_in, 1, 2)
    sizes = [SB_WIDTH] * 3 + [MLA_Q_LORA, MLA_KV_LORA, MLA_ROPE_DIM] + [RWKV_WIDTH] * 3 + [
        RWKV_DECAY_LORA, RWKV_A_LORA, RWKV_GATE_LORA]
    idx = [int(i) for i in np.cumsum(sizes)[:-1]]
    sbq, sbk, sbv, cq, ckv, krope, r, k, v, xw, xa, xg = jnp.split(wt, idx, axis=1)
    sbq = sbq * (SB_HEAD_DIM ** -0.5)
    half = MLA_ROPE_DIM // 2
    krope_sw = jnp.concatenate([krope[:, half:], krope[:, :half]], axis=1)
    rows = [r, k, v, cq, ckv, xw, xa, xg, krope, krope_sw]
    used = sum(t.shape[1] for t in rows)
    rows.append(jnp.zeros((wt.shape[0], Z_F32_COLS - used, wt.shape[2]), wt.dtype))
    rows += [sbq, sbk, sbv]
    return jnp.concatenate(rows, axis=1).astype(BF16)


def _prep_mla_w(w_uq, w_ukv):
    q = w_uq.reshape(MLA_Q_LORA, MLA_HEADS, MLA_NOPE_DIM + MLA_ROPE_DIM)
    nope, pe = q[..., :MLA_NOPE_DIM], q[..., MLA_NOPE_DIM:]
    zpad = jnp.zeros_like(pe)
    wq = jnp.concatenate([nope, pe, zpad, _swap_halves(pe), zpad], axis=-1)
    return wq.reshape(MLA_Q_LORA, MLA_HEADS * 384).astype(BF16), w_ukv.astype(BF16)


def _rope_table(positions):
    half = MLA_ROPE_DIM // 2
    inv_freq = ROPE_THETA ** (-jnp.arange(half, dtype=F32) / half)
    ang = positions.astype(F32)[:, None] * inv_freq
    cos, sin = jnp.cos(ang), jnp.sin(ang)
    z = jnp.zeros((positions.shape[0], MLA_ROPE_DIM), F32)
    return jnp.concatenate([cos, cos, z, -sin, sin, z], axis=-1)


def _pick_tile(n, candidates):
    for c in candidates:
        if n % c == 0:
            return c
    raise ValueError(f"no tile for {n}")


def kernel(x, positions, ffn1_norm, ffn1_gate, ffn1_up, ffn1_down, mix_norm, w_in, mla_q_norm,
           mla_w_uq, mla_kv_norm, mla_w_ukv, rwkv_mu, rwkv_w0, rwkv_w2, rwkv_a0, rwkv_a2, rwkv_g2,
           rwkv_k_k, rwkv_k_a, rwkv_r_k, rwkv_ln_w, rwkv_ln_b, sb_out_norm, mla_out_norm, w_out,
           ffn2_norm, ffn2_gate, ffn2_up, ffn2_down, final_norm):
    bsz, s, d = x.shape
    assert bsz == 1 and d == SB_WIDTH + MLA_WIDTH + RWKV_WIDTH
    depth = w_in.shape[0]
    d_ff = ffn1_gate.shape[-1]
    tm = _pick_tile(s, (1024, 512, 256, 128))
    tf = _pick_tile(d_ff, (512, 256, 128))
    tn_down = 512
    tm_io = _pick_tile(s, (512, 256, 128))
    x = x[0]
    tab = _rope_table(positions[0])
    w = RWKV_WIDTH
    zl = jnp.zeros((RWKV_DECAY_LORA, w), F32)
    ffn_w = [(wg, wu, wd.astype(BF16))
             for wg, wu, wd in ((ffn1_gate, ffn1_up, ffn1_down), (ffn2_gate, ffn2_up, ffn2_down))]
    w_in_b = _prep_w_in(w_in)
    w_out_b = w_out.astype(BF16)

    def ffn(x, g, ws, l):
        act = ffn_up(x, g, ws[0], ws[1], l, tm=tm, tn=tf)
        return ffn_down(x, act, ws[2], l, tm=tm, tn=tn_down)

    for l in range(depth):
        x = ffn(x, ffn1_norm[l], ffn_w[0], l)

        z, zb = norm_proj(x, mix_norm[l], w_in_b, l, tm=tm)

        y_sb = sb_attention(zb, sb_out_norm[l])

        wq, wkv = _prep_mla_w(mla_w_uq[l], mla_w_ukv[l])
        q, k, v = mla_prep(z, tab, mla_q_norm[l], mla_kv_norm[l], wq, wkv, ts=tm_io)
        y_mla = mla_attention(q, k, v, mla_out_norm[l])

        mu = rwkv_mu[l]
        p = {
            "mu_rkv": mu[:3 * w].reshape(3, w),
            "mu_lora": mu[3 * w:3 * w + 128].reshape(1, 128),
            "mu_g": mu[3 * w + 128:].reshape(1, 128),
            "w0": rwkv_w0[l].reshape(1, w),
            "w2": jnp.concatenate([rwkv_w2[l], zl], axis=0).astype(BF16),
            "a0": rwkv_a0[l].reshape(1, w),
            "a2": jnp.concatenate([zl, rwkv_a2[l]], axis=0).astype(BF16),
            "g2": rwkv_g2[l].astype(BF16),
            "k_k": rwkv_k_k[l].reshape(1, w),
            "k_a": rwkv_k_a[l].reshape(1, w),
            "r_k": rwkv_r_k[l].reshape(1, w),
            "ln_w": rwkv_ln_w[l].reshape(1, w),
            "ln_b": rwkv_ln_b[l].reshape(1, w),
        }
        y_rwkv = rwkv_mix(z, p)

        x = out_proj(x, y_sb, y_mla, y_rwkv, w_out_b, l, tm=tm_io)

        x = ffn(x, ffn2_norm[l], ffn_w[1], l)

    return final_rmsnorm(x, final_norm, tm=tm_io)[None]
```
